```python
import math
import jax
import jax.numpy as jnp
from jax import lax
import numpy as np

D_MODEL = 1024
BATCH = 32
SEQ = 256
DEPTH = 2
DEC_BATCH = 4
DEC_SEQ = 4096
PAST_LEN = 256

F32 = jnp.float32
GRID_W = 64
EPS = 1e-6

A_HEADS = 4
A_DIM = 64
A_VDIM = 2 * A_DIM
ROPE_BASE = 10000.0
Q_BLOCK = 128
POOL_WINDOWS = (2, 4, 8, 16)
POOL_GROUP = 128
POOL_WIDTH = POOL_GROUP * len(POOL_WINDOWS)
C_HEADS = 4
C_DK = 128
C_DV = 128
C_CONV = 3
C_CHUNK = 64
M_HEADS = 4
M_DK = 128
M_DV = 128
M_CHUNK = 64
N_BRANCH = 4
BRANCH_WIDTH = 512
N_EXPERTS = 32
TOP_K = 4
D_EXPERT = 1024
SWIGLU_LIMIT = 7.0
SWIGLU_ALPHA = 1.702
MOE_BLOCK = 128

IN_SPLITS = (
    A_HEADS * 2 * A_DIM, A_HEADS * 2 * A_DIM, A_HEADS * A_VDIM,
    POOL_WIDTH,
    C_HEADS * C_DK, C_HEADS * C_DK, C_HEADS * C_DV, C_HEADS * C_DV,
    2 * C_HEADS, 2 * C_HEADS,
    M_HEADS * M_DK, M_HEADS * M_DK, M_HEADS * M_DV, M_HEADS * M_DV,
    2 * M_HEADS, 2 * M_HEADS,
)
D_IN = sum(IN_SPLITS)

kernel_name = 'hybrid_flow_trunk_ctx_prefix_step'


def rms_norm(x, g):
    xf = x.astype(F32)
    y = xf * lax.rsqrt(jnp.mean(xf * xf, axis=-1, keepdims=True) + EPS)
    return (y * g.astype(F32)).astype(x.dtype)


def l2_normalize(x):
    return x * lax.rsqrt(jnp.sum(x * x, axis=-1, keepdims=True) + EPS)


def split_cols(p, sizes):
    offsets = [int(o) for o in np.cumsum(sizes)[:-1]]
    return jnp.split(p, offsets, axis=-1)


def split_heads(a, n_heads):
    b, l, _ = a.shape
    return a.reshape(b, l, n_heads, -1).transpose(0, 2, 1, 3)


def merge_heads(a):
    b, h, l, d = a.shape
    return a.transpose(0, 2, 1, 3).reshape(b, l, h * d)


def flip_seq(a):
    return jnp.flip(a, axis=2)


def to_chunks(a, size):
    b, h, l = a.shape[:3]
    return a.reshape((b, h, l // size, size) + a.shape[3:])


def axial_rope_tables(L):
    rows = L // GRID_W
    row = jnp.repeat(jnp.arange(rows, dtype=F32), GRID_W)
    col = jnp.tile(jnp.arange(GRID_W, dtype=F32), rows)
    n_freq = A_DIM // 4
    inv = ROPE_BASE ** (-jnp.arange(n_freq, dtype=F32) / n_freq)
    ang = jnp.stack([row[:, None] * inv, col[:, None] * inv], axis=1)
    return jnp.cos(ang), jnp.sin(ang)


def apply_axial_rope(x, cos, sin):
    shp = x.shape
    xr = x.astype(F32).reshape(shp[:-1] + (2, 2, A_DIM // 4))
    x1, x2 = xr[..., 0, :], xr[..., 1, :]
    c, s = cos[:, None], sin[:, None]
    out = jnp.stack([x1 * c - x2 * s, x2 * c + x1 * s], axis=-2)
    return out.reshape(shp).astype(x.dtype)


def diff_attention(q, k, v, lam):
    B, H, Lq = q.shape[:3]
    nb = Lq // Q_BLOCK
    qb = q.reshape(B, H, nb, Q_BLOCK, 2, A_DIM).transpose(2, 0, 1, 3, 4, 5)
    scale = A_DIM ** -0.5

    def block(q_blk):
        s = jnp.einsum('bhqmd,bhkmd->bhmqk', q_blk, k, preferred_element_type=F32) * scale
        pr = jax.nn.softmax(s, axis=-1)
        a = pr[:, :, 0] - lam * pr[:, :, 1]
        return jnp.einsum('bhqk,bhkd->bhqd', a.astype(v.dtype), v)

    o = lax.map(block, qb)
    return o.transpose(1, 2, 0, 3, 4).reshape(B, H, Lq, A_VDIM)


def pool_mixer(xb, w, b, scale):
    B, L, _ = xb.shape
    G = len(POOL_WINDOWS)
    xf = xb.astype(F32).reshape(B, L, G, POOL_GROUP)
    csum = jnp.concatenate([jnp.zeros_like(xf[:, :1]), jnp.cumsum(xf, axis=1)], axis=1)
    t = jnp.arange(L)
    means = []
    for g, win in enumerate(POOL_WINDOWS):
        lo = jnp.clip(t - win // 2, 0, L)
        hi = jnp.clip(t + win // 2, 0, L)
        cnt = (hi - lo).astype(F32)[None, :, None]
        cg = csum[:, :, g]
        means.append((cg[:, hi] - cg[:, lo]) / cnt)
    pooled = (jnp.stack(means, axis=2) - xf).astype(xb.dtype)
    y = jnp.einsum('blgc,gcd->blgd', pooled, w) + b
    return y.reshape(B, L, POOL_WIDTH) * scale


def centred_depthwise_conv(x, w):
    K = w.shape[0]
    return lax.conv_general_dilated(x, w[:, None, :].astype(x.dtype), window_strides=(1,),
                                    padding=[(K // 2, K // 2)],
                                    dimension_numbers=('NWC', 'WIO', 'NWC'),
                                    feature_group_count=x.shape[-1])


def gated_delta_chunked(q, k, v, g, beta, S0):
    B, H, L, _ = q.shape
    DV = v.shape[-1]
    q, k, v, g, beta = (to_chunks(a, C_CHUNK) for a in (q, k, v, g, beta))
    incl = jnp.tril(jnp.ones((C_CHUNK, C_CHUNK), bool))
    strict = jnp.tril(jnp.ones((C_CHUNK, C_CHUNK), bool), -1)
    gc = jnp.cumsum(g, axis=-1)
    decay = jnp.exp(jnp.where(incl, gc[..., :, None] - gc[..., None, :], -jnp.inf))
    kb = k * beta[..., None]
    vb = v * beta[..., None]
    kk = jnp.einsum('bhnid,bhnjd->bhnij', kb, k) * decay
    m_tri = jnp.eye(C_CHUNK, dtype=F32) + jnp.where(strict, kk, 0.0)
    rhs = jnp.concatenate([vb, kb * jnp.exp(gc)[..., None]], axis=-1)
    sol = lax.linalg.triangular_solve(m_tri, rhs, left_side=True, lower=True, unit_diagonal=True)
    u, w = sol[..., :DV], sol[..., DV:]
    qk = jnp.einsum('bhnid,bhnjd->bhnij', q, k) * decay
    qg = q * jnp.exp(gc)[..., None]
    kd = k * jnp.exp(gc[..., -1:] - gc)[..., None]
    g_last = jnp.exp(gc[..., -1])

    def step(S, xs):
        u_c, w_c, qk_c, qg_c, kd_c, gl_c = xs
        v_new = u_c - jnp.einsum('bhcd,bhde->bhce', w_c, S)
        o = jnp.einsum('bhcd,bhde->bhce', qg_c, S) + jnp.einsum('bhij,bhje->bhie', qk_c, v_new)
        S = S * gl_c[..., None, None] + jnp.einsum('bhcd,bhce->bhde', kd_c, v_new)
        return S, o

    xs = tuple(jnp.moveaxis(a, 2, 0) for a in (u, w, qk, qg, kd, g_last))
    S, o = lax.scan(step, S0, xs)
    return jnp.moveaxis(o, 0, 2).reshape(B, H, L, DV), S


def mlstm_chunked(q, k, v, ig, fg, C0, n0, m0):
    B, H, L, _ = q.shape
    q, k, v, ig, fg = (to_chunks(a, M_CHUNK) for a in (q, k, v, ig, fg))
    incl = jnp.tril(jnp.ones((M_CHUNK, M_CHUNK), bool))
    b = jnp.cumsum(jax.nn.log_sigmoid(fg), axis=-1)
    dmat = jnp.where(incl, b[..., :, None] - b[..., None, :] + ig[..., None, :], -jnp.inf)
    m_intra = jnp.max(dmat, axis=-1)
    e = b[..., -1:] - b + ig
    m_end = jnp.max(e, axis=-1)
    qk = jnp.einsum('bhnid,bhnjd->bhnij', q, k)

    def step(carry, xs):
        C, n, m = carry
        q_c, k_c, v_c, b_c, d_c, mi_c, qk_c, e_c, me_c = xs
        m_t = jnp.maximum(b_c + m[..., None], mi_c)
        w_inter = jnp.exp(b_c + m[..., None] - m_t)
        p = jnp.exp(d_c - m_t[..., None]) * qk_c
        num = w_inter[..., None] * jnp.einsum('bhcd,bhde->bhce', q_c, C) + jnp.einsum('bhij,bhje->bhie', p, v_c)
        den = w_inter * jnp.einsum('bhcd,bhd->bhc', q_c, n) + jnp.sum(p, axis=-1)
        h = num / jnp.maximum(jnp.abs(den), jnp.exp(-m_t))[..., None]
        m_new = jnp.maximum(b_c[..., -1] + m, me_c)
        carry_decay = jnp.exp(b_c[..., -1] + m - m_new)
        kw = k_c * jnp.exp(e_c - m_new[..., None])[..., None]
        C = carry_decay[..., None, None] * C + jnp.einsum('bhcd,bhce->bhde', kw, v_c)
        n = carry_decay[..., None] * n + jnp.sum(kw, axis=2)
        return (C, n, m_new), h

    xs = tuple(jnp.moveaxis(a, 2, 0) for a in (q, k, v, b, dmat, m_intra, qk, e, m_end))
    (C, n, m), h = lax.scan(step, (C0, n0, m0), xs)
    return jnp.moveaxis(h, 0, 2).reshape(B, H, L, -1), C, n, m


def token_mixers(h, lidx, lp, ctx):
    B, L, _ = h.shape
    (aq, ak, av, p_in, cq, ck, cv, c_gate, c_beta, c_dec,
     mq, mk, mv, m_og, m_ig, m_fg) = split_cols(h @ lp['w_in'], IN_SPLITS)

    aq = aq.reshape(B, L, A_HEADS, 2, A_DIM).transpose(0, 2, 1, 3, 4)
    ak = ak.reshape(B, L, A_HEADS, 2, A_DIM).transpose(0, 2, 1, 3, 4)
    av = split_heads(av, A_HEADS)
    if ctx is None:
        keys, vals = ak, av
    else:
        cos, sin = axial_rope_tables(L)
        aq = apply_axial_rope(aq, cos, sin)
        ak = apply_axial_rope(ak, cos, sin)
        k_ctx = ctx['k'].astype(ak.dtype).reshape(B, A_HEADS, -1, 2, A_DIM)
        keys = jnp.concatenate([k_ctx, ak], axis=2)
        vals = jnp.concatenate([ctx['v'].astype(av.dtype), av], axis=2)
    lam_init = 0.8 - 0.6 * math.exp(-0.3 * lidx)
    lq1, lk1, lq2, lk2 = lp['attn_lambda'].astype(F32)
    lam = jnp.exp(jnp.sum(lq1 * lk1)) - jnp.exp(jnp.sum(lq2 * lk2)) + lam_init
    ao = rms_norm(diff_attention(aq, keys, vals, lam), lp['attn_norm']) * (1.0 - lam_init)
    branch_a = merge_heads(ao)

    branch_b = pool_mixer(p_in, lp['pool_w'], lp['pool_b'], lp['pool_scale'])

    cqkv = jax.nn.silu(centred_depthwise_conv(jnp.concatenate([cq, ck, cv], axis=-1), lp['gdn_conv']))
    cq, ck, cv = jnp.split(cqkv, 3, axis=-1)
    cq = l2_normalize(split_heads(cq, C_HEADS).astype(F32)) * (C_DK ** -0.5)
    ck = l2_normalize(split_heads(ck, C_HEADS).astype(F32))
    cv = split_heads(cv, C_HEADS).astype(F32)
    beta = jax.nn.sigmoid(c_beta.astype(F32)).reshape(B, L, 2, C_HEADS).transpose(2, 0, 3, 1)
    dec = c_dec.astype(F32).reshape(B, L, 2, C_HEADS)
    g = (-jnp.exp(lp['gdn_A_log'].astype(F32))
         * jax.nn.softplus(dec + lp['gdn_dt_bias'].astype(F32))).transpose(2, 0, 3, 1)
    if ctx is None:
        S0 = jnp.zeros((2, B, C_HEADS, C_DK, C_DV), F32)
        mC0 = jnp.zeros((2, B, M_HEADS, M_DK, M_DV), F32)
        mn0 = jnp.zeros((2, B, M_HEADS, M_DK), F32)
        mm0 = jnp.zeros((2, B, M_HEADS), F32)
    else:
        S0 = jnp.moveaxis(ctx['gdn'].astype(F32), 1, 0)
        mC0 = jnp.moveaxis(ctx['mC'].astype(F32), 1, 0)
        mn0 = jnp.moveaxis(ctx['mn'].astype(F32), 1, 0)
        mm0 = jnp.moveaxis(ctx['mm'].astype(F32), 1, 0)
    o_f, S_f = gated_delta_chunked(cq, ck, cv, g[0], beta[0], S0[0])
    o_b, S_b = gated_delta_chunked(flip_seq(cq), flip_seq(ck), flip_seq(cv),
                                   flip_seq(g[1]), flip_seq(beta[1]), S0[1])
    co = rms_norm(o_f + flip_seq(o_b), lp['gdn_norm']) * jax.nn.silu(split_heads(c_gate, C_HEADS).astype(F32))
    branch_c = merge_heads(co).astype(h.dtype)

    mq = split_heads(mq, M_HEADS).astype(F32)
    mk = split_heads(mk, M_HEADS).astype(F32) * (M_DK ** -0.5)
    mv = split_heads(mv, M_HEADS).astype(F32)
    ig = (m_ig.astype(F32).reshape(B, L, 2, M_HEADS) + lp['mlstm_bias_i'].astype(F32)).transpose(2, 0, 3, 1)
    fg = (m_fg.astype(F32).reshape(B, L, 2, M_HEADS) + lp['mlstm_bias_f'].astype(F32)).transpose(2, 0, 3, 1)
    h_f, C_f, n_f, m_f = mlstm_chunked(mq, mk, mv, ig[0], fg[0], mC0[0], mn0[0], mm0[0])
    h_b, C_b, n_b, m_b = mlstm_chunked(flip_seq(mq), flip_seq(mk), flip_seq(mv),
                                       flip_seq(ig[1]), flip_seq(fg[1]), mC0[1], mn0[1], mm0[1])
    mo = rms_norm(h_f + flip_seq(h_b), lp['mlstm_norm']) * jax.nn.sigmoid(split_heads(m_og, M_HEADS).astype(F32))
    branch_d = merge_heads(mo).astype(h.dtype)

    ctx_out = None
    if ctx is None:
        ctx_out = {
            'k': ak.reshape(B, A_HEADS, L, 2 * A_DIM),
            'v': av,
            'gdn': jnp.stack([S_f, S_b], axis=1),
            'mC': jnp.stack([C_f, C_b], axis=1),
            'mn': jnp.stack([n_f, n_b], axis=1),
            'mm': jnp.stack([m_f, m_b], axis=1),
        }
    return [branch_a, branch_b, branch_c, branch_d], ctx_out


def moe_ffn(x, router_w, router_b, w_gu, b_gu, w_dn, b_dn):
    B, L, D = x.shape
    xt = x.reshape(-1, D)
    N = xt.shape[0]
    NK = N * TOP_K
    logits = (xt @ router_w + router_b).astype(F32)
    top_val, top_idx = lax.top_k(logits, TOP_K)
    gate = jax.nn.softmax(top_val, axis=-1)
    flat_e = top_idx.reshape(-1)
    order = jnp.argsort(flat_e)
    sorted_e = flat_e[order]
    counts = jnp.bincount(flat_e, length=N_EXPERTS)
    padded = (counts + MOE_BLOCK - 1) // MOE_BLOCK * MOE_BLOCK
    pad_end = jnp.cumsum(padded)
    pad_start = pad_end - padded
    start = jnp.cumsum(counts) - counts
    dest = pad_start[sorted_e] + jnp.arange(NK) - start[sorted_e]
    n_blocks = (NK + N_EXPERTS * (MOE_BLOCK - 1) + MOE_BLOCK - 1) // MOE_BLOCK
    P = n_blocks * MOE_BLOCK
    row_tok = jnp.full((P,), N, jnp.int32).at[dest].set((order // TOP_K).astype(jnp.int32))
    row_gate = jnp.zeros((P,), F32).at[dest].set(gate.reshape(-1)[order])
    block_e = jnp.minimum(jnp.searchsorted(pad_end, jnp.arange(n_blocks) * MOE_BLOCK, side='right'),
                          N_EXPERTS - 1)
    x_rows = jnp.concatenate([xt, jnp.zeros((1, D), xt.dtype)], axis=0)[row_tok]
    x_rows = x_rows.reshape(n_blocks, MOE_BLOCK, D)

    def expert_block(args):
        xb, e = args
        hg, hu = jnp.split(xb @ w_gu[e] + b_gu[e], 2, axis=-1)
        hg = jnp.minimum(hg, SWIGLU_LIMIT)
        hu = jnp.clip(hu, -SWIGLU_LIMIT, SWIGLU_LIMIT)
        act = (hu + 1.0) * (hg * jax.nn.sigmoid(SWIGLU_ALPHA * hg))
        return act @ w_dn[e] + b_dn[e]

    y_rows = lax.map(expert_block, (x_rows, block_e)).reshape(P, D)
    y = jnp.zeros((N + 1, D), F32).at[row_tok].add(y_rows.astype(F32) * row_gate[:, None])[:N]
    return y.astype(x.dtype).reshape(B, L, D)


def trunk_layer(x, cond, lidx, lp, ctx):
    mod = (jax.nn.silu(cond) @ lp['ada_w'] + lp['ada_b']).reshape(-1, 1, 6, D_MODEL)
    shift1, scale1, gate1, shift2, scale2, gate2 = [mod[:, :, i] for i in range(6)]
    h = rms_norm(x, lp['norm_g'][0]) * (1.0 + scale1) + shift1
    branches, ctx_out = token_mixers(h, lidx, lp, ctx)
    gates = jax.nn.sigmoid(h @ lp['w_bgate'] + lp['b_bgate'])
    merged = 0.0
    for i, br in enumerate(branches):
        merged = merged + gates[..., i * D_MODEL:(i + 1) * D_MODEL] * (br @ lp['w_branch'][i])
    x = x + gate1 * rms_norm(merged @ lp['w_out'], lp['norm_g'][1])
    h = rms_norm(x, lp['norm_g'][2]) * (1.0 + scale2) + shift2
    y = moe_ffn(h, lp['router_w'], lp['router_b'], lp['moe_w_gu'], lp['moe_b_gu'], lp['moe_w_dn'], lp['moe_b_dn'])
    x = x + gate2 * rms_norm(y, lp['norm_g'][3])
    return x, ctx_out


def setup_inputs(seed: int = 0) -> dict:
    key = jax.random.key(seed)
    ks = iter(jax.random.split(key, 48))

    def nrm(shape, scale):
        return jax.random.normal(next(ks), shape, F32) * scale

    G = len(POOL_WINDOWS)
    dt = jnp.exp(jax.random.uniform(next(ks), (DEPTH, 2, C_HEADS), F32, math.log(1e-3), math.log(1e-1)))
    return {
        'x_prompt': nrm((BATCH, SEQ, D_MODEL), 1.0),
        'x_sample': nrm((DEC_BATCH, DEC_SEQ, D_MODEL), 1.0),
        'cache_attn_k': nrm((DEC_BATCH, DEPTH, A_HEADS, PAST_LEN, 2 * A_DIM), 1.0),
        'cache_attn_v': nrm((DEC_BATCH, DEPTH, A_HEADS, PAST_LEN, A_VDIM), 1.0),
        'state_gdn': nrm((DEC_BATCH, DEPTH, 2, C_HEADS, C_DK, C_DV), 0.1),
        'state_mlstm_C': nrm((DEC_BATCH, DEPTH, 2, M_HEADS, M_DK, M_DV), 0.1),
        'state_mlstm_n': nrm((DEC_BATCH, DEPTH, 2, M_HEADS, M_DK), 0.1),
        'state_mlstm_m': nrm((DEC_BATCH, DEPTH, 2, M_HEADS), 1.0),
        'c': nrm((DEC_BATCH, D_MODEL), 1.0),
        'c_ctx': nrm((D_MODEL,), 1.0),
        'ada_w': nrm((DEPTH, D_MODEL, 6 * D_MODEL), 0.5 * D_MODEL ** -0.5),
        'ada_b': nrm((DEPTH, 6 * D_MODEL), 0.02),
        'norm_g': 1.0 + nrm((DEPTH, 4, D_MODEL), 0.02),
        'w_in': nrm((DEPTH, D_MODEL, D_IN), D_MODEL ** -0.5),
        'w_bgate': nrm((DEPTH, D_MODEL, N_BRANCH * D_MODEL), D_MODEL ** -0.5),
        'b_bgate': nrm((DEPTH, N_BRANCH * D_MODEL), 0.02),
        'w_branch': nrm((DEPTH, N_BRANCH, BRANCH_WIDTH, D_MODEL), BRANCH_WIDTH ** -0.5),
        'w_out': nrm((DEPTH, D_MODEL, D_MODEL), D_MODEL ** -0.5),
        'attn_lambda': nrm((DEPTH, 4, A_DIM), 0.1),
        'attn_norm': 1.0 + nrm((DEPTH, A_VDIM), 0.02),
        'pool_w': nrm((DEPTH, G, POOL_GROUP, POOL_GROUP), POOL_GROUP ** -0.5),
        'pool_b': nrm((DEPTH, G, POOL_GROUP), 0.02),
        'pool_scale': 0.5 + nrm((DEPTH, POOL_WIDTH), 0.05),
        'gdn_conv': nrm((DEPTH, C_CONV, 3 * C_HEADS * C_DK), C_CONV ** -0.5),
        'gdn_A_log': jnp.log(jax.random.uniform(next(ks), (DEPTH, 2, C_HEADS), F32, 1.0, 16.0)),
        'gdn_dt_bias': dt + jnp.log(-jnp.expm1(-dt)),
        'gdn_norm': 1.0 + nrm((DEPTH, C_DV), 0.02),
        'mlstm_bias_i': nrm((DEPTH, 2, M_HEADS), 0.1),
        'mlstm_bias_f': jnp.linspace(3.0, 6.0, M_HEADS, dtype=F32)[None, None] + nrm((DEPTH, 2, M_HEADS), 0.1),
        'mlstm_norm': 1.0 + nrm((DEPTH, M_DV), 0.02),
        'router_w': nrm((DEPTH, D_MODEL, N_EXPERTS), D_MODEL ** -0.5),
        'router_b': nrm((DEPTH, N_EXPERTS), 0.01),
        'moe_w_gu': nrm((DEPTH, N_EXPERTS, D_MODEL, 2 * D_EXPERT), D_MODEL ** -0.5),
        'moe_b_gu': nrm((DEPTH, N_EXPERTS, 2 * D_EXPERT), 0.02),
        'moe_w_dn': nrm((DEPTH, N_EXPERTS, D_EXPERT, D_MODEL), D_EXPERT ** -0.5),
        'moe_b_dn': nrm((DEPTH, N_EXPERTS, D_MODEL), 0.02),
    }


def reference(x_prompt, x_sample, cache_attn_k, cache_attn_v, state_gdn, state_mlstm_C, state_mlstm_n,
              state_mlstm_m, c, c_ctx, ada_w, ada_b, norm_g, w_in, w_bgate, b_bgate, w_branch, w_out,
              attn_lambda, attn_norm, pool_w, pool_b, pool_scale, gdn_conv, gdn_A_log, gdn_dt_bias, gdn_norm,
              mlstm_bias_i, mlstm_bias_f, mlstm_norm, router_w, router_b, moe_w_gu, moe_b_gu, moe_w_dn, moe_b_dn):
    y_p = x_prompt
    y_s = x_sample
    ctx_states = []
    for l in range(DEPTH):
        lp = {
            'ada_w': ada_w[l], 'ada_b': ada_b[l], 'norm_g': norm_g[l], 'w_in': w_in[l],
            'w_bgate': w_bgate[l], 'b_bgate': b_bgate[l], 'w_branch': w_branch[l], 'w_out': w_out[l],
            'attn_lambda': attn_lambda[l], 'attn_norm': attn_norm[l],
            'pool_w': pool_w[l], 'pool_b': pool_b[l], 'pool_scale': pool_scale[l],
            'gdn_conv': gdn_conv[l], 'gdn_A_log': gdn_A_log[l], 'gdn_dt_bias': gdn_dt_bias[l],
            'gdn_norm': gdn_norm[l], 'mlstm_bias_i': mlstm_bias_i[l], 'mlstm_bias_f': mlstm_bias_f[l],
            'mlstm_norm': mlstm_norm[l], 'router_w': router_w[l], 'router_b': router_b[l],
            'moe_w_gu': moe_w_gu[l], 'moe_b_gu': moe_b_gu[l], 'moe_w_dn': moe_w_dn[l], 'moe_b_dn': moe_b_dn[l],
        }
        y_p, st = trunk_layer(y_p, c_ctx, l, lp, None)
        ctx_states.append(st)
        cache_l = {
            'k': cache_attn_k[:, l], 'v': cache_attn_v[:, l], 'gdn': state_gdn[:, l],
            'mC': state_mlstm_C[:, l], 'mn': state_mlstm_n[:, l], 'mm': state_mlstm_m[:, l],
        }
        y_s, _ = trunk_layer(y_s, c, l, lp, cache_l)
    dt = x_prompt.dtype
    new_attn_k = jnp.stack([s['k'] for s in ctx_states], axis=1).astype(dt)
    new_attn_v = jnp.stack([s['v'] for s in ctx_states], axis=1).astype(dt)
    new_gdn = jnp.stack([s['gdn'] for s in ctx_states], axis=1).astype(dt)
    new_mlstm_C = jnp.stack([s['mC'] for s in ctx_states], axis=1).astype(dt)
    new_mlstm_n = jnp.stack([s['mn'] for s in ctx_states], axis=1).astype(dt)
    new_mlstm_m = jnp.stack([s['mm'] for s in ctx_states], axis=1).astype(dt)
    return (y_p, y_s, new_attn_k, new_attn_v, new_gdn, new_mlstm_C, new_mlstm_n, new_mlstm_m)
```

```python
import functools
import math

import jax
import jax.numpy as jnp
from jax import lax
from jax.experimental import pallas as pl
from jax.experimental.pallas import tpu as pltpu

F32 = jnp.float32
BF16 = jnp.bfloat16

D_MODEL = 1024
DEPTH = 2
GRID_W = 64
EPS = 1e-6
LANE = 128
HEADS = 4
HEAD_DIM = 128
A_DIM = 64
ROPE_BASE = 10000.0
POOL_WINDOWS = (2, 4, 8, 16)
CHUNK = 64
N_BRANCH = 4
BRANCH_WIDTH = 512
N_EXPERTS = 32
TOP_K = 4
D_EXPERT = 1024
SWIGLU_LIMIT = 7.0
SWIGLU_ALPHA = 1.702
MOE_ROWS = 256
VMEM_LIMIT = 56 * 1024 * 1024

COL_AQ, COL_AK, COL_AV, COL_POOL, COL_CQ, COL_CK, COL_CV, COL_CG, COL_MQ, COL_MK, COL_MV, COL_MG, COL_SM = range(13)
N_PROJ = 13 * BRANCH_WIDTH
SM_BETA, SM_DEC, SM_IG, SM_FG = 0, 2, 4, 6


def _params(*sem):
    return pltpu.CompilerParams(dimension_semantics=sem, vmem_limit_bytes=VMEM_LIMIT)


def _dot(a, b):
    return jnp.dot(a.astype(BF16), b.astype(BF16), preferred_element_type=F32)


def _dot_nt(a, b):
    return lax.dot_general(a.astype(BF16), b.astype(BF16), (((1,), (1,)), ((), ())),
                           preferred_element_type=F32)


def _dot_tn(a, b):
    return lax.dot_general(a.astype(BF16), b.astype(BF16), (((0,), (0,)), ((), ())),
                           preferred_element_type=F32)


def _split(a):
    hi = a.astype(BF16)
    lo = (a - hi.astype(F32)).astype(BF16)
    return hi, lo


def _dot3(a, b):
    ah, al = _split(a)
    bh, bl = _split(b)
    d = lambda x, y: jnp.dot(x, y, preferred_element_type=F32)
    return d(ah, bh) + (d(ah, bl) + d(al, bh))


def _dot_mask(mask, b):
    m = jnp.where(mask, 1.0, 0.0).astype(BF16)
    b0 = b.astype(BF16)
    r1 = b - b0.astype(F32)
    b1 = r1.astype(BF16)
    b2 = (r1 - b1.astype(F32)).astype(BF16)
    d = lambda y: jnp.dot(m, y, preferred_element_type=F32)
    return d(b0) + (d(b1) + d(b2))


def _rms(x):
    return x * lax.rsqrt(jnp.mean(x * x, axis=-1, keepdims=True) + EPS)


def _sigmoid(x):
    return 1.0 / (1.0 + jnp.exp(-x))


def _softplus(x):
    return jnp.maximum(x, 0.0) + jnp.log(1.0 + jnp.exp(-jnp.abs(x)))


def _proj_kernel(x_ref, g_ref, sh_ref, sc_ref, w_ref, b_ref, o_ref, h_scr, *, sigmoid):
    @pl.when(pl.program_id(1) == 0)
    def _():
        y = _rms(x_ref[...]) * g_ref[...]
        h_scr[...] = (y * (1.0 + sc_ref[0]) + sh_ref[0]).astype(BF16)

    acc = jnp.dot(h_scr[...], w_ref[...], preferred_element_type=F32) + b_ref[...]
    o_ref[...] = _sigmoid(acc) if sigmoid else acc


def _norm_mod_matmul(x, g, shift, scale, w, bias, *, rows_per_mod, sigmoid, tm, tn):
    T, Dm = x.shape
    N = w.shape[1]
    tm = min(tm, rows_per_mod)
    mod_idx = lambda i, j: ((i * tm) // rows_per_mod, 0, 0)
    return pl.pallas_call(
        functools.partial(_proj_kernel, sigmoid=sigmoid),
        out_shape=jax.ShapeDtypeStruct((T, N), F32),
        grid=(T // tm, N // tn),
        in_specs=[
            pl.BlockSpec((tm, Dm), lambda i, j: (i, 0)),
            pl.BlockSpec((1, Dm), lambda i, j: (0, 0)),
            pl.BlockSpec((1, 1, Dm), mod_idx),
            pl.BlockSpec((1, 1, Dm), mod_idx),
            pl.BlockSpec((Dm, tn), lambda i, j: (0, j)),
            pl.BlockSpec((1, tn), lambda i, j: (0, j)),
        ],
        out_specs=pl.BlockSpec((tm, tn), lambda i, j: (i, j)),
        scratch_shapes=[pltpu.VMEM((tm, Dm), BF16)],
        compiler_params=_params("parallel", "arbitrary"),
        name="proj",
    )(x, g, shift, scale, w, bias)


def _rope(x, cos, sin):
    lane = lax.broadcasted_iota(jnp.int32, x.shape, 1)
    first = (lane % 32) < 16
    partner = jnp.where(first, pltpu.roll(x, LANE - 16, 1), pltpu.roll(x, 16, 1))
    return x * cos + partner * sin


def _attn_kernel(*refs, has_cache, n_ctx, out_scale):
    if has_cache:
        (lam_ref, q_ref, k_ref, v_ref, ck_ref, cv_ref, cosq_ref, sinq_ref, cosk_ref, sink_ref, nrm_ref,
         o_ref, kall, vall) = refs
    else:
        lam_ref, q_ref, k_ref, v_ref, nrm_ref, o_ref, ko_ref, vo_ref, kall, vall = refs

    @pl.when(pl.program_id(2) == 0)
    def _():
        k = k_ref[...]
        v = v_ref[...]
        if has_cache:
            kall[pl.ds(0, n_ctx), :] = ck_ref[...].astype(BF16)
            vall[pl.ds(0, n_ctx), :] = cv_ref[...].astype(BF16)
            k = _rope(k, cosk_ref[...], sink_ref[...])
        else:
            ko_ref[...] = k
            vo_ref[...] = v
        kall[pl.ds(n_ctx, k.shape[0]), :] = k.astype(BF16)
        vall[pl.ds(n_ctx, k.shape[0]), :] = v.astype(BF16)

    q = q_ref[...]
    if has_cache:
        q = _rope(q, cosq_ref[...], sinq_ref[...])
    q = q * (A_DIM ** -0.5)
    lane = lax.broadcasted_iota(jnp.int32, q.shape, 1)
    q1 = jnp.where(lane < A_DIM, q, 0.0)
    q2 = jnp.where(lane >= A_DIM, q, 0.0)
    keys = kall[...]
    lam = lam_ref[0, 0]

    def probs(qm):
        s = _dot_nt(qm, keys)
        p = jnp.exp(s - jnp.max(s, axis=-1, keepdims=True))
        return p, jnp.sum(p, axis=-1, keepdims=True)

    p1, l1 = probs(q1)
    p2, l2 = probs(q2)
    a = p1 * (1.0 / l1) - p2 * (lam / l2)
    o = jnp.dot(a.astype(BF16), vall[...], preferred_element_type=F32)
    o_ref[...] = _rms(o) * nrm_ref[...] * out_scale


def _attention(P, lam, norm, *, B, L, tq, lam_init, cache=None):
    T = B * L
    nq = L // tq
    has_cache = cache is not None
    n_ctx = cache["k"].shape[3] if has_cache else 0
    colq, colk, colv = COL_AQ * HEADS, COL_AK * HEADS, COL_AV * HEADS
    in_specs = [
        pl.BlockSpec(memory_space=pltpu.SMEM),
        pl.BlockSpec((tq, LANE), lambda b, h, i: (b * nq + i, colq + h)),
        pl.BlockSpec((L, LANE), lambda b, h, i: (b, colk + h)),
        pl.BlockSpec((L, LANE), lambda b, h, i: (b, colv + h)),
    ]
    args = [lam, P, P, P]
    if has_cache:
        l = cache["layer"]
        cspec = pl.BlockSpec((None, None, None, n_ctx, LANE), lambda b, h, i: (b, l, h, 0, 0))
        in_specs += [cspec, cspec,
                     pl.BlockSpec((tq, LANE), lambda b, h, i: (i, 0)),
                     pl.BlockSpec((tq, LANE), lambda b, h, i: (i, 0)),
                     pl.BlockSpec((L, LANE), lambda b, h, i: (0, 0)),
                     pl.BlockSpec((L, LANE), lambda b, h, i: (0, 0))]
        args += [cache["k"], cache["v"], cache["cos"], cache["sin"], cache["cos"], cache["sin"]]
    in_specs.append(pl.BlockSpec((1, LANE), lambda b, h, i: (0, 0)))
    args.append(norm)
    out_shape = [jax.ShapeDtypeStruct((T, HEADS * LANE), F32)]
    out_specs = [pl.BlockSpec((tq, LANE), lambda b, h, i: (b * nq + i, h))]
    if not has_cache:
        kv_shape = jax.ShapeDtypeStruct((B, HEADS, L, LANE), F32)
        kv_spec = pl.BlockSpec((None, None, L, LANE), lambda b, h, i: (b, h, 0, 0))
        out_shape += [kv_shape, kv_shape]
        out_specs += [kv_spec, kv_spec]
    return pl.pallas_call(
        functools.partial(_attn_kernel, has_cache=has_cache, n_ctx=n_ctx, out_scale=1.0 - lam_init),
        out_shape=out_shape,
        grid=(B, HEADS, nq),
        in_specs=in_specs,
        out_specs=out_specs,
        scratch_shapes=[pltpu.VMEM((n_ctx + L, LANE), BF16), pltpu.VMEM((n_ctx + L, LANE), BF16)],
        compiler_params=_params("parallel", "parallel", "arbitrary"),
        name="diff_attention",
    )(*args)


POOL_PAD = 16


def _pool_kernel(x_ref, w_ref, b_ref, s_ref, o_ref, pad, *, L):
    zeros = jnp.zeros((POOL_PAD, LANE), F32)
    pad[pl.ds(0, POOL_PAD), :] = zeros
    pad[pl.ds(POOL_PAD + L, POOL_PAD), :] = zeros
    x = x_ref[...]
    pad[pl.ds(POOL_PAD, L), :] = x
    t = lax.broadcasted_iota(jnp.int32, (L, LANE), 0)
    g = pl.program_id(1)
    for gi, win in enumerate(POOL_WINDOWS):
        @pl.when(g == gi)
        def _(win=win):
            half = win // 2
            acc = pad[pl.ds(POOL_PAD - half, L), :]
            for k in range(1 - half, half):
                acc = acc + pad[pl.ds(POOL_PAD + k, L), :]
            cnt = (jnp.minimum(t + half, L) - jnp.maximum(t - half, 0)).astype(F32)
            pooled = acc / cnt - x
            o_ref[...] = (_dot(pooled, w_ref[...]) + b_ref[...]) * s_ref[...]


def _pool(P, w, b, scale, *, B, L):
    T = B * L
    G = len(POOL_WINDOWS)
    return pl.pallas_call(
        functools.partial(_pool_kernel, L=L),
        out_shape=jax.ShapeDtypeStruct((T, G * LANE), F32),
        grid=(B, G),
        in_specs=[
            pl.BlockSpec((L, LANE), lambda bi, g: (bi, COL_POOL * HEADS + g)),
            pl.BlockSpec((None, LANE, LANE), lambda bi, g: (g, 0, 0)),
            pl.BlockSpec((None, 1, LANE), lambda bi, g: (g, 0, 0)),
            pl.BlockSpec((1, LANE), lambda bi, g: (0, g)),
        ],
        out_specs=pl.BlockSpec((L, LANE), lambda bi, g: (bi, g)),
        scratch_shapes=[pltpu.VMEM((L + 2 * POOL_PAD, LANE), F32)],
        compiler_params=_params("parallel", "parallel"),
        name="pool_mixer",
    )(P, w, b, scale)


def _chunk_masks(backward):
    row = lax.broadcasted_iota(jnp.int32, (CHUNK, CHUNK), 0)
    col = lax.broadcasted_iota(jnp.int32, (CHUNK, CHUNK), 1)
    if backward:
        return row <= col, row < col, row >= col
    return row >= col, row > col, row <= col


def _chunk_cumsum(g, incl, incl_t):
    gb = jnp.broadcast_to(g, (CHUNK, CHUNK))
    col_form = _dot_mask(incl, gb)
    row_form = _dot_mask(jnp.ones((CHUNK, CHUNK), jnp.bool_), jnp.where(incl_t, gb, 0.0))
    return col_form, row_form


def _unit_triangular_inverse(n):
    eye = (lax.broadcasted_iota(jnp.int32, (CHUNK, CHUNK), 0)
           == lax.broadcasted_iota(jnp.int32, (CHUNK, CHUNK), 1)).astype(F32)
    inv = eye - n
    power = _dot3(n, n)
    steps = int(math.log2(CHUNK)) - 1
    for s in range(steps):
        inv = inv + _dot3(inv, power)
        if s + 1 < steps:
            power = _dot3(power, power)
    return inv


CONV_PAD = 8


def _gdn_kernel(*refs, L, has_state):
    alog_ref, dtb_ref, q_ref, k_ref, v_ref, gate_ref, sm_ref, cwq_ref, cwk_ref, cwv_ref, nrm_ref = refs[:11]
    refs = refs[11:]
    if has_state:
        s0_ref, o_ref, pad, qs, ks, vs, o_f, o_b, state = refs
    else:
        o_ref, sout_ref, pad, qs, ks, vs, o_f, o_b, state = refs
    h = pl.program_id(1)
    n_chunks = L // CHUNK

    zeros = jnp.zeros((CONV_PAD, LANE), F32)
    pad[pl.ds(0, CONV_PAD), :] = zeros
    pad[pl.ds(CONV_PAD + L, CONV_PAD), :] = zeros

    def conv_silu(x_ref, w_ref):
        pad[pl.ds(CONV_PAD, L), :] = x_ref[...]
        w = w_ref[...]
        y = (pad[pl.ds(CONV_PAD - 1, L), :] * w[0:1] + pad[pl.ds(CONV_PAD, L), :] * w[1:2]
             + pad[pl.ds(CONV_PAD + 1, L), :] * w[2:3])
        return y * _sigmoid(y)

    def l2n(x):
        return x * lax.rsqrt(jnp.sum(x * x, axis=-1, keepdims=True) + EPS)

    qs[...] = l2n(conv_silu(q_ref, cwq_ref)) * (HEAD_DIM ** -0.5)
    ks[...] = l2n(conv_silu(k_ref, cwk_ref))
    vs[...] = conv_silu(v_ref, cwv_ref)
    if has_state:
        state[...] = s0_ref[...]
    else:
        state[...] = jnp.zeros_like(state)

    def chunk(c, d, o_scr):
        rows = pl.ds(pl.multiple_of(c * CHUNK, CHUNK), CHUNK)
        q, k, v, sm = qs[rows, :], ks[rows, :], vs[rows, :], sm_ref[rows, :]
        beta = _sigmoid(sm[:, SM_BETA + d:SM_BETA + d + 1])
        dec = sm[:, SM_DEC + d:SM_DEC + d + 1]
        g = -jnp.exp(alog_ref[d, h]) * _softplus(dec + dtb_ref[d, h])
        incl, strict, incl_t = _chunk_masks(d == 1)
        cum_i, cum_j = _chunk_cumsum(g, incl, incl_t)
        gc = cum_i[:, 0:1]
        total = jnp.sum(g, axis=0, keepdims=True)
        decay = jnp.where(incl, jnp.exp(jnp.where(incl, cum_i - cum_j, 0.0)), 0.0)
        kb = k * beta
        vb = v * beta
        kk = _dot_nt(kb, k) * decay
        inv = _unit_triangular_inverse(jnp.where(strict, kk, 0.0))
        egc = jnp.exp(gc)
        sol = _dot3(inv, jnp.concatenate([vb, kb * egc], axis=-1))
        u, w = sol[:, :HEAD_DIM], sol[:, HEAD_DIM:]
        qk = _dot_nt(q, k) * decay
        S = state[d]
        v_new = u - _dot(w, S)
        o_scr[rows, :] = _dot(q * egc, S) + _dot(qk, v_new)
        state[d] = S * jnp.exp(total) + _dot_tn(k * jnp.exp(total - gc), v_new)

    def body(n, carry):
        chunk(n, 0, o_f)
        chunk(n_chunks - 1 - n, 1, o_b)
        return carry

    lax.fori_loop(0, n_chunks, body, 0)
    gate = gate_ref[...]
    o_ref[...] = _rms(o_f[...] + o_b[...]) * nrm_ref[...] * (gate * _sigmoid(gate))
    if not has_state:
        sout_ref[...] = state[...]


def _col_spec(L, col):
    return pl.BlockSpec((L, LANE), lambda b, h: (b, col * HEADS + h))


def _gdn(P, a_log, dt_bias, conv_w, norm, *, B, L, state=None):
    T = B * L
    has_state = state is not None
    smem = pl.BlockSpec(memory_space=pltpu.SMEM)
    in_specs = [smem, smem, _col_spec(L, COL_CQ), _col_spec(L, COL_CK), _col_spec(L, COL_CV),
                _col_spec(L, COL_CG), _col_spec(L, COL_SM)]
    in_specs += [pl.BlockSpec((3, LANE), lambda b, h, j=j: (0, j * HEADS + h)) for j in range(3)]
    in_specs.append(pl.BlockSpec((1, LANE), lambda b, h: (0, 0)))
    args = [a_log, dt_bias, P, P, P, P, P, conv_w, conv_w, conv_w, norm]
    out_shape = [jax.ShapeDtypeStruct((T, HEADS * LANE), F32)]
    out_specs = [pl.BlockSpec((L, LANE), lambda b, h: (b, h))]
    if has_state:
        arr, l = state
        in_specs.append(pl.BlockSpec((None, None, 2, None, HEAD_DIM, HEAD_DIM), lambda b, h: (b, l, 0, h, 0, 0)))
        args.append(arr)
    else:
        out_shape.append(jax.ShapeDtypeStruct((B, 2, HEADS, HEAD_DIM, HEAD_DIM), F32))
        out_specs.append(pl.BlockSpec((None, 2, None, HEAD_DIM, HEAD_DIM), lambda b, h: (b, 0, h, 0, 0)))
    seq = pltpu.VMEM((L, LANE), F32)
    return pl.pallas_call(
        functools.partial(_gdn_kernel, L=L, has_state=has_state),
        out_shape=out_shape,
        grid=(B, HEADS),
        in_specs=in_specs,
        out_specs=out_specs,
        scratch_shapes=[pltpu.VMEM((L + 2 * CONV_PAD, LANE), F32), seq, seq, seq, seq, seq,
                        pltpu.VMEM((2, HEAD_DIM, HEAD_DIM), F32)],
        compiler_params=_params("parallel", "parallel"),
        name="gated_deltanet",
    )(*args)


def _mlstm_kernel(*refs, L, has_state):
    bi_ref, bf_ref, q_ref, k_ref, v_ref, gate_ref, sm_ref, nrm_ref = refs[:8]
    refs = refs[8:]
    if has_state:
        c0_ref, n0_ref, m0_ref, o_ref, h_f, h_b, c_st, n_st, m_st = refs
    else:
        o_ref, cout_ref, nout_ref, mout_ref, h_f, h_b, c_st, n_st, m_st = refs
    h = pl.program_id(1)
    n_chunks = L // CHUNK

    if has_state:
        c_st[...] = c0_ref[...]
        n_st[...] = n0_ref[...]
        m_st[...] = m0_ref[...]
    else:
        c_st[...] = jnp.zeros_like(c_st)
        n_st[...] = jnp.zeros_like(n_st)
        m_st[...] = jnp.zeros_like(m_st)

    def chunk(c, d, h_scr):
        rows = pl.ds(pl.multiple_of(c * CHUNK, CHUNK), CHUNK)
        q, v, sm = q_ref[rows, :], v_ref[rows, :], sm_ref[rows, :]
        k = k_ref[rows, :] * (HEAD_DIM ** -0.5)
        ig = sm[:, SM_IG + d:SM_IG + d + 1] + bi_ref[d, h]
        fg = sm[:, SM_FG + d:SM_FG + d + 1] + bf_ref[d, h]
        lf = -_softplus(-fg)
        incl, _, incl_t = _chunk_masks(d == 1)
        lfb = jnp.broadcast_to(lf, (CHUNK, CHUNK))
        igb = jnp.broadcast_to(ig, (CHUNK, CHUNK))
        eye = (lax.broadcasted_iota(jnp.int32, (CHUNK, CHUNK), 0)
               == lax.broadcasted_iota(jnp.int32, (CHUNK, CHUNK), 1))
        cum_i = _dot_mask(incl, lfb)
        rows_j = _dot_mask(jnp.ones((CHUNK, CHUNK), jnp.bool_),
                           jnp.where(incl_t, lfb, 0.0) - jnp.where(eye, igb, 0.0))
        dm = cum_i - rows_j
        b = cum_i[:, 0:1]
        total = jnp.sum(lf, axis=0, keepdims=True)
        m_intra = jnp.max(jnp.where(incl, dm, -jnp.inf), axis=-1, keepdims=True)
        e = total - b + ig
        m_end = jnp.max(e, axis=0, keepdims=True)
        qk = _dot_nt(q, k)
        C = c_st[d]
        n = n_st[d]
        m = m_st[d][:, 0:1]
        m_t = jnp.maximum(b + m, m_intra)
        w_inter = jnp.exp(b + m - m_t)
        p = jnp.where(incl, jnp.exp(jnp.where(incl, dm - m_t, 0.0)), 0.0) * qk
        num = w_inter * _dot(q, C) + _dot(p, v)
        den = w_inter * jnp.sum(q * n, axis=-1, keepdims=True) + jnp.sum(p, axis=-1, keepdims=True)
        h_scr[rows, :] = num / jnp.maximum(jnp.abs(den), jnp.exp(-m_t))
        m_new = jnp.maximum(total + m, m_end)
        carry_decay = jnp.exp(total + m - m_new)
        kw = k * jnp.exp(e - m_new)
        c_st[d] = carry_decay * C + _dot_tn(kw, v)
        n_st[d] = carry_decay * n + jnp.sum(kw, axis=0, keepdims=True)
        m_st[d] = jnp.broadcast_to(m_new, (1, LANE))

    def body(i, carry):
        chunk(i, 0, h_f)
        chunk(n_chunks - 1 - i, 1, h_b)
        return carry

    lax.fori_loop(0, n_chunks, body, 0)
    o_ref[...] = _rms(h_f[...] + h_b[...]) * nrm_ref[...] * _sigmoid(gate_ref[...])
    if not has_state:
        cout_ref[...] = c_st[...]
        nout_ref[...] = n_st[...]
        mout_ref[...] = m_st[...]


def _mlstm(P, bias_i, bias_f, norm, *, B, L, state=None):
    T = B * L
    has_state = state is not None
    smem = pl.BlockSpec(memory_space=pltpu.SMEM)
    in_specs = [smem, smem, _col_spec(L, COL_MQ), _col_spec(L, COL_MK), _col_spec(L, COL_MV),
                _col_spec(L, COL_MG), _col_spec(L, COL_SM), pl.BlockSpec((1, LANE), lambda b, h: (0, 0))]
    args = [bias_i, bias_f, P, P, P, P, P, norm]
    out_shape = [jax.ShapeDtypeStruct((T, HEADS * LANE), F32)]
    out_specs = [pl.BlockSpec((L, LANE), lambda b, h: (b, h))]
    vec_spec = pl.BlockSpec((None, None, 2, 1, LANE), lambda b, h: (b, h, 0, 0, 0))
    if has_state:
        c_arr, l, n_arr, m_arr = state
        in_specs += [pl.BlockSpec((None, None, 2, None, HEAD_DIM, HEAD_DIM), lambda b, h: (b, l, 0, h, 0, 0)),
                     vec_spec, vec_spec]
        args += [c_arr, n_arr, m_arr]
    else:
        out_shape += [jax.ShapeDtypeStruct((B, 2, HEADS, HEAD_DIM, HEAD_DIM), F32),
                      jax.ShapeDtypeStruct((B, HEADS, 2, 1, LANE), F32),
                      jax.ShapeDtypeStruct((B, HEADS, 2, 1, LANE), F32)]
        out_specs += [pl.BlockSpec((None, 2, None, HEAD_DIM, HEAD_DIM), lambda b, h: (b, 0, h, 0, 0)),
                      vec_spec, vec_spec]
    seq = pltpu.VMEM((L, LANE), F32)
    return pl.pallas_call(
        functools.partial(_mlstm_kernel, L=L, has_state=has_state),
        out_shape=out_shape,
        grid=(B, HEADS),
        in_specs=in_specs,
        out_specs=out_specs,
        scratch_shapes=[seq, seq, pltpu.VMEM((2, HEAD_DIM, HEAD_DIM), F32), pltpu.VMEM((2, 1, LANE), F32),
                        pltpu.VMEM((2, 1, LANE), F32)],
        compiler_params=_params("parallel", "parallel"),
        name="mlstm",
    )(*args)


def _merge_kernel(x_ref, gates_ref, a_ref, b_ref, c_ref, d_ref, wb_ref, wo_ref, g1_ref, gate1_ref, g2_ref,
                  sh2_ref, sc2_ref, rw_ref, rb_ref, xo_ref, h2_ref, lg_ref):
    merged = None
    for i, br in enumerate((a_ref, b_ref, c_ref, d_ref)):
        term = gates_ref[:, i * D_MODEL:(i + 1) * D_MODEL] * _dot(br[...], wb_ref[i])
        merged = term if merged is None else merged + term
    t = _rms(_dot(merged, wo_ref[...])) * g1_ref[...]
    x = x_ref[...] + gate1_ref[0] * t
    xo_ref[...] = x
    h2 = _rms(x) * g2_ref[...] * (1.0 + sc2_ref[0]) + sh2_ref[0]
    h2_ref[...] = h2.astype(BF16)
    lg_ref[...] = _dot3(h2, rw_ref[...]) + rb_ref[...]


def _merge(x, gates, branches, wb, wo, g1, gate1, g2, shift2, scale2, rw, rb, *, rows_per_mod, tm):
    T = x.shape[0]
    tm = min(tm, rows_per_mod)
    row = lambda n: pl.BlockSpec((tm, n), lambda i: (i, 0))
    const = lambda *shape: pl.BlockSpec(shape, lambda i: (0,) * len(shape))
    mod = pl.BlockSpec((1, 1, D_MODEL), lambda i: ((i * tm) // rows_per_mod, 0, 0))
    return pl.pallas_call(
        _merge_kernel,
        out_shape=[jax.ShapeDtypeStruct((T, D_MODEL), F32), jax.ShapeDtypeStruct((T, D_MODEL), BF16),
                   jax.ShapeDtypeStruct((T, LANE), F32)],
        grid=(T // tm,),
        in_specs=[row(D_MODEL), row(N_BRANCH * D_MODEL)] + [row(BRANCH_WIDTH)] * 4
                 + [const(N_BRANCH, BRANCH_WIDTH, D_MODEL), const(D_MODEL, D_MODEL), const(1, D_MODEL), mod,
                    const(1, D_MODEL), mod, mod, const(D_MODEL, LANE), const(1, LANE)],
        out_specs=[row(D_MODEL), row(D_MODEL), row(LANE)],
        compiler_params=_params("parallel"),
        name="merge",
    )(x, gates, *branches, wb, wo, g1, gate1, g2, shift2, scale2, rw, rb)


def _moe_kernel(be_ref, nu_ref, x_ref, wgu_ref, bgu_ref, wdn_ref, bdn_ref, gate_ref, y_ref):
    i = pl.program_id(0)

    @pl.when(i < nu_ref[0])
    def _():
        hgu = jnp.dot(x_ref[...], wgu_ref[...], preferred_element_type=F32) + bgu_ref[...]
        hg = jnp.minimum(hgu[:, :D_EXPERT], SWIGLU_LIMIT)
        hu = jnp.clip(hgu[:, D_EXPERT:], -SWIGLU_LIMIT, SWIGLU_LIMIT)
        act = (hu + 1.0) * (hg * _sigmoid(SWIGLU_ALPHA * hg))
        y = _dot(act, wdn_ref[...]) + bdn_ref[...]
        y_ref[...] = y * gate_ref[...]

    @pl.when(i >= nu_ref[0])
    def _():
        y_ref[...] = jnp.zeros_like(y_ref)


def _moe_blocks(x_rows, row_gate, block_e, n_used, w_gu, b_gu, w_dn, b_dn):
    P = x_rows.shape[0]
    n_blocks = P // MOE_ROWS
    blk = lambda i, be, nu: jnp.minimum(i, nu[0] - 1)
    exp = lambda i, be, nu: (be[jnp.minimum(i, nu[0] - 1)], 0, 0)
    return pl.pallas_call(
        _moe_kernel,
        out_shape=jax.ShapeDtypeStruct((P, D_MODEL), F32),
        grid_spec=pltpu.PrefetchScalarGridSpec(
            num_scalar_prefetch=2,
            grid=(n_blocks,),
            in_specs=[
                pl.BlockSpec((MOE_ROWS, D_MODEL), lambda i, be, nu: (blk(i, be, nu), 0)),
                pl.BlockSpec((None, D_MODEL, 2 * D_EXPERT), exp),
                pl.BlockSpec((None, 1, 2 * D_EXPERT), exp),
                pl.BlockSpec((None, D_EXPERT, D_MODEL), exp),
                pl.BlockSpec((None, 1, D_MODEL), exp),
                pl.BlockSpec((MOE_ROWS, 1), lambda i, be, nu: (blk(i, be, nu), 0)),
            ],
            out_specs=pl.BlockSpec((MOE_ROWS, D_MODEL), lambda i, be, nu: (i, 0)),
        ),
        compiler_params=_params("arbitrary"),
        name="moe_experts",
    )(block_e, n_used, x_rows, w_gu, b_gu, w_dn, b_dn, row_gate)


def _moe(h2, logits, w_gu, b_gu, w_dn, b_dn):
    N = h2.shape[0]
    NK = N * TOP_K
    top_val, top_idx = lax.top_k(logits, TOP_K)
    gate = jax.nn.softmax(top_val, axis=-1)
    flat_e = top_idx.reshape(-1)
    order = jnp.argsort(flat_e)
    sorted_e = flat_e[order]
    counts = jnp.bincount(flat_e, length=N_EXPERTS)
    padded = (counts + MOE_ROWS - 1) // MOE_ROWS * MOE_ROWS
    pad_end = jnp.cumsum(padded)
    pad_start = pad_end - padded
    start = jnp.cumsum(counts) - counts
    dest = (pad_start[sorted_e] + jnp.arange(NK) - start[sorted_e]).astype(jnp.int32)
    n_blocks = (NK + N_EXPERTS * (MOE_ROWS - 1) + MOE_ROWS - 1) // MOE_ROWS
    P = n_blocks * MOE_ROWS
    row_tok = jnp.full((P,), N, jnp.int32).at[dest].set((order // TOP_K).astype(jnp.int32))
    row_gate = jnp.zeros((P,), F32).at[dest].set(gate.reshape(-1)[order])
    block_e = jnp.minimum(jnp.searchsorted(pad_end, jnp.arange(n_blocks) * MOE_ROWS, side='right'),
                          N_EXPERTS - 1).astype(jnp.int32)
    n_used = (pad_end[-1] // MOE_ROWS).astype(jnp.int32).reshape(1)
    x_rows = jnp.concatenate([h2, jnp.zeros((1, D_MODEL), h2.dtype)], axis=0)[row_tok]
    y_rows = _moe_blocks(x_rows, row_gate[:, None], block_e, n_used, w_gu, b_gu, w_dn, b_dn)
    slot = jnp.zeros((NK,), jnp.int32).at[order].set(dest).reshape(N, TOP_K)
    y = y_rows[slot[:, 0]]
    for kk in range(1, TOP_K):
        y = y + y_rows[slot[:, kk]]
    return y


def _residual_kernel(x_ref, y_ref, g_ref, gate_ref, o_ref):
    o_ref[...] = x_ref[...] + gate_ref[0] * (_rms(y_ref[...]) * g_ref[...])


def _residual(x, y, g, gate, *, rows_per_mod, tm):
    T = x.shape[0]
    tm = min(tm, rows_per_mod)
    row = pl.BlockSpec((tm, D_MODEL), lambda i: (i, 0))
    return pl.pallas_call(
        _residual_kernel,
        out_shape=jax.ShapeDtypeStruct((T, D_MODEL), F32),
        grid=(T // tm,),
        in_specs=[row, row, pl.BlockSpec((1, D_MODEL), lambda i: (0, 0)),
                  pl.BlockSpec((1, 1, D_MODEL), lambda i: ((i * tm) // rows_per_mod, 0, 0))],
        out_specs=row,
        compiler_params=_params("parallel"),
        name="residual",
    )(x, y, g, gate)


def _rope_tables(L):
    t = jnp.arange(L)
    lane = jnp.arange(LANE)
    axis = (lane % A_DIM) // 32
    half = (lane % 32) // 16
    n_freq = A_DIM // 4
    inv = ROPE_BASE ** (-(lane % n_freq).astype(F32) / n_freq)
    pos = jnp.where(axis[None, :] == 0, (t // GRID_W)[:, None], (t % GRID_W)[:, None]).astype(F32)
    ang = pos * inv[None, :]
    return jnp.cos(ang), jnp.where(half[None, :] == 0, -jnp.sin(ang), jnp.sin(ang))


def _arrange_w_in(w_in):
    sizes = [512] * 8 + [8, 8] + [512] * 4 + [8, 8]
    offs = [0]
    for s in sizes:
        offs.append(offs[-1] + s)
    part = lambda i: w_in[:, offs[i]:offs[i + 1]]
    big = [part(i) for i in (0, 1, 2, 3, 4, 5, 6, 7, 10, 11, 12, 13)]
    small = jnp.zeros((w_in.shape[0], HEADS, LANE), w_in.dtype)
    for base, i in ((SM_BETA, 8), (SM_DEC, 9), (SM_IG, 14), (SM_FG, 15)):
        cols = part(i).reshape(-1, 2, HEADS)
        for d in range(2):
            small = small.at[:, :, base + d].set(cols[:, d, :])
    return jnp.concatenate(big + [small.reshape(w_in.shape[0], HEADS * LANE)], axis=1)


def _layer(x, cond, lidx, lp, *, B, L, cache):
    T = B * L
    is_ctx = cache is None
    mod = (jax.nn.silu(cond) @ lp['ada_w'] + lp['ada_b']).reshape(-1, 6, 1, D_MODEL)
    shift1, scale1, gate1, shift2, scale2, gate2 = [mod[:, i] for i in range(6)]
    rows_per_mod = T if mod.shape[0] == 1 else L
    g = lp['norm_g']

    tm = 1024
    P = _norm_mod_matmul(x, g[0:1], shift1, scale1, lp['w_in'], jnp.zeros((1, N_PROJ), F32),
                         rows_per_mod=rows_per_mod, sigmoid=False, tm=tm, tn=N_PROJ // 4)
    gates = _norm_mod_matmul(x, g[0:1], shift1, scale1, lp['w_bgate'], lp['b_bgate'][None],
                             rows_per_mod=rows_per_mod, sigmoid=True, tm=tm, tn=2048)

    lam_init = 0.8 - 0.6 * math.exp(-0.3 * lidx)
    lq1, lk1, lq2, lk2 = lp['attn_lambda']
    lam = (jnp.exp(jnp.sum(lq1 * lk1)) - jnp.exp(jnp.sum(lq2 * lk2)) + lam_init).reshape(1, 1)
    attn_norm = lp['attn_norm'][None]
    if is_ctx:
        br_a, new_k, new_v = _attention(P, lam, attn_norm, B=B, L=L, tq=L, lam_init=lam_init)
    else:
        (br_a,) = _attention(P, lam, attn_norm, B=B, L=L, tq=128, lam_init=lam_init, cache=cache['attn'])

    br_b = _pool(P, lp['pool_w'], lp['pool_b'][:, None], lp['pool_scale'][None], B=B, L=L)

    gdn_args = (P, lp['gdn_A_log'], lp['gdn_dt_bias'], lp['gdn_conv'], lp['gdn_norm'][None])
    ml_args = (P, lp['mlstm_bias_i'], lp['mlstm_bias_f'], lp['mlstm_norm'][None])
    if is_ctx:
        br_c, new_gdn = _gdn(*gdn_args, B=B, L=L)
        br_d, new_c, new_n, new_m = _mlstm(*ml_args, B=B, L=L)
        ctx_out = (new_k, new_v, new_gdn, new_c,
                   new_n[:, :, :, 0, :HEAD_DIM].transpose(0, 2, 1, 3), new_m[:, :, :, 0, 0].transpose(0, 2, 1))
    else:
        (br_c,) = _gdn(*gdn_args, B=B, L=L, state=cache['gdn'])
        (br_d,) = _mlstm(*ml_args, B=B, L=L, state=cache['mlstm'])
        ctx_out = None

    x, h2, logits = _merge(x, gates, (br_a, br_b, br_c, br_d), lp['w_branch'], lp['w_out'], g[1:2], gate1,
                           g[2:3], shift2, scale2, lp['router_w'], lp['router_b'],
                           rows_per_mod=rows_per_mod, tm=256)
    y = _moe(h2, logits[:, :N_EXPERTS], lp['moe_w_gu'], lp['moe_b_gu'], lp['moe_w_dn'], lp['moe_b_dn'])
    x = _residual(x, y, g[3:4], gate2, rows_per_mod=rows_per_mod, tm=512)
    return x, ctx_out


def kernel(x_prompt, x_sample, cache_attn_k, cache_attn_v, state_gdn, state_mlstm_C, state_mlstm_n, state_mlstm_m, c, c_ctx, ada_w, ada_b, norm_g, w_in, w_bgate, b_bgate, w_branch, w_out, attn_lambda, attn_norm, pool_w, pool_b, pool_scale, gdn_conv, gdn_A_log, gdn_dt_bias, gdn_norm, mlstm_bias_i, mlstm_bias_f, mlstm_norm, router_w, router_b, moe_w_gu, moe_b_gu, moe_w_dn, moe_b_dn):
    Bp, Lp, _ = x_prompt.shape
    Bs, Ls, _ = x_sample.shape
    y_p = x_prompt.reshape(Bp * Lp, D_MODEL)
    y_s = x_sample.reshape(Bs * Ls, D_MODEL)
    cos, sin = _rope_tables(Ls)
    n0 = state_mlstm_n.transpose(1, 0, 3, 2, 4)[:, :, :, :, None, :]
    m0 = jnp.broadcast_to(state_mlstm_m.transpose(1, 0, 3, 2)[..., None, None], n0.shape)
    ctx_states = []
    for l in range(DEPTH):
        lp = {
            'ada_w': ada_w[l], 'ada_b': ada_b[l], 'norm_g': norm_g[l],
            'w_in': _arrange_w_in(w_in[l]).astype(BF16), 'w_bgate': w_bgate[l].astype(BF16),
            'b_bgate': b_bgate[l], 'w_branch': w_branch[l].astype(BF16), 'w_out': w_out[l].astype(BF16),
            'attn_lambda': attn_lambda[l], 'attn_norm': attn_norm[l],
            'pool_w': pool_w[l], 'pool_b': pool_b[l], 'pool_scale': pool_scale[l],
            'gdn_conv': gdn_conv[l], 'gdn_A_log': gdn_A_log[l], 'gdn_dt_bias': gdn_dt_bias[l],
            'gdn_norm': gdn_norm[l], 'mlstm_bias_i': mlstm_bias_i[l], 'mlstm_bias_f': mlstm_bias_f[l],
            'mlstm_norm': mlstm_norm[l],
            'router_w': jnp.pad(router_w[l], ((0, 0), (0, LANE - N_EXPERTS))),
            'router_b': jnp.pad(router_b[l], (0, LANE - N_EXPERTS))[None],
            'moe_w_gu': moe_w_gu[l].astype(BF16), 'moe_b_gu': moe_b_gu[l][:, None],
            'moe_w_dn': moe_w_dn[l].astype(BF16), 'moe_b_dn': moe_b_dn[l][:, None],
        }
        y_p, st = _layer(y_p, c_ctx[None], l, lp, B=Bp, L=Lp, cache=None)
        ctx_states.append(st)
        cache = {
            'attn': {'k': cache_attn_k, 'v': cache_attn_v, 'layer': l, 'cos': cos, 'sin': sin},
            'gdn': (state_gdn, l),
            'mlstm': (state_mlstm_C, l, n0[l], m0[l]),
        }
        y_s, _ = _layer(y_s, c, l, lp, B=Bs, L=Ls, cache=cache)
    outs = [jnp.stack([s[i] for s in ctx_states], axis=1) for i in range(6)]
    return (y_p.reshape(Bp, Lp, D_MODEL), y_s.reshape(Bs, Ls, D_MODEL), *outs)
```

```python
import functools
import math

import jax
import jax.numpy as jnp
from jax import lax
from jax.experimental import pallas as pl
from jax.experimental.pallas import tpu as pltpu

F32 = jnp.float32
BF16 = jnp.bfloat16

D_MODEL = 1024
DEPTH = 2
GRID_W = 64
EPS = 1e-6
LANE = 128
HEADS = 4
HEAD_DIM = 128
A_DIM = 64
ROPE_BASE = 10000.0
POOL_WINDOWS = (2, 4, 8, 16)
CHUNK = 64
N_BRANCH = 4
BRANCH_WIDTH = 512
N_EXPERTS = 32
TOP_K = 4
D_EXPERT = 1024
SWIGLU_LIMIT = 7.0
SWIGLU_ALPHA = 1.702
MOE_ROWS = 256
VMEM_LIMIT = 56 * 1024 * 1024

COL_AQ, COL_AK, COL_AV, COL_POOL, COL_CQ, COL_CK, COL_CV, COL_CG, COL_MQ, COL_MK, COL_MV, COL_MG, COL_SM = range(13)
N_PROJ = 13 * BRANCH_WIDTH
SM_BETA, SM_DEC, SM_IG, SM_FG = 0, 2, 4, 6


def _params(*sem):
    return pltpu.CompilerParams(dimension_semantics=sem, vmem_limit_bytes=VMEM_LIMIT)


def _dot(a, b):
    return jnp.dot(a.astype(BF16), b.astype(BF16), preferred_element_type=F32)


def _dot_nt(a, b):
    return lax.dot_general(a.astype(BF16), b.astype(BF16), (((1,), (1,)), ((), ())),
                           preferred_element_type=F32)


def _dot_tn(a, b):
    return lax.dot_general(a.astype(BF16), b.astype(BF16), (((0,), (0,)), ((), ())),
                           preferred_element_type=F32)


def _split(a):
    hi = a.astype(BF16)
    lo = (a - hi.astype(F32)).astype(BF16)
    return hi, lo


def _dot3(a, b):
    ah, al = _split(a)
    bh, bl = _split(b)
    d = lambda x, y: jnp.dot(x, y, preferred_element_type=F32)
    return d(ah, bh) + (d(ah, bl) + d(al, bh))


def _dot_mask(mask, b):
    m = jnp.where(mask, 1.0, 0.0).astype(BF16)
    b0 = b.astype(BF16)
    r1 = b - b0.astype(F32)
    b1 = r1.astype(BF16)
    b2 = (r1 - b1.astype(F32)).astype(BF16)
    d = lambda y: jnp.dot(m, y, preferred_element_type=F32)
    return d(b0) + (d(b1) + d(b2))


def _rms(x):
    return x * lax.rsqrt(jnp.mean(x * x, axis=-1, keepdims=True) + EPS)


def _sigmoid(x):
    return 1.0 / (1.0 + jnp.exp(-x))


def _softplus(x):
    return jnp.maximum(x, 0.0) + jnp.log(1.0 + jnp.exp(-jnp.abs(x)))


def _proj_kernel(x_ref, g_ref, sh_ref, sc_ref, w_ref, b_ref, o_ref, h_scr, *, sigmoid):
    @pl.when(pl.program_id(1) == 0)
    def _():
        y = _rms(x_ref[...]) * g_ref[...]
        h_scr[...] = (y * (1.0 + sc_ref[0]) + sh_ref[0]).astype(BF16)

    acc = jnp.dot(h_scr[...], w_ref[...], preferred_element_type=F32) + b_ref[...]
    o_ref[...] = _sigmoid(acc) if sigmoid else acc


def _norm_mod_matmul(x, g, shift, scale, w, bias, *, rows_per_mod, sigmoid, tm, tn):
    T, Dm = x.shape
    N = w.shape[1]
    tm = min(tm, rows_per_mod)
    mod_idx = lambda i, j: ((i * tm) // rows_per_mod, 0, 0)
    return pl.pallas_call(
        functools.partial(_proj_kernel, sigmoid=sigmoid),
        out_shape=jax.ShapeDtypeStruct((T, N), F32),
        grid=(T // tm, N // tn),
        in_specs=[
            pl.BlockSpec((tm, Dm), lambda i, j: (i, 0)),
            pl.BlockSpec((1, Dm), lambda i, j: (0, 0)),
            pl.BlockSpec((1, 1, Dm), mod_idx),
            pl.BlockSpec((1, 1, Dm), mod_idx),
            pl.BlockSpec((Dm, tn), lambda i, j: (0, j)),
            pl.BlockSpec((1, tn), lambda i, j: (0, j)),
        ],
        out_specs=pl.BlockSpec((tm, tn), lambda i, j: (i, j)),
        scratch_shapes=[pltpu.VMEM((tm, Dm), BF16)],
        compiler_params=_params("parallel", "arbitrary"),
        name="proj",
    )(x, g, shift, scale, w, bias)


def _rope(x, cos, sin):
    lane = lax.broadcasted_iota(jnp.int32, x.shape, 1)
    first = (lane % 32) < 16
    partner = jnp.where(first, pltpu.roll(x, LANE - 16, 1), pltpu.roll(x, 16, 1))
    return x * cos + partner * sin


def _attn_kernel(*refs, has_cache, n_ctx, out_scale):
    if has_cache:
        (lam_ref, q_ref, k_ref, v_ref, ck_ref, cv_ref, cosq_ref, sinq_ref, cosk_ref, sink_ref, nrm_ref,
         o_ref, kall, vall) = refs
    else:
        lam_ref, q_ref, k_ref, v_ref, nrm_ref, o_ref, ko_ref, vo_ref, kall, vall = refs

    @pl.when(pl.program_id(2) == 0)
    def _():
        k = k_ref[...]
        v = v_ref[...]
        if has_cache:
            kall[pl.ds(0, n_ctx), :] = ck_ref[...].astype(BF16)
            vall[pl.ds(0, n_ctx), :] = cv_ref[...].astype(BF16)
            k = _rope(k, cosk_ref[...], sink_ref[...])
        else:
            ko_ref[...] = k
            vo_ref[...] = v
        kall[pl.ds(n_ctx, k.shape[0]), :] = k.astype(BF16)
        vall[pl.ds(n_ctx, k.shape[0]), :] = v.astype(BF16)

    q = q_ref[...]
    if has_cache:
        q = _rope(q, cosq_ref[...], sinq_ref[...])
    q = q * (A_DIM ** -0.5)
    lane = lax.broadcasted_iota(jnp.int32, q.shape, 1)
    q1 = jnp.where(lane < A_DIM, q, 0.0)
    q2 = jnp.where(lane >= A_DIM, q, 0.0)
    keys = kall[...]
    lam = lam_ref[0, 0]

    def probs(qm):
        s = _dot_nt(qm, keys)
        p = jnp.exp(s - jnp.max(s, axis=-1, keepdims=True))
        return p, jnp.sum(p, axis=-1, keepdims=True)

    p1, l1 = probs(q1)
    p2, l2 = probs(q2)
    a = p1 * (1.0 / l1) - p2 * (lam / l2)
    o = jnp.dot(a.astype(BF16), vall[...], preferred_element_type=F32)
    o_ref[...] = _rms(o) * nrm_ref[...] * out_scale


def _attention(P, lam, norm, *, B, L, tq, lam_init, cache=None):
    T = B * L
    nq = L // tq
    has_cache = cache is not None
    n_ctx = cache["k"].shape[3] if has_cache else 0
    colq, colk, colv = COL_AQ * HEADS, COL_AK * HEADS, COL_AV * HEADS
    in_specs = [
        pl.BlockSpec(memory_space=pltpu.SMEM),
        pl.BlockSpec((tq, LANE), lambda b, h, i: (b * nq + i, colq + h)),
        pl.BlockSpec((L, LANE), lambda b, h, i: (b, colk + h)),
        pl.BlockSpec((L, LANE), lambda b, h, i: (b, colv + h)),
    ]
    args = [lam, P, P, P]
    if has_cache:
        l = cache["layer"]
        cspec = pl.BlockSpec((None, None, None, n_ctx, LANE), lambda b, h, i: (b, l, h, 0, 0))
        in_specs += [cspec, cspec,
                     pl.BlockSpec((tq, LANE), lambda b, h, i: (i, 0)),
                     pl.BlockSpec((tq, LANE), lambda b, h, i: (i, 0)),
                     pl.BlockSpec((L, LANE), lambda b, h, i: (0, 0)),
                     pl.BlockSpec((L, LANE), lambda b, h, i: (0, 0))]
        args += [cache["k"], cache["v"], cache["cos"], cache["sin"], cache["cos"], cache["sin"]]
    in_specs.append(pl.BlockSpec((1, LANE), lambda b, h, i: (0, 0)))
    args.append(norm)
    out_shape = [jax.ShapeDtypeStruct((T, HEADS * LANE), F32)]
    out_specs = [pl.BlockSpec((tq, LANE), lambda b, h, i: (b * nq + i, h))]
    if not has_cache:
        kv_shape = jax.ShapeDtypeStruct((B, HEADS, L, LANE), F32)
        kv_spec = pl.BlockSpec((None, None, L, LANE), lambda b, h, i: (b, h, 0, 0))
        out_shape += [kv_shape, kv_shape]
        out_specs += [kv_spec, kv_spec]
    return pl.pallas_call(
        functools.partial(_attn_kernel, has_cache=has_cache, n_ctx=n_ctx, out_scale=1.0 - lam_init),
        out_shape=out_shape,
        grid=(B, HEADS, nq),
        in_specs=in_specs,
        out_specs=out_specs,
        scratch_shapes=[pltpu.VMEM((n_ctx + L, LANE), BF16), pltpu.VMEM((n_ctx + L, LANE), BF16)],
        compiler_params=_params("parallel", "parallel", "arbitrary"),
        name="diff_attention",
    )(*args)


POOL_PAD = 16


def _pool_kernel(x_ref, w_ref, b_ref, s_ref, o_ref, pad, *, L):
    zeros = jnp.zeros((POOL_PAD, LANE), F32)
    pad[pl.ds(0, POOL_PAD), :] = zeros
    pad[pl.ds(POOL_PAD + L, POOL_PAD), :] = zeros
    x = x_ref[...]
    pad[pl.ds(POOL_PAD, L), :] = x
    t = lax.broadcasted_iota(jnp.int32, (L, LANE), 0)
    g = pl.program_id(1)
    for gi, win in enumerate(POOL_WINDOWS):
        @pl.when(g == gi)
        def _(win=win):
            half = win // 2
            acc = pad[pl.ds(POOL_PAD - half, L), :]
            for k in range(1 - half, half):
                acc = acc + pad[pl.ds(POOL_PAD + k, L), :]
            cnt = (jnp.minimum(t + half, L) - jnp.maximum(t - half, 0)).astype(F32)
            pooled = acc / cnt - x
            o_ref[...] = (_dot(pooled, w_ref[...]) + b_ref[...]) * s_ref[...]


def _pool(P, w, b, scale, *, B, L):
    T = B * L
    G = len(POOL_WINDOWS)
    return pl.pallas_call(
        functools.partial(_pool_kernel, L=L),
        out_shape=jax.ShapeDtypeStruct((T, G * LANE), F32),
        grid=(B, G),
        in_specs=[
            pl.BlockSpec((L, LANE), lambda bi, g: (bi, COL_POOL * HEADS + g)),
            pl.BlockSpec((None, LANE, LANE), lambda bi, g: (g, 0, 0)),
            pl.BlockSpec((None, 1, LANE), lambda bi, g: (g, 0, 0)),
            pl.BlockSpec((1, LANE), lambda bi, g: (0, g)),
        ],
        out_specs=pl.BlockSpec((L, LANE), lambda bi, g: (bi, g)),
        scratch_shapes=[pltpu.VMEM((L + 2 * POOL_PAD, LANE), F32)],
        compiler_params=_params("parallel", "parallel"),
        name="pool_mixer",
    )(P, w, b, scale)


GROUP_CHUNKS = 2
GROUP = GROUP_CHUNKS * CHUNK
GROUPS_PER_STEP = 2
STEP_ROWS = GROUPS_PER_STEP * GROUP


def _group_masks(backward):
    row = lax.broadcasted_iota(jnp.int32, (GROUP, GROUP), 0)
    col = lax.broadcasted_iota(jnp.int32, (GROUP, GROUP), 1)
    same = (row // CHUNK) == (col // CHUNK)
    lower, upper = row >= col, row <= col
    if backward:
        lower, upper = upper, lower
    return dict(same=same, incl=same & lower, strict=same & lower & (row != col), incl_t=same & upper,
                eye=row == col)


INV_BASE = 8


def _block_masks():
    row = lax.broadcasted_iota(jnp.int32, (GROUP, GROUP), 0)
    col = lax.broadcasted_iota(jnp.int32, (GROUP, GROUP), 1)
    sizes = [INV_BASE << i for i in range(int(math.log2(CHUNK // INV_BASE)) + 1)]
    same = [(row // s) == (col // s) for s in sizes]
    return [same[0]] + [cur & jnp.logical_not(prev) for prev, cur in zip(same[:-1], same[1:])]


def _unit_triangular_inverses(ns, eye, blocks):
    base = [jnp.where(blocks[0], n, 0.0) for n in ns]
    invs = [eye - b for b in base]
    pws = [_dot(b, b) for b in base]
    rounds = int(math.log2(INV_BASE)) - 1
    for s in range(rounds):
        invs = [t + _dot(t, p) for t, p in zip(invs, pws)]
        if s + 1 < rounds:
            pws = [_dot(p, p) for p in pws]
    for join in blocks[1:]:
        offs = [jnp.where(join, n, 0.0) for n in ns]
        xs = [_dot(o, t) for o, t in zip(offs, invs)]
        invs = [t - _dot(t, x) for t, x in zip(invs, xs)]
    return invs


def _dot_mask2(mask, b):
    m = jnp.where(mask, 1.0, 0.0).astype(BF16)
    b0, b1 = _split(b)
    return jnp.dot(m, b0, preferred_element_type=F32) + jnp.dot(m, b1, preferred_element_type=F32)


def _per_chunk(x, reduce):
    parts = [reduce(x[c * CHUNK:(c + 1) * CHUNK], axis=0, keepdims=True) for c in range(GROUP_CHUNKS)]
    col = jnp.concatenate([jnp.broadcast_to(p, (CHUNK, 1)) for p in parts], axis=0)
    return parts, col


def _scan_order(step, n_steps):
    base = (step * STEP_ROWS, (n_steps - 1 - step) * STEP_ROWS)
    fwd = [(gi, c) for gi in range(GROUPS_PER_STEP) for c in range(GROUP_CHUNKS)]
    return base, (fwd, fwd[::-1])


CONV_PAD = 8


def _gdn_kernel(*refs, L, has_state):
    alog_ref, dtb_ref, q_ref, k_ref, v_ref, gate_ref, sm_ref, cwq_ref, cwk_ref, cwv_ref, nrm_ref = refs[:11]
    refs = refs[11:]
    if has_state:
        s0_ref, o_ref, pad, qs, ks, vs, o_f, o_b, state = refs
    else:
        o_ref, sout_ref, pad, qs, ks, vs, o_f, o_b, state = refs
    h = pl.program_id(1)
    n_chunks = L // CHUNK

    zeros = jnp.zeros((CONV_PAD, LANE), F32)
    pad[pl.ds(0, CONV_PAD), :] = zeros
    pad[pl.ds(CONV_PAD + L, CONV_PAD), :] = zeros

    def conv_silu(x_ref, w_ref):
        pad[pl.ds(CONV_PAD, L), :] = x_ref[...]
        w = w_ref[...]
        y = (pad[pl.ds(CONV_PAD - 1, L), :] * w[0:1] + pad[pl.ds(CONV_PAD, L), :] * w[1:2]
             + pad[pl.ds(CONV_PAD + 1, L), :] * w[2:3])
        return y * _sigmoid(y)

    def l2n(x):
        return x * lax.rsqrt(jnp.sum(x * x, axis=-1, keepdims=True) + EPS)

    qs[...] = l2n(conv_silu(q_ref, cwq_ref)) * (HEAD_DIM ** -0.5)
    ks[...] = l2n(conv_silu(k_ref, cwk_ref))
    vs[...] = conv_silu(v_ref, cwv_ref)
    if has_state:
        state[...] = s0_ref[...]
    else:
        state[...] = jnp.zeros_like(state)

    n_steps = L // STEP_ROWS
    o_scr = (o_f, o_b)
    masks = (_group_masks(False), _group_masks(True))
    eye = jnp.where(masks[0]['eye'], 1.0, 0.0)
    blocks = _block_masks()
    items = [(d, gi) for gi in range(GROUPS_PER_STEP) for d in (0, 1)]
    chunks = [slice(c * CHUNK, (c + 1) * CHUNK) for c in range(GROUP_CHUNKS)]

    def body(step, carry):
        base, order = _scan_order(step, n_steps)
        it = {}
        for key in items:
            d, gi = key
            rows = pl.ds(pl.multiple_of(base[d] + gi * GROUP, GROUP), GROUP)
            q, k, v, sm = qs[rows, :], ks[rows, :], vs[rows, :], sm_ref[rows, :]
            beta = _sigmoid(sm[:, SM_BETA + d:SM_BETA + d + 1])
            dec = sm[:, SM_DEC + d:SM_DEC + d + 1]
            g = -jnp.exp(alog_ref[d, h]) * _softplus(dec + dtb_ref[d, h])
            it[key] = dict(q=q, k=k, kb=k * beta, vb=v * beta, g=g, gb=jnp.broadcast_to(g, (GROUP, GROUP)))
        for (d, gi), x in it.items():
            m = masks[d]
            x['cum_i'] = _dot_mask2(m['incl'], x['gb'])
            x['cum_j'] = _dot_mask2(m['same'], jnp.where(m['incl_t'], x['gb'], 0.0))
            x['kk'] = _dot_nt(x['kb'], x['k'])
            x['qk'] = _dot_nt(x['q'], x['k'])
        for (d, gi), x in it.items():
            m = masks[d]
            decay = jnp.where(m['incl'], jnp.exp(jnp.where(m['incl'], x['cum_i'] - x['cum_j'], 0.0)), 0.0)
            x['gc'] = x['cum_i'][:, 0:1]
            x['tot'], x['tot_col'] = _per_chunk(x['g'], jnp.sum)
            x['qk'] = x['qk'] * decay
            x['n'] = jnp.where(m['strict'], x['kk'] * decay, 0.0)
        invs = _unit_triangular_inverses([x['n'] for x in it.values()], eye, blocks)
        for x, inv in zip(it.values(), invs):
            x['inv'] = inv
            x['egc'] = jnp.exp(x['gc'])
            x['sol'] = _dot3(x['inv'], jnp.concatenate([x['vb'], x['kb'] * x['egc']], axis=-1))
        for x in it.values():
            u, w = x['sol'][:, :HEAD_DIM], x['sol'][:, HEAD_DIM:]
            x['qp'] = x['q'] * x['egc'] - _dot(x['qk'], w)
            x['op'] = _dot(x['qk'], u)
            kd = x['k'] * jnp.exp(x['tot_col'] - x['gc'])
            x['ab'] = [_dot_tn(kd[cs], x['sol'][cs]) for cs in chunks]
            x['gl'] = [jnp.exp(t) for t in x['tot']]
        S = [state[0], state[1]]
        for stp in range(len(order[0])):
            for d in (0, 1):
                gi, c = order[d][stp]
                x = it[(d, gi)]
                ab = x['ab'][c]
                r = _dot(jnp.concatenate([ab[:, HEAD_DIM:], x['qp'][chunks[c]]], axis=0), S[d])
                rows = pl.ds(pl.multiple_of(base[d] + gi * GROUP + c * CHUNK, CHUNK), CHUNK)
                o_scr[d][rows, :] = x['op'][chunks[c]] + r[HEAD_DIM:]
                S[d] = x['gl'][c] * S[d] + (ab[:, :HEAD_DIM] - r[:HEAD_DIM])
        state[0] = S[0]
        state[1] = S[1]
        return carry

    lax.fori_loop(0, n_steps, body, 0)
    gate = gate_ref[...]
    o_ref[...] = _rms(o_f[...] + o_b[...]) * nrm_ref[...] * (gate * _sigmoid(gate))
    if not has_state:
        sout_ref[...] = state[...]


def _col_spec(L, col):
    return pl.BlockSpec((L, LANE), lambda b, h: (b, col * HEADS + h))


def _gdn(P, a_log, dt_bias, conv_w, norm, *, B, L, state=None):
    T = B * L
    has_state = state is not None
    smem = pl.BlockSpec(memory_space=pltpu.SMEM)
    in_specs = [smem, smem, _col_spec(L, COL_CQ), _col_spec(L, COL_CK), _col_spec(L, COL_CV),
                _col_spec(L, COL_CG), _col_spec(L, COL_SM)]
    in_specs += [pl.BlockSpec((3, LANE), lambda b, h, j=j: (0, j * HEADS + h)) for j in range(3)]
    in_specs.append(pl.BlockSpec((1, LANE), lambda b, h: (0, 0)))
    args = [a_log, dt_bias, P, P, P, P, P, conv_w, conv_w, conv_w, norm]
    out_shape = [jax.ShapeDtypeStruct((T, HEADS * LANE), F32)]
    out_specs = [pl.BlockSpec((L, LANE), lambda b, h: (b, h))]
    if has_state:
        arr, l = state
        in_specs.append(pl.BlockSpec((None, None, 2, None, HEAD_DIM, HEAD_DIM), lambda b, h: (b, l, 0, h, 0, 0)))
        args.append(arr)
    else:
        out_shape.append(jax.ShapeDtypeStruct((B, 2, HEADS, HEAD_DIM, HEAD_DIM), F32))
        out_specs.append(pl.BlockSpec((None, 2, None, HEAD_DIM, HEAD_DIM), lambda b, h: (b, 0, h, 0, 0)))
    seq = pltpu.VMEM((L, LANE), F32)
    return pl.pallas_call(
        functools.partial(_gdn_kernel, L=L, has_state=has_state),
        out_shape=out_shape,
        grid=(B, HEADS),
        in_specs=in_specs,
        out_specs=out_specs,
        scratch_shapes=[pltpu.VMEM((L + 2 * CONV_PAD, LANE), F32), seq, seq, seq, seq, seq,
                        pltpu.VMEM((2, HEAD_DIM, HEAD_DIM), F32)],
        compiler_params=_params("parallel", "parallel"),
        name="gated_deltanet",
    )(*args)


def _mlstm_kernel(*refs, L, has_state):
    bi_ref, bf_ref, q_ref, k_ref, v_ref, gate_ref, sm_ref, nrm_ref = refs[:8]
    refs = refs[8:]
    if has_state:
        c0_ref, n0_ref, m0_ref, o_ref, h_f, h_b, c_st, n_st, m_st = refs
    else:
        o_ref, cout_ref, nout_ref, mout_ref, h_f, h_b, c_st, n_st, m_st = refs
    h = pl.program_id(1)
    n_chunks = L // CHUNK

    if has_state:
        c_st[...] = c0_ref[...]
        n_st[...] = n0_ref[...]
        m_st[...] = m0_ref[...]
    else:
        c_st[...] = jnp.zeros_like(c_st)
        n_st[...] = jnp.zeros_like(n_st)
        m_st[...] = jnp.zeros_like(m_st)

    n_steps = L // STEP_ROWS
    h_scr = (h_f, h_b)
    masks = (_group_masks(False), _group_masks(True))
    items = [(d, gi) for gi in range(GROUPS_PER_STEP) for d in (0, 1)]
    chunks = [slice(c * CHUNK, (c + 1) * CHUNK) for c in range(GROUP_CHUNKS)]

    def body(step, carry):
        base, order = _scan_order(step, n_steps)
        it = {}
        for key in items:
            d, gi = key
            rows = pl.ds(pl.multiple_of(base[d] + gi * GROUP, GROUP), GROUP)
            q, v, sm = q_ref[rows, :], v_ref[rows, :], sm_ref[rows, :]
            k = k_ref[rows, :] * (HEAD_DIM ** -0.5)
            ig = sm[:, SM_IG + d:SM_IG + d + 1] + bi_ref[d, h]
            fg = sm[:, SM_FG + d:SM_FG + d + 1] + bf_ref[d, h]
            lf = -_softplus(-fg)
            it[key] = dict(q=q, k=k, v=v, ig=ig, lf=lf, lfb=jnp.broadcast_to(lf, (GROUP, GROUP)),
                           igb=jnp.broadcast_to(ig, (GROUP, GROUP)))
        for (d, gi), x in it.items():
            m = masks[d]
            x['cum_i'] = _dot_mask2(m['incl'], x['lfb'])
            x['cum_j'] = _dot_mask2(m['same'], jnp.where(m['incl_t'], x['lfb'], 0.0)
                                    - jnp.where(m['eye'], x['igb'], 0.0))
            x['qk'] = _dot_nt(x['q'], x['k'])
        for (d, gi), x in it.items():
            m = masks[d]
            dm = x['cum_i'] - x['cum_j']
            x['b'] = x['cum_i'][:, 0:1]
            x['tot'], tot_col = _per_chunk(x['lf'], jnp.sum)
            x['m_intra'] = jnp.max(jnp.where(m['incl'], dm, -jnp.inf), axis=-1, keepdims=True)
            e = tot_col - x['b'] + x['ig']
            x['m_end'], m_end_col = _per_chunk(e, jnp.max)
            p = jnp.where(m['incl'], jnp.exp(jnp.where(m['incl'], dm - x['m_intra'], 0.0)), 0.0) * x['qk']
            x['p_sum'] = jnp.sum(p, axis=-1, keepdims=True)
            x['p'] = p
            x['kw'] = x['k'] * jnp.exp(e - m_end_col)
        for x in it.values():
            x['pv'] = _dot(x['p'], x['v'])
            x['kv'] = [_dot_tn(x['kw'][cs], x['v'][cs]) for cs in chunks]
            x['k_sum'] = [jnp.sum(x['kw'][cs], axis=0, keepdims=True) for cs in chunks]
        outs = []
        for d in (0, 1):
            C, n, m = c_st[d], n_st[d], m_st[d][:, 0:1]
            for gi, c in order[d]:
                x = it[(d, gi)]
                cs = chunks[c]
                m_t = jnp.maximum(x['b'][cs] + m, x['m_intra'][cs])
                w_inter = jnp.exp(x['b'][cs] + m - m_t)
                local = jnp.exp(x['m_intra'][cs] - m_t)
                den = (w_inter * jnp.sum(x['q'][cs] * n, axis=-1, keepdims=True) + local * x['p_sum'][cs])
                scale = 1.0 / jnp.maximum(jnp.abs(den), jnp.exp(-m_t))
                outs.append((d, gi, c, C, w_inter * scale, local * scale))
                m_new = jnp.maximum(x['tot'][c] + m, x['m_end'][c])
                carry_decay = jnp.exp(x['tot'][c] + m - m_new)
                local_new = jnp.exp(x['m_end'][c] - m_new)
                C = carry_decay * C + local_new * x['kv'][c]
                n = carry_decay * n + local_new * x['k_sum'][c]
                m = m_new
            c_st[d] = C
            n_st[d] = n
            m_st[d] = jnp.broadcast_to(m, (1, LANE))
        for d, gi, c, C, w_scale, p_scale in outs:
            x = it[(d, gi)]
            cs = chunks[c]
            rows = pl.ds(pl.multiple_of(base[d] + gi * GROUP + c * CHUNK, CHUNK), CHUNK)
            h_scr[d][rows, :] = w_scale * _dot(x['q'][cs], C) + p_scale * x['pv'][cs]
        return carry

    lax.fori_loop(0, n_steps, body, 0)
    o_ref[...] = _rms(h_f[...] + h_b[...]) * nrm_ref[...] * _sigmoid(gate_ref[...])
    if not has_state:
        cout_ref[...] = c_st[...]
        nout_ref[...] = n_st[...]
        mout_ref[...] = m_st[...]


def _mlstm(P, bias_i, bias_f, norm, *, B, L, state=None):
    T = B * L
    has_state = state is not None
    smem = pl.BlockSpec(memory_space=pltpu.SMEM)
    in_specs = [smem, smem, _col_spec(L, COL_MQ), _col_spec(L, COL_MK), _col_spec(L, COL_MV),
                _col_spec(L, COL_MG), _col_spec(L, COL_SM), pl.BlockSpec((1, LANE), lambda b, h: (0, 0))]
    args = [bias_i, bias_f, P, P, P, P, P, norm]
    out_shape = [jax.ShapeDtypeStruct((T, HEADS * LANE), F32)]
    out_specs = [pl.BlockSpec((L, LANE), lambda b, h: (b, h))]
    vec_spec = pl.BlockSpec((None, None, 2, 1, LANE), lambda b, h: (b, h, 0, 0, 0))
    if has_state:
        c_arr, l, n_arr, m_arr = state
        in_specs += [pl.BlockSpec((None, None, 2, None, HEAD_DIM, HEAD_DIM), lambda b, h: (b, l, 0, h, 0, 0)),
                     vec_spec, vec_spec]
        args += [c_arr, n_arr, m_arr]
    else:
        out_shape += [jax.ShapeDtypeStruct((B, 2, HEADS, HEAD_DIM, HEAD_DIM), F32),
                      jax.ShapeDtypeStruct((B, HEADS, 2, 1, LANE), F32),
                      jax.ShapeDtypeStruct((B, HEADS, 2, 1, LANE), F32)]
        out_specs += [pl.BlockSpec((None, 2, None, HEAD_DIM, HEAD_DIM), lambda b, h: (b, 0, h, 0, 0)),
                      vec_spec, vec_spec]
    seq = pltpu.VMEM((L, LANE), F32)
    return pl.pallas_call(
        functools.partial(_mlstm_kernel, L=L, has_state=has_state),
        out_shape=out_shape,
        grid=(B, HEADS),
        in_specs=in_specs,
        out_specs=out_specs,
        scratch_shapes=[seq, seq, pltpu.VMEM((2, HEAD_DIM, HEAD_DIM), F32), pltpu.VMEM((2, 1, LANE), F32),
                        pltpu.VMEM((2, 1, LANE), F32)],
        compiler_params=_params("parallel", "parallel"),
        name="mlstm",
    )(*args)


def _merge_kernel(x_ref, gates_ref, a_ref, b_ref, c_ref, d_ref, wb_ref, wo_ref, g1_ref, gate1_ref, g2_ref,
                  sh2_ref, sc2_ref, rw_ref, rb_ref, xo_ref, h2_ref, lg_ref):
    merged = None
    for i, br in enumerate((a_ref, b_ref, c_ref, d_ref)):
        term = gates_ref[:, i * D_MODEL:(i + 1) * D_MODEL] * _dot(br[...], wb_ref[i])
        merged = term if merged is None else merged + term
    t = _rms(_dot(merged, wo_ref[...])) * g1_ref[...]
    x = x_ref[...] + gate1_ref[0] * t
    xo_ref[...] = x
    h2 = _rms(x) * g2_ref[...] * (1.0 + sc2_ref[0]) + sh2_ref[0]
    h2_ref[...] = h2.astype(BF16)
    lg_ref[...] = _dot3(h2, rw_ref[...]) + rb_ref[...]


def _merge(x, gates, branches, wb, wo, g1, gate1, g2, shift2, scale2, rw, rb, *, rows_per_mod, tm):
    T = x.shape[0]
    tm = min(tm, rows_per_mod)
    row = lambda n: pl.BlockSpec((tm, n), lambda i: (i, 0))
    const = lambda *shape: pl.BlockSpec(shape, lambda i: (0,) * len(shape))
    mod = pl.BlockSpec((1, 1, D_MODEL), lambda i: ((i * tm) // rows_per_mod, 0, 0))
    return pl.pallas_call(
        _merge_kernel,
        out_shape=[jax.ShapeDtypeStruct((T, D_MODEL), F32), jax.ShapeDtypeStruct((T, D_MODEL), BF16),
                   jax.ShapeDtypeStruct((T, LANE), F32)],
        grid=(T // tm,),
        in_specs=[row(D_MODEL), row(N_BRANCH * D_MODEL)] + [row(BRANCH_WIDTH)] * 4
                 + [const(N_BRANCH, BRANCH_WIDTH, D_MODEL), const(D_MODEL, D_MODEL), const(1, D_MODEL), mod,
                    const(1, D_MODEL), mod, mod, const(D_MODEL, LANE), const(1, LANE)],
        out_specs=[row(D_MODEL), row(D_MODEL), row(LANE)],
        compiler_params=_params("parallel"),
        name="merge",
    )(x, gates, *branches, wb, wo, g1, gate1, g2, shift2, scale2, rw, rb)


def _moe_kernel(be_ref, nu_ref, x_ref, wgu_ref, bgu_ref, wdn_ref, bdn_ref, gate_ref, y_ref):
    i = pl.program_id(0)

    @pl.when(i < nu_ref[0])
    def _():
        hgu = jnp.dot(x_ref[...], wgu_ref[...], preferred_element_type=F32) + bgu_ref[...]
        hg = jnp.minimum(hgu[:, :D_EXPERT], SWIGLU_LIMIT)
        hu = jnp.clip(hgu[:, D_EXPERT:], -SWIGLU_LIMIT, SWIGLU_LIMIT)
        act = (hu + 1.0) * (hg * _sigmoid(SWIGLU_ALPHA * hg))
        y = _dot(act, wdn_ref[...]) + bdn_ref[...]
        y_ref[...] = y * gate_ref[...]

    @pl.when(i >= nu_ref[0])
    def _():
        y_ref[...] = jnp.zeros_like(y_ref)


def _moe_blocks(x_rows, row_gate, block_e, n_used, w_gu, b_gu, w_dn, b_dn):
    P = x_rows.shape[0]
    n_blocks = P // MOE_ROWS
    blk = lambda i, be, nu: jnp.minimum(i, nu[0] - 1)
    exp = lambda i, be, nu: (be[jnp.minimum(i, nu[0] - 1)], 0, 0)
    return pl.pallas_call(
        _moe_kernel,
        out_shape=jax.ShapeDtypeStruct((P, D_MODEL), F32),
        grid_spec=pltpu.PrefetchScalarGridSpec(
            num_scalar_prefetch=2,
            grid=(n_blocks,),
            in_specs=[
                pl.BlockSpec((MOE_ROWS, D_MODEL), lambda i, be, nu: (blk(i, be, nu), 0)),
                pl.BlockSpec((None, D_MODEL, 2 * D_EXPERT), exp),
                pl.BlockSpec((None, 1, 2 * D_EXPERT), exp),
                pl.BlockSpec((None, D_EXPERT, D_MODEL), exp),
                pl.BlockSpec((None, 1, D_MODEL), exp),
                pl.BlockSpec((MOE_ROWS, 1), lambda i, be, nu: (blk(i, be, nu), 0)),
            ],
            out_specs=pl.BlockSpec((MOE_ROWS, D_MODEL), lambda i, be, nu: (i, 0)),
        ),
        compiler_params=_params("arbitrary"),
        name="moe_experts",
    )(block_e, n_used, x_rows, w_gu, b_gu, w_dn, b_dn, row_gate)


def _moe(h2, logits, w_gu, b_gu, w_dn, b_dn):
    N = h2.shape[0]
    NK = N * TOP_K
    top_val, top_idx = lax.top_k(logits, TOP_K)
    gate = jax.nn.softmax(top_val, axis=-1)
    flat_e = top_idx.reshape(-1)
    order = jnp.argsort(flat_e)
    sorted_e = flat_e[order]
    counts = jnp.bincount(flat_e, length=N_EXPERTS)
    padded = (counts + MOE_ROWS - 1) // MOE_ROWS * MOE_ROWS
    pad_end = jnp.cumsum(padded)
    pad_start = pad_end - padded
    start = jnp.cumsum(counts) - counts
    dest = (pad_start[sorted_e] + jnp.arange(NK) - start[sorted_e]).astype(jnp.int32)
    n_blocks = (NK + N_EXPERTS * (MOE_ROWS - 1) + MOE_ROWS - 1) // MOE_ROWS
    P = n_blocks * MOE_ROWS
    row_tok = jnp.full((P,), N, jnp.int32).at[dest].set((order // TOP_K).astype(jnp.int32))
    row_gate = jnp.zeros((P,), F32).at[dest].set(gate.reshape(-1)[order])
    block_e = jnp.minimum(jnp.searchsorted(pad_end, jnp.arange(n_blocks) * MOE_ROWS, side='right'),
                          N_EXPERTS - 1).astype(jnp.int32)
    n_used = (pad_end[-1] // MOE_ROWS).astype(jnp.int32).reshape(1)
    x_rows = jnp.concatenate([h2, jnp.zeros((1, D_MODEL), h2.dtype)], axis=0)[row_tok]
    y_rows = _moe_blocks(x_rows, row_gate[:, None], block_e, n_used, w_gu, b_gu, w_dn, b_dn)
    slot = jnp.zeros((NK,), jnp.int32).at[order].set(dest).reshape(N, TOP_K)
    y = y_rows[slot[:, 0]]
    for kk in range(1, TOP_K):
        y = y + y_rows[slot[:, kk]]
    return y


def _residual_kernel(x_ref, y_ref, g_ref, gate_ref, o_ref):
    o_ref[...] = x_ref[...] + gate_ref[0] * (_rms(y_ref[...]) * g_ref[...])


def _residual(x, y, g, gate, *, rows_per_mod, tm):
    T = x.shape[0]
    tm = min(tm, rows_per_mod)
    row = pl.BlockSpec((tm, D_MODEL), lambda i: (i, 0))
    return pl.pallas_call(
        _residual_kernel,
        out_shape=jax.ShapeDtypeStruct((T, D_MODEL), F32),
        grid=(T // tm,),
        in_specs=[row, row, pl.BlockSpec((1, D_MODEL), lambda i: (0, 0)),
                  pl.BlockSpec((1, 1, D_MODEL), lambda i: ((i * tm) // rows_per_mod, 0, 0))],
        out_specs=row,
        compiler_params=_params("parallel"),
        name="residual",
    )(x, y, g, gate)


def _rope_tables(L):
    t = jnp.arange(L)
    lane = jnp.arange(LANE)
    axis = (lane % A_DIM) // 32
    half = (lane % 32) // 16
    n_freq = A_DIM // 4
    inv = ROPE_BASE ** (-(lane % n_freq).astype(F32) / n_freq)
    pos = jnp.where(axis[None, :] == 0, (t // GRID_W)[:, None], (t % GRID_W)[:, None]).astype(F32)
    ang = pos * inv[None, :]
    return jnp.cos(ang), jnp.where(half[None, :] == 0, -jnp.sin(ang), jnp.sin(ang))


def _arrange_w_in(w_in):
    sizes = [512] * 8 + [8, 8] + [512] * 4 + [8, 8]
    offs = [0]
    for s in sizes:
        offs.append(offs[-1] + s)
    part = lambda i: w_in[:, offs[i]:offs[i + 1]]
    big = [part(i) for i in (0, 1, 2, 3, 4, 5, 6, 7, 10, 11, 12, 13)]
    small = jnp.zeros((w_in.shape[0], HEADS, LANE), w_in.dtype)
    for base, i in ((SM_BETA, 8), (SM_DEC, 9), (SM_IG, 14), (SM_FG, 15)):
        cols = part(i).reshape(-1, 2, HEADS)
        for d in range(2):
            small = small.at[:, :, base + d].set(cols[:, d, :])
    return jnp.concatenate(big + [small.reshape(w_in.shape[0], HEADS * LANE)], axis=1)


def _layer(x, cond, lidx, lp, *, B, L, cache):
    T = B * L
    is_ctx = cache is None
    mod = (jax.nn.silu(cond) @ lp['ada_w'] + lp['ada_b']).reshape(-1, 6, 1, D_MODEL)
    shift1, scale1, gate1, shift2, scale2, gate2 = [mod[:, i] for i in range(6)]
    rows_per_mod = T if mod.shape[0] == 1 else L
    g = lp['norm_g']

    tm = 1024
    P = _norm_mod_matmul(x, g[0:1], shift1, scale1, lp['w_in'], jnp.zeros((1, N_PROJ), F32),
                         rows_per_mod=rows_per_mod, sigmoid=False, tm=tm, tn=N_PROJ // 4)
    gates = _norm_mod_matmul(x, g[0:1], shift1, scale1, lp['w_bgate'], lp['b_bgate'][None],
                             rows_per_mod=rows_per_mod, sigmoid=True, tm=tm, tn=2048)

    lam_init = 0.8 - 0.6 * math.exp(-0.3 * lidx)
    lq1, lk1, lq2, lk2 = lp['attn_lambda']
    lam = (jnp.exp(jnp.sum(lq1 * lk1)) - jnp.exp(jnp.sum(lq2 * lk2)) + lam_init).reshape(1, 1)
    attn_norm = lp['attn_norm'][None]
    if is_ctx:
        br_a, new_k, new_v = _attention(P, lam, attn_norm, B=B, L=L, tq=L, lam_init=lam_init)
    else:
        (br_a,) = _attention(P, lam, attn_norm, B=B, L=L, tq=128, lam_init=lam_init, cache=cache['attn'])

    br_b = _pool(P, lp['pool_w'], lp['pool_b'][:, None], lp['pool_scale'][None], B=B, L=L)

    gdn_args = (P, lp['gdn_A_log'], lp['gdn_dt_bias'], lp['gdn_conv'], lp['gdn_norm'][None])
    ml_args = (P, lp['mlstm_bias_i'], lp['mlstm_bias_f'], lp['mlstm_norm'][None])
    if is_ctx:
        br_c, new_gdn = _gdn(*gdn_args, B=B, L=L)
        br_d, new_c, new_n, new_m = _mlstm(*ml_args, B=B, L=L)
        ctx_out = (new_k, new_v, new_gdn, new_c,
                   new_n[:, :, :, 0, :HEAD_DIM].transpose(0, 2, 1, 3), new_m[:, :, :, 0, 0].transpose(0, 2, 1))
    else:
        (br_c,) = _gdn(*gdn_args, B=B, L=L, state=cache['gdn'])
        (br_d,) = _mlstm(*ml_args, B=B, L=L, state=cache['mlstm'])
        ctx_out = None

    x, h2, logits = _merge(x, gates, (br_a, br_b, br_c, br_d), lp['w_branch'], lp['w_out'], g[1:2], gate1,
                           g[2:3], shift2, scale2, lp['router_w'], lp['router_b'],
                           rows_per_mod=rows_per_mod, tm=256)
    y = _moe(h2, logits[:, :N_EXPERTS], lp['moe_w_gu'], lp['moe_b_gu'], lp['moe_w_dn'], lp['moe_b_dn'])
    x = _residual(x, y, g[3:4], gate2, rows_per_mod=rows_per_mod, tm=512)
    return x, ctx_out


def kernel(x_prompt, x_sample, cache_attn_k, cache_attn_v, state_gdn, state_mlstm_C, state_mlstm_n, state_mlstm_m, c, c_ctx, ada_w, ada_b, norm_g, w_in, w_bgate, b_bgate, w_branch, w_out, attn_lambda, attn_norm, pool_w, pool_b, pool_scale, gdn_conv, gdn_A_log, gdn_dt_bias, gdn_norm, mlstm_bias_i, mlstm_bias_f, mlstm_norm, router_w, router_b, moe_w_gu, moe_b_gu, moe_w_dn, moe_b_dn):
    Bp, Lp, _ = x_prompt.shape
    Bs, Ls, _ = x_sample.shape
    y_p = x_prompt.reshape(Bp * Lp, D_MODEL)
    y_s = x_sample.reshape(Bs * Ls, D_MODEL)
    cos, sin = _rope_tables(Ls)
    n0 = state_mlstm_n.transpose(1, 0, 3, 2, 4)[:, :, :, :, None, :]
    m0 = jnp.broadcast_to(state_mlstm_m.transpose(1, 0, 3, 2)[..., None, None], n0.shape)
    ctx_states = []
    for l in range(DEPTH):
        lp = {
            'ada_w': ada_w[l], 'ada_b': ada_b[l], 'norm_g': norm_g[l],
            'w_in': _arrange_w_in(w_in[l]).astype(BF16), 'w_bgate': w_bgate[l].astype(BF16),
            'b_bgate': b_bgate[l], 'w_branch': w_branch[l].astype(BF16), 'w_out': w_out[l].astype(BF16),
            'attn_lambda': attn_lambda[l], 'attn_norm': attn_norm[l],
            'pool_w': pool_w[l], 'pool_b': pool_b[l], 'pool_scale': pool_scale[l],
            'gdn_conv': gdn_conv[l], 'gdn_A_log': gdn_A_log[l], 'gdn_dt_bias': gdn_dt_bias[l],
            'gdn_norm': gdn_norm[l], 'mlstm_bias_i': mlstm_bias_i[l], 'mlstm_bias_f': mlstm_bias_f[l],
            'mlstm_norm': mlstm_norm[l],
            'router_w': jnp.pad(router_w[l], ((0, 0), (0, LANE - N_EXPERTS))),
            'router_b': jnp.pad(router_b[l], (0, LANE - N_EXPERTS))[None],
            'moe_w_gu': moe_w_gu[l].astype(BF16), 'moe_b_gu': moe_b_gu[l][:, None],
            'moe_w_dn': moe_w_dn[l].astype(BF16), 'moe_b_dn': moe_b_dn[l][:, None],
        }
        y_p, st = _layer(y_p, c_ctx[None], l, lp, B=Bp, L=Lp, cache=None)
        ctx_states.append(st)
        cache = {
            'attn': {'k': cache_attn_k, 'v': cache_attn_v, 'layer': l, 'cos': cos, 'sin': sin},
            'gdn': (state_gdn, l),
            'mlstm': (state_mlstm_C, l, n0[l], m0[l]),
        }
        y_s, _ = _layer(y_s, c, l, lp, B=Bs, L=Ls, cache=cache)
    outs = [jnp.stack([s[i] for s in ctx_states], axis=1) for i in range(6)]
    return (y_p.reshape(Bp, Lp, D_MODEL), y_s.reshape(Bs, Ls, D_MODEL), *outs)
```

```python
import functools
import math

import jax
import jax.numpy as jnp
from jax import lax
from jax.experimental import pallas as pl
from jax.experimental.pallas import tpu as pltpu

F32 = jnp.float32
BF16 = jnp.bfloat16

D_MODEL = 1024
DEPTH = 2
GRID_W = 64
EPS = 1e-6
LANE = 128
HEADS = 4
HEAD_DIM = 128
A_DIM = 64
ROPE_BASE = 10000.0
POOL_WINDOWS = (2, 4, 8, 16)
CHUNK = 64
N_BRANCH = 4
BRANCH_WIDTH = 512
N_EXPERTS = 32
TOP_K = 4
D_EXPERT = 1024
SWIGLU_LIMIT = 7.0
SWIGLU_ALPHA = 1.702
MOE_ROWS = 256
VMEM_LIMIT = 56 * 1024 * 1024

COL_AQ, COL_AK, COL_AV, COL_POOL, COL_CQ, COL_CK, COL_CV, COL_CG, COL_MQ, COL_MK, COL_MV, COL_MG, COL_SM = range(13)
N_PROJ = 13 * BRANCH_WIDTH
SM_BETA, SM_DEC, SM_IG, SM_FG = 0, 2, 4, 6


def _params(*sem):
    return pltpu.CompilerParams(dimension_semantics=sem, vmem_limit_bytes=VMEM_LIMIT)


def _dot(a, b):
    return jnp.dot(a.astype(BF16), b.astype(BF16), preferred_element_type=F32)


def _dot_nt(a, b):
    return lax.dot_general(a.astype(BF16), b.astype(BF16), (((1,), (1,)), ((), ())),
                           preferred_element_type=F32)


def _dot_tn(a, b):
    return lax.dot_general(a.astype(BF16), b.astype(BF16), (((0,), (0,)), ((), ())),
                           preferred_element_type=F32)


def _split(a):
    hi = a.astype(BF16)
    lo = (a - hi.astype(F32)).astype(BF16)
    return hi, lo


def _dot3(a, b):
    ah, al = _split(a)
    bh, bl = _split(b)
    d = lambda x, y: jnp.dot(x, y, preferred_element_type=F32)
    return d(ah, bh) + (d(ah, bl) + d(al, bh))


def _dot_mask(mask, b):
    m = jnp.where(mask, 1.0, 0.0).astype(BF16)
    b0 = b.astype(BF16)
    r1 = b - b0.astype(F32)
    b1 = r1.astype(BF16)
    b2 = (r1 - b1.astype(F32)).astype(BF16)
    d = lambda y: jnp.dot(m, y, preferred_element_type=F32)
    return d(b0) + (d(b1) + d(b2))


def _rms(x):
    return x * lax.rsqrt(jnp.mean(x * x, axis=-1, keepdims=True) + EPS)


def _sigmoid(x):
    return 1.0 / (1.0 + jnp.exp(-x))


def _softplus(x):
    return jnp.maximum(x, 0.0) + jnp.log(1.0 + jnp.exp(-jnp.abs(x)))


def _proj_kernel(x_ref, g_ref, sh_ref, sc_ref, w_ref, b_ref, o_ref, h_scr, *, sigmoid):
    @pl.when(pl.program_id(1) == 0)
    def _():
        y = _rms(x_ref[...]) * g_ref[...]
        h_scr[...] = (y * (1.0 + sc_ref[0]) + sh_ref[0]).astype(BF16)

    acc = jnp.dot(h_scr[...], w_ref[...], preferred_element_type=F32) + b_ref[...]
    o_ref[...] = _sigmoid(acc) if sigmoid else acc


def _norm_mod_matmul(x, g, shift, scale, w, bias, *, rows_per_mod, sigmoid, tm, tn):
    T, Dm = x.shape
    N = w.shape[1]
    tm = min(tm, rows_per_mod)
    mod_idx = lambda i, j: ((i * tm) // rows_per_mod, 0, 0)
    return pl.pallas_call(
        functools.partial(_proj_kernel, sigmoid=sigmoid),
        out_shape=jax.ShapeDtypeStruct((T, N), F32),
        grid=(T // tm, N // tn),
        in_specs=[
            pl.BlockSpec((tm, Dm), lambda i, j: (i, 0)),
            pl.BlockSpec((1, Dm), lambda i, j: (0, 0)),
            pl.BlockSpec((1, 1, Dm), mod_idx),
            pl.BlockSpec((1, 1, Dm), mod_idx),
            pl.BlockSpec((Dm, tn), lambda i, j: (0, j)),
            pl.BlockSpec((1, tn), lambda i, j: (0, j)),
        ],
        out_specs=pl.BlockSpec((tm, tn), lambda i, j: (i, j)),
        scratch_shapes=[pltpu.VMEM((tm, Dm), BF16)],
        compiler_params=_params("parallel", "arbitrary"),
        name="proj",
    )(x, g, shift, scale, w, bias)


def _rope(x, cos, sin):
    lane = lax.broadcasted_iota(jnp.int32, x.shape, 1)
    first = (lane % 32) < 16
    partner = jnp.where(first, pltpu.roll(x, LANE - 16, 1), pltpu.roll(x, 16, 1))
    return x * cos + partner * sin


def _attn_kernel(*refs, has_cache, n_ctx, out_scale):
    if has_cache:
        (lam_ref, q_ref, k_ref, v_ref, ck_ref, cv_ref, cosq_ref, sinq_ref, cosk_ref, sink_ref, nrm_ref,
         o_ref, kall, vall) = refs
    else:
        lam_ref, q_ref, k_ref, v_ref, nrm_ref, o_ref, ko_ref, vo_ref, kall, vall = refs

    @pl.when(pl.program_id(2) == 0)
    def _():
        k = k_ref[...]
        v = v_ref[...]
        if has_cache:
            kall[pl.ds(0, n_ctx), :] = ck_ref[...].astype(BF16)
            vall[pl.ds(0, n_ctx), :] = cv_ref[...].astype(BF16)
            k = _rope(k, cosk_ref[...], sink_ref[...])
        else:
            ko_ref[...] = k
            vo_ref[...] = v
        kall[pl.ds(n_ctx, k.shape[0]), :] = k.astype(BF16)
        vall[pl.ds(n_ctx, k.shape[0]), :] = v.astype(BF16)

    q = q_ref[...]
    if has_cache:
        q = _rope(q, cosq_ref[...], sinq_ref[...])
    q = q * (A_DIM ** -0.5)
    lane = lax.broadcasted_iota(jnp.int32, q.shape, 1)
    q1 = jnp.where(lane < A_DIM, q, 0.0)
    q2 = jnp.where(lane >= A_DIM, q, 0.0)
    keys = kall[...]
    lam = lam_ref[0, 0]

    def probs(qm):
        s = _dot_nt(qm, keys)
        p = jnp.exp(s - jnp.max(s, axis=-1, keepdims=True))
        return p, jnp.sum(p, axis=-1, keepdims=True)

    p1, l1 = probs(q1)
    p2, l2 = probs(q2)
    a = p1 * (1.0 / l1) - p2 * (lam / l2)
    o = jnp.dot(a.astype(BF16), vall[...], preferred_element_type=F32)
    o_ref[...] = _rms(o) * nrm_ref[...] * out_scale


def _attention(P, lam, norm, *, B, L, tq, lam_init, cache=None):
    T = B * L
    nq = L // tq
    has_cache = cache is not None
    n_ctx = cache["k"].shape[3] if has_cache else 0
    colq, colk, colv = COL_AQ * HEADS, COL_AK * HEADS, COL_AV * HEADS
    in_specs = [
        pl.BlockSpec(memory_space=pltpu.SMEM),
        pl.BlockSpec((tq, LANE), lambda b, h, i: (b * nq + i, colq + h)),
        pl.BlockSpec((L, LANE), lambda b, h, i: (b, colk + h)),
        pl.BlockSpec((L, LANE), lambda b, h, i: (b, colv + h)),
    ]
    args = [lam, P, P, P]
    if has_cache:
        l = cache["layer"]
        cspec = pl.BlockSpec((None, None, None, n_ctx, LANE), lambda b, h, i: (b, l, h, 0, 0))
        in_specs += [cspec, cspec,
                     pl.BlockSpec((tq, LANE), lambda b, h, i: (i, 0)),
                     pl.BlockSpec((tq, LANE), lambda b, h, i: (i, 0)),
                     pl.BlockSpec((L, LANE), lambda b, h, i: (0, 0)),
                     pl.BlockSpec((L, LANE), lambda b, h, i: (0, 0))]
        args += [cache["k"], cache["v"], cache["cos"], cache["sin"], cache["cos"], cache["sin"]]
    in_specs.append(pl.BlockSpec((1, LANE), lambda b, h, i: (0, 0)))
    args.append(norm)
    out_shape = [jax.ShapeDtypeStruct((T, HEADS * LANE), F32)]
    out_specs = [pl.BlockSpec((tq, LANE), lambda b, h, i: (b * nq + i, h))]
    if not has_cache:
        kv_shape = jax.ShapeDtypeStruct((B, HEADS, L, LANE), F32)
        kv_spec = pl.BlockSpec((None, None, L, LANE), lambda b, h, i: (b, h, 0, 0))
        out_shape += [kv_shape, kv_shape]
        out_specs += [kv_spec, kv_spec]
    return pl.pallas_call(
        functools.partial(_attn_kernel, has_cache=has_cache, n_ctx=n_ctx, out_scale=1.0 - lam_init),
        out_shape=out_shape,
        grid=(B, HEADS, nq),
        in_specs=in_specs,
        out_specs=out_specs,
        scratch_shapes=[pltpu.VMEM((n_ctx + L, LANE), BF16), pltpu.VMEM((n_ctx + L, LANE), BF16)],
        compiler_params=_params("parallel", "parallel", "arbitrary"),
        name="diff_attention",
    )(*args)


POOL_PAD = 16


def _pool_kernel(x_ref, w_ref, b_ref, s_ref, o_ref, pad, *, L):
    zeros = jnp.zeros((POOL_PAD, LANE), F32)
    pad[pl.ds(0, POOL_PAD), :] = zeros
    pad[pl.ds(POOL_PAD + L, POOL_PAD), :] = zeros
    x = x_ref[...]
    pad[pl.ds(POOL_PAD, L), :] = x
    t = lax.broadcasted_iota(jnp.int32, (L, LANE), 0)
    g = pl.program_id(1)
    for gi, win in enumerate(POOL_WINDOWS):
        @pl.when(g == gi)
        def _(win=win):
            half = win // 2
            acc = pad[pl.ds(POOL_PAD - half, L), :]
            for k in range(1 - half, half):
                acc = acc + pad[pl.ds(POOL_PAD + k, L), :]
            cnt = (jnp.minimum(t + half, L) - jnp.maximum(t - half, 0)).astype(F32)
            pooled = acc / cnt - x
            o_ref[...] = (_dot(pooled, w_ref[...]) + b_ref[...]) * s_ref[...]


def _pool(P, w, b, scale, *, B, L):
    T = B * L
    G = len(POOL_WINDOWS)
    return pl.pallas_call(
        functools.partial(_pool_kernel, L=L),
        out_shape=jax.ShapeDtypeStruct((T, G * LANE), F32),
        grid=(B, G),
        in_specs=[
            pl.BlockSpec((L, LANE), lambda bi, g: (bi, COL_POOL * HEADS + g)),
            pl.BlockSpec((None, LANE, LANE), lambda bi, g: (g, 0, 0)),
            pl.BlockSpec((None, 1, LANE), lambda bi, g: (g, 0, 0)),
            pl.BlockSpec((1, LANE), lambda bi, g: (0, g)),
        ],
        out_specs=pl.BlockSpec((L, LANE), lambda bi, g: (bi, g)),
        scratch_shapes=[pltpu.VMEM((L + 2 * POOL_PAD, LANE), F32)],
        compiler_params=_params("parallel", "parallel"),
        name="pool_mixer",
    )(P, w, b, scale)


GROUP_CHUNKS = 2
GROUP = GROUP_CHUNKS * CHUNK
GROUPS_PER_STEP = 2
STEP_ROWS = GROUPS_PER_STEP * GROUP


def _group_masks(backward):
    row = lax.broadcasted_iota(jnp.int32, (GROUP, GROUP), 0)
    col = lax.broadcasted_iota(jnp.int32, (GROUP, GROUP), 1)
    same = (row // CHUNK) == (col // CHUNK)
    lower, upper = row >= col, row <= col
    if backward:
        lower, upper = upper, lower
    return dict(same=same, incl=same & lower, strict=same & lower & (row != col), incl_t=same & upper,
                eye=row == col)


INV_BASE = 8


def _block_masks():
    row = lax.broadcasted_iota(jnp.int32, (GROUP, GROUP), 0)
    col = lax.broadcasted_iota(jnp.int32, (GROUP, GROUP), 1)
    sizes = [INV_BASE << i for i in range(int(math.log2(CHUNK // INV_BASE)) + 1)]
    same = [(row // s) == (col // s) for s in sizes]
    return [same[0]] + [cur & jnp.logical_not(prev) for prev, cur in zip(same[:-1], same[1:])]


def _unit_triangular_inverses(ns, eye, blocks):
    base = [jnp.where(blocks[0], n, 0.0) for n in ns]
    invs = [eye - b for b in base]
    pws = [_dot(b, b) for b in base]
    rounds = int(math.log2(INV_BASE)) - 1
    for s in range(rounds):
        invs = [t + _dot(t, p) for t, p in zip(invs, pws)]
        if s + 1 < rounds:
            pws = [_dot(p, p) for p in pws]
    for join in blocks[1:]:
        offs = [jnp.where(join, n, 0.0) for n in ns]
        xs = [_dot(o, t) for o, t in zip(offs, invs)]
        invs = [t - _dot(t, x) for t, x in zip(invs, xs)]
    return invs


def _dot_mask2(mask, b):
    m = jnp.where(mask, 1.0, 0.0).astype(BF16)
    b0, b1 = _split(b)
    return jnp.dot(m, b0, preferred_element_type=F32) + jnp.dot(m, b1, preferred_element_type=F32)


def _per_chunk(x, reduce):
    parts = [reduce(x[c * CHUNK:(c + 1) * CHUNK], axis=0, keepdims=True) for c in range(GROUP_CHUNKS)]
    col = jnp.concatenate([jnp.broadcast_to(p, (CHUNK, 1)) for p in parts], axis=0)
    return parts, col


def _scan_order(step, n_steps):
    base = (step * STEP_ROWS, (n_steps - 1 - step) * STEP_ROWS)
    fwd = [(gi, c) for gi in range(GROUPS_PER_STEP) for c in range(GROUP_CHUNKS)]
    return base, (fwd, fwd[::-1])


CONV_PAD = 8


def _gdn_kernel(*refs, L, has_state):
    alog_ref, dtb_ref, q_ref, k_ref, v_ref, gate_ref, sm_ref, cwq_ref, cwk_ref, cwv_ref, nrm_ref = refs[:11]
    refs = refs[11:]
    if has_state:
        s0_ref, o_ref, pad, qs, ks, vs, o_f, o_b, state = refs
    else:
        o_ref, sout_ref, pad, qs, ks, vs, o_f, o_b, state = refs
    h = pl.program_id(1)
    n_chunks = L // CHUNK

    zeros = jnp.zeros((CONV_PAD, LANE), F32)
    pad[pl.ds(0, CONV_PAD), :] = zeros
    pad[pl.ds(CONV_PAD + L, CONV_PAD), :] = zeros

    def conv_silu(x_ref, w_ref):
        pad[pl.ds(CONV_PAD, L), :] = x_ref[...]
        w = w_ref[...]
        y = (pad[pl.ds(CONV_PAD - 1, L), :] * w[0:1] + pad[pl.ds(CONV_PAD, L), :] * w[1:2]
             + pad[pl.ds(CONV_PAD + 1, L), :] * w[2:3])
        return y * _sigmoid(y)

    def l2n(x):
        return x * lax.rsqrt(jnp.sum(x * x, axis=-1, keepdims=True) + EPS)

    qs[...] = l2n(conv_silu(q_ref, cwq_ref)) * (HEAD_DIM ** -0.5)
    ks[...] = l2n(conv_silu(k_ref, cwk_ref))
    vs[...] = conv_silu(v_ref, cwv_ref)
    if has_state:
        state[...] = s0_ref[...]
    else:
        state[...] = jnp.zeros_like(state)

    n_steps = L // STEP_ROWS
    o_scr = (o_f, o_b)
    masks = (_group_masks(False), _group_masks(True))
    eye = jnp.where(masks[0]['eye'], 1.0, 0.0)
    blocks = _block_masks()
    items = [(d, gi) for gi in range(GROUPS_PER_STEP) for d in (0, 1)]
    chunks = [slice(c * CHUNK, (c + 1) * CHUNK) for c in range(GROUP_CHUNKS)]

    def body(step, carry):
        base, order = _scan_order(step, n_steps)
        it = {}
        for key in items:
            d, gi = key
            rows = pl.ds(pl.multiple_of(base[d] + gi * GROUP, GROUP), GROUP)
            q, k, v, sm = qs[rows, :], ks[rows, :], vs[rows, :], sm_ref[rows, :]
            beta = _sigmoid(sm[:, SM_BETA + d:SM_BETA + d + 1])
            dec = sm[:, SM_DEC + d:SM_DEC + d + 1]
            g = -jnp.exp(alog_ref[d, h]) * _softplus(dec + dtb_ref[d, h])
            it[key] = dict(q=q, k=k, kb=k * beta, vb=v * beta, g=g, gb=jnp.broadcast_to(g, (GROUP, GROUP)))
        for (d, gi), x in it.items():
            m = masks[d]
            x['cum_i'] = _dot_mask2(m['incl'], x['gb'])
            x['cum_j'] = _dot_mask2(m['same'], jnp.where(m['incl_t'], x['gb'], 0.0))
            x['kk'] = _dot_nt(x['kb'], x['k'])
            x['qk'] = _dot_nt(x['q'], x['k'])
        for (d, gi), x in it.items():
            m = masks[d]
            decay = jnp.where(m['incl'], jnp.exp(jnp.where(m['incl'], x['cum_i'] - x['cum_j'], 0.0)), 0.0)
            x['gc'] = x['cum_i'][:, 0:1]
            x['tot'], x['tot_col'] = _per_chunk(x['g'], jnp.sum)
            x['qk'] = x['qk'] * decay
            x['n'] = jnp.where(m['strict'], x['kk'] * decay, 0.0)
        invs = _unit_triangular_inverses([x['n'] for x in it.values()], eye, blocks)
        for x, inv in zip(it.values(), invs):
            x['inv'] = inv
            x['egc'] = jnp.exp(x['gc'])
            x['sol'] = _dot3(x['inv'], jnp.concatenate([x['vb'], x['kb'] * x['egc']], axis=-1))
        for x in it.values():
            u, w = x['sol'][:, :HEAD_DIM], x['sol'][:, HEAD_DIM:]
            x['qp'] = x['q'] * x['egc'] - _dot(x['qk'], w)
            x['op'] = _dot(x['qk'], u)
            kd = x['k'] * jnp.exp(x['tot_col'] - x['gc'])
            x['ab'] = [_dot_tn(kd[cs], x['sol'][cs]) for cs in chunks]
            x['gl'] = [jnp.exp(t) for t in x['tot']]
        S = [state[0], state[1]]
        for stp in range(len(order[0])):
            for d in (0, 1):
                gi, c = order[d][stp]
                x = it[(d, gi)]
                ab = x['ab'][c]
                r = _dot(jnp.concatenate([ab[:, HEAD_DIM:], x['qp'][chunks[c]]], axis=0), S[d])
                rows = pl.ds(pl.multiple_of(base[d] + gi * GROUP + c * CHUNK, CHUNK), CHUNK)
                o_scr[d][rows, :] = x['op'][chunks[c]] + r[HEAD_DIM:]
                S[d] = x['gl'][c] * S[d] + (ab[:, :HEAD_DIM] - r[:HEAD_DIM])
        state[0] = S[0]
        state[1] = S[1]
        return carry

    lax.fori_loop(0, n_steps, body, 0)
    gate = gate_ref[...]
    o_ref[...] = _rms(o_f[...] + o_b[...]) * nrm_ref[...] * (gate * _sigmoid(gate))
    if not has_state:
        sout_ref[...] = state[...]


def _col_spec(L, col):
    return pl.BlockSpec((L, LANE), lambda b, h: (b, col * HEADS + h))


def _gdn(P, a_log, dt_bias, conv_w, norm, *, B, L, state=None):
    T = B * L
    has_state = state is not None
    smem = pl.BlockSpec(memory_space=pltpu.SMEM)
    in_specs = [smem, smem, _col_spec(L, COL_CQ), _col_spec(L, COL_CK), _col_spec(L, COL_CV),
                _col_spec(L, COL_CG), _col_spec(L, COL_SM)]
    in_specs += [pl.BlockSpec((3, LANE), lambda b, h, j=j: (0, j * HEADS + h)) for j in range(3)]
    in_specs.append(pl.BlockSpec((1, LANE), lambda b, h: (0, 0)))
    args = [a_log, dt_bias, P, P, P, P, P, conv_w, conv_w, conv_w, norm]
    out_shape = [jax.ShapeDtypeStruct((T, HEADS * LANE), F32)]
    out_specs = [pl.BlockSpec((L, LANE), lambda b, h: (b, h))]
    if has_state:
        arr, l = state
        in_specs.append(pl.BlockSpec((None, None, 2, None, HEAD_DIM, HEAD_DIM), lambda b, h: (b, l, 0, h, 0, 0)))
        args.append(arr)
    else:
        out_shape.append(jax.ShapeDtypeStruct((B, 2, HEADS, HEAD_DIM, HEAD_DIM), F32))
        out_specs.append(pl.BlockSpec((None, 2, None, HEAD_DIM, HEAD_DIM), lambda b, h: (b, 0, h, 0, 0)))
    seq = pltpu.VMEM((L, LANE), F32)
    return pl.pallas_call(
        functools.partial(_gdn_kernel, L=L, has_state=has_state),
        out_shape=out_shape,
        grid=(B, HEADS),
        in_specs=in_specs,
        out_specs=out_specs,
        scratch_shapes=[pltpu.VMEM((L + 2 * CONV_PAD, LANE), F32), seq, seq, seq, seq, seq,
                        pltpu.VMEM((2, HEAD_DIM, HEAD_DIM), F32)],
        compiler_params=_params("parallel", "parallel"),
        name="gated_deltanet",
    )(*args)


def _mlstm_kernel(*refs, L, has_state):
    bi_ref, bf_ref, q_ref, k_ref, v_ref, gate_ref, sm_ref, nrm_ref = refs[:8]
    refs = refs[8:]
    if has_state:
        c0_ref, n0_ref, m0_ref, o_ref, h_f, h_b, c_st, n_st, m_st = refs
    else:
        o_ref, cout_ref, nout_ref, mout_ref, h_f, h_b, c_st, n_st, m_st = refs
    h = pl.program_id(1)
    n_chunks = L // CHUNK

    if has_state:
        c_st[...] = c0_ref[...]
        n_st[...] = n0_ref[...]
        m_st[...] = m0_ref[...]
    else:
        c_st[...] = jnp.zeros_like(c_st)
        n_st[...] = jnp.zeros_like(n_st)
        m_st[...] = jnp.zeros_like(m_st)

    n_steps = L // STEP_ROWS
    h_scr = (h_f, h_b)
    masks = (_group_masks(False), _group_masks(True))
    items = [(d, gi) for gi in range(GROUPS_PER_STEP) for d in (0, 1)]
    chunks = [slice(c * CHUNK, (c + 1) * CHUNK) for c in range(GROUP_CHUNKS)]

    def body(step, carry):
        base, order = _scan_order(step, n_steps)
        it = {}
        for key in items:
            d, gi = key
            rows = pl.ds(pl.multiple_of(base[d] + gi * GROUP, GROUP), GROUP)
            q, v, sm = q_ref[rows, :], v_ref[rows, :], sm_ref[rows, :]
            k = k_ref[rows, :] * (HEAD_DIM ** -0.5)
            ig = sm[:, SM_IG + d:SM_IG + d + 1] + bi_ref[d, h]
            fg = sm[:, SM_FG + d:SM_FG + d + 1] + bf_ref[d, h]
            lf = -_softplus(-fg)
            it[key] = dict(q=q, k=k, v=v, ig=ig, lf=lf, lfb=jnp.broadcast_to(lf, (GROUP, GROUP)),
                           igb=jnp.broadcast_to(ig, (GROUP, GROUP)))
        for (d, gi), x in it.items():
            m = masks[d]
            x['cum_i'] = _dot_mask2(m['incl'], x['lfb'])
            x['cum_j'] = _dot_mask2(m['same'], jnp.where(m['incl_t'], x['lfb'], 0.0)
                                    - jnp.where(m['eye'], x['igb'], 0.0))
            x['qk'] = _dot_nt(x['q'], x['k'])
        for (d, gi), x in it.items():
            m = masks[d]
            dm = x['cum_i'] - x['cum_j']
            x['b'] = x['cum_i'][:, 0:1]
            x['tot'], tot_col = _per_chunk(x['lf'], jnp.sum)
            x['m_intra'] = jnp.max(jnp.where(m['incl'], dm, -jnp.inf), axis=-1, keepdims=True)
            e = tot_col - x['b'] + x['ig']
            x['m_end'], m_end_col = _per_chunk(e, jnp.max)
            p = jnp.where(m['incl'], jnp.exp(jnp.where(m['incl'], dm - x['m_intra'], 0.0)), 0.0) * x['qk']
            x['p_sum'] = jnp.sum(p, axis=-1, keepdims=True)
            x['p'] = p
            x['kw'] = x['k'] * jnp.exp(e - m_end_col)
        for x in it.values():
            x['pv'] = _dot(x['p'], x['v'])
            x['kv'] = [_dot_tn(x['kw'][cs], x['v'][cs]) for cs in chunks]
            x['k_sum'] = [jnp.sum(x['kw'][cs], axis=0, keepdims=True) for cs in chunks]
        outs = []
        for d in (0, 1):
            C, n, m = c_st[d], n_st[d], m_st[d][:, 0:1]
            for gi, c in order[d]:
                x = it[(d, gi)]
                cs = chunks[c]
                m_t = jnp.maximum(x['b'][cs] + m, x['m_intra'][cs])
                w_inter = jnp.exp(x['b'][cs] + m - m_t)
                local = jnp.exp(x['m_intra'][cs] - m_t)
                den = (w_inter * jnp.sum(x['q'][cs] * n, axis=-1, keepdims=True) + local * x['p_sum'][cs])
                scale = 1.0 / jnp.maximum(jnp.abs(den), jnp.exp(-m_t))
                outs.append((d, gi, c, C, w_inter * scale, local * scale))
                m_new = jnp.maximum(x['tot'][c] + m, x['m_end'][c])
                carry_decay = jnp.exp(x['tot'][c] + m - m_new)
                local_new = jnp.exp(x['m_end'][c] - m_new)
                C = carry_decay * C + local_new * x['kv'][c]
                n = carry_decay * n + local_new * x['k_sum'][c]
                m = m_new
            c_st[d] = C
            n_st[d] = n
            m_st[d] = jnp.broadcast_to(m, (1, LANE))
        for d, gi, c, C, w_scale, p_scale in outs:
            x = it[(d, gi)]
            cs = chunks[c]
            rows = pl.ds(pl.multiple_of(base[d] + gi * GROUP + c * CHUNK, CHUNK), CHUNK)
            h_scr[d][rows, :] = w_scale * _dot(x['q'][cs], C) + p_scale * x['pv'][cs]
        return carry

    lax.fori_loop(0, n_steps, body, 0)
    o_ref[...] = _rms(h_f[...] + h_b[...]) * nrm_ref[...] * _sigmoid(gate_ref[...])
    if not has_state:
        cout_ref[...] = c_st[...]
        nout_ref[...] = n_st[...]
        mout_ref[...] = m_st[...]


def _mlstm(P, bias_i, bias_f, norm, *, B, L, state=None):
    T = B * L
    has_state = state is not None
    smem = pl.BlockSpec(memory_space=pltpu.SMEM)
    in_specs = [smem, smem, _col_spec(L, COL_MQ), _col_spec(L, COL_MK), _col_spec(L, COL_MV),
                _col_spec(L, COL_MG), _col_spec(L, COL_SM), pl.BlockSpec((1, LANE), lambda b, h: (0, 0))]
    args = [bias_i, bias_f, P, P, P, P, P, norm]
    out_shape = [jax.ShapeDtypeStruct((T, HEADS * LANE), F32)]
    out_specs = [pl.BlockSpec((L, LANE), lambda b, h: (b, h))]
    vec_spec = pl.BlockSpec((None, None, 2, 1, LANE), lambda b, h: (b, h, 0, 0, 0))
    if has_state:
        c_arr, l, n_arr, m_arr = state
        in_specs += [pl.BlockSpec((None, None, 2, None, HEAD_DIM, HEAD_DIM), lambda b, h: (b, l, 0, h, 0, 0)),
                     vec_spec, vec_spec]
        args += [c_arr, n_arr, m_arr]
    else:
        out_shape += [jax.ShapeDtypeStruct((B, 2, HEADS, HEAD_DIM, HEAD_DIM), F32),
                      jax.ShapeDtypeStruct((B, HEADS, 2, 1, LANE), F32),
                      jax.ShapeDtypeStruct((B, HEADS, 2, 1, LANE), F32)]
        out_specs += [pl.BlockSpec((None, 2, None, HEAD_DIM, HEAD_DIM), lambda b, h: (b, 0, h, 0, 0)),
                      vec_spec, vec_spec]
    seq = pltpu.VMEM((L, LANE), F32)
    return pl.pallas_call(
        functools.partial(_mlstm_kernel, L=L, has_state=has_state),
        out_shape=out_shape,
        grid=(B, HEADS),
        in_specs=in_specs,
        out_specs=out_specs,
        scratch_shapes=[seq, seq, pltpu.VMEM((2, HEAD_DIM, HEAD_DIM), F32), pltpu.VMEM((2, 1, LANE), F32),
                        pltpu.VMEM((2, 1, LANE), F32)],
        compiler_params=_params("parallel", "parallel"),
        name="mlstm",
    )(*args)


def _merge_kernel(x_ref, gates_ref, a_ref, b_ref, c_ref, d_ref, wb_ref, wo_ref, g1_ref, gate1_ref, g2_ref,
                  sh2_ref, sc2_ref, rw_ref, rb_ref, xo_ref, h2_ref, lg_ref):
    merged = None
    for i, br in enumerate((a_ref, b_ref, c_ref, d_ref)):
        term = gates_ref[:, i * D_MODEL:(i + 1) * D_MODEL] * _dot(br[...], wb_ref[i])
        merged = term if merged is None else merged + term
    t = _rms(_dot(merged, wo_ref[...])) * g1_ref[...]
    x = x_ref[...] + gate1_ref[0] * t
    xo_ref[...] = x
    h2 = _rms(x) * g2_ref[...] * (1.0 + sc2_ref[0]) + sh2_ref[0]
    h2_ref[...] = h2
    lg_ref[...] = _dot3(h2, rw_ref[...]) + rb_ref[...]


def _merge(x, gates, branches, wb, wo, g1, gate1, g2, shift2, scale2, rw, rb, *, rows_per_mod, tm):
    T = x.shape[0]
    tm = min(tm, rows_per_mod)
    row = lambda n: pl.BlockSpec((tm, n), lambda i: (i, 0))
    const = lambda *shape: pl.BlockSpec(shape, lambda i: (0,) * len(shape))
    mod = pl.BlockSpec((1, 1, D_MODEL), lambda i: ((i * tm) // rows_per_mod, 0, 0))
    return pl.pallas_call(
        _merge_kernel,
        out_shape=[jax.ShapeDtypeStruct((T, D_MODEL), F32), jax.ShapeDtypeStruct((T, D_MODEL), F32),
                   jax.ShapeDtypeStruct((T, LANE), F32)],
        grid=(T // tm,),
        in_specs=[row(D_MODEL), row(N_BRANCH * D_MODEL)] + [row(BRANCH_WIDTH)] * 4
                 + [const(N_BRANCH, BRANCH_WIDTH, D_MODEL), const(D_MODEL, D_MODEL), const(1, D_MODEL), mod,
                    const(1, D_MODEL), mod, mod, const(D_MODEL, LANE), const(1, LANE)],
        out_specs=[row(D_MODEL), row(D_MODEL), row(LANE)],
        compiler_params=_params("parallel"),
        name="merge",
    )(x, gates, *branches, wb, wo, g1, gate1, g2, shift2, scale2, rw, rb)


def _moe_kernel(be_ref, nu_ref, nv_ref, idx_hbm, h2_hbm, wgu_ref, bgu_ref, wdn_ref, bdn_ref, out_hbm,
                idx_smem, xbuf, ybuf, pad_rows, sem_idx, sem_in, sem_out):
    i = pl.program_id(0)
    n_used = nu_ref[0]
    slot = i % 2
    other = 1 - slot

    def idx_copy(block, s):
        return pltpu.make_async_copy(idx_hbm.at[block], idx_smem.at[s], sem_idx.at[s])

    def row_in(s, r, tok):
        return pltpu.make_async_copy(h2_hbm.at[pl.ds(tok, 1), :], xbuf.at[s, pl.ds(r, 1), :], sem_in.at[s])

    def row_out(s, r, dst):
        return pltpu.make_async_copy(ybuf.at[s, pl.ds(r, 1), :], out_hbm.at[pl.ds(dst, 1), :], sem_out.at[s])

    def row_drop(s, r):
        return pltpu.make_async_copy(ybuf.at[s, pl.ds(r, 1), :], pad_rows.at[s, pl.ds(r, 1), :], sem_out.at[s])

    def start_gather(s):
        def body(r, carry):
            row_in(s, r, idx_smem[s, 0, r]).start()
            return carry
        lax.fori_loop(0, MOE_ROWS, body, 0, unroll=8)

    def start_scatter(s, n_valid):
        def keep(r, carry):
            row_out(s, r, idx_smem[s, 1, r]).start()
            return carry

        def drop(r, carry):
            row_drop(s, r).start()
            return carry
        lax.fori_loop(0, n_valid, keep, 0)
        lax.fori_loop(n_valid, MOE_ROWS, drop, 0)

    def wait_rows(copy, s):
        for _ in range(MOE_ROWS):
            copy(s, 0, 0).wait()

    @pl.when(i == 0)
    def _():
        idx_copy(0, 0).start()
        idx_copy(0, 0).wait()
        start_gather(0)

    @pl.when(i < n_used)
    def _():
        @pl.when(i + 1 < n_used)
        def _():
            idx_copy(i + 1, other).start()

        wait_rows(row_in, slot)

        @pl.when(i + 1 < n_used)
        def _():
            idx_copy(i + 1, other).wait()
            start_gather(other)

        @pl.when(i >= 2)
        def _():
            wait_rows(row_out, slot)

        hgu = _dot(xbuf[slot], wgu_ref[...]) + bgu_ref[...]
        hg = jnp.minimum(hgu[:, :D_EXPERT], SWIGLU_LIMIT)
        hu = jnp.clip(hgu[:, D_EXPERT:], -SWIGLU_LIMIT, SWIGLU_LIMIT)
        act = (hu + 1.0) * (hg * _sigmoid(SWIGLU_ALPHA * hg))
        ybuf[slot] = _dot(act, wdn_ref[...]) + bdn_ref[...]
        start_scatter(slot, nv_ref[i])

        @pl.when(i == n_used - 1)
        def _():
            wait_rows(row_out, slot)

            @pl.when(i >= 1)
            def _():
                wait_rows(row_out, other)


def _moe_blocks(idx, h2, block_e, n_used, n_valid, w_gu, b_gu, w_dn, b_dn, n_out_rows):
    n_blocks = idx.shape[0]
    exp = lambda i, be, nu, nv: (be[jnp.minimum(i, nu[0] - 1)], 0, 0)
    any_spec = pl.BlockSpec(memory_space=pl.ANY)
    return pl.pallas_call(
        _moe_kernel,
        out_shape=jax.ShapeDtypeStruct((n_out_rows, D_MODEL), F32),
        grid_spec=pltpu.PrefetchScalarGridSpec(
            num_scalar_prefetch=3,
            grid=(n_blocks,),
            in_specs=[
                any_spec,
                any_spec,
                pl.BlockSpec((None, D_MODEL, 2 * D_EXPERT), exp),
                pl.BlockSpec((None, 1, 2 * D_EXPERT), exp),
                pl.BlockSpec((None, D_EXPERT, D_MODEL), exp),
                pl.BlockSpec((None, 1, D_MODEL), exp),
            ],
            out_specs=any_spec,
            scratch_shapes=[pltpu.SMEM((2, 2, MOE_ROWS), jnp.int32),
                            pltpu.VMEM((2, MOE_ROWS, D_MODEL), F32),
                            pltpu.VMEM((2, MOE_ROWS, D_MODEL), F32),
                            pltpu.VMEM((2, MOE_ROWS, D_MODEL), F32),
                            pltpu.SemaphoreType.DMA((2,)),
                            pltpu.SemaphoreType.DMA((2,)),
                            pltpu.SemaphoreType.DMA((2,))],
        ),
        compiler_params=_params("arbitrary"),
        name="moe_experts",
    )(block_e, n_used, n_valid, idx, h2, w_gu, b_gu, w_dn, b_dn)


def _moe(h2, logits, w_gu, b_gu, w_dn, b_dn):
    N = h2.shape[0]
    NK = N * TOP_K
    top_val, top_idx = lax.top_k(logits, TOP_K)
    gate = jax.nn.softmax(top_val, axis=-1)
    flat_e = top_idx.reshape(-1)
    order = jnp.argsort(flat_e).astype(jnp.int32)
    counts = jnp.sum((flat_e[:, None] == jnp.arange(N_EXPERTS)[None, :]).astype(jnp.int32), axis=0)
    padded = (counts + MOE_ROWS - 1) // MOE_ROWS * MOE_ROWS
    pad_end = jnp.cumsum(padded)
    pad_start = pad_end - padded
    start = jnp.cumsum(counts) - counts
    n_blocks = (NK + N_EXPERTS * (MOE_ROWS - 1) + MOE_ROWS - 1) // MOE_ROWS
    P = n_blocks * MOE_ROWS
    block_e = jnp.minimum(jnp.searchsorted(pad_end, jnp.arange(n_blocks) * MOE_ROWS, side='right'),
                          N_EXPERTS - 1).astype(jnp.int32)
    n_used = (pad_end[-1] // MOE_ROWS).astype(jnp.int32).reshape(1)
    pos = jnp.arange(P, dtype=jnp.int32)
    e_pos = jnp.repeat(block_e, MOE_ROWS)
    rank = pos - pad_start[e_pos].astype(jnp.int32)
    valid = rank < counts[e_pos]
    flat = order[jnp.clip(start[e_pos].astype(jnp.int32) + rank, 0, NK - 1)]
    tok = jnp.where(valid, flat // TOP_K, 0)
    dst = jnp.where(valid, (flat % TOP_K) * N + flat // TOP_K, 0)
    idx = jnp.stack([tok.reshape(n_blocks, MOE_ROWS), dst.reshape(n_blocks, MOE_ROWS)], axis=1)
    n_valid = jnp.sum(valid.reshape(n_blocks, MOE_ROWS).astype(jnp.int32), axis=1)
    expert_out = _moe_blocks(idx, h2, block_e, n_used, n_valid, w_gu, b_gu, w_dn, b_dn, NK)
    return expert_out, gate


def _combine_kernel(x_ref, y0_ref, y1_ref, y2_ref, y3_ref, w_ref, g_ref, gate_ref, o_ref):
    w = w_ref[...]
    y = None
    for k, y_ref in enumerate((y0_ref, y1_ref, y2_ref, y3_ref)):
        term = y_ref[...] * w[:, k:k + 1]
        y = term if y is None else y + term
    o_ref[...] = x_ref[...] + gate_ref[0] * (_rms(y) * g_ref[...])


def _combine(x, expert_out, weights, g, gate, *, rows_per_mod, tm):
    T = x.shape[0]
    tm = min(tm, rows_per_mod)
    row = pl.BlockSpec((tm, D_MODEL), lambda i: (i, 0))
    slabs = [pl.BlockSpec((tm, D_MODEL), lambda i, k=k: (k * (T // tm) + i, 0)) for k in range(TOP_K)]
    return pl.pallas_call(
        _combine_kernel,
        out_shape=jax.ShapeDtypeStruct((T, D_MODEL), F32),
        grid=(T // tm,),
        in_specs=[row] + slabs + [pl.BlockSpec((tm, TOP_K), lambda i: (i, 0)),
                                  pl.BlockSpec((1, D_MODEL), lambda i: (0, 0)),
                                  pl.BlockSpec((1, 1, D_MODEL), lambda i: ((i * tm) // rows_per_mod, 0, 0))],
        out_specs=row,
        compiler_params=_params("parallel"),
        name="combine",
    )(x, *[expert_out] * TOP_K, weights, g, gate)


def _rope_tables(L):
    t = jnp.arange(L)
    lane = jnp.arange(LANE)
    axis = (lane % A_DIM) // 32
    half = (lane % 32) // 16
    n_freq = A_DIM // 4
    inv = ROPE_BASE ** (-(lane % n_freq).astype(F32) / n_freq)
    pos = jnp.where(axis[None, :] == 0, (t // GRID_W)[:, None], (t % GRID_W)[:, None]).astype(F32)
    ang = pos * inv[None, :]
    return jnp.cos(ang), jnp.where(half[None, :] == 0, -jnp.sin(ang), jnp.sin(ang))


def _arrange_w_in(w_in):
    sizes = [512] * 8 + [8, 8] + [512] * 4 + [8, 8]
    offs = [0]
    for s in sizes:
        offs.append(offs[-1] + s)
    part = lambda i: w_in[:, offs[i]:offs[i + 1]]
    big = [part(i) for i in (0, 1, 2, 3, 4, 5, 6, 7, 10, 11, 12, 13)]
    small = jnp.zeros((w_in.shape[0], HEADS, LANE), w_in.dtype)
    for base, i in ((SM_BETA, 8), (SM_DEC, 9), (SM_IG, 14), (SM_FG, 15)):
        cols = part(i).reshape(-1, 2, HEADS)
        for d in range(2):
            small = small.at[:, :, base + d].set(cols[:, d, :])
    return jnp.concatenate(big + [small.reshape(w_in.shape[0], HEADS * LANE)], axis=1)


def _layer(x, cond, lidx, lp, *, B, L, cache):
    T = B * L
    is_ctx = cache is None
    mod = (jax.nn.silu(cond) @ lp['ada_w'] + lp['ada_b']).reshape(-1, 6, 1, D_MODEL)
    shift1, scale1, gate1, shift2, scale2, gate2 = [mod[:, i] for i in range(6)]
    rows_per_mod = T if mod.shape[0] == 1 else L
    g = lp['norm_g']

    tm = 1024
    P = _norm_mod_matmul(x, g[0:1], shift1, scale1, lp['w_in'], jnp.zeros((1, N_PROJ), F32),
                         rows_per_mod=rows_per_mod, sigmoid=False, tm=tm, tn=N_PROJ // 4)
    gates = _norm_mod_matmul(x, g[0:1], shift1, scale1, lp['w_bgate'], lp['b_bgate'][None],
                             rows_per_mod=rows_per_mod, sigmoid=True, tm=tm, tn=2048)

    lam_init = 0.8 - 0.6 * math.exp(-0.3 * lidx)
    lq1, lk1, lq2, lk2 = lp['attn_lambda']
    lam = (jnp.exp(jnp.sum(lq1 * lk1)) - jnp.exp(jnp.sum(lq2 * lk2)) + lam_init).reshape(1, 1)
    attn_norm = lp['attn_norm'][None]
    if is_ctx:
        br_a, new_k, new_v = _attention(P, lam, attn_norm, B=B, L=L, tq=L, lam_init=lam_init)
    else:
        (br_a,) = _attention(P, lam, attn_norm, B=B, L=L, tq=128, lam_init=lam_init, cache=cache['attn'])

    br_b = _pool(P, lp['pool_w'], lp['pool_b'][:, None], lp['pool_scale'][None], B=B, L=L)

    gdn_args = (P, lp['gdn_A_log'], lp['gdn_dt_bias'], lp['gdn_conv'], lp['gdn_norm'][None])
    ml_args = (P, lp['mlstm_bias_i'], lp['mlstm_bias_f'], lp['mlstm_norm'][None])
    if is_ctx:
        br_c, new_gdn = _gdn(*gdn_args, B=B, L=L)
        br_d, new_c, new_n, new_m = _mlstm(*ml_args, B=B, L=L)
        ctx_out = (new_k, new_v, new_gdn, new_c,
                   new_n[:, :, :, 0, :HEAD_DIM].transpose(0, 2, 1, 3), new_m[:, :, :, 0, 0].transpose(0, 2, 1))
    else:
        (br_c,) = _gdn(*gdn_args, B=B, L=L, state=cache['gdn'])
        (br_d,) = _mlstm(*ml_args, B=B, L=L, state=cache['mlstm'])
        ctx_out = None

    x, h2, logits = _merge(x, gates, (br_a, br_b, br_c, br_d), lp['w_branch'], lp['w_out'], g[1:2], gate1,
                           g[2:3], shift2, scale2, lp['router_w'], lp['router_b'],
                           rows_per_mod=rows_per_mod, tm=256)
    expert_out, weights = _moe(h2, logits[:, :N_EXPERTS], lp['moe_w_gu'], lp['moe_b_gu'], lp['moe_w_dn'],
                               lp['moe_b_dn'])
    x = _combine(x, expert_out, weights, g[3:4], gate2, rows_per_mod=rows_per_mod, tm=512)
    return x, ctx_out


def kernel(x_prompt, x_sample, cache_attn_k, cache_attn_v, state_gdn, state_mlstm_C, state_mlstm_n, state_mlstm_m, c, c_ctx, ada_w, ada_b, norm_g, w_in, w_bgate, b_bgate, w_branch, w_out, attn_lambda, attn_norm, pool_w, pool_b, pool_scale, gdn_conv, gdn_A_log, gdn_dt_bias, gdn_norm, mlstm_bias_i, mlstm_bias_f, mlstm_norm, router_w, router_b, moe_w_gu, moe_b_gu, moe_w_dn, moe_b_dn):
    Bp, Lp, _ = x_prompt.shape
    Bs, Ls, _ = x_sample.shape
    y_p = x_prompt.reshape(Bp * Lp, D_MODEL)
    y_s = x_sample.reshape(Bs * Ls, D_MODEL)
    cos, sin = _rope_tables(Ls)
    n0 = state_mlstm_n.transpose(1, 0, 3, 2, 4)[:, :, :, :, None, :]
    m0 = jnp.broadcast_to(state_mlstm_m.transpose(1, 0, 3, 2)[..., None, None], n0.shape)
    ctx_states = []
    for l in range(DEPTH):
        lp = {
            'ada_w': ada_w[l], 'ada_b': ada_b[l], 'norm_g': norm_g[l],
            'w_in': _arrange_w_in(w_in[l]).astype(BF16), 'w_bgate': w_bgate[l].astype(BF16),
            'b_bgate': b_bgate[l], 'w_branch': w_branch[l].astype(BF16), 'w_out': w_out[l].astype(BF16),
            'attn_lambda': attn_lambda[l], 'attn_norm': attn_norm[l],
            'pool_w': pool_w[l], 'pool_b': pool_b[l], 'pool_scale': pool_scale[l],
            'gdn_conv': gdn_conv[l], 'gdn_A_log': gdn_A_log[l], 'gdn_dt_bias': gdn_dt_bias[l],
            'gdn_norm': gdn_norm[l], 'mlstm_bias_i': mlstm_bias_i[l], 'mlstm_bias_f': mlstm_bias_f[l],
            'mlstm_norm': mlstm_norm[l],
            'router_w': jnp.pad(router_w[l], ((0, 0), (0, LANE - N_EXPERTS))),
            'router_b': jnp.pad(router_b[l], (0, LANE - N_EXPERTS))[None],
            'moe_w_gu': moe_w_gu[l].astype(BF16), 'moe_b_gu': moe_b_gu[l][:, None],
            'moe_w_dn': moe_w_dn[l].astype(BF16), 'moe_b_dn': moe_b_dn[l][:, None],
        }
        y_p, st = _layer(y_p, c_ctx[None], l, lp, B=Bp, L=Lp, cache=None)
        ctx_states.append(st)
        cache = {
            'attn': {'k': cache_attn_k, 'v': cache_attn_v, 'layer': l, 'cos': cos, 'sin': sin},
            'gdn': (state_gdn, l),
            'mlstm': (state_mlstm_C, l, n0[l], m0[l]),
        }
        y_s, _ = _layer(y_s, c, l, lp, B=Bs, L=Ls, cache=cache)
    outs = [jnp.stack([s[i] for s in ctx_states], axis=1) for i in range(6)]
    return (y_p.reshape(Bp, Lp, D_MODEL), y_s.reshape(Bs, Ls, D_MODEL), *outs)
```

```python
import functools
import math

import jax
import jax.numpy as jnp
from jax import lax
from jax.experimental import pallas as pl
from jax.experimental.pallas import tpu as pltpu

F32 = jnp.float32
BF16 = jnp.bfloat16

D_MODEL = 1024
DEPTH = 2
GRID_W = 64
EPS = 1e-6
LANE = 128
HEADS = 4
HEAD_DIM = 128
A_DIM = 64
ROPE_BASE = 10000.0
POOL_WINDOWS = (2, 4, 8, 16)
CHUNK = 64
N_BRANCH = 4
BRANCH_WIDTH = 512
N_EXPERTS = 32
TOP_K = 4
D_EXPERT = 1024
SWIGLU_LIMIT = 7.0
SWIGLU_ALPHA = 1.702
MOE_ROWS = 256
VMEM_LIMIT = 56 * 1024 * 1024

COL_AQ, COL_AK, COL_AV, COL_POOL, COL_CQ, COL_CK, COL_CV, COL_CG, COL_MQ, COL_MK, COL_MV, COL_MG, COL_SM = range(13)
N_PROJ = 13 * BRANCH_WIDTH
SM_BETA, SM_DEC, SM_IG, SM_FG = 0, 2, 4, 6


def _params(*sem):
    return pltpu.CompilerParams(dimension_semantics=sem, vmem_limit_bytes=VMEM_LIMIT)


def _dot(a, b):
    return jnp.dot(a.astype(BF16), b.astype(BF16), preferred_element_type=F32)


def _dot_nt(a, b):
    return lax.dot_general(a.astype(BF16), b.astype(BF16), (((1,), (1,)), ((), ())),
                           preferred_element_type=F32)


def _dot_tn(a, b):
    return lax.dot_general(a.astype(BF16), b.astype(BF16), (((0,), (0,)), ((), ())),
                           preferred_element_type=F32)


def _split(a):
    hi = a.astype(BF16)
    lo = (a - hi.astype(F32)).astype(BF16)
    return hi, lo


def _dot3(a, b):
    ah, al = _split(a)
    bh, bl = _split(b)
    d = lambda x, y: jnp.dot(x, y, preferred_element_type=F32)
    return d(ah, bh) + (d(ah, bl) + d(al, bh))


def _dot_mask(mask, b):
    m = jnp.where(mask, 1.0, 0.0).astype(BF16)
    b0 = b.astype(BF16)
    r1 = b - b0.astype(F32)
    b1 = r1.astype(BF16)
    b2 = (r1 - b1.astype(F32)).astype(BF16)
    d = lambda y: jnp.dot(m, y, preferred_element_type=F32)
    return d(b0) + (d(b1) + d(b2))


def _rms(x):
    return x * lax.rsqrt(jnp.mean(x * x, axis=-1, keepdims=True) + EPS)


def _sigmoid(x):
    return 1.0 / (1.0 + jnp.exp(-x))


def _softplus(x):
    return jnp.maximum(x, 0.0) + jnp.log(1.0 + jnp.exp(-jnp.abs(x)))


def _proj_kernel(x_ref, g_ref, sh_ref, sc_ref, w_ref, b_ref, o_ref, h_scr, *, sigmoid):
    @pl.when(pl.program_id(1) == 0)
    def _():
        y = _rms(x_ref[...]) * g_ref[...]
        h_scr[...] = (y * (1.0 + sc_ref[0]) + sh_ref[0]).astype(BF16)

    acc = jnp.dot(h_scr[...], w_ref[...], preferred_element_type=F32) + b_ref[...]
    o_ref[...] = _sigmoid(acc) if sigmoid else acc


def _norm_mod_matmul(x, g, shift, scale, w, bias, *, rows_per_mod, sigmoid, tm, tn):
    T, Dm = x.shape
    N = w.shape[1]
    tm = min(tm, rows_per_mod)
    mod_idx = lambda i, j: ((i * tm) // rows_per_mod, 0, 0)
    return pl.pallas_call(
        functools.partial(_proj_kernel, sigmoid=sigmoid),
        out_shape=jax.ShapeDtypeStruct((T, N), F32),
        grid=(T // tm, N // tn),
        in_specs=[
            pl.BlockSpec((tm, Dm), lambda i, j: (i, 0)),
            pl.BlockSpec((1, Dm), lambda i, j: (0, 0)),
            pl.BlockSpec((1, 1, Dm), mod_idx),
            pl.BlockSpec((1, 1, Dm), mod_idx),
            pl.BlockSpec((Dm, tn), lambda i, j: (0, j)),
            pl.BlockSpec((1, tn), lambda i, j: (0, j)),
        ],
        out_specs=pl.BlockSpec((tm, tn), lambda i, j: (i, j)),
        scratch_shapes=[pltpu.VMEM((tm, Dm), BF16)],
        compiler_params=_params("parallel", "arbitrary"),
        name="proj",
    )(x, g, shift, scale, w, bias)


def _rope(x, cos, sin):
    lane = lax.broadcasted_iota(jnp.int32, x.shape, 1)
    first = (lane % 32) < 16
    partner = jnp.where(first, pltpu.roll(x, LANE - 16, 1), pltpu.roll(x, 16, 1))
    return x * cos + partner * sin


def _attn_kernel(*refs, has_cache, n_ctx, out_scale):
    if has_cache:
        (lam_ref, q_ref, k_ref, v_ref, ck_ref, cv_ref, cosq_ref, sinq_ref, cosk_ref, sink_ref, nrm_ref,
         o_ref, kall, vall) = refs
    else:
        lam_ref, q_ref, k_ref, v_ref, nrm_ref, o_ref, ko_ref, vo_ref, kall, vall = refs

    @pl.when(pl.program_id(2) == 0)
    def _():
        k = k_ref[...]
        v = v_ref[...]
        if has_cache:
            kall[pl.ds(0, n_ctx), :] = ck_ref[...].astype(BF16)
            vall[pl.ds(0, n_ctx), :] = cv_ref[...].astype(BF16)
            k = _rope(k, cosk_ref[...], sink_ref[...])
        else:
            ko_ref[...] = k
            vo_ref[...] = v
        kall[pl.ds(n_ctx, k.shape[0]), :] = k.astype(BF16)
        vall[pl.ds(n_ctx, k.shape[0]), :] = v.astype(BF16)

    q = q_ref[...]
    if has_cache:
        q = _rope(q, cosq_ref[...], sinq_ref[...])
    q = q * (A_DIM ** -0.5)
    lane = lax.broadcasted_iota(jnp.int32, q.shape, 1)
    q1 = jnp.where(lane < A_DIM, q, 0.0)
    q2 = jnp.where(lane >= A_DIM, q, 0.0)
    keys = kall[...]
    lam = lam_ref[0, 0]

    def probs(qm):
        s = _dot_nt(qm, keys)
        p = jnp.exp(s - jnp.max(s, axis=-1, keepdims=True))
        return p, jnp.sum(p, axis=-1, keepdims=True)

    p1, l1 = probs(q1)
    p2, l2 = probs(q2)
    a = p1 * (1.0 / l1) - p2 * (lam / l2)
    o = jnp.dot(a.astype(BF16), vall[...], preferred_element_type=F32)
    o_ref[...] = _rms(o) * nrm_ref[...] * out_scale


def _attention(P, lam, norm, *, B, L, tq, lam_init, cache=None):
    T = B * L
    nq = L // tq
    has_cache = cache is not None
    n_ctx = cache["k"].shape[3] if has_cache else 0
    colq, colk, colv = COL_AQ * HEADS, COL_AK * HEADS, COL_AV * HEADS
    in_specs = [
        pl.BlockSpec(memory_space=pltpu.SMEM),
        pl.BlockSpec((tq, LANE), lambda b, h, i: (b * nq + i, colq + h)),
        pl.BlockSpec((L, LANE), lambda b, h, i: (b, colk + h)),
        pl.BlockSpec((L, LANE), lambda b, h, i: (b, colv + h)),
    ]
    args = [lam, P, P, P]
    if has_cache:
        l = cache["layer"]
        cspec = pl.BlockSpec((None, None, None, n_ctx, LANE), lambda b, h, i: (b, l, h, 0, 0))
        in_specs += [cspec, cspec,
                     pl.BlockSpec((tq, LANE), lambda b, h, i: (i, 0)),
                     pl.BlockSpec((tq, LANE), lambda b, h, i: (i, 0)),
                     pl.BlockSpec((L, LANE), lambda b, h, i: (0, 0)),
                     pl.BlockSpec((L, LANE), lambda b, h, i: (0, 0))]
        args += [cache["k"], cache["v"], cache["cos"], cache["sin"], cache["cos"], cache["sin"]]
    in_specs.append(pl.BlockSpec((1, LANE), lambda b, h, i: (0, 0)))
    args.append(norm)
    out_shape = [jax.ShapeDtypeStruct((T, HEADS * LANE), F32)]
    out_specs = [pl.BlockSpec((tq, LANE), lambda b, h, i: (b * nq + i, h))]
    if not has_cache:
        kv_shape = jax.ShapeDtypeStruct((B, HEADS, L, LANE), F32)
        kv_spec = pl.BlockSpec((None, None, L, LANE), lambda b, h, i: (b, h, 0, 0))
        out_shape += [kv_shape, kv_shape]
        out_specs += [kv_spec, kv_spec]
    return pl.pallas_call(
        functools.partial(_attn_kernel, has_cache=has_cache, n_ctx=n_ctx, out_scale=1.0 - lam_init),
        out_shape=out_shape,
        grid=(B, HEADS, nq),
        in_specs=in_specs,
        out_specs=out_specs,
        scratch_shapes=[pltpu.VMEM((n_ctx + L, LANE), BF16), pltpu.VMEM((n_ctx + L, LANE), BF16)],
        compiler_params=_params("parallel", "parallel", "arbitrary"),
        name="diff_attention",
    )(*args)


POOL_PAD = 16


def _pool_kernel(x_ref, w_ref, b_ref, s_ref, o_ref, pad, *, L):
    zeros = jnp.zeros((POOL_PAD, LANE), F32)
    pad[pl.ds(0, POOL_PAD), :] = zeros
    pad[pl.ds(POOL_PAD + L, POOL_PAD), :] = zeros
    x = x_ref[...]
    pad[pl.ds(POOL_PAD, L), :] = x
    t = lax.broadcasted_iota(jnp.int32, (L, LANE), 0)
    g = pl.program_id(1)
    for gi, win in enumerate(POOL_WINDOWS):
        @pl.when(g == gi)
        def _(win=win):
            half = win // 2
            acc = pad[pl.ds(POOL_PAD - half, L), :]
            for k in range(1 - half, half):
                acc = acc + pad[pl.ds(POOL_PAD + k, L), :]
            cnt = (jnp.minimum(t + half, L) - jnp.maximum(t - half, 0)).astype(F32)
            pooled = acc / cnt - x
            o_ref[...] = (_dot(pooled, w_ref[...]) + b_ref[...]) * s_ref[...]


def _pool(P, w, b, scale, *, B, L):
    T = B * L
    G = len(POOL_WINDOWS)
    return pl.pallas_call(
        functools.partial(_pool_kernel, L=L),
        out_shape=jax.ShapeDtypeStruct((T, G * LANE), F32),
        grid=(B, G),
        in_specs=[
            pl.BlockSpec((L, LANE), lambda bi, g: (bi, COL_POOL * HEADS + g)),
            pl.BlockSpec((None, LANE, LANE), lambda bi, g: (g, 0, 0)),
            pl.BlockSpec((None, 1, LANE), lambda bi, g: (g, 0, 0)),
            pl.BlockSpec((1, LANE), lambda bi, g: (0, g)),
        ],
        out_specs=pl.BlockSpec((L, LANE), lambda bi, g: (bi, g)),
        scratch_shapes=[pltpu.VMEM((L + 2 * POOL_PAD, LANE), F32)],
        compiler_params=_params("parallel", "parallel"),
        name="pool_mixer",
    )(P, w, b, scale)


GROUP_CHUNKS = 2
GROUP = GROUP_CHUNKS * CHUNK
GROUPS_PER_STEP = 2
STEP_ROWS = GROUPS_PER_STEP * GROUP


def _group_masks(backward):
    row = lax.broadcasted_iota(jnp.int32, (GROUP, GROUP), 0)
    col = lax.broadcasted_iota(jnp.int32, (GROUP, GROUP), 1)
    same = (row // CHUNK) == (col // CHUNK)
    lower, upper = row >= col, row <= col
    if backward:
        lower, upper = upper, lower
    return dict(same=same, incl=same & lower, strict=same & lower & (row != col), incl_t=same & upper,
                eye=row == col)


INV_BASE = 8


def _block_masks():
    row = lax.broadcasted_iota(jnp.int32, (GROUP, GROUP), 0)
    col = lax.broadcasted_iota(jnp.int32, (GROUP, GROUP), 1)
    sizes = [INV_BASE << i for i in range(int(math.log2(CHUNK // INV_BASE)) + 1)]
    same = [(row // s) == (col // s) for s in sizes]
    return [same[0]] + [cur & jnp.logical_not(prev) for prev, cur in zip(same[:-1], same[1:])]


def _unit_triangular_inverses(ns, eye, blocks):
    base = [jnp.where(blocks[0], n, 0.0) for n in ns]
    invs = [eye - b for b in base]
    pws = [_dot(b, b) for b in base]
    rounds = int(math.log2(INV_BASE)) - 1
    for s in range(rounds):
        invs = [t + _dot(t, p) for t, p in zip(invs, pws)]
        if s + 1 < rounds:
            pws = [_dot(p, p) for p in pws]
    for join in blocks[1:]:
        offs = [jnp.where(join, n, 0.0) for n in ns]
        xs = [_dot(o, t) for o, t in zip(offs, invs)]
        invs = [t - _dot(t, x) for t, x in zip(invs, xs)]
    return invs


def _dot_mask2(mask, b):
    m = jnp.where(mask, 1.0, 0.0).astype(BF16)
    b0, b1 = _split(b)
    return jnp.dot(m, b0, preferred_element_type=F32) + jnp.dot(m, b1, preferred_element_type=F32)


def _per_chunk(x, reduce):
    parts = [reduce(x[c * CHUNK:(c + 1) * CHUNK], axis=0, keepdims=True) for c in range(GROUP_CHUNKS)]
    col = jnp.concatenate([jnp.broadcast_to(p, (CHUNK, 1)) for p in parts], axis=0)
    return parts, col


def _scan_order(step, n_steps):
    base = (step * STEP_ROWS, (n_steps - 1 - step) * STEP_ROWS)
    fwd = [(gi, c) for gi in range(GROUPS_PER_STEP) for c in range(GROUP_CHUNKS)]
    return base, (fwd, fwd[::-1])


CONV_PAD = 8


def _gdn_kernel(*refs, L, has_state):
    alog_ref, dtb_ref, q_ref, k_ref, v_ref, gate_ref, sm_ref, cwq_ref, cwk_ref, cwv_ref, nrm_ref = refs[:11]
    refs = refs[11:]
    if has_state:
        s0_ref, o_ref, pad, qs, ks, vs, o_f, o_b, state = refs
    else:
        o_ref, sout_ref, pad, qs, ks, vs, o_f, o_b, state = refs
    h = pl.program_id(1)
    n_chunks = L // CHUNK

    zeros = jnp.zeros((CONV_PAD, LANE), F32)
    pad[pl.ds(0, CONV_PAD), :] = zeros
    pad[pl.ds(CONV_PAD + L, CONV_PAD), :] = zeros

    def conv_silu(x_ref, w_ref):
        pad[pl.ds(CONV_PAD, L), :] = x_ref[...]
        w = w_ref[...]
        y = (pad[pl.ds(CONV_PAD - 1, L), :] * w[0:1] + pad[pl.ds(CONV_PAD, L), :] * w[1:2]
             + pad[pl.ds(CONV_PAD + 1, L), :] * w[2:3])
        return y * _sigmoid(y)

    def l2n(x):
        return x * lax.rsqrt(jnp.sum(x * x, axis=-1, keepdims=True) + EPS)

    qs[...] = l2n(conv_silu(q_ref, cwq_ref)) * (HEAD_DIM ** -0.5)
    ks[...] = l2n(conv_silu(k_ref, cwk_ref))
    vs[...] = conv_silu(v_ref, cwv_ref)
    if has_state:
        state[...] = s0_ref[...]
    else:
        state[...] = jnp.zeros_like(state)

    n_steps = L // STEP_ROWS
    o_scr = (o_f, o_b)
    masks = (_group_masks(False), _group_masks(True))
    eye = jnp.where(masks[0]['eye'], 1.0, 0.0)
    blocks = _block_masks()
    items = [(d, gi) for gi in range(GROUPS_PER_STEP) for d in (0, 1)]
    chunks = [slice(c * CHUNK, (c + 1) * CHUNK) for c in range(GROUP_CHUNKS)]

    def body(step, carry):
        base, order = _scan_order(step, n_steps)
        it = {}
        for key in items:
            d, gi = key
            rows = pl.ds(pl.multiple_of(base[d] + gi * GROUP, GROUP), GROUP)
            q, k, v, sm = qs[rows, :], ks[rows, :], vs[rows, :], sm_ref[rows, :]
            beta = _sigmoid(sm[:, SM_BETA + d:SM_BETA + d + 1])
            dec = sm[:, SM_DEC + d:SM_DEC + d + 1]
            g = -jnp.exp(alog_ref[d, h]) * _softplus(dec + dtb_ref[d, h])
            it[key] = dict(q=q, k=k, kb=k * beta, vb=v * beta, g=g, gb=jnp.broadcast_to(g, (GROUP, GROUP)))
        for (d, gi), x in it.items():
            m = masks[d]
            x['cum_i'] = _dot_mask2(m['incl'], x['gb'])
            x['cum_j'] = _dot_mask2(m['same'], jnp.where(m['incl_t'], x['gb'], 0.0))
            x['kk'] = _dot_nt(x['kb'], x['k'])
            x['qk'] = _dot_nt(x['q'], x['k'])
        for (d, gi), x in it.items():
            m = masks[d]
            decay = jnp.where(m['incl'], jnp.exp(jnp.where(m['incl'], x['cum_i'] - x['cum_j'], 0.0)), 0.0)
            x['gc'] = x['cum_i'][:, 0:1]
            x['tot'], x['tot_col'] = _per_chunk(x['g'], jnp.sum)
            x['qk'] = x['qk'] * decay
            x['n'] = jnp.where(m['strict'], x['kk'] * decay, 0.0)
        invs = _unit_triangular_inverses([x['n'] for x in it.values()], eye, blocks)
        for x, inv in zip(it.values(), invs):
            x['inv'] = inv
            x['egc'] = jnp.exp(x['gc'])
            x['sol'] = _dot3(x['inv'], jnp.concatenate([x['vb'], x['kb'] * x['egc']], axis=-1))
        for x in it.values():
            u, w = x['sol'][:, :HEAD_DIM], x['sol'][:, HEAD_DIM:]
            x['qp'] = x['q'] * x['egc'] - _dot(x['qk'], w)
            x['op'] = _dot(x['qk'], u)
            kd = x['k'] * jnp.exp(x['tot_col'] - x['gc'])
            x['ab'] = [_dot_tn(kd[cs], x['sol'][cs]) for cs in chunks]
            x['gl'] = [jnp.exp(t) for t in x['tot']]
        S = [state[0], state[1]]
        for stp in range(len(order[0])):
            for d in (0, 1):
                gi, c = order[d][stp]
                x = it[(d, gi)]
                ab = x['ab'][c]
                r = _dot(jnp.concatenate([ab[:, HEAD_DIM:], x['qp'][chunks[c]]], axis=0), S[d])
                rows = pl.ds(pl.multiple_of(base[d] + gi * GROUP + c * CHUNK, CHUNK), CHUNK)
                o_scr[d][rows, :] = x['op'][chunks[c]] + r[HEAD_DIM:]
                S[d] = x['gl'][c] * S[d] + (ab[:, :HEAD_DIM] - r[:HEAD_DIM])
        state[0] = S[0]
        state[1] = S[1]
        return carry

    lax.fori_loop(0, n_steps, body, 0)
    gate = gate_ref[...]
    o_ref[...] = _rms(o_f[...] + o_b[...]) * nrm_ref[...] * (gate * _sigmoid(gate))
    if not has_state:
        sout_ref[...] = state[...]


def _col_spec(L, col):
    return pl.BlockSpec((L, LANE), lambda b, h: (b, col * HEADS + h))


def _gdn(P, a_log, dt_bias, conv_w, norm, *, B, L, state=None):
    T = B * L
    has_state = state is not None
    smem = pl.BlockSpec(memory_space=pltpu.SMEM)
    in_specs = [smem, smem, _col_spec(L, COL_CQ), _col_spec(L, COL_CK), _col_spec(L, COL_CV),
                _col_spec(L, COL_CG), _col_spec(L, COL_SM)]
    in_specs += [pl.BlockSpec((3, LANE), lambda b, h, j=j: (0, j * HEADS + h)) for j in range(3)]
    in_specs.append(pl.BlockSpec((1, LANE), lambda b, h: (0, 0)))
    args = [a_log, dt_bias, P, P, P, P, P, conv_w, conv_w, conv_w, norm]
    out_shape = [jax.ShapeDtypeStruct((T, HEADS * LANE), F32)]
    out_specs = [pl.BlockSpec((L, LANE), lambda b, h: (b, h))]
    if has_state:
        arr, l = state
        in_specs.append(pl.BlockSpec((None, None, 2, None, HEAD_DIM, HEAD_DIM), lambda b, h: (b, l, 0, h, 0, 0)))
        args.append(arr)
    else:
        out_shape.append(jax.ShapeDtypeStruct((B, 2, HEADS, HEAD_DIM, HEAD_DIM), F32))
        out_specs.append(pl.BlockSpec((None, 2, None, HEAD_DIM, HEAD_DIM), lambda b, h: (b, 0, h, 0, 0)))
    seq = pltpu.VMEM((L, LANE), F32)
    return pl.pallas_call(
        functools.partial(_gdn_kernel, L=L, has_state=has_state),
        out_shape=out_shape,
        grid=(B, HEADS),
        in_specs=in_specs,
        out_specs=out_specs,
        scratch_shapes=[pltpu.VMEM((L + 2 * CONV_PAD, LANE), F32), seq, seq, seq, seq, seq,
                        pltpu.VMEM((2, HEAD_DIM, HEAD_DIM), F32)],
        compiler_params=_params("parallel", "parallel"),
        name="gated_deltanet",
    )(*args)


def _mlstm_kernel(*refs, L, has_state):
    bi_ref, bf_ref, q_ref, k_ref, v_ref, gate_ref, sm_ref, nrm_ref = refs[:8]
    refs = refs[8:]
    if has_state:
        c0_ref, n0_ref, m0_ref, o_ref, h_f, h_b, c_st, n_st, m_st = refs
    else:
        o_ref, cout_ref, nout_ref, mout_ref, h_f, h_b, c_st, n_st, m_st = refs
    h = pl.program_id(1)
    n_chunks = L // CHUNK

    if has_state:
        c_st[...] = c0_ref[...]
        n_st[...] = n0_ref[...]
        m_st[...] = m0_ref[...]
    else:
        c_st[...] = jnp.zeros_like(c_st)
        n_st[...] = jnp.zeros_like(n_st)
        m_st[...] = jnp.zeros_like(m_st)

    n_steps = L // STEP_ROWS
    h_scr = (h_f, h_b)
    masks = (_group_masks(False), _group_masks(True))
    items = [(d, gi) for gi in range(GROUPS_PER_STEP) for d in (0, 1)]
    chunks = [slice(c * CHUNK, (c + 1) * CHUNK) for c in range(GROUP_CHUNKS)]

    def body(step, carry):
        base, order = _scan_order(step, n_steps)
        it = {}
        for key in items:
            d, gi = key
            rows = pl.ds(pl.multiple_of(base[d] + gi * GROUP, GROUP), GROUP)
            q, v, sm = q_ref[rows, :], v_ref[rows, :], sm_ref[rows, :]
            k = k_ref[rows, :] * (HEAD_DIM ** -0.5)
            ig = sm[:, SM_IG + d:SM_IG + d + 1] + bi_ref[d, h]
            fg = sm[:, SM_FG + d:SM_FG + d + 1] + bf_ref[d, h]
            lf = -_softplus(-fg)
            it[key] = dict(q=q, k=k, v=v, ig=ig, lf=lf, lfb=jnp.broadcast_to(lf, (GROUP, GROUP)),
                           igb=jnp.broadcast_to(ig, (GROUP, GROUP)))
        for (d, gi), x in it.items():
            m = masks[d]
            x['cum_i'] = _dot_mask2(m['incl'], x['lfb'])
            x['cum_j'] = _dot_mask2(m['same'], jnp.where(m['incl_t'], x['lfb'], 0.0)
                                    - jnp.where(m['eye'], x['igb'], 0.0))
            x['qk'] = _dot_nt(x['q'], x['k'])
        for (d, gi), x in it.items():
            m = masks[d]
            dm = x['cum_i'] - x['cum_j']
            x['b'] = x['cum_i'][:, 0:1]
            x['tot'], tot_col = _per_chunk(x['lf'], jnp.sum)
            x['m_intra'] = jnp.max(jnp.where(m['incl'], dm, -jnp.inf), axis=-1, keepdims=True)
            e = tot_col - x['b'] + x['ig']
            x['m_end'], m_end_col = _per_chunk(e, jnp.max)
            p = jnp.where(m['incl'], jnp.exp(jnp.where(m['incl'], dm - x['m_intra'], 0.0)), 0.0) * x['qk']
            x['p_sum'] = jnp.sum(p, axis=-1, keepdims=True)
            x['p'] = p
            x['kw'] = x['k'] * jnp.exp(e - m_end_col)
        for x in it.values():
            x['pv'] = _dot(x['p'], x['v'])
            x['kv'] = [_dot_tn(x['kw'][cs], x['v'][cs]) for cs in chunks]
            x['k_sum'] = [jnp.sum(x['kw'][cs], axis=0, keepdims=True) for cs in chunks]
        outs = []
        for d in (0, 1):
            C, n, m = c_st[d], n_st[d], m_st[d][:, 0:1]
            for gi, c in order[d]:
                x = it[(d, gi)]
                cs = chunks[c]
                m_t = jnp.maximum(x['b'][cs] + m, x['m_intra'][cs])
                w_inter = jnp.exp(x['b'][cs] + m - m_t)
                local = jnp.exp(x['m_intra'][cs] - m_t)
                den = (w_inter * jnp.sum(x['q'][cs] * n, axis=-1, keepdims=True) + local * x['p_sum'][cs])
                scale = 1.0 / jnp.maximum(jnp.abs(den), jnp.exp(-m_t))
                outs.append((d, gi, c, C, w_inter * scale, local * scale))
                m_new = jnp.maximum(x['tot'][c] + m, x['m_end'][c])
                carry_decay = jnp.exp(x['tot'][c] + m - m_new)
                local_new = jnp.exp(x['m_end'][c] - m_new)
                C = carry_decay * C + local_new * x['kv'][c]
                n = carry_decay * n + local_new * x['k_sum'][c]
                m = m_new
            c_st[d] = C
            n_st[d] = n
            m_st[d] = jnp.broadcast_to(m, (1, LANE))
        for d, gi, c, C, w_scale, p_scale in outs:
            x = it[(d, gi)]
            cs = chunks[c]
            rows = pl.ds(pl.multiple_of(base[d] + gi * GROUP + c * CHUNK, CHUNK), CHUNK)
            h_scr[d][rows, :] = w_scale * _dot(x['q'][cs], C) + p_scale * x['pv'][cs]
        return carry

    lax.fori_loop(0, n_steps, body, 0)
    o_ref[...] = _rms(h_f[...] + h_b[...]) * nrm_ref[...] * _sigmoid(gate_ref[...])
    if not has_state:
        cout_ref[...] = c_st[...]
        nout_ref[...] = n_st[...]
        mout_ref[...] = m_st[...]


def _mlstm(P, bias_i, bias_f, norm, *, B, L, state=None):
    T = B * L
    has_state = state is not None
    smem = pl.BlockSpec(memory_space=pltpu.SMEM)
    in_specs = [smem, smem, _col_spec(L, COL_MQ), _col_spec(L, COL_MK), _col_spec(L, COL_MV),
                _col_spec(L, COL_MG), _col_spec(L, COL_SM), pl.BlockSpec((1, LANE), lambda b, h: (0, 0))]
    args = [bias_i, bias_f, P, P, P, P, P, norm]
    out_shape = [jax.ShapeDtypeStruct((T, HEADS * LANE), F32)]
    out_specs = [pl.BlockSpec((L, LANE), lambda b, h: (b, h))]
    vec_spec = pl.BlockSpec((None, None, 2, 1, LANE), lambda b, h: (b, h, 0, 0, 0))
    if has_state:
        c_arr, l, n_arr, m_arr = state
        in_specs += [pl.BlockSpec((None, None, 2, None, HEAD_DIM, HEAD_DIM), lambda b, h: (b, l, 0, h, 0, 0)),
                     vec_spec, vec_spec]
        args += [c_arr, n_arr, m_arr]
    else:
        out_shape += [jax.ShapeDtypeStruct((B, 2, HEADS, HEAD_DIM, HEAD_DIM), F32),
                      jax.ShapeDtypeStruct((B, HEADS, 2, 1, LANE), F32),
                      jax.ShapeDtypeStruct((B, HEADS, 2, 1, LANE), F32)]
        out_specs += [pl.BlockSpec((None, 2, None, HEAD_DIM, HEAD_DIM), lambda b, h: (b, 0, h, 0, 0)),
                      vec_spec, vec_spec]
    seq = pltpu.VMEM((L, LANE), F32)
    return pl.pallas_call(
        functools.partial(_mlstm_kernel, L=L, has_state=has_state),
        out_shape=out_shape,
        grid=(B, HEADS),
        in_specs=in_specs,
        out_specs=out_specs,
        scratch_shapes=[seq, seq, pltpu.VMEM((2, HEAD_DIM, HEAD_DIM), F32), pltpu.VMEM((2, 1, LANE), F32),
                        pltpu.VMEM((2, 1, LANE), F32)],
        compiler_params=_params("parallel", "parallel"),
        name="mlstm",
    )(*args)


def _merge_kernel(x_ref, gates_ref, a_ref, b_ref, c_ref, d_ref, wb_ref, wo_ref, g1_ref, gate1_ref, g2_ref,
                  sh2_ref, sc2_ref, rw_ref, rb_ref, xo_ref, h2_ref, lg_ref):
    merged = None
    for i, br in enumerate((a_ref, b_ref, c_ref, d_ref)):
        term = gates_ref[:, i * D_MODEL:(i + 1) * D_MODEL] * _dot(br[...], wb_ref[i])
        merged = term if merged is None else merged + term
    t = _rms(_dot(merged, wo_ref[...])) * g1_ref[...]
    x = x_ref[...] + gate1_ref[0] * t
    xo_ref[...] = x
    h2 = _rms(x) * g2_ref[...] * (1.0 + sc2_ref[0]) + sh2_ref[0]
    h2_ref[...] = h2
    lg_ref[...] = _dot3(h2, rw_ref[...]) + rb_ref[...]


def _merge(x, gates, branches, wb, wo, g1, gate1, g2, shift2, scale2, rw, rb, *, rows_per_mod, tm):
    T = x.shape[0]
    tm = min(tm, rows_per_mod)
    row = lambda n: pl.BlockSpec((tm, n), lambda i: (i, 0))
    const = lambda *shape: pl.BlockSpec(shape, lambda i: (0,) * len(shape))
    mod = pl.BlockSpec((1, 1, D_MODEL), lambda i: ((i * tm) // rows_per_mod, 0, 0))
    return pl.pallas_call(
        _merge_kernel,
        out_shape=[jax.ShapeDtypeStruct((T, D_MODEL), F32), jax.ShapeDtypeStruct((T, D_MODEL), F32),
                   jax.ShapeDtypeStruct((T, LANE), F32)],
        grid=(T // tm,),
        in_specs=[row(D_MODEL), row(N_BRANCH * D_MODEL)] + [row(BRANCH_WIDTH)] * 4
                 + [const(N_BRANCH, BRANCH_WIDTH, D_MODEL), const(D_MODEL, D_MODEL), const(1, D_MODEL), mod,
                    const(1, D_MODEL), mod, mod, const(D_MODEL, LANE), const(1, LANE)],
        out_specs=[row(D_MODEL), row(D_MODEL), row(LANE)],
        compiler_params=_params("parallel"),
        name="merge",
    )(x, gates, *branches, wb, wo, g1, gate1, g2, shift2, scale2, rw, rb)


def _moe_kernel(be_ref, nu_ref, nv_ref, idx_hbm, h2_hbm, wgu_ref, bgu_ref, wdn_ref, bdn_ref, out_hbm,
                idx_smem, x0, x1, y0, y1, p0, p1, sem_idx, sem_in, sem_out):
    i = pl.program_id(0)
    n_used = nu_ref[0]
    last_block = pl.num_programs(0) - 1
    xbuf, ybuf, pad_rows = (x0, x1), (y0, y1), (p0, p1)

    def idx_copy(block, s):
        return pltpu.make_async_copy(idx_hbm.at[block], idx_smem.at[s], sem_idx.at[s])

    def row_in(s, r, tok):
        return pltpu.make_async_copy(h2_hbm.at[pl.ds(tok, 1), :], xbuf[s].at[pl.ds(r, 1), :], sem_in.at[s])

    def row_out(s, r, dst):
        return pltpu.make_async_copy(ybuf[s].at[pl.ds(r, 1), :], out_hbm.at[pl.ds(dst, 1), :], sem_out.at[s])

    def row_drop(s, r):
        return pltpu.make_async_copy(ybuf[s].at[pl.ds(r, 1), :], pad_rows[s].at[pl.ds(r, 1), :], sem_out.at[s])

    def wait_rows(copy, s):
        for _ in range(MOE_ROWS):
            copy(s, 0, 0).wait()

    @pl.when(i == 0)
    def _():
        idx_copy(0, 0).start()
        idx_copy(0, 0).wait()

        def body(r, carry):
            row_in(0, r, idx_smem[0, 0, r]).start()
            return carry
        lax.fori_loop(0, MOE_ROWS, body, 0, unroll=8)

    def step(slot):
        other = 1 - slot
        nxt = jnp.minimum(i + 1, last_block)
        idx_copy(nxt, other).start()
        wait_rows(row_in, slot)
        idx_copy(nxt, other).wait()

        @pl.when(i >= 2)
        def _():
            wait_rows(row_out, slot)

        for r in range(MOE_ROWS):
            row_in(other, r, idx_smem[other, 0, r]).start()
        hgu = _dot(xbuf[slot][...], wgu_ref[...]) + bgu_ref[...]
        hg = jnp.minimum(hgu[:, :D_EXPERT], SWIGLU_LIMIT)
        hu = jnp.clip(hgu[:, D_EXPERT:], -SWIGLU_LIMIT, SWIGLU_LIMIT)
        act = (hu + 1.0) * (hg * _sigmoid(SWIGLU_ALPHA * hg))
        ybuf[slot][...] = _dot(act, wdn_ref[...]) + bdn_ref[...]

        n_valid = nv_ref[i]

        @pl.when(n_valid == MOE_ROWS)
        def _():
            for r in range(MOE_ROWS):
                row_out(slot, r, idx_smem[slot, 1, r]).start()

        @pl.when(n_valid < MOE_ROWS)
        def _():
            def keep(r, carry):
                row_out(slot, r, idx_smem[slot, 1, r]).start()
                return carry

            def drop(r, carry):
                row_drop(slot, r).start()
                return carry
            lax.fori_loop(0, n_valid, keep, 0)
            lax.fori_loop(n_valid, MOE_ROWS, drop, 0)

        @pl.when(i == n_used - 1)
        def _():
            wait_rows(row_out, slot)
            wait_rows(row_in, other)

            @pl.when(i >= 1)
            def _():
                wait_rows(row_out, other)

    for parity in (0, 1):
        pl.when((i < n_used) & (i % 2 == parity))(functools.partial(step, parity))


def _moe_blocks(idx, h2, block_e, n_used, n_valid, w_gu, b_gu, w_dn, b_dn, n_out_rows):
    n_blocks = idx.shape[0]
    exp = lambda i, be, nu, nv: (be[jnp.minimum(i, nu[0] - 1)], 0, 0)
    any_spec = pl.BlockSpec(memory_space=pl.ANY)
    return pl.pallas_call(
        _moe_kernel,
        out_shape=jax.ShapeDtypeStruct((n_out_rows, D_MODEL), F32),
        grid_spec=pltpu.PrefetchScalarGridSpec(
            num_scalar_prefetch=3,
            grid=(n_blocks,),
            in_specs=[
                any_spec,
                any_spec,
                pl.BlockSpec((None, D_MODEL, 2 * D_EXPERT), exp),
                pl.BlockSpec((None, 1, 2 * D_EXPERT), exp),
                pl.BlockSpec((None, D_EXPERT, D_MODEL), exp),
                pl.BlockSpec((None, 1, D_MODEL), exp),
            ],
            out_specs=any_spec,
            scratch_shapes=[pltpu.SMEM((2, 2, MOE_ROWS), jnp.int32)]
                           + [pltpu.VMEM((MOE_ROWS, D_MODEL), F32)] * 6
                           + [pltpu.SemaphoreType.DMA((2,)),
                            pltpu.SemaphoreType.DMA((2,)),
                            pltpu.SemaphoreType.DMA((2,))],
        ),
        compiler_params=_params("arbitrary"),
        name="moe_experts",
    )(block_e, n_used, n_valid, idx, h2, w_gu, b_gu, w_dn, b_dn)


def _moe(h2, logits, w_gu, b_gu, w_dn, b_dn):
    N = h2.shape[0]
    NK = N * TOP_K
    top_val, top_idx = lax.top_k(logits, TOP_K)
    gate = jax.nn.softmax(top_val, axis=-1)
    flat_e = top_idx.reshape(-1)
    order = jnp.argsort(flat_e).astype(jnp.int32)
    counts = jnp.sum((flat_e[:, None] == jnp.arange(N_EXPERTS)[None, :]).astype(jnp.int32), axis=0)
    padded = (counts + MOE_ROWS - 1) // MOE_ROWS * MOE_ROWS
    pad_end = jnp.cumsum(padded)
    pad_start = pad_end - padded
    start = jnp.cumsum(counts) - counts
    n_blocks = (NK + N_EXPERTS * (MOE_ROWS - 1) + MOE_ROWS - 1) // MOE_ROWS
    P = n_blocks * MOE_ROWS
    block_e = jnp.minimum(jnp.searchsorted(pad_end, jnp.arange(n_blocks) * MOE_ROWS, side='right'),
                          N_EXPERTS - 1).astype(jnp.int32)
    n_used = (pad_end[-1] // MOE_ROWS).astype(jnp.int32).reshape(1)
    pos = jnp.arange(P, dtype=jnp.int32)
    e_pos = jnp.repeat(block_e, MOE_ROWS)
    rank = pos - pad_start[e_pos].astype(jnp.int32)
    valid = rank < counts[e_pos]
    flat = order[jnp.clip(start[e_pos].astype(jnp.int32) + rank, 0, NK - 1)]
    tok = jnp.where(valid, flat // TOP_K, 0)
    dst = jnp.where(valid, (flat % TOP_K) * N + flat // TOP_K, 0)
    idx = jnp.stack([tok.reshape(n_blocks, MOE_ROWS), dst.reshape(n_blocks, MOE_ROWS)], axis=1)
    n_valid = jnp.sum(valid.reshape(n_blocks, MOE_ROWS).astype(jnp.int32), axis=1)
    expert_out = _moe_blocks(idx, h2, block_e, n_used, n_valid, w_gu, b_gu, w_dn, b_dn, NK)
    return expert_out, gate


def _combine_kernel(x_ref, y0_ref, y1_ref, y2_ref, y3_ref, w_ref, g_ref, gate_ref, o_ref):
    w = w_ref[...]
    y = None
    for k, y_ref in enumerate((y0_ref, y1_ref, y2_ref, y3_ref)):
        term = y_ref[...] * w[:, k:k + 1]
        y = term if y is None else y + term
    o_ref[...] = x_ref[...] + gate_ref[0] * (_rms(y) * g_ref[...])


def _combine(x, expert_out, weights, g, gate, *, rows_per_mod, tm):
    T = x.shape[0]
    tm = min(tm, rows_per_mod)
    row = pl.BlockSpec((tm, D_MODEL), lambda i: (i, 0))
    slabs = [pl.BlockSpec((tm, D_MODEL), lambda i, k=k: (k * (T // tm) + i, 0)) for k in range(TOP_K)]
    return pl.pallas_call(
        _combine_kernel,
        out_shape=jax.ShapeDtypeStruct((T, D_MODEL), F32),
        grid=(T // tm,),
        in_specs=[row] + slabs + [pl.BlockSpec((tm, TOP_K), lambda i: (i, 0)),
                                  pl.BlockSpec((1, D_MODEL), lambda i: (0, 0)),
                                  pl.BlockSpec((1, 1, D_MODEL), lambda i: ((i * tm) // rows_per_mod, 0, 0))],
        out_specs=row,
        compiler_params=_params("parallel"),
        name="combine",
    )(x, *[expert_out] * TOP_K, weights, g, gate)


def _rope_tables(L):
    t = jnp.arange(L)
    lane = jnp.arange(LANE)
    axis = (lane % A_DIM) // 32
    half = (lane % 32) // 16
    n_freq = A_DIM // 4
    inv = ROPE_BASE ** (-(lane % n_freq).astype(F32) / n_freq)
    pos = jnp.where(axis[None, :] == 0, (t // GRID_W)[:, None], (t % GRID_W)[:, None]).astype(F32)
    ang = pos * inv[None, :]
    return jnp.cos(ang), jnp.where(half[None, :] == 0, -jnp.sin(ang), jnp.sin(ang))


def _arrange_w_in(w_in):
    sizes = [512] * 8 + [8, 8] + [512] * 4 + [8, 8]
    offs = [0]
    for s in sizes:
        offs.append(offs[-1] + s)
    part = lambda i: w_in[:, offs[i]:offs[i + 1]]
    big = [part(i) for i in (0, 1, 2, 3, 4, 5, 6, 7, 10, 11, 12, 13)]
    small = jnp.zeros((w_in.shape[0], HEADS, LANE), w_in.dtype)
    for base, i in ((SM_BETA, 8), (SM_DEC, 9), (SM_IG, 14), (SM_FG, 15)):
        cols = part(i).reshape(-1, 2, HEADS)
        for d in range(2):
            small = small.at[:, :, base + d].set(cols[:, d, :])
    return jnp.concatenate(big + [small.reshape(w_in.shape[0], HEADS * LANE)], axis=1)


def _layer(x, cond, lidx, lp, *, B, L, cache):
    T = B * L
    is_ctx = cache is None
    mod = (jax.nn.silu(cond) @ lp['ada_w'] + lp['ada_b']).reshape(-1, 6, 1, D_MODEL)
    shift1, scale1, gate1, shift2, scale2, gate2 = [mod[:, i] for i in range(6)]
    rows_per_mod = T if mod.shape[0] == 1 else L
    g = lp['norm_g']

    tm = 1024
    P = _norm_mod_matmul(x, g[0:1], shift1, scale1, lp['w_in'], jnp.zeros((1, N_PROJ), F32),
                         rows_per_mod=rows_per_mod, sigmoid=False, tm=tm, tn=N_PROJ // 4)
    gates = _norm_mod_matmul(x, g[0:1], shift1, scale1, lp['w_bgate'], lp['b_bgate'][None],
                             rows_per_mod=rows_per_mod, sigmoid=True, tm=tm, tn=2048)

    lam_init = 0.8 - 0.6 * math.exp(-0.3 * lidx)
    lq1, lk1, lq2, lk2 = lp['attn_lambda']
    lam = (jnp.exp(jnp.sum(lq1 * lk1)) - jnp.exp(jnp.sum(lq2 * lk2)) + lam_init).reshape(1, 1)
    attn_norm = lp['attn_norm'][None]
    if is_ctx:
        br_a, new_k, new_v = _attention(P, lam, attn_norm, B=B, L=L, tq=L, lam_init=lam_init)
    else:
        (br_a,) = _attention(P, lam, attn_norm, B=B, L=L, tq=128, lam_init=lam_init, cache=cache['attn'])

    br_b = _pool(P, lp['pool_w'], lp['pool_b'][:, None], lp['pool_scale'][None], B=B, L=L)

    gdn_args = (P, lp['gdn_A_log'], lp['gdn_dt_bias'], lp['gdn_conv'], lp['gdn_norm'][None])
    ml_args = (P, lp['mlstm_bias_i'], lp['mlstm_bias_f'], lp['mlstm_norm'][None])
    if is_ctx:
        br_c, new_gdn = _gdn(*gdn_args, B=B, L=L)
        br_d, new_c, new_n, new_m = _mlstm(*ml_args, B=B, L=L)
        ctx_out = (new_k, new_v, new_gdn, new_c,
                   new_n[:, :, :, 0, :HEAD_DIM].transpose(0, 2, 1, 3), new_m[:, :, :, 0, 0].transpose(0, 2, 1))
    else:
        (br_c,) = _gdn(*gdn_args, B=B, L=L, state=cache['gdn'])
        (br_d,) = _mlstm(*ml_args, B=B, L=L, state=cache['mlstm'])
        ctx_out = None

    x, h2, logits = _merge(x, gates, (br_a, br_b, br_c, br_d), lp['w_branch'], lp['w_out'], g[1:2], gate1,
                           g[2:3], shift2, scale2, lp['router_w'], lp['router_b'],
                           rows_per_mod=rows_per_mod, tm=256)
    expert_out, weights = _moe(h2, logits[:, :N_EXPERTS], lp['moe_w_gu'], lp['moe_b_gu'], lp['moe_w_dn'],
                               lp['moe_b_dn'])
    x = _combine(x, expert_out, weights, g[3:4], gate2, rows_per_mod=rows_per_mod, tm=512)
    return x, ctx_out


def kernel(x_prompt, x_sample, cache_attn_k, cache_attn_v, state_gdn, state_mlstm_C, state_mlstm_n, state_mlstm_m, c, c_ctx, ada_w, ada_b, norm_g, w_in, w_bgate, b_bgate, w_branch, w_out, attn_lambda, attn_norm, pool_w, pool_b, pool_scale, gdn_conv, gdn_A_log, gdn_dt_bias, gdn_norm, mlstm_bias_i, mlstm_bias_f, mlstm_norm, router_w, router_b, moe_w_gu, moe_b_gu, moe_w_dn, moe_b_dn):
    Bp, Lp, _ = x_prompt.shape
    Bs, Ls, _ = x_sample.shape
    y_p = x_prompt.reshape(Bp * Lp, D_MODEL)
    y_s = x_sample.reshape(Bs * Ls, D_MODEL)
    cos, sin = _rope_tables(Ls)
    n0 = state_mlstm_n.transpose(1, 0, 3, 2, 4)[:, :, :, :, None, :]
    m0 = jnp.broadcast_to(state_mlstm_m.transpose(1, 0, 3, 2)[..., None, None], n0.shape)
    ctx_states = []
    for l in range(DEPTH):
        lp = {
            'ada_w': ada_w[l], 'ada_b': ada_b[l], 'norm_g': norm_g[l],
            'w_in': _arrange_w_in(w_in[l]).astype(BF16), 'w_bgate': w_bgate[l].astype(BF16),
            'b_bgate': b_bgate[l], 'w_branch': w_branch[l].astype(BF16), 'w_out': w_out[l].astype(BF16),
            'attn_lambda': attn_lambda[l], 'attn_norm': attn_norm[l],
            'pool_w': pool_w[l], 'pool_b': pool_b[l], 'pool_scale': pool_scale[l],
            'gdn_conv': gdn_conv[l], 'gdn_A_log': gdn_A_log[l], 'gdn_dt_bias': gdn_dt_bias[l],
            'gdn_norm': gdn_norm[l], 'mlstm_bias_i': mlstm_bias_i[l], 'mlstm_bias_f': mlstm_bias_f[l],
            'mlstm_norm': mlstm_norm[l],
            'router_w': jnp.pad(router_w[l], ((0, 0), (0, LANE - N_EXPERTS))),
            'router_b': jnp.pad(router_b[l], (0, LANE - N_EXPERTS))[None],
            'moe_w_gu': moe_w_gu[l].astype(BF16), 'moe_b_gu': moe_b_gu[l][:, None],
            'moe_w_dn': moe_w_dn[l].astype(BF16), 'moe_b_dn': moe_b_dn[l][:, None],
        }
        y_p, st = _layer(y_p, c_ctx[None], l, lp, B=Bp, L=Lp, cache=None)
        ctx_states.append(st)
        cache = {
            'attn': {'k': cache_attn_k, 'v': cache_attn_v, 'layer': l, 'cos': cos, 'sin': sin},
            'gdn': (state_gdn, l),
            'mlstm': (state_mlstm_C, l, n0[l], m0[l]),
        }
        y_s, _ = _layer(y_s, c, l, lp, B=Bs, L=Ls, cache=cache)
    outs = [jnp.stack([s[i] for s in ctx_states], axis=1) for i in range(6)]
    return (y_p.reshape(Bp, Lp, D_MODEL), y_s.reshape(Bs, Ls, D_MODEL), *outs)
```

```python
import functools
import math

import jax
import jax.numpy as jnp
from jax import lax
from jax.experimental import pallas as pl
from jax.experimental.pallas import tpu as pltpu

F32 = jnp.float32
BF16 = jnp.bfloat16

D_MODEL = 1024
DEPTH = 2
GRID_W = 64
EPS = 1e-6
LANE = 128
HEADS = 4
HEAD_DIM = 128
A_DIM = 64
ROPE_BASE = 10000.0
POOL_WINDOWS = (2, 4, 8, 16)
CHUNK = 64
N_BRANCH = 4
BRANCH_WIDTH = 512
N_EXPERTS = 32
TOP_K = 4
D_EXPERT = 1024
SWIGLU_LIMIT = 7.0
SWIGLU_ALPHA = 1.702
MOE_ROWS = 256
VMEM_LIMIT = 56 * 1024 * 1024

COL_AQ, COL_AK, COL_AV, COL_POOL, COL_CQ, COL_CK, COL_CV, COL_CG, COL_MQ, COL_MK, COL_MV, COL_MG, COL_SM = range(13)
N_PROJ = 13 * BRANCH_WIDTH
SM_BETA, SM_DEC, SM_IG, SM_FG = 0, 2, 4, 6


def _params(*sem):
    return pltpu.CompilerParams(dimension_semantics=sem, vmem_limit_bytes=VMEM_LIMIT)


def _dot(a, b):
    return jnp.dot(a.astype(BF16), b.astype(BF16), preferred_element_type=F32)


def _dot_nt(a, b):
    return lax.dot_general(a.astype(BF16), b.astype(BF16), (((1,), (1,)), ((), ())),
                           preferred_element_type=F32)


def _dot_tn(a, b):
    return lax.dot_general(a.astype(BF16), b.astype(BF16), (((0,), (0,)), ((), ())),
                           preferred_element_type=F32)


def _split(a):
    hi = a.astype(BF16)
    lo = (a - hi.astype(F32)).astype(BF16)
    return hi, lo


def _dot3(a, b):
    ah, al = _split(a)
    bh, bl = _split(b)
    d = lambda x, y: jnp.dot(x, y, preferred_element_type=F32)
    return d(ah, bh) + (d(ah, bl) + d(al, bh))


def _dot_mask(mask, b):
    m = jnp.where(mask, 1.0, 0.0).astype(BF16)
    b0 = b.astype(BF16)
    r1 = b - b0.astype(F32)
    b1 = r1.astype(BF16)
    b2 = (r1 - b1.astype(F32)).astype(BF16)
    d = lambda y: jnp.dot(m, y, preferred_element_type=F32)
    return d(b0) + (d(b1) + d(b2))


def _rms(x):
    return x * lax.rsqrt(jnp.mean(x * x, axis=-1, keepdims=True) + EPS)


def _sigmoid(x):
    return 1.0 / (1.0 + jnp.exp(-x))


def _softplus(x):
    return jnp.maximum(x, 0.0) + jnp.log(1.0 + jnp.exp(-jnp.abs(x)))


def _proj_kernel(x_ref, g_ref, sh_ref, sc_ref, w_ref, b_ref, o_ref, h_scr, *, sigmoid):
    @pl.when(pl.program_id(1) == 0)
    def _():
        y = _rms(x_ref[...]) * g_ref[...]
        h_scr[...] = (y * (1.0 + sc_ref[0]) + sh_ref[0]).astype(BF16)

    acc = jnp.dot(h_scr[...], w_ref[...], preferred_element_type=F32) + b_ref[...]
    o_ref[...] = _sigmoid(acc) if sigmoid else acc


def _norm_mod_matmul(x, g, shift, scale, w, bias, *, rows_per_mod, sigmoid, tm, tn):
    T, Dm = x.shape
    N = w.shape[1]
    tm = min(tm, rows_per_mod)
    mod_idx = lambda i, j: ((i * tm) // rows_per_mod, 0, 0)
    return pl.pallas_call(
        functools.partial(_proj_kernel, sigmoid=sigmoid),
        out_shape=jax.ShapeDtypeStruct((T, N), F32),
        grid=(T // tm, N // tn),
        in_specs=[
            pl.BlockSpec((tm, Dm), lambda i, j: (i, 0)),
            pl.BlockSpec((1, Dm), lambda i, j: (0, 0)),
            pl.BlockSpec((1, 1, Dm), mod_idx),
            pl.BlockSpec((1, 1, Dm), mod_idx),
            pl.BlockSpec((Dm, tn), lambda i, j: (0, j)),
            pl.BlockSpec((1, tn), lambda i, j: (0, j)),
        ],
        out_specs=pl.BlockSpec((tm, tn), lambda i, j: (i, j)),
        scratch_shapes=[pltpu.VMEM((tm, Dm), BF16)],
        compiler_params=_params("parallel", "arbitrary"),
        name="proj",
    )(x, g, shift, scale, w, bias)


def _rope(x, cos, sin):
    lane = lax.broadcasted_iota(jnp.int32, x.shape, 1)
    first = (lane % 32) < 16
    partner = jnp.where(first, pltpu.roll(x, LANE - 16, 1), pltpu.roll(x, 16, 1))
    return x * cos + partner * sin


def _attn_kernel(*refs, has_cache, n_ctx, out_scale):
    if has_cache:
        (lam_ref, q_ref, k_ref, v_ref, ck_ref, cv_ref, cosq_ref, sinq_ref, cosk_ref, sink_ref, nrm_ref,
         o_ref, kall, vall) = refs
    else:
        lam_ref, q_ref, k_ref, v_ref, nrm_ref, o_ref, ko_ref, vo_ref, kall, vall = refs

    @pl.when(pl.program_id(2) == 0)
    def _():
        k = k_ref[...]
        v = v_ref[...]
        if has_cache:
            kall[pl.ds(0, n_ctx), :] = ck_ref[...].astype(BF16)
            vall[pl.ds(0, n_ctx), :] = cv_ref[...].astype(BF16)
            k = _rope(k, cosk_ref[...], sink_ref[...])
        else:
            ko_ref[...] = k
            vo_ref[...] = v
        kall[pl.ds(n_ctx, k.shape[0]), :] = k.astype(BF16)
        vall[pl.ds(n_ctx, k.shape[0]), :] = v.astype(BF16)

    q = q_ref[...]
    if has_cache:
        q = _rope(q, cosq_ref[...], sinq_ref[...])
    q = q * (A_DIM ** -0.5 * math.log2(math.e))
    lane = lax.broadcasted_iota(jnp.int32, q.shape, 1)
    q1 = jnp.where(lane < A_DIM, q, 0.0)
    q2 = jnp.where(lane >= A_DIM, q, 0.0)
    keys = kall[...]
    lam = lam_ref[0, 0]

    def probs(qm):
        s = _dot_nt(qm, keys)
        p = jnp.exp2(s - jnp.max(s, axis=-1, keepdims=True))
        return p, jnp.sum(p, axis=-1, keepdims=True)

    p1, l1 = probs(q1)
    p2, l2 = probs(q2)
    a = p1 * (1.0 / l1) - p2 * (lam / l2)
    o = jnp.dot(a.astype(BF16), vall[...], preferred_element_type=F32)
    o_ref[...] = _rms(o) * nrm_ref[...] * out_scale


def _attention(P, lam, norm, *, B, L, tq, lam_init, cache=None):
    T = B * L
    nq = L // tq
    has_cache = cache is not None
    n_ctx = cache["k"].shape[3] if has_cache else 0
    colq, colk, colv = COL_AQ * HEADS, COL_AK * HEADS, COL_AV * HEADS
    in_specs = [
        pl.BlockSpec(memory_space=pltpu.SMEM),
        pl.BlockSpec((tq, LANE), lambda b, h, i: (b * nq + i, colq + h)),
        pl.BlockSpec((L, LANE), lambda b, h, i: (b, colk + h)),
        pl.BlockSpec((L, LANE), lambda b, h, i: (b, colv + h)),
    ]
    args = [lam, P, P, P]
    if has_cache:
        l = cache["layer"]
        cspec = pl.BlockSpec((None, None, None, n_ctx, LANE), lambda b, h, i: (b, l, h, 0, 0))
        in_specs += [cspec, cspec,
                     pl.BlockSpec((tq, LANE), lambda b, h, i: (i, 0)),
                     pl.BlockSpec((tq, LANE), lambda b, h, i: (i, 0)),
                     pl.BlockSpec((L, LANE), lambda b, h, i: (0, 0)),
                     pl.BlockSpec((L, LANE), lambda b, h, i: (0, 0))]
        args += [cache["k"], cache["v"], cache["cos"], cache["sin"], cache["cos"], cache["sin"]]
    in_specs.append(pl.BlockSpec((1, LANE), lambda b, h, i: (0, 0)))
    args.append(norm)
    out_shape = [jax.ShapeDtypeStruct((T, HEADS * LANE), F32)]
    out_specs = [pl.BlockSpec((tq, LANE), lambda b, h, i: (b * nq + i, h))]
    if not has_cache:
        kv_shape = jax.ShapeDtypeStruct((B, HEADS, L, LANE), F32)
        kv_spec = pl.BlockSpec((None, None, L, LANE), lambda b, h, i: (b, h, 0, 0))
        out_shape += [kv_shape, kv_shape]
        out_specs += [kv_spec, kv_spec]
    return pl.pallas_call(
        functools.partial(_attn_kernel, has_cache=has_cache, n_ctx=n_ctx, out_scale=1.0 - lam_init),
        out_shape=out_shape,
        grid=(B, HEADS, nq),
        in_specs=in_specs,
        out_specs=out_specs,
        scratch_shapes=[pltpu.VMEM((n_ctx + L, LANE), BF16), pltpu.VMEM((n_ctx + L, LANE), BF16)],
        compiler_params=_params("parallel", "parallel", "arbitrary"),
        name="diff_attention",
    )(*args)


POOL_PAD = 16


def _pool_kernel(x_ref, w_ref, b_ref, s_ref, o_ref, pad, *, L):
    zeros = jnp.zeros((POOL_PAD, LANE), F32)
    pad[pl.ds(0, POOL_PAD), :] = zeros
    pad[pl.ds(POOL_PAD + L, POOL_PAD), :] = zeros
    x = x_ref[...]
    pad[pl.ds(POOL_PAD, L), :] = x
    t = lax.broadcasted_iota(jnp.int32, (L, LANE), 0)
    g = pl.program_id(1)
    for gi, win in enumerate(POOL_WINDOWS):
        @pl.when(g == gi)
        def _(win=win):
            half = win // 2
            acc = pad[pl.ds(POOL_PAD - half, L), :]
            for k in range(1 - half, half):
                acc = acc + pad[pl.ds(POOL_PAD + k, L), :]
            cnt = (jnp.minimum(t + half, L) - jnp.maximum(t - half, 0)).astype(F32)
            pooled = acc / cnt - x
            o_ref[...] = (_dot(pooled, w_ref[...]) + b_ref[...]) * s_ref[...]


def _pool(P, w, b, scale, *, B, L):
    T = B * L
    G = len(POOL_WINDOWS)
    return pl.pallas_call(
        functools.partial(_pool_kernel, L=L),
        out_shape=jax.ShapeDtypeStruct((T, G * LANE), F32),
        grid=(B, G),
        in_specs=[
            pl.BlockSpec((L, LANE), lambda bi, g: (bi, COL_POOL * HEADS + g)),
            pl.BlockSpec((None, LANE, LANE), lambda bi, g: (g, 0, 0)),
            pl.BlockSpec((None, 1, LANE), lambda bi, g: (g, 0, 0)),
            pl.BlockSpec((1, LANE), lambda bi, g: (0, g)),
        ],
        out_specs=pl.BlockSpec((L, LANE), lambda bi, g: (bi, g)),
        scratch_shapes=[pltpu.VMEM((L + 2 * POOL_PAD, LANE), F32)],
        compiler_params=_params("parallel", "parallel"),
        name="pool_mixer",
    )(P, w, b, scale)


GROUP_CHUNKS = 2
GROUP = GROUP_CHUNKS * CHUNK
MAX_GROUPS_PER_STEP = 4


def _groups_per_step(L):
    return min(MAX_GROUPS_PER_STEP, L // GROUP)


def _group_masks(backward):
    row = lax.broadcasted_iota(jnp.int32, (GROUP, GROUP), 0)
    col = lax.broadcasted_iota(jnp.int32, (GROUP, GROUP), 1)
    same = (row // CHUNK) == (col // CHUNK)
    lower, upper = row >= col, row <= col
    if backward:
        lower, upper = upper, lower
    return dict(same=same, incl=same & lower, strict=same & lower & (row != col), incl_t=same & upper,
                eye=row == col)


INV_BASE = 8


def _block_masks():
    row = lax.broadcasted_iota(jnp.int32, (GROUP, GROUP), 0)
    col = lax.broadcasted_iota(jnp.int32, (GROUP, GROUP), 1)
    sizes = [INV_BASE << i for i in range(int(math.log2(CHUNK // INV_BASE)) + 1)]
    same = [(row // s) == (col // s) for s in sizes]
    return [same[0]] + [cur & jnp.logical_not(prev) for prev, cur in zip(same[:-1], same[1:])]


def _unit_triangular_inverses(ns, eye, blocks):
    base = [jnp.where(blocks[0], n, 0.0) for n in ns]
    invs = [eye - b for b in base]
    pws = [_dot(b, b) for b in base]
    rounds = int(math.log2(INV_BASE)) - 1
    for s in range(rounds):
        invs = [t + _dot(t, p) for t, p in zip(invs, pws)]
        if s + 1 < rounds:
            pws = [_dot(p, p) for p in pws]
    for join in blocks[1:]:
        offs = [jnp.where(join, n, 0.0) for n in ns]
        xs = [_dot(o, t) for o, t in zip(offs, invs)]
        invs = [t - _dot(t, x) for t, x in zip(invs, xs)]
    return invs


def _dot_mask2(mask, b):
    m = jnp.where(mask, 1.0, 0.0).astype(BF16)
    b0, b1 = _split(b)
    return jnp.dot(m, b0, preferred_element_type=F32) + jnp.dot(m, b1, preferred_element_type=F32)


def _per_chunk(x, reduce):
    parts = [reduce(x[c * CHUNK:(c + 1) * CHUNK], axis=0, keepdims=True) for c in range(GROUP_CHUNKS)]
    col = jnp.concatenate([jnp.broadcast_to(p, (CHUNK, 1)) for p in parts], axis=0)
    return parts, col


def _scan_order(step, n_steps, groups):
    base = (step * groups * GROUP, (n_steps - 1 - step) * groups * GROUP)
    fwd = [(gi, c) for gi in range(groups) for c in range(GROUP_CHUNKS)]
    return base, (fwd, fwd[::-1])


CONV_PAD = 8


def _gdn_kernel(*refs, L, has_state):
    alog_ref, dtb_ref, q_ref, k_ref, v_ref, gate_ref, sm_ref, cwq_ref, cwk_ref, cwv_ref, nrm_ref = refs[:11]
    refs = refs[11:]
    if has_state:
        s0_ref, o_ref, pad, qs, ks, vs, o_f, o_b, state = refs
    else:
        o_ref, sout_ref, pad, qs, ks, vs, o_f, o_b, state = refs
    h = pl.program_id(1)
    n_chunks = L // CHUNK

    zeros = jnp.zeros((CONV_PAD, LANE), F32)
    pad[pl.ds(0, CONV_PAD), :] = zeros
    pad[pl.ds(CONV_PAD + L, CONV_PAD), :] = zeros

    def conv_silu(x_ref, w_ref):
        pad[pl.ds(CONV_PAD, L), :] = x_ref[...]
        w = w_ref[...]
        y = (pad[pl.ds(CONV_PAD - 1, L), :] * w[0:1] + pad[pl.ds(CONV_PAD, L), :] * w[1:2]
             + pad[pl.ds(CONV_PAD + 1, L), :] * w[2:3])
        return y * _sigmoid(y)

    def l2n(x):
        return x * lax.rsqrt(jnp.sum(x * x, axis=-1, keepdims=True) + EPS)

    qs[...] = l2n(conv_silu(q_ref, cwq_ref)) * (HEAD_DIM ** -0.5)
    ks[...] = l2n(conv_silu(k_ref, cwk_ref))
    vs[...] = conv_silu(v_ref, cwv_ref)
    if has_state:
        state[...] = s0_ref[...]
    else:
        state[...] = jnp.zeros_like(state)

    groups = _groups_per_step(L)
    n_steps = L // (groups * GROUP)
    o_scr = (o_f, o_b)
    masks = (_group_masks(False), _group_masks(True))
    eye = jnp.where(masks[0]['eye'], 1.0, 0.0)
    blocks = _block_masks()
    items = [(d, gi) for gi in range(groups) for d in (0, 1)]
    chunks = [slice(c * CHUNK, (c + 1) * CHUNK) for c in range(GROUP_CHUNKS)]

    def body(step, carry):
        base, order = _scan_order(step, n_steps, groups)
        it = {}
        for key in items:
            d, gi = key
            rows = pl.ds(pl.multiple_of(base[d] + gi * GROUP, GROUP), GROUP)
            q, k, v, sm = qs[rows, :], ks[rows, :], vs[rows, :], sm_ref[rows, :]
            beta = _sigmoid(sm[:, SM_BETA + d:SM_BETA + d + 1])
            dec = sm[:, SM_DEC + d:SM_DEC + d + 1]
            g = -jnp.exp(alog_ref[d, h]) * _softplus(dec + dtb_ref[d, h])
            it[key] = dict(q=q, k=k, kb=k * beta, vb=v * beta, g=g, gb=jnp.broadcast_to(g, (GROUP, GROUP)))
        for (d, gi), x in it.items():
            m = masks[d]
            x['cum_i'] = _dot_mask2(m['incl'], x['gb'])
            x['cum_j'] = _dot_mask2(m['same'], jnp.where(m['incl_t'], x['gb'], 0.0))
            x['kk'] = _dot_nt(x['kb'], x['k'])
            x['qk'] = _dot_nt(x['q'], x['k'])
        for (d, gi), x in it.items():
            m = masks[d]
            decay = jnp.where(m['incl'], jnp.exp(jnp.where(m['incl'], x['cum_i'] - x['cum_j'], 0.0)), 0.0)
            x['gc'] = x['cum_i'][:, 0:1]
            x['tot'], x['tot_col'] = _per_chunk(x['g'], jnp.sum)
            x['qk'] = x['qk'] * decay
            x['n'] = jnp.where(m['strict'], x['kk'] * decay, 0.0)
        invs = _unit_triangular_inverses([x['n'] for x in it.values()], eye, blocks)
        for x, inv in zip(it.values(), invs):
            x['inv'] = inv
            x['egc'] = jnp.exp(x['gc'])
            x['sol'] = _dot3(x['inv'], jnp.concatenate([x['vb'], x['kb'] * x['egc']], axis=-1))
        for x in it.values():
            u, w = x['sol'][:, :HEAD_DIM], x['sol'][:, HEAD_DIM:]
            x['qp'] = x['q'] * x['egc'] - _dot(x['qk'], w)
            x['op'] = _dot(x['qk'], u)
            kd = x['k'] * jnp.exp(x['tot_col'] - x['gc'])
            x['ab'] = [_dot_tn(kd[cs], x['sol'][cs]) for cs in chunks]
            x['gl'] = [jnp.exp(t) for t in x['tot']]
        S = [state[0], state[1]]
        for stp in range(len(order[0])):
            for d in (0, 1):
                gi, c = order[d][stp]
                x = it[(d, gi)]
                ab = x['ab'][c]
                r = _dot(jnp.concatenate([ab[:, HEAD_DIM:], x['qp'][chunks[c]]], axis=0), S[d])
                rows = pl.ds(pl.multiple_of(base[d] + gi * GROUP + c * CHUNK, CHUNK), CHUNK)
                o_scr[d][rows, :] = x['op'][chunks[c]] + r[HEAD_DIM:]
                S[d] = x['gl'][c] * S[d] + (ab[:, :HEAD_DIM] - r[:HEAD_DIM])
        state[0] = S[0]
        state[1] = S[1]
        return carry

    lax.fori_loop(0, n_steps, body, 0)
    gate = gate_ref[...]
    o_ref[...] = _rms(o_f[...] + o_b[...]) * nrm_ref[...] * (gate * _sigmoid(gate))
    if not has_state:
        sout_ref[...] = state[...]


def _col_spec(L, col):
    return pl.BlockSpec((L, LANE), lambda b, h: (b, col * HEADS + h))


def _gdn(P, a_log, dt_bias, conv_w, norm, *, B, L, state=None):
    T = B * L
    has_state = state is not None
    smem = pl.BlockSpec(memory_space=pltpu.SMEM)
    in_specs = [smem, smem, _col_spec(L, COL_CQ), _col_spec(L, COL_CK), _col_spec(L, COL_CV),
                _col_spec(L, COL_CG), _col_spec(L, COL_SM)]
    in_specs += [pl.BlockSpec((3, LANE), lambda b, h, j=j: (0, j * HEADS + h)) for j in range(3)]
    in_specs.append(pl.BlockSpec((1, LANE), lambda b, h: (0, 0)))
    args = [a_log, dt_bias, P, P, P, P, P, conv_w, conv_w, conv_w, norm]
    out_shape = [jax.ShapeDtypeStruct((T, HEADS * LANE), F32)]
    out_specs = [pl.BlockSpec((L, LANE), lambda b, h: (b, h))]
    if has_state:
        arr, l = state
        in_specs.append(pl.BlockSpec((None, None, 2, None, HEAD_DIM, HEAD_DIM), lambda b, h: (b, l, 0, h, 0, 0)))
        args.append(arr)
    else:
        out_shape.append(jax.ShapeDtypeStruct((B, 2, HEADS, HEAD_DIM, HEAD_DIM), F32))
        out_specs.append(pl.BlockSpec((None, 2, None, HEAD_DIM, HEAD_DIM), lambda b, h: (b, 0, h, 0, 0)))
    seq = pltpu.VMEM((L, LANE), F32)
    return pl.pallas_call(
        functools.partial(_gdn_kernel, L=L, has_state=has_state),
        out_shape=out_shape,
        grid=(B, HEADS),
        in_specs=in_specs,
        out_specs=out_specs,
        scratch_shapes=[pltpu.VMEM((L + 2 * CONV_PAD, LANE), F32), seq, seq, seq, seq, seq,
                        pltpu.VMEM((2, HEAD_DIM, HEAD_DIM), F32)],
        compiler_params=_params("parallel", "parallel"),
        name="gated_deltanet",
    )(*args)


def _mlstm_kernel(*refs, L, has_state):
    bi_ref, bf_ref, q_ref, k_ref, v_ref, gate_ref, sm_ref, nrm_ref = refs[:8]
    refs = refs[8:]
    if has_state:
        c0_ref, n0_ref, m0_ref, o_ref, h_f, h_b, c_st, n_st, m_st = refs
    else:
        o_ref, cout_ref, nout_ref, mout_ref, h_f, h_b, c_st, n_st, m_st = refs
    h = pl.program_id(1)
    n_chunks = L // CHUNK

    if has_state:
        c_st[...] = c0_ref[...]
        n_st[...] = n0_ref[...]
        m_st[...] = m0_ref[...]
    else:
        c_st[...] = jnp.zeros_like(c_st)
        n_st[...] = jnp.zeros_like(n_st)
        m_st[...] = jnp.zeros_like(m_st)

    groups = _groups_per_step(L)
    n_steps = L // (groups * GROUP)
    h_scr = (h_f, h_b)
    masks = (_group_masks(False), _group_masks(True))
    items = [(d, gi) for gi in range(groups) for d in (0, 1)]
    chunks = [slice(c * CHUNK, (c + 1) * CHUNK) for c in range(GROUP_CHUNKS)]

    def body(step, carry):
        base, order = _scan_order(step, n_steps, groups)
        it = {}
        for key in items:
            d, gi = key
            rows = pl.ds(pl.multiple_of(base[d] + gi * GROUP, GROUP), GROUP)
            q, v, sm = q_ref[rows, :], v_ref[rows, :], sm_ref[rows, :]
            k = k_ref[rows, :] * (HEAD_DIM ** -0.5)
            ig = sm[:, SM_IG + d:SM_IG + d + 1] + bi_ref[d, h]
            fg = sm[:, SM_FG + d:SM_FG + d + 1] + bf_ref[d, h]
            lf = -_softplus(-fg)
            it[key] = dict(q=q, k=k, v=v, ig=ig, lf=lf, lfb=jnp.broadcast_to(lf, (GROUP, GROUP)),
                           igb=jnp.broadcast_to(ig, (GROUP, GROUP)))
        for (d, gi), x in it.items():
            m = masks[d]
            x['cum_i'] = _dot_mask2(m['incl'], x['lfb'])
            x['cum_j'] = _dot_mask2(m['same'], jnp.where(m['incl_t'], x['lfb'], 0.0)
                                    - jnp.where(m['eye'], x['igb'], 0.0))
            x['qk'] = _dot_nt(x['q'], x['k'])
        for (d, gi), x in it.items():
            m = masks[d]
            dm = x['cum_i'] - x['cum_j']
            x['b'] = x['cum_i'][:, 0:1]
            x['tot'], tot_col = _per_chunk(x['lf'], jnp.sum)
            x['m_intra'] = jnp.max(jnp.where(m['incl'], dm, -jnp.inf), axis=-1, keepdims=True)
            e = tot_col - x['b'] + x['ig']
            x['m_end'], m_end_col = _per_chunk(e, jnp.max)
            p = jnp.where(m['incl'], jnp.exp(jnp.where(m['incl'], dm - x['m_intra'], 0.0)), 0.0) * x['qk']
            x['p_sum'] = jnp.sum(p, axis=-1, keepdims=True)
            x['p'] = p
            x['kw'] = x['k'] * jnp.exp(e - m_end_col)
        for x in it.values():
            x['pv'] = _dot(x['p'], x['v'])
            x['kv'] = [_dot_tn(x['kw'][cs], x['v'][cs]) for cs in chunks]
            x['k_sum'] = [jnp.sum(x['kw'][cs], axis=0, keepdims=True) for cs in chunks]
        outs = []
        for d in (0, 1):
            C, n, m = c_st[d], n_st[d], m_st[d][:, 0:1]
            for gi, c in order[d]:
                x = it[(d, gi)]
                cs = chunks[c]
                m_t = jnp.maximum(x['b'][cs] + m, x['m_intra'][cs])
                w_inter = jnp.exp(x['b'][cs] + m - m_t)
                local = jnp.exp(x['m_intra'][cs] - m_t)
                den = (w_inter * jnp.sum(x['q'][cs] * n, axis=-1, keepdims=True) + local * x['p_sum'][cs])
                scale = 1.0 / jnp.maximum(jnp.abs(den), jnp.exp(-m_t))
                outs.append((d, gi, c, C, w_inter * scale, local * scale))
                m_new = jnp.maximum(x['tot'][c] + m, x['m_end'][c])
                carry_decay = jnp.exp(x['tot'][c] + m - m_new)
                local_new = jnp.exp(x['m_end'][c] - m_new)
                C = carry_decay * C + local_new * x['kv'][c]
                n = carry_decay * n + local_new * x['k_sum'][c]
                m = m_new
            c_st[d] = C
            n_st[d] = n
            m_st[d] = jnp.broadcast_to(m, (1, LANE))
        for d, gi, c, C, w_scale, p_scale in outs:
            x = it[(d, gi)]
            cs = chunks[c]
            rows = pl.ds(pl.multiple_of(base[d] + gi * GROUP + c * CHUNK, CHUNK), CHUNK)
            h_scr[d][rows, :] = w_scale * _dot(x['q'][cs], C) + p_scale * x['pv'][cs]
        return carry

    lax.fori_loop(0, n_steps, body, 0)
    o_ref[...] = _rms(h_f[...] + h_b[...]) * nrm_ref[...] * _sigmoid(gate_ref[...])
    if not has_state:
        cout_ref[...] = c_st[...]
        nout_ref[...] = n_st[...]
        mout_ref[...] = m_st[...]


def _mlstm(P, bias_i, bias_f, norm, *, B, L, state=None):
    T = B * L
    has_state = state is not None
    smem = pl.BlockSpec(memory_space=pltpu.SMEM)
    in_specs = [smem, smem, _col_spec(L, COL_MQ), _col_spec(L, COL_MK), _col_spec(L, COL_MV),
                _col_spec(L, COL_MG), _col_spec(L, COL_SM), pl.BlockSpec((1, LANE), lambda b, h: (0, 0))]
    args = [bias_i, bias_f, P, P, P, P, P, norm]
    out_shape = [jax.ShapeDtypeStruct((T, HEADS * LANE), F32)]
    out_specs = [pl.BlockSpec((L, LANE), lambda b, h: (b, h))]
    vec_spec = pl.BlockSpec((None, None, 2, 1, LANE), lambda b, h: (b, h, 0, 0, 0))
    if has_state:
        c_arr, l, n_arr, m_arr = state
        in_specs += [pl.BlockSpec((None, None, 2, None, HEAD_DIM, HEAD_DIM), lambda b, h: (b, l, 0, h, 0, 0)),
                     vec_spec, vec_spec]
        args += [c_arr, n_arr, m_arr]
    else:
        out_shape += [jax.ShapeDtypeStruct((B, 2, HEADS, HEAD_DIM, HEAD_DIM), F32),
                      jax.ShapeDtypeStruct((B, HEADS, 2, 1, LANE), F32),
                      jax.ShapeDtypeStruct((B, HEADS, 2, 1, LANE), F32)]
        out_specs += [pl.BlockSpec((None, 2, None, HEAD_DIM, HEAD_DIM), lambda b, h: (b, 0, h, 0, 0)),
                      vec_spec, vec_spec]
    seq = pltpu.VMEM((L, LANE), F32)
    return pl.pallas_call(
        functools.partial(_mlstm_kernel, L=L, has_state=has_state),
        out_shape=out_shape,
        grid=(B, HEADS),
        in_specs=in_specs,
        out_specs=out_specs,
        scratch_shapes=[seq, seq, pltpu.VMEM((2, HEAD_DIM, HEAD_DIM), F32), pltpu.VMEM((2, 1, LANE), F32),
                        pltpu.VMEM((2, 1, LANE), F32)],
        compiler_params=_params("parallel", "parallel"),
        name="mlstm",
    )(*args)


def _merge_kernel(x_ref, gates_ref, a_ref, b_ref, c_ref, d_ref, wb_ref, wo_ref, g1_ref, gate1_ref, g2_ref,
                  sh2_ref, sc2_ref, rw_ref, rb_ref, xo_ref, h2_ref, lg_ref):
    merged = None
    for i, br in enumerate((a_ref, b_ref, c_ref, d_ref)):
        term = gates_ref[:, i * D_MODEL:(i + 1) * D_MODEL] * _dot(br[...], wb_ref[i])
        merged = term if merged is None else merged + term
    t = _rms(_dot(merged, wo_ref[...])) * g1_ref[...]
    x = x_ref[...] + gate1_ref[0] * t
    xo_ref[...] = x
    h2 = _rms(x) * g2_ref[...] * (1.0 + sc2_ref[0]) + sh2_ref[0]
    h2_ref[...] = h2
    lg_ref[...] = _dot3(h2, rw_ref[...]) + rb_ref[...]


def _merge(x, gates, branches, wb, wo, g1, gate1, g2, shift2, scale2, rw, rb, *, rows_per_mod, tm):
    T = x.shape[0]
    tm = min(tm, rows_per_mod)
    row = lambda n: pl.BlockSpec((tm, n), lambda i: (i, 0))
    const = lambda *shape: pl.BlockSpec(shape, lambda i: (0,) * len(shape))
    mod = pl.BlockSpec((1, 1, D_MODEL), lambda i: ((i * tm) // rows_per_mod, 0, 0))
    return pl.pallas_call(
        _merge_kernel,
        out_shape=[jax.ShapeDtypeStruct((T, D_MODEL), F32), jax.ShapeDtypeStruct((T, D_MODEL), F32),
                   jax.ShapeDtypeStruct((T, LANE), F32)],
        grid=(T // tm,),
        in_specs=[row(D_MODEL), row(N_BRANCH * D_MODEL)] + [row(BRANCH_WIDTH)] * 4
                 + [const(N_BRANCH, BRANCH_WIDTH, D_MODEL), const(D_MODEL, D_MODEL), const(1, D_MODEL), mod,
                    const(1, D_MODEL), mod, mod, const(D_MODEL, LANE), const(1, LANE)],
        out_specs=[row(D_MODEL), row(D_MODEL), row(LANE)],
        compiler_params=_params("parallel"),
        name="merge",
    )(x, gates, *branches, wb, wo, g1, gate1, g2, shift2, scale2, rw, rb)


def _moe_kernel(be_ref, nu_ref, nv_ref, idx_hbm, h2_hbm, wgu_ref, bgu_ref, wdn_ref, bdn_ref, out_hbm,
                idx_smem, x0, x1, y0, y1, p0, p1, sem_idx, sem_in, sem_out):
    i = pl.program_id(0)
    n_used = nu_ref[0]
    last_block = pl.num_programs(0) - 1
    xbuf, ybuf, pad_rows = (x0, x1), (y0, y1), (p0, p1)

    def idx_copy(block, s):
        return pltpu.make_async_copy(idx_hbm.at[block], idx_smem.at[s], sem_idx.at[s])

    def row_in(s, r, tok):
        return pltpu.make_async_copy(h2_hbm.at[pl.ds(tok, 1), :], xbuf[s].at[pl.ds(r, 1), :], sem_in.at[s])

    def row_out(s, r, dst):
        return pltpu.make_async_copy(ybuf[s].at[pl.ds(r, 1), :], out_hbm.at[pl.ds(dst, 1), :], sem_out.at[s])

    def row_drop(s, r):
        return pltpu.make_async_copy(ybuf[s].at[pl.ds(r, 1), :], pad_rows[s].at[pl.ds(r, 1), :], sem_out.at[s])

    def wait_rows(copy, s):
        for _ in range(MOE_ROWS):
            copy(s, 0, 0).wait()

    @pl.when(i == 0)
    def _():
        idx_copy(0, 0).start()
        idx_copy(0, 0).wait()

        def body(r, carry):
            row_in(0, r, idx_smem[0, 0, r]).start()
            return carry
        lax.fori_loop(0, MOE_ROWS, body, 0, unroll=8)

    def step(slot):
        other = 1 - slot
        nxt = jnp.minimum(i + 1, last_block)
        idx_copy(nxt, other).start()
        wait_rows(row_in, slot)
        idx_copy(nxt, other).wait()

        @pl.when(i >= 2)
        def _():
            wait_rows(row_out, slot)

        for r in range(MOE_ROWS):
            row_in(other, r, idx_smem[other, 0, r]).start(priority=r % 2)
        hgu =_dot(xbuf[slot][...], wgu_ref[...]) + bgu_ref[...]
        hg = jnp.minimum(hgu[:, :D_EXPERT], SWIGLU_LIMIT)
        hu = jnp.clip(hgu[:, D_EXPERT:], -SWIGLU_LIMIT, SWIGLU_LIMIT)
        act = (hu + 1.0) * (hg * _sigmoid(SWIGLU_ALPHA * hg))
        ybuf[slot][...] = _dot(act, wdn_ref[...]) + bdn_ref[...]

        n_valid = nv_ref[i]

        @pl.when(n_valid == MOE_ROWS)
        def _():
            for r in range(MOE_ROWS):
                row_out(slot, r, idx_smem[slot, 1, r]).start(priority=r % 2)

        @pl.when(n_valid < MOE_ROWS)
        def _():
            def keep(r, carry):
                row_out(slot, r, idx_smem[slot, 1, r]).start()
                return carry

            def drop(r, carry):
                row_drop(slot, r).start()
                return carry
            lax.fori_loop(0, n_valid, keep, 0)
            lax.fori_loop(n_valid, MOE_ROWS, drop, 0)

        @pl.when(i == n_used - 1)
        def _():
            wait_rows(row_out, slot)
            wait_rows(row_in, other)

            @pl.when(i >= 1)
            def _():
                wait_rows(row_out, other)

    for parity in (0, 1):
        pl.when((i < n_used) & (i % 2 == parity))(functools.partial(step, parity))


def _moe_blocks(idx, h2, block_e, n_used, n_valid, w_gu, b_gu, w_dn, b_dn, n_out_rows):
    n_blocks = idx.shape[0]
    exp = lambda i, be, nu, nv: (be[jnp.minimum(i, nu[0] - 1)], 0, 0)
    any_spec = pl.BlockSpec(memory_space=pl.ANY)
    return pl.pallas_call(
        _moe_kernel,
        out_shape=jax.ShapeDtypeStruct((n_out_rows, D_MODEL), F32),
        grid_spec=pltpu.PrefetchScalarGridSpec(
            num_scalar_prefetch=3,
            grid=(n_blocks,),
            in_specs=[
                any_spec,
                any_spec,
                pl.BlockSpec((None, D_MODEL, 2 * D_EXPERT), exp),
                pl.BlockSpec((None, 1, 2 * D_EXPERT), exp),
                pl.BlockSpec((None, D_EXPERT, D_MODEL), exp),
                pl.BlockSpec((None, 1, D_MODEL), exp),
            ],
            out_specs=any_spec,
            scratch_shapes=[pltpu.SMEM((2, 2, MOE_ROWS), jnp.int32)]
                           + [pltpu.VMEM((MOE_ROWS, D_MODEL), F32)] * 6
                           + [pltpu.SemaphoreType.DMA((2,)),
                            pltpu.SemaphoreType.DMA((2,)),
                            pltpu.SemaphoreType.DMA((2,))],
        ),
        compiler_params=_params("arbitrary"),
        name="moe_experts",
    )(block_e, n_used, n_valid, idx, h2, w_gu, b_gu, w_dn, b_dn)


def _moe(h2, logits, w_gu, b_gu, w_dn, b_dn):
    N = h2.shape[0]
    NK = N * TOP_K
    top_val, top_idx = lax.top_k(logits, TOP_K)
    gate = jax.nn.softmax(top_val, axis=-1)
    flat_e = top_idx.reshape(-1)
    order = jnp.argsort(flat_e).astype(jnp.int32)
    counts = jnp.sum((flat_e[:, None] == jnp.arange(N_EXPERTS)[None, :]).astype(jnp.int32), axis=0)
    padded = (counts + MOE_ROWS - 1) // MOE_ROWS * MOE_ROWS
    pad_end = jnp.cumsum(padded)
    pad_start = pad_end - padded
    start = jnp.cumsum(counts) - counts
    n_blocks = (NK + N_EXPERTS * (MOE_ROWS - 1) + MOE_ROWS - 1) // MOE_ROWS
    P = n_blocks * MOE_ROWS
    block_first = jnp.arange(n_blocks, dtype=jnp.int32) * MOE_ROWS
    block_e = jnp.minimum(jnp.sum((block_first[:, None] >= pad_end[None, :]).astype(jnp.int32), axis=1),
                          N_EXPERTS - 1)
    n_used = (pad_end[-1] // MOE_ROWS).astype(jnp.int32).reshape(1)
    pos = jnp.arange(P, dtype=jnp.int32)
    e_pos = jnp.repeat(block_e, MOE_ROWS)
    rank = pos - pad_start[e_pos].astype(jnp.int32)
    valid = rank < counts[e_pos]
    flat = order[jnp.clip(start[e_pos].astype(jnp.int32) + rank, 0, NK - 1)]
    tok = jnp.where(valid, flat // TOP_K, 0)
    dst = jnp.where(valid, (flat % TOP_K) * N + flat // TOP_K, 0)
    idx = jnp.stack([tok.reshape(n_blocks, MOE_ROWS), dst.reshape(n_blocks, MOE_ROWS)], axis=1)
    n_valid = jnp.sum(valid.reshape(n_blocks, MOE_ROWS).astype(jnp.int32), axis=1)
    expert_out = _moe_blocks(idx, h2, block_e, n_used, n_valid, w_gu, b_gu, w_dn, b_dn, NK)
    return expert_out, gate


def _combine_kernel(x_ref, y0_ref, y1_ref, y2_ref, y3_ref, w_ref, g_ref, gate_ref, o_ref):
    w = w_ref[...]
    y = None
    for k, y_ref in enumerate((y0_ref, y1_ref, y2_ref, y3_ref)):
        term = y_ref[...] * w[:, k:k + 1]
        y = term if y is None else y + term
    o_ref[...] = x_ref[...] + gate_ref[0] * (_rms(y) * g_ref[...])


def _combine(x, expert_out, weights, g, gate, *, rows_per_mod, tm):
    T = x.shape[0]
    tm = min(tm, rows_per_mod)
    row = pl.BlockSpec((tm, D_MODEL), lambda i: (i, 0))
    slabs = [pl.BlockSpec((tm, D_MODEL), lambda i, k=k: (k * (T // tm) + i, 0)) for k in range(TOP_K)]
    return pl.pallas_call(
        _combine_kernel,
        out_shape=jax.ShapeDtypeStruct((T, D_MODEL), F32),
        grid=(T // tm,),
        in_specs=[row] + slabs + [pl.BlockSpec((tm, TOP_K), lambda i: (i, 0)),
                                  pl.BlockSpec((1, D_MODEL), lambda i: (0, 0)),
                                  pl.BlockSpec((1, 1, D_MODEL), lambda i: ((i * tm) // rows_per_mod, 0, 0))],
        out_specs=row,
        compiler_params=_params("parallel"),
        name="combine",
    )(x, *[expert_out] * TOP_K, weights, g, gate)


def _rope_tables(L):
    t = jnp.arange(L)
    lane = jnp.arange(LANE)
    axis = (lane % A_DIM) // 32
    half = (lane % 32) // 16
    n_freq = A_DIM // 4
    inv = ROPE_BASE ** (-(lane % n_freq).astype(F32) / n_freq)
    pos = jnp.where(axis[None, :] == 0, (t // GRID_W)[:, None], (t % GRID_W)[:, None]).astype(F32)
    ang = pos * inv[None, :]
    return jnp.cos(ang), jnp.where(half[None, :] == 0, -jnp.sin(ang), jnp.sin(ang))


def _arrange_w_in(w_in):
    sizes = [512] * 8 + [8, 8] + [512] * 4 + [8, 8]
    offs = [0]
    for s in sizes:
        offs.append(offs[-1] + s)
    part = lambda i: w_in[:, offs[i]:offs[i + 1]]
    big = [part(i) for i in (0, 1, 2, 3, 4, 5, 6, 7, 10, 11, 12, 13)]
    small = jnp.zeros((w_in.shape[0], HEADS, LANE), w_in.dtype)
    for base, i in ((SM_BETA, 8), (SM_DEC, 9), (SM_IG, 14), (SM_FG, 15)):
        cols = part(i).reshape(-1, 2, HEADS)
        for d in range(2):
            small = small.at[:, :, base + d].set(cols[:, d, :])
    return jnp.concatenate(big + [small.reshape(w_in.shape[0], HEADS * LANE)], axis=1)


def _layer(x, cond, lidx, lp, *, B, L, cache):
    T = B * L
    is_ctx = cache is None
    mod = (jax.nn.silu(cond) @ lp['ada_w'] + lp['ada_b']).reshape(-1, 6, 1, D_MODEL)
    shift1, scale1, gate1, shift2, scale2, gate2 = [mod[:, i] for i in range(6)]
    rows_per_mod = T if mod.shape[0] == 1 else L
    g = lp['norm_g']

    tm = 1024
    P = _norm_mod_matmul(x, g[0:1], shift1, scale1, lp['w_in'], jnp.zeros((1, N_PROJ), F32),
                         rows_per_mod=rows_per_mod, sigmoid=False, tm=tm, tn=N_PROJ // 4)
    gates = _norm_mod_matmul(x, g[0:1], shift1, scale1, lp['w_bgate'], lp['b_bgate'][None],
                             rows_per_mod=rows_per_mod, sigmoid=True, tm=tm, tn=2048)

    lam_init = 0.8 - 0.6 * math.exp(-0.3 * lidx)
    lq1, lk1, lq2, lk2 = lp['attn_lambda']
    lam = (jnp.exp(jnp.sum(lq1 * lk1)) - jnp.exp(jnp.sum(lq2 * lk2)) + lam_init).reshape(1, 1)
    attn_norm = lp['attn_norm'][None]
    if is_ctx:
        br_a, new_k, new_v = _attention(P, lam, attn_norm, B=B, L=L, tq=L, lam_init=lam_init)
    else:
        (br_a,) = _attention(P, lam, attn_norm, B=B, L=L, tq=128, lam_init=lam_init, cache=cache['attn'])

    br_b = _pool(P, lp['pool_w'], lp['pool_b'][:, None], lp['pool_scale'][None], B=B, L=L)

    gdn_args = (P, lp['gdn_A_log'], lp['gdn_dt_bias'], lp['gdn_conv'], lp['gdn_norm'][None])
    ml_args = (P, lp['mlstm_bias_i'], lp['mlstm_bias_f'], lp['mlstm_norm'][None])
    if is_ctx:
        br_c, new_gdn = _gdn(*gdn_args, B=B, L=L)
        br_d, new_c, new_n, new_m = _mlstm(*ml_args, B=B, L=L)
        ctx_out = (new_k, new_v, new_gdn, new_c,
                   new_n[:, :, :, 0, :HEAD_DIM].transpose(0, 2, 1, 3), new_m[:, :, :, 0, 0].transpose(0, 2, 1))
    else:
        (br_c,) = _gdn(*gdn_args, B=B, L=L, state=cache['gdn'])
        (br_d,) = _mlstm(*ml_args, B=B, L=L, state=cache['mlstm'])
        ctx_out = None

    x, h2, logits = _merge(x, gates, (br_a, br_b, br_c, br_d), lp['w_branch'], lp['w_out'], g[1:2], gate1,
                           g[2:3], shift2, scale2, lp['router_w'], lp['router_b'],
                           rows_per_mod=rows_per_mod, tm=256)
    expert_out, weights = _moe(h2, logits[:, :N_EXPERTS], lp['moe_w_gu'], lp['moe_b_gu'], lp['moe_w_dn'],
                               lp['moe_b_dn'])
    x = _combine(x, expert_out, weights, g[3:4], gate2, rows_per_mod=rows_per_mod, tm=512)
    return x, ctx_out


def kernel(x_prompt, x_sample, cache_attn_k, cache_attn_v, state_gdn, state_mlstm_C, state_mlstm_n, state_mlstm_m, c, c_ctx, ada_w, ada_b, norm_g, w_in, w_bgate, b_bgate, w_branch, w_out, attn_lambda, attn_norm, pool_w, pool_b, pool_scale, gdn_conv, gdn_A_log, gdn_dt_bias, gdn_norm, mlstm_bias_i, mlstm_bias_f, mlstm_norm, router_w, router_b, moe_w_gu, moe_b_gu, moe_w_dn, moe_b_dn):
    Bp, Lp, _ = x_prompt.shape
    Bs, Ls, _ = x_sample.shape
    y_p = x_prompt.reshape(Bp * Lp, D_MODEL)
    y_s = x_sample.reshape(Bs * Ls, D_MODEL)
    cos, sin = _rope_tables(Ls)
    n0 = state_mlstm_n.transpose(1, 0, 3, 2, 4)[:, :, :, :, None, :]
    m0 = jnp.broadcast_to(state_mlstm_m.transpose(1, 0, 3, 2)[..., None, None], n0.shape)
    ctx_states = []
    for l in range(DEPTH):
        lp = {
            'ada_w': ada_w[l], 'ada_b': ada_b[l], 'norm_g': norm_g[l],
            'w_in': _arrange_w_in(w_in[l]).astype(BF16), 'w_bgate': w_bgate[l].astype(BF16),
            'b_bgate': b_bgate[l], 'w_branch': w_branch[l].astype(BF16), 'w_out': w_out[l].astype(BF16),
            'attn_lambda': attn_lambda[l], 'attn_norm': attn_norm[l],
            'pool_w': pool_w[l], 'pool_b': pool_b[l], 'pool_scale': pool_scale[l],
            'gdn_conv': gdn_conv[l], 'gdn_A_log': gdn_A_log[l], 'gdn_dt_bias': gdn_dt_bias[l],
            'gdn_norm': gdn_norm[l], 'mlstm_bias_i': mlstm_bias_i[l], 'mlstm_bias_f': mlstm_bias_f[l],
            'mlstm_norm': mlstm_norm[l],
            'router_w': jnp.pad(router_w[l], ((0, 0), (0, LANE - N_EXPERTS))),
            'router_b': jnp.pad(router_b[l], (0, LANE - N_EXPERTS))[None],
            'moe_w_gu': moe_w_gu[l].astype(BF16), 'moe_b_gu': moe_b_gu[l][:, None],
            'moe_w_dn': moe_w_dn[l].astype(BF16), 'moe_b_dn': moe_b_dn[l][:, None],
        }
        y_p, st = _layer(y_p, c_ctx[None], l, lp, B=Bp, L=Lp, cache=None)
        ctx_states.append(st)
        cache = {
            'attn': {'k': cache_attn_k, 'v': cache_attn_v, 'layer': l, 'cos': cos, 'sin': sin},
            'gdn': (state_gdn, l),
            'mlstm': (state_mlstm_C, l, n0[l], m0[l]),
        }
        y_s, _ = _layer(y_s, c, l, lp, B=Bs, L=Ls, cache=cache)
    outs = [jnp.stack([s[i] for s in ctx_states], axis=1) for i in range(6)]
    return (y_p.reshape(Bp, Lp, D_MODEL), y_s.reshape(Bs, Ls, D_MODEL), *outs)
```

```python
import functools
import math

import jax
import jax.numpy as jnp
from jax import lax
from jax.experimental import pallas as pl
from jax.experimental.pallas import tpu as pltpu

F32 = jnp.float32
BF16 = jnp.bfloat16

D_MODEL = 1024
DEPTH = 2
GRID_W = 64
EPS = 1e-6
LANE = 128
HEADS = 4
HEAD_DIM = 128
A_DIM = 64
ROPE_BASE = 10000.0
POOL_WINDOWS = (2, 4, 8, 16)
CHUNK = 64
N_BRANCH = 4
BRANCH_WIDTH = 512
N_EXPERTS = 32
TOP_K = 4
D_EXPERT = 1024
SWIGLU_LIMIT = 7.0
SWIGLU_ALPHA = 1.702
MOE_ROWS = 256
VMEM_LIMIT = 56 * 1024 * 1024

COL_AQ, COL_AK, COL_AV, COL_POOL, COL_CQ, COL_CK, COL_CV, COL_CG, COL_MQ, COL_MK, COL_MV, COL_MG, COL_SM = range(13)
N_PROJ = 13 * BRANCH_WIDTH
SM_BETA, SM_DEC, SM_IG, SM_FG = 0, 2, 4, 6


def _params(*sem):
    return pltpu.CompilerParams(dimension_semantics=sem, vmem_limit_bytes=VMEM_LIMIT)


def _dot(a, b):
    return jnp.dot(a.astype(BF16), b.astype(BF16), preferred_element_type=F32)


def _dot_nt(a, b):
    return lax.dot_general(a.astype(BF16), b.astype(BF16), (((1,), (1,)), ((), ())),
                           preferred_element_type=F32)


def _dot_tn(a, b):
    return lax.dot_general(a.astype(BF16), b.astype(BF16), (((0,), (0,)), ((), ())),
                           preferred_element_type=F32)


def _split(a):
    hi = a.astype(BF16)
    lo = (a - hi.astype(F32)).astype(BF16)
    return hi, lo


def _dot3(a, b):
    ah, al = _split(a)
    bh, bl = _split(b)
    d = lambda x, y: jnp.dot(x, y, preferred_element_type=F32)
    return d(ah, bh) + (d(ah, bl) + d(al, bh))


def _dot_mask(mask, b):
    m = jnp.where(mask, 1.0, 0.0).astype(BF16)
    b0 = b.astype(BF16)
    r1 = b - b0.astype(F32)
    b1 = r1.astype(BF16)
    b2 = (r1 - b1.astype(F32)).astype(BF16)
    d = lambda y: jnp.dot(m, y, preferred_element_type=F32)
    return d(b0) + (d(b1) + d(b2))


ROW_TILE = 8
assert ROW_TILE * LANE == D_MODEL


def _store_row_tiles(ref, x):
    rows = x.shape[0]
    for c in range(ROW_TILE):
        ref[pl.ds(c, rows, stride=ROW_TILE), :] = x[:, c * LANE:(c + 1) * LANE]


def _load_row_tiles(ref, rows, dtype=F32):
    return jnp.concatenate([ref[pl.ds(c, rows, stride=ROW_TILE), :].astype(dtype) for c in range(ROW_TILE)],
                           axis=1)


def _rms(x):
    return x * lax.rsqrt(jnp.mean(x * x, axis=-1, keepdims=True) + EPS)


def _sigmoid(x):
    return 1.0 / (1.0 + jnp.exp(-x))


def _softplus(x):
    return jnp.maximum(x, 0.0) + jnp.log(1.0 + jnp.exp(-jnp.abs(x)))


def _proj_kernel(x_ref, g_ref, sh_ref, sc_ref, w_ref, b_ref, o_ref, h_scr, *, sigmoid):
    @pl.when(pl.program_id(1) == 0)
    def _():
        y = _rms(x_ref[...]) * g_ref[...]
        h_scr[...] = (y * (1.0 + sc_ref[0]) + sh_ref[0]).astype(BF16)

    acc = jnp.dot(h_scr[...], w_ref[...], preferred_element_type=F32) + b_ref[...]
    o_ref[...] = _sigmoid(acc) if sigmoid else acc


def _norm_mod_matmul(x, g, shift, scale, w, bias, *, rows_per_mod, sigmoid, tm, tn):
    T, Dm = x.shape
    N = w.shape[1]
    tm = min(tm, rows_per_mod)
    mod_idx = lambda i, j: ((i * tm) // rows_per_mod, 0, 0)
    return pl.pallas_call(
        functools.partial(_proj_kernel, sigmoid=sigmoid),
        out_shape=jax.ShapeDtypeStruct((T, N), F32),
        grid=(T // tm, N // tn),
        in_specs=[
            pl.BlockSpec((tm, Dm), lambda i, j: (i, 0)),
            pl.BlockSpec((1, Dm), lambda i, j: (0, 0)),
            pl.BlockSpec((1, 1, Dm), mod_idx),
            pl.BlockSpec((1, 1, Dm), mod_idx),
            pl.BlockSpec((Dm, tn), lambda i, j: (0, j)),
            pl.BlockSpec((1, tn), lambda i, j: (0, j)),
        ],
        out_specs=pl.BlockSpec((tm, tn), lambda i, j: (i, j)),
        scratch_shapes=[pltpu.VMEM((tm, Dm), BF16)],
        compiler_params=_params("parallel", "arbitrary"),
        name="proj",
    )(x, g, shift, scale, w, bias)


def _rope(x, cos, sin):
    lane = lax.broadcasted_iota(jnp.int32, x.shape, 1)
    first = (lane % 32) < 16
    partner = jnp.where(first, pltpu.roll(x, LANE - 16, 1), pltpu.roll(x, 16, 1))
    return x * cos + partner * sin


def _attn_kernel(*refs, has_cache, n_ctx, out_scale):
    if has_cache:
        (lam_ref, q_ref, k_ref, v_ref, ck_ref, cv_ref, cosq_ref, sinq_ref, cosk_ref, sink_ref, nrm_ref,
         o_ref, kall, vall) = refs
    else:
        lam_ref, q_ref, k_ref, v_ref, nrm_ref, o_ref, ko_ref, vo_ref, kall, vall = refs

    @pl.when(pl.program_id(2) == 0)
    def _():
        k = k_ref[...]
        v = v_ref[...]
        if has_cache:
            kall[pl.ds(0, n_ctx), :] = ck_ref[...].astype(BF16)
            vall[pl.ds(0, n_ctx), :] = cv_ref[...].astype(BF16)
            k = _rope(k, cosk_ref[...], sink_ref[...])
        else:
            ko_ref[...] = k
            vo_ref[...] = v
        kall[pl.ds(n_ctx, k.shape[0]), :] = k.astype(BF16)
        vall[pl.ds(n_ctx, k.shape[0]), :] = v.astype(BF16)

    q = q_ref[...]
    if has_cache:
        q = _rope(q, cosq_ref[...], sinq_ref[...])
    q = q * (A_DIM ** -0.5 * math.log2(math.e))
    lane = lax.broadcasted_iota(jnp.int32, q.shape, 1)
    q1 = jnp.where(lane < A_DIM, q, 0.0)
    q2 = jnp.where(lane >= A_DIM, q, 0.0)
    keys = kall[...]
    lam = lam_ref[0, 0]

    def probs(qm):
        s = _dot_nt(qm, keys)
        p = jnp.exp2(s - jnp.max(s, axis=-1, keepdims=True))
        return p, jnp.sum(p, axis=-1, keepdims=True)

    p1, l1 = probs(q1)
    p2, l2 = probs(q2)
    a = p1 * (1.0 / l1) - p2 * (lam / l2)
    o = jnp.dot(a.astype(BF16), vall[...], preferred_element_type=F32)
    o_ref[...] = _rms(o) * nrm_ref[...] * out_scale


def _attention(P, lam, norm, *, B, L, tq, lam_init, cache=None):
    T = B * L
    nq = L // tq
    has_cache = cache is not None
    n_ctx = cache["k"].shape[3] if has_cache else 0
    colq, colk, colv = COL_AQ * HEADS, COL_AK * HEADS, COL_AV * HEADS
    in_specs = [
        pl.BlockSpec(memory_space=pltpu.SMEM),
        pl.BlockSpec((tq, LANE), lambda b, h, i: (b * nq + i, colq + h)),
        pl.BlockSpec((L, LANE), lambda b, h, i: (b, colk + h)),
        pl.BlockSpec((L, LANE), lambda b, h, i: (b, colv + h)),
    ]
    args = [lam, P, P, P]
    if has_cache:
        l = cache["layer"]
        cspec = pl.BlockSpec((None, None, None, n_ctx, LANE), lambda b, h, i: (b, l, h, 0, 0))
        in_specs += [cspec, cspec,
                     pl.BlockSpec((tq, LANE), lambda b, h, i: (i, 0)),
                     pl.BlockSpec((tq, LANE), lambda b, h, i: (i, 0)),
                     pl.BlockSpec((L, LANE), lambda b, h, i: (0, 0)),
                     pl.BlockSpec((L, LANE), lambda b, h, i: (0, 0))]
        args += [cache["k"], cache["v"], cache["cos"], cache["sin"], cache["cos"], cache["sin"]]
    in_specs.append(pl.BlockSpec((1, LANE), lambda b, h, i: (0, 0)))
    args.append(norm)
    out_shape = [jax.ShapeDtypeStruct((T, HEADS * LANE), F32)]
    out_specs = [pl.BlockSpec((tq, LANE), lambda b, h, i: (b * nq + i, h))]
    if not has_cache:
        kv_shape = jax.ShapeDtypeStruct((B, HEADS, L, LANE), F32)
        kv_spec = pl.BlockSpec((None, None, L, LANE), lambda b, h, i: (b, h, 0, 0))
        out_shape += [kv_shape, kv_shape]
        out_specs += [kv_spec, kv_spec]
    return pl.pallas_call(
        functools.partial(_attn_kernel, has_cache=has_cache, n_ctx=n_ctx, out_scale=1.0 - lam_init),
        out_shape=out_shape,
        grid=(B, HEADS, nq),
        in_specs=in_specs,
        out_specs=out_specs,
        scratch_shapes=[pltpu.VMEM((n_ctx + L, LANE), BF16), pltpu.VMEM((n_ctx + L, LANE), BF16)],
        compiler_params=_params("parallel", "parallel", "arbitrary"),
        name="diff_attention",
    )(*args)


POOL_PAD = 16


def _pool_kernel(x_ref, w_ref, b_ref, s_ref, o_ref, pad, *, L):
    zeros = jnp.zeros((POOL_PAD, LANE), F32)
    pad[pl.ds(0, POOL_PAD), :] = zeros
    pad[pl.ds(POOL_PAD + L, POOL_PAD), :] = zeros
    x = x_ref[...]
    pad[pl.ds(POOL_PAD, L), :] = x
    t = lax.broadcasted_iota(jnp.int32, (L, LANE), 0)
    g = pl.program_id(1)
    for gi, win in enumerate(POOL_WINDOWS):
        @pl.when(g == gi)
        def _(win=win):
            half = win // 2
            acc = pad[pl.ds(POOL_PAD - half, L), :]
            for k in range(1 - half, half):
                acc = acc + pad[pl.ds(POOL_PAD + k, L), :]
            cnt = (jnp.minimum(t + half, L) - jnp.maximum(t - half, 0)).astype(F32)
            pooled = acc / cnt - x
            o_ref[...] = (_dot(pooled, w_ref[...]) + b_ref[...]) * s_ref[...]


def _pool(P, w, b, scale, *, B, L):
    T = B * L
    G = len(POOL_WINDOWS)
    return pl.pallas_call(
        functools.partial(_pool_kernel, L=L),
        out_shape=jax.ShapeDtypeStruct((T, G * LANE), F32),
        grid=(B, G),
        in_specs=[
            pl.BlockSpec((L, LANE), lambda bi, g: (bi, COL_POOL * HEADS + g)),
            pl.BlockSpec((None, LANE, LANE), lambda bi, g: (g, 0, 0)),
            pl.BlockSpec((None, 1, LANE), lambda bi, g: (g, 0, 0)),
            pl.BlockSpec((1, LANE), lambda bi, g: (0, g)),
        ],
        out_specs=pl.BlockSpec((L, LANE), lambda bi, g: (bi, g)),
        scratch_shapes=[pltpu.VMEM((L + 2 * POOL_PAD, LANE), F32)],
        compiler_params=_params("parallel", "parallel"),
        name="pool_mixer",
    )(P, w, b, scale)


GROUP_CHUNKS = 2
GROUP = GROUP_CHUNKS * CHUNK
MAX_GROUPS_PER_STEP = 4


def _groups_per_step(L):
    return min(MAX_GROUPS_PER_STEP, L // GROUP)


def _group_masks(backward):
    row = lax.broadcasted_iota(jnp.int32, (GROUP, GROUP), 0)
    col = lax.broadcasted_iota(jnp.int32, (GROUP, GROUP), 1)
    same = (row // CHUNK) == (col // CHUNK)
    lower, upper = row >= col, row <= col
    if backward:
        lower, upper = upper, lower
    return dict(same=same, incl=same & lower, strict=same & lower & (row != col), incl_t=same & upper,
                eye=row == col)


INV_BASE = 8


def _block_masks():
    row = lax.broadcasted_iota(jnp.int32, (GROUP, GROUP), 0)
    col = lax.broadcasted_iota(jnp.int32, (GROUP, GROUP), 1)
    sizes = [INV_BASE << i for i in range(int(math.log2(CHUNK // INV_BASE)) + 1)]
    same = [(row // s) == (col // s) for s in sizes]
    return [same[0]] + [cur & jnp.logical_not(prev) for prev, cur in zip(same[:-1], same[1:])]


def _unit_triangular_inverses(ns, eye, blocks):
    base = [jnp.where(blocks[0], n, 0.0) for n in ns]
    invs = [eye - b for b in base]
    pws = [_dot(b, b) for b in base]
    rounds = int(math.log2(INV_BASE)) - 1
    for s in range(rounds):
        invs = [t + _dot(t, p) for t, p in zip(invs, pws)]
        if s + 1 < rounds:
            pws = [_dot(p, p) for p in pws]
    for join in blocks[1:]:
        offs = [jnp.where(join, n, 0.0) for n in ns]
        xs = [_dot(o, t) for o, t in zip(offs, invs)]
        invs = [t - _dot(t, x) for t, x in zip(invs, xs)]
    return invs


def _dot_mask2(mask, b):
    m = jnp.where(mask, 1.0, 0.0).astype(BF16)
    b0, b1 = _split(b)
    return jnp.dot(m, b0, preferred_element_type=F32) + jnp.dot(m, b1, preferred_element_type=F32)


def _per_chunk(x, reduce):
    parts = [reduce(x[c * CHUNK:(c + 1) * CHUNK], axis=0, keepdims=True) for c in range(GROUP_CHUNKS)]
    col = jnp.concatenate([jnp.broadcast_to(p, (CHUNK, 1)) for p in parts], axis=0)
    return parts, col


def _scan_order(step, n_steps, groups):
    base = (step * groups * GROUP, (n_steps - 1 - step) * groups * GROUP)
    fwd = [(gi, c) for gi in range(groups) for c in range(GROUP_CHUNKS)]
    return base, (fwd, fwd[::-1])


CONV_PAD = 8


def _gdn_kernel(*refs, L, has_state):
    alog_ref, dtb_ref, q_ref, k_ref, v_ref, gate_ref, sm_ref, cwq_ref, cwk_ref, cwv_ref, nrm_ref = refs[:11]
    refs = refs[11:]
    if has_state:
        s0_ref, o_ref, pad, qs, ks, vs, o_f, o_b, state = refs
    else:
        o_ref, sout_ref, pad, qs, ks, vs, o_f, o_b, state = refs
    h = pl.program_id(1)
    n_chunks = L // CHUNK

    zeros = jnp.zeros((CONV_PAD, LANE), F32)
    pad[pl.ds(0, CONV_PAD), :] = zeros
    pad[pl.ds(CONV_PAD + L, CONV_PAD), :] = zeros

    def conv_silu(x_ref, w_ref):
        pad[pl.ds(CONV_PAD, L), :] = x_ref[...]
        w = w_ref[...]
        y = (pad[pl.ds(CONV_PAD - 1, L), :] * w[0:1] + pad[pl.ds(CONV_PAD, L), :] * w[1:2]
             + pad[pl.ds(CONV_PAD + 1, L), :] * w[2:3])
        return y * _sigmoid(y)

    def l2n(x):
        return x * lax.rsqrt(jnp.sum(x * x, axis=-1, keepdims=True) + EPS)

    qs[...] = l2n(conv_silu(q_ref, cwq_ref)) * (HEAD_DIM ** -0.5)
    ks[...] = l2n(conv_silu(k_ref, cwk_ref))
    vs[...] = conv_silu(v_ref, cwv_ref)
    if has_state:
        state[...] = s0_ref[...]
    else:
        state[...] = jnp.zeros_like(state)

    groups = _groups_per_step(L)
    n_steps = L // (groups * GROUP)
    o_scr = (o_f, o_b)
    masks = (_group_masks(False), _group_masks(True))
    eye = jnp.where(masks[0]['eye'], 1.0, 0.0)
    blocks = _block_masks()
    items = [(d, gi) for gi in range(groups) for d in (0, 1)]
    chunks = [slice(c * CHUNK, (c + 1) * CHUNK) for c in range(GROUP_CHUNKS)]

    def body(step, carry):
        base, order = _scan_order(step, n_steps, groups)
        it = {}
        for key in items:
            d, gi = key
            rows = pl.ds(pl.multiple_of(base[d] + gi * GROUP, GROUP), GROUP)
            q, k, v, sm = qs[rows, :], ks[rows, :], vs[rows, :], sm_ref[rows, :]
            beta = _sigmoid(sm[:, SM_BETA + d:SM_BETA + d + 1])
            dec = sm[:, SM_DEC + d:SM_DEC + d + 1]
            g = -jnp.exp(alog_ref[d, h]) * _softplus(dec + dtb_ref[d, h])
            it[key] = dict(q=q, k=k, kb=k * beta, vb=v * beta, g=g, gb=jnp.broadcast_to(g, (GROUP, GROUP)))
        for (d, gi), x in it.items():
            m = masks[d]
            x['cum_i'] = _dot_mask2(m['incl'], x['gb'])
            x['cum_j'] = _dot_mask2(m['same'], jnp.where(m['incl_t'], x['gb'], 0.0))
            x['kk'] = _dot_nt(x['kb'], x['k'])
            x['qk'] = _dot_nt(x['q'], x['k'])
        for (d, gi), x in it.items():
            m = masks[d]
            decay = jnp.where(m['incl'], jnp.exp(jnp.where(m['incl'], x['cum_i'] - x['cum_j'], 0.0)), 0.0)
            x['gc'] = x['cum_i'][:, 0:1]
            x['tot'], x['tot_col'] = _per_chunk(x['g'], jnp.sum)
            x['qk'] = x['qk'] * decay
            x['n'] = jnp.where(m['strict'], x['kk'] * decay, 0.0)
        invs = _unit_triangular_inverses([x['n'] for x in it.values()], eye, blocks)
        for x, inv in zip(it.values(), invs):
            x['inv'] = inv
            x['egc'] = jnp.exp(x['gc'])
            x['sol'] = _dot3(x['inv'], jnp.concatenate([x['vb'], x['kb'] * x['egc']], axis=-1))
        for x in it.values():
            u, w = x['sol'][:, :HEAD_DIM], x['sol'][:, HEAD_DIM:]
            x['qp'] = x['q'] * x['egc'] - _dot(x['qk'], w)
            x['op'] = _dot(x['qk'], u)
            kd = x['k'] * jnp.exp(x['tot_col'] - x['gc'])
            x['ab'] = [_dot_tn(kd[cs], x['sol'][cs]) for cs in chunks]
            x['gl'] = [jnp.exp(t) for t in x['tot']]
        S = [state[0], state[1]]
        for stp in range(len(order[0])):
            for d in (0, 1):
                gi, c = order[d][stp]
                x = it[(d, gi)]
                ab = x['ab'][c]
                r = _dot(jnp.concatenate([ab[:, HEAD_DIM:], x['qp'][chunks[c]]], axis=0), S[d])
                rows = pl.ds(pl.multiple_of(base[d] + gi * GROUP + c * CHUNK, CHUNK), CHUNK)
                o_scr[d][rows, :] = x['op'][chunks[c]] + r[HEAD_DIM:]
                S[d] = x['gl'][c] * S[d] + (ab[:, :HEAD_DIM] - r[:HEAD_DIM])
        state[0] = S[0]
        state[1] = S[1]
        return carry

    lax.fori_loop(0, n_steps, body, 0)
    gate = gate_ref[...]
    o_ref[...] = _rms(o_f[...] + o_b[...]) * nrm_ref[...] * (gate * _sigmoid(gate))
    if not has_state:
        sout_ref[...] = state[...]


def _col_spec(L, col):
    return pl.BlockSpec((L, LANE), lambda b, h: (b, col * HEADS + h))


def _gdn(P, a_log, dt_bias, conv_w, norm, *, B, L, state=None):
    T = B * L
    has_state = state is not None
    smem = pl.BlockSpec(memory_space=pltpu.SMEM)
    in_specs = [smem, smem, _col_spec(L, COL_CQ), _col_spec(L, COL_CK), _col_spec(L, COL_CV),
                _col_spec(L, COL_CG), _col_spec(L, COL_SM)]
    in_specs += [pl.BlockSpec((3, LANE), lambda b, h, j=j: (0, j * HEADS + h)) for j in range(3)]
    in_specs.append(pl.BlockSpec((1, LANE), lambda b, h: (0, 0)))
    args = [a_log, dt_bias, P, P, P, P, P, conv_w, conv_w, conv_w, norm]
    out_shape = [jax.ShapeDtypeStruct((T, HEADS * LANE), F32)]
    out_specs = [pl.BlockSpec((L, LANE), lambda b, h: (b, h))]
    if has_state:
        arr, l = state
        in_specs.append(pl.BlockSpec((None, None, 2, None, HEAD_DIM, HEAD_DIM), lambda b, h: (b, l, 0, h, 0, 0)))
        args.append(arr)
    else:
        out_shape.append(jax.ShapeDtypeStruct((B, 2, HEADS, HEAD_DIM, HEAD_DIM), F32))
        out_specs.append(pl.BlockSpec((None, 2, None, HEAD_DIM, HEAD_DIM), lambda b, h: (b, 0, h, 0, 0)))
    seq = pltpu.VMEM((L, LANE), F32)
    return pl.pallas_call(
        functools.partial(_gdn_kernel, L=L, has_state=has_state),
        out_shape=out_shape,
        grid=(B, HEADS),
        in_specs=in_specs,
        out_specs=out_specs,
        scratch_shapes=[pltpu.VMEM((L + 2 * CONV_PAD, LANE), F32), seq, seq, seq, seq, seq,
                        pltpu.VMEM((2, HEAD_DIM, HEAD_DIM), F32)],
        compiler_params=_params("parallel", "parallel"),
        name="gated_deltanet",
    )(*args)


def _mlstm_kernel(*refs, L, has_state):
    bi_ref, bf_ref, q_ref, k_ref, v_ref, gate_ref, sm_ref, nrm_ref = refs[:8]
    refs = refs[8:]
    if has_state:
        c0_ref, n0_ref, m0_ref, o_ref, h_f, h_b, c_st, n_st, m_st = refs
    else:
        o_ref, cout_ref, nout_ref, mout_ref, h_f, h_b, c_st, n_st, m_st = refs
    h = pl.program_id(1)
    n_chunks = L // CHUNK

    if has_state:
        c_st[...] = c0_ref[...]
        n_st[...] = n0_ref[...]
        m_st[...] = m0_ref[...]
    else:
        c_st[...] = jnp.zeros_like(c_st)
        n_st[...] = jnp.zeros_like(n_st)
        m_st[...] = jnp.zeros_like(m_st)

    groups = _groups_per_step(L)
    n_steps = L // (groups * GROUP)
    h_scr = (h_f, h_b)
    masks = (_group_masks(False), _group_masks(True))
    items = [(d, gi) for gi in range(groups) for d in (0, 1)]
    chunks = [slice(c * CHUNK, (c + 1) * CHUNK) for c in range(GROUP_CHUNKS)]

    def body(step, carry):
        base, order = _scan_order(step, n_steps, groups)
        it = {}
        for key in items:
            d, gi = key
            rows = pl.ds(pl.multiple_of(base[d] + gi * GROUP, GROUP), GROUP)
            q, v, sm = q_ref[rows, :], v_ref[rows, :], sm_ref[rows, :]
            k = k_ref[rows, :] * (HEAD_DIM ** -0.5)
            ig = sm[:, SM_IG + d:SM_IG + d + 1] + bi_ref[d, h]
            fg = sm[:, SM_FG + d:SM_FG + d + 1] + bf_ref[d, h]
            lf = -_softplus(-fg)
            it[key] = dict(q=q, k=k, v=v, ig=ig, lf=lf, lfb=jnp.broadcast_to(lf, (GROUP, GROUP)),
                           igb=jnp.broadcast_to(ig, (GROUP, GROUP)))
        for (d, gi), x in it.items():
            m = masks[d]
            x['cum_i'] = _dot_mask2(m['incl'], x['lfb'])
            x['cum_j'] = _dot_mask2(m['same'], jnp.where(m['incl_t'], x['lfb'], 0.0)
                                    - jnp.where(m['eye'], x['igb'], 0.0))
            x['qk'] = _dot_nt(x['q'], x['k'])
        for (d, gi), x in it.items():
            m = masks[d]
            dm = x['cum_i'] - x['cum_j']
            x['b'] = x['cum_i'][:, 0:1]
            x['tot'], tot_col = _per_chunk(x['lf'], jnp.sum)
            x['m_intra'] = jnp.max(jnp.where(m['incl'], dm, -jnp.inf), axis=-1, keepdims=True)
            e = tot_col - x['b'] + x['ig']
            x['m_end'], m_end_col = _per_chunk(e, jnp.max)
            p = jnp.where(m['incl'], jnp.exp(jnp.where(m['incl'], dm - x['m_intra'], 0.0)), 0.0) * x['qk']
            x['p_sum'] = jnp.sum(p, axis=-1, keepdims=True)
            x['p'] = p
            x['kw'] = x['k'] * jnp.exp(e - m_end_col)
        for x in it.values():
            x['pv'] = _dot(x['p'], x['v'])
            x['kv'] = [_dot_tn(x['kw'][cs], x['v'][cs]) for cs in chunks]
            x['k_sum'] = [jnp.sum(x['kw'][cs], axis=0, keepdims=True) for cs in chunks]
        outs = []
        for d in (0, 1):
            C, n, m = c_st[d], n_st[d], m_st[d][:, 0:1]
            for gi, c in order[d]:
                x = it[(d, gi)]
                cs = chunks[c]
                m_t = jnp.maximum(x['b'][cs] + m, x['m_intra'][cs])
                w_inter = jnp.exp(x['b'][cs] + m - m_t)
                local = jnp.exp(x['m_intra'][cs] - m_t)
                den = (w_inter * jnp.sum(x['q'][cs] * n, axis=-1, keepdims=True) + local * x['p_sum'][cs])
                scale = 1.0 / jnp.maximum(jnp.abs(den), jnp.exp(-m_t))
                outs.append((d, gi, c, C, w_inter * scale, local * scale))
                m_new = jnp.maximum(x['tot'][c] + m, x['m_end'][c])
                carry_decay = jnp.exp(x['tot'][c] + m - m_new)
                local_new = jnp.exp(x['m_end'][c] - m_new)
                C = carry_decay * C + local_new * x['kv'][c]
                n = carry_decay * n + local_new * x['k_sum'][c]
                m = m_new
            c_st[d] = C
            n_st[d] = n
            m_st[d] = jnp.broadcast_to(m, (1, LANE))
        for d, gi, c, C, w_scale, p_scale in outs:
            x = it[(d, gi)]
            cs = chunks[c]
            rows = pl.ds(pl.multiple_of(base[d] + gi * GROUP + c * CHUNK, CHUNK), CHUNK)
            h_scr[d][rows, :] = w_scale * _dot(x['q'][cs], C) + p_scale * x['pv'][cs]
        return carry

    lax.fori_loop(0, n_steps, body, 0)
    o_ref[...] = _rms(h_f[...] + h_b[...]) * nrm_ref[...] * _sigmoid(gate_ref[...])
    if not has_state:
        cout_ref[...] = c_st[...]
        nout_ref[...] = n_st[...]
        mout_ref[...] = m_st[...]


def _mlstm(P, bias_i, bias_f, norm, *, B, L, state=None):
    T = B * L
    has_state = state is not None
    smem = pl.BlockSpec(memory_space=pltpu.SMEM)
    in_specs = [smem, smem, _col_spec(L, COL_MQ), _col_spec(L, COL_MK), _col_spec(L, COL_MV),
                _col_spec(L, COL_MG), _col_spec(L, COL_SM), pl.BlockSpec((1, LANE), lambda b, h: (0, 0))]
    args = [bias_i, bias_f, P, P, P, P, P, norm]
    out_shape = [jax.ShapeDtypeStruct((T, HEADS * LANE), F32)]
    out_specs = [pl.BlockSpec((L, LANE), lambda b, h: (b, h))]
    vec_spec = pl.BlockSpec((None, None, 2, 1, LANE), lambda b, h: (b, h, 0, 0, 0))
    if has_state:
        c_arr, l, n_arr, m_arr = state
        in_specs += [pl.BlockSpec((None, None, 2, None, HEAD_DIM, HEAD_DIM), lambda b, h: (b, l, 0, h, 0, 0)),
                     vec_spec, vec_spec]
        args += [c_arr, n_arr, m_arr]
    else:
        out_shape += [jax.ShapeDtypeStruct((B, 2, HEADS, HEAD_DIM, HEAD_DIM), F32),
                      jax.ShapeDtypeStruct((B, HEADS, 2, 1, LANE), F32),
                      jax.ShapeDtypeStruct((B, HEADS, 2, 1, LANE), F32)]
        out_specs += [pl.BlockSpec((None, 2, None, HEAD_DIM, HEAD_DIM), lambda b, h: (b, 0, h, 0, 0)),
                      vec_spec, vec_spec]
    seq = pltpu.VMEM((L, LANE), F32)
    return pl.pallas_call(
        functools.partial(_mlstm_kernel, L=L, has_state=has_state),
        out_shape=out_shape,
        grid=(B, HEADS),
        in_specs=in_specs,
        out_specs=out_specs,
        scratch_shapes=[seq, seq, pltpu.VMEM((2, HEAD_DIM, HEAD_DIM), F32), pltpu.VMEM((2, 1, LANE), F32),
                        pltpu.VMEM((2, 1, LANE), F32)],
        compiler_params=_params("parallel", "parallel"),
        name="mlstm",
    )(*args)


def _merge_kernel(x_ref, gates_ref, a_ref, b_ref, c_ref, d_ref, wb_ref, wo_ref, g1_ref, gate1_ref, g2_ref,
                  sh2_ref, sc2_ref, rw_ref, rb_ref, xo_ref, h2_ref, lg_ref):
    merged = None
    for i, br in enumerate((a_ref, b_ref, c_ref, d_ref)):
        term = gates_ref[:, i * D_MODEL:(i + 1) * D_MODEL] * _dot(br[...], wb_ref[i])
        merged = term if merged is None else merged + term
    t = _rms(_dot(merged, wo_ref[...])) * g1_ref[...]
    x = x_ref[...] + gate1_ref[0] * t
    xo_ref[...] = x
    h2 = _rms(x) * g2_ref[...] * (1.0 + sc2_ref[0]) + sh2_ref[0]
    _store_row_tiles(h2_ref, h2)
    lg_ref[...] = _dot3(h2, rw_ref[...]) + rb_ref[...]


def _merge(x, gates, branches, wb, wo, g1, gate1, g2, shift2, scale2, rw, rb, *, rows_per_mod, tm):
    T = x.shape[0]
    tm = min(tm, rows_per_mod)
    row = lambda n: pl.BlockSpec((tm, n), lambda i: (i, 0))
    const = lambda *shape: pl.BlockSpec(shape, lambda i: (0,) * len(shape))
    mod = pl.BlockSpec((1, 1, D_MODEL), lambda i: ((i * tm) // rows_per_mod, 0, 0))
    return pl.pallas_call(
        _merge_kernel,
        out_shape=[jax.ShapeDtypeStruct((T, D_MODEL), F32), jax.ShapeDtypeStruct((T * ROW_TILE, LANE), F32),
                   jax.ShapeDtypeStruct((T, LANE), F32)],
        grid=(T // tm,),
        in_specs=[row(D_MODEL), row(N_BRANCH * D_MODEL)] + [row(BRANCH_WIDTH)] * 4
                 + [const(N_BRANCH, BRANCH_WIDTH, D_MODEL), const(D_MODEL, D_MODEL), const(1, D_MODEL), mod,
                    const(1, D_MODEL), mod, mod, const(D_MODEL, LANE), const(1, LANE)],
        out_specs=[row(D_MODEL), pl.BlockSpec((tm * ROW_TILE, LANE), lambda i: (i, 0)), row(LANE)],
        compiler_params=_params("parallel"),
        name="merge",
    )(x, gates, *branches, wb, wo, g1, gate1, g2, shift2, scale2, rw, rb)


def _moe_kernel(be_ref, nu_ref, nv_ref, idx_hbm, h2_hbm, wgu_ref, bgu_ref, wdn_ref, bdn_ref, out_hbm,
                idx_smem, x0, x1, y0, y1, p0, p1, sem_idx, sem_in, sem_out):
    i = pl.program_id(0)
    n_used = nu_ref[0]
    last_block = pl.num_programs(0) - 1
    xbuf, ybuf, pad_rows = (x0, x1), (y0, y1), (p0, p1)

    def idx_copy(block, s):
        return pltpu.make_async_copy(idx_hbm.at[block], idx_smem.at[s], sem_idx.at[s])

    def tile(ref, first_row):
        if not isinstance(first_row, int):
            first_row = pl.multiple_of(first_row, ROW_TILE)
        return ref.at[pl.ds(first_row, ROW_TILE), :]

    def row_in(s, r, tok_row):
        return pltpu.make_async_copy(tile(h2_hbm, tok_row), tile(xbuf[s], r * ROW_TILE), sem_in.at[s])

    def row_out(s, r, dst_row):
        return pltpu.make_async_copy(tile(ybuf[s], r * ROW_TILE), tile(out_hbm, dst_row), sem_out.at[s])

    def row_drop(s, r):
        return pltpu.make_async_copy(tile(ybuf[s], r * ROW_TILE), tile(pad_rows[s], r * ROW_TILE), sem_out.at[s])

    def wait_rows(copy, s):
        for _ in range(MOE_ROWS):
            copy(s, 0, 0).wait()

    @pl.when(i == 0)
    def _():
        idx_copy(0, 0).start()
        idx_copy(0, 0).wait()

        def body(r, carry):
            row_in(0, r, idx_smem[0, 0, r]).start()
            return carry
        lax.fori_loop(0, MOE_ROWS, body, 0, unroll=8)

    def step(slot):
        other = 1 - slot
        nxt = jnp.minimum(i + 1, last_block)
        idx_copy(nxt, other).start()
        wait_rows(row_in, slot)
        idx_copy(nxt, other).wait()

        @pl.when(i >= 2)
        def _():
            wait_rows(row_out, slot)

        for r in range(MOE_ROWS):
            row_in(other, r, idx_smem[other, 0, r]).start()
        hgu = _dot(_load_row_tiles(xbuf[slot], MOE_ROWS, BF16), wgu_ref[...]) + bgu_ref[...]
        hg = jnp.minimum(hgu[:, :D_EXPERT], SWIGLU_LIMIT)
        hu = jnp.clip(hgu[:, D_EXPERT:], -SWIGLU_LIMIT, SWIGLU_LIMIT)
        act = (hu + 1.0) * (hg * _sigmoid(SWIGLU_ALPHA * hg))
        _store_row_tiles(ybuf[slot], _dot(act, wdn_ref[...]) + bdn_ref[...])

        n_valid = nv_ref[i]

        @pl.when(n_valid == MOE_ROWS)
        def _():
            for r in range(MOE_ROWS):
                row_out(slot, r, idx_smem[slot, 1, r]).start()

        @pl.when(n_valid < MOE_ROWS)
        def _():
            def keep(r, carry):
                row_out(slot, r, idx_smem[slot, 1, r]).start()
                return carry

            def drop(r, carry):
                row_drop(slot, r).start()
                return carry
            lax.fori_loop(0, n_valid, keep, 0)
            lax.fori_loop(n_valid, MOE_ROWS, drop, 0)

        @pl.when(i == n_used - 1)
        def _():
            wait_rows(row_out, slot)
            wait_rows(row_in, other)

            @pl.when(i >= 1)
            def _():
                wait_rows(row_out, other)

    for parity in (0, 1):
        pl.when((i < n_used) & (i % 2 == parity))(functools.partial(step, parity))


def _moe_blocks(idx, h2, block_e, n_used, n_valid, w_gu, b_gu, w_dn, b_dn, n_out_rows):
    n_blocks = idx.shape[0]
    exp = lambda i, be, nu, nv: (be[jnp.minimum(i, nu[0] - 1)], 0, 0)
    any_spec = pl.BlockSpec(memory_space=pl.ANY)
    return pl.pallas_call(
        _moe_kernel,
        out_shape=jax.ShapeDtypeStruct((n_out_rows * ROW_TILE, LANE), F32),
        grid_spec=pltpu.PrefetchScalarGridSpec(
            num_scalar_prefetch=3,
            grid=(n_blocks,),
            in_specs=[
                any_spec,
                any_spec,
                pl.BlockSpec((None, D_MODEL, 2 * D_EXPERT), exp),
                pl.BlockSpec((None, 1, 2 * D_EXPERT), exp),
                pl.BlockSpec((None, D_EXPERT, D_MODEL), exp),
                pl.BlockSpec((None, 1, D_MODEL), exp),
            ],
            out_specs=any_spec,
            scratch_shapes=[pltpu.SMEM((2, 2, MOE_ROWS), jnp.int32)]
                           + [pltpu.VMEM((MOE_ROWS * ROW_TILE, LANE), F32)] * 6
                           + [pltpu.SemaphoreType.DMA((2,)),
                            pltpu.SemaphoreType.DMA((2,)),
                            pltpu.SemaphoreType.DMA((2,))],
        ),
        compiler_params=_params("arbitrary"),
        name="moe_experts",
    )(block_e, n_used, n_valid, idx, h2, w_gu, b_gu, w_dn, b_dn)


def _moe(h2, logits, w_gu, b_gu, w_dn, b_dn):
    N = logits.shape[0]
    NK = N * TOP_K
    top_val, top_idx = lax.top_k(logits, TOP_K)
    gate = jax.nn.softmax(top_val, axis=-1)
    flat_e = top_idx.reshape(-1)
    order = jnp.argsort(flat_e).astype(jnp.int32)
    counts = jnp.sum((flat_e[:, None] == jnp.arange(N_EXPERTS)[None, :]).astype(jnp.int32), axis=0)
    padded = (counts + MOE_ROWS - 1) // MOE_ROWS * MOE_ROWS
    pad_end = jnp.cumsum(padded)
    pad_start = pad_end - padded
    start = jnp.cumsum(counts) - counts
    n_blocks = (NK + N_EXPERTS * (MOE_ROWS - 1) + MOE_ROWS - 1) // MOE_ROWS
    P = n_blocks * MOE_ROWS
    block_first = jnp.arange(n_blocks, dtype=jnp.int32) * MOE_ROWS
    block_e = jnp.minimum(jnp.sum((block_first[:, None] >= pad_end[None, :]).astype(jnp.int32), axis=1),
                          N_EXPERTS - 1)
    n_used = (pad_end[-1] // MOE_ROWS).astype(jnp.int32).reshape(1)
    pos = jnp.arange(P, dtype=jnp.int32)
    e_pos = jnp.repeat(block_e, MOE_ROWS)
    rank = pos - pad_start[e_pos].astype(jnp.int32)
    valid = rank < counts[e_pos]
    flat = order[jnp.clip(start[e_pos].astype(jnp.int32) + rank, 0, NK - 1)]
    tok = jnp.where(valid, flat // TOP_K, 0)
    dst = jnp.where(valid, (flat % TOP_K) * N + flat // TOP_K, 0)
    idx = jnp.stack([tok.reshape(n_blocks, MOE_ROWS), dst.reshape(n_blocks, MOE_ROWS)], axis=1) * ROW_TILE
    n_valid = jnp.sum(valid.reshape(n_blocks, MOE_ROWS).astype(jnp.int32), axis=1)
    expert_out = _moe_blocks(idx, h2, block_e, n_used, n_valid, w_gu, b_gu, w_dn, b_dn, NK)
    return expert_out, gate


def _combine_kernel(x_ref, y0_ref, y1_ref, y2_ref, y3_ref, w_ref, g_ref, gate_ref, o_ref):
    w = w_ref[...]
    y = None
    for k, y_ref in enumerate((y0_ref, y1_ref, y2_ref, y3_ref)):
        term = _load_row_tiles(y_ref, w.shape[0]) * w[:, k:k + 1]
        y = term if y is None else y + term
    o_ref[...] = x_ref[...] + gate_ref[0] * (_rms(y) * g_ref[...])


def _combine(x, expert_out, weights, g, gate, *, rows_per_mod, tm):
    T = x.shape[0]
    tm = min(tm, rows_per_mod)
    row = pl.BlockSpec((tm, D_MODEL), lambda i: (i, 0))
    slabs = [pl.BlockSpec((tm * ROW_TILE, LANE), lambda i, k=k: (k * (T // tm) + i, 0)) for k in range(TOP_K)]
    return pl.pallas_call(
        _combine_kernel,
        out_shape=jax.ShapeDtypeStruct((T, D_MODEL), F32),
        grid=(T // tm,),
        in_specs=[row] + slabs + [pl.BlockSpec((tm, TOP_K), lambda i: (i, 0)),
                                  pl.BlockSpec((1, D_MODEL), lambda i: (0, 0)),
                                  pl.BlockSpec((1, 1, D_MODEL), lambda i: ((i * tm) // rows_per_mod, 0, 0))],
        out_specs=row,
        compiler_params=_params("parallel"),
        name="combine",
    )(x, *[expert_out] * TOP_K, weights, g, gate)


def _rope_tables(L):
    t = jnp.arange(L)
    lane = jnp.arange(LANE)
    axis = (lane % A_DIM) // 32
    half = (lane % 32) // 16
    n_freq = A_DIM // 4
    inv = ROPE_BASE ** (-(lane % n_freq).astype(F32) / n_freq)
    pos = jnp.where(axis[None, :] == 0, (t // GRID_W)[:, None], (t % GRID_W)[:, None]).astype(F32)
    ang = pos * inv[None, :]
    return jnp.cos(ang), jnp.where(half[None, :] == 0, -jnp.sin(ang), jnp.sin(ang))


def _arrange_w_in(w_in):
    sizes = [512] * 8 + [8, 8] + [512] * 4 + [8, 8]
    offs = [0]
    for s in sizes:
        offs.append(offs[-1] + s)
    part = lambda i: w_in[:, offs[i]:offs[i + 1]]
    big = [part(i) for i in (0, 1, 2, 3, 4, 5, 6, 7, 10, 11, 12, 13)]
    small = jnp.zeros((w_in.shape[0], HEADS, LANE), w_in.dtype)
    for base, i in ((SM_BETA, 8), (SM_DEC, 9), (SM_IG, 14), (SM_FG, 15)):
        cols = part(i).reshape(-1, 2, HEADS)
        for d in range(2):
            small = small.at[:, :, base + d].set(cols[:, d, :])
    return jnp.concatenate(big + [small.reshape(w_in.shape[0], HEADS * LANE)], axis=1)


def _layer(x, cond, lidx, lp, *, B, L, cache):
    T = B * L
    is_ctx = cache is None
    mod = (jax.nn.silu(cond) @ lp['ada_w'] + lp['ada_b']).reshape(-1, 6, 1, D_MODEL)
    shift1, scale1, gate1, shift2, scale2, gate2 = [mod[:, i] for i in range(6)]
    rows_per_mod = T if mod.shape[0] == 1 else L
    g = lp['norm_g']

    tm = 1024
    P = _norm_mod_matmul(x, g[0:1], shift1, scale1, lp['w_in'], jnp.zeros((1, N_PROJ), F32),
                         rows_per_mod=rows_per_mod, sigmoid=False, tm=tm, tn=N_PROJ // 4)
    gates = _norm_mod_matmul(x, g[0:1], shift1, scale1, lp['w_bgate'], lp['b_bgate'][None],
                             rows_per_mod=rows_per_mod, sigmoid=True, tm=tm, tn=2048)

    lam_init = 0.8 - 0.6 * math.exp(-0.3 * lidx)
    lq1, lk1, lq2, lk2 = lp['attn_lambda']
    lam = (jnp.exp(jnp.sum(lq1 * lk1)) - jnp.exp(jnp.sum(lq2 * lk2)) + lam_init).reshape(1, 1)
    attn_norm = lp['attn_norm'][None]
    if is_ctx:
        br_a, new_k, new_v = _attention(P, lam, attn_norm, B=B, L=L, tq=L, lam_init=lam_init)
    else:
        (br_a,) = _attention(P, lam, attn_norm, B=B, L=L, tq=128, lam_init=lam_init, cache=cache['attn'])

    br_b = _pool(P, lp['pool_w'], lp['pool_b'][:, None], lp['pool_scale'][None], B=B, L=L)

    gdn_args = (P, lp['gdn_A_log'], lp['gdn_dt_bias'], lp['gdn_conv'], lp['gdn_norm'][None])
    ml_args = (P, lp['mlstm_bias_i'], lp['mlstm_bias_f'], lp['mlstm_norm'][None])
    if is_ctx:
        br_c, new_gdn = _gdn(*gdn_args, B=B, L=L)
        br_d, new_c, new_n, new_m = _mlstm(*ml_args, B=B, L=L)
        ctx_out = (new_k, new_v, new_gdn, new_c,
                   new_n[:, :, :, 0, :HEAD_DIM].transpose(0, 2, 1, 3), new_m[:, :, :, 0, 0].transpose(0, 2, 1))
    else:
        (br_c,) = _gdn(*gdn_args, B=B, L=L, state=cache['gdn'])
        (br_d,) = _mlstm(*ml_args, B=B, L=L, state=cache['mlstm'])
        ctx_out = None

    x, h2, logits = _merge(x, gates, (br_a, br_b, br_c, br_d), lp['w_branch'], lp['w_out'], g[1:2], gate1,
                           g[2:3], shift2, scale2, lp['router_w'], lp['router_b'],
                           rows_per_mod=rows_per_mod, tm=256)
    expert_out, weights = _moe(h2, logits[:, :N_EXPERTS], lp['moe_w_gu'], lp['moe_b_gu'], lp['moe_w_dn'],
                               lp['moe_b_dn'])
    x = _combine(x, expert_out, weights, g[3:4], gate2, rows_per_mod=rows_per_mod, tm=512)
    return x, ctx_out


def kernel(x_prompt, x_sample, cache_attn_k, cache_attn_v, state_gdn, state_mlstm_C, state_mlstm_n, state_mlstm_m, c, c_ctx, ada_w, ada_b, norm_g, w_in, w_bgate, b_bgate, w_branch, w_out, attn_lambda, attn_norm, pool_w, pool_b, pool_scale, gdn_conv, gdn_A_log, gdn_dt_bias, gdn_norm, mlstm_bias_i, mlstm_bias_f, mlstm_norm, router_w, router_b, moe_w_gu, moe_b_gu, moe_w_dn, moe_b_dn):
    Bp, Lp, _ = x_prompt.shape
    Bs, Ls, _ = x_sample.shape
    y_p = x_prompt.reshape(Bp * Lp, D_MODEL)
    y_s = x_sample.reshape(Bs * Ls, D_MODEL)
    cos, sin = _rope_tables(Ls)
    n0 = state_mlstm_n.transpose(1, 0, 3, 2, 4)[:, :, :, :, None, :]
    m0 = jnp.broadcast_to(state_mlstm_m.transpose(1, 0, 3, 2)[..., None, None], n0.shape)
    ctx_states = []
    for l in range(DEPTH):
        lp = {
            'ada_w': ada_w[l], 'ada_b': ada_b[l], 'norm_g': norm_g[l],
            'w_in': _arrange_w_in(w_in[l]).astype(BF16), 'w_bgate': w_bgate[l].astype(BF16),
            'b_bgate': b_bgate[l], 'w_branch': w_branch[l].astype(BF16), 'w_out': w_out[l].astype(BF16),
            'attn_lambda': attn_lambda[l], 'attn_norm': attn_norm[l],
            'pool_w': pool_w[l], 'pool_b': pool_b[l], 'pool_scale': pool_scale[l],
            'gdn_conv': gdn_conv[l], 'gdn_A_log': gdn_A_log[l], 'gdn_dt_bias': gdn_dt_bias[l],
            'gdn_norm': gdn_norm[l], 'mlstm_bias_i': mlstm_bias_i[l], 'mlstm_bias_f': mlstm_bias_f[l],
            'mlstm_norm': mlstm_norm[l],
            'router_w': jnp.pad(router_w[l], ((0, 0), (0, LANE - N_EXPERTS))),
            'router_b': jnp.pad(router_b[l], (0, LANE - N_EXPERTS))[None],
            'moe_w_gu': moe_w_gu[l].astype(BF16), 'moe_b_gu': moe_b_gu[l][:, None],
            'moe_w_dn': moe_w_dn[l].astype(BF16), 'moe_b_dn': moe_b_dn[l][:, None],
        }
        y_p, st = _layer(y_p, c_ctx[None], l, lp, B=Bp, L=Lp, cache=None)
        ctx_states.append(st)
        cache = {
            'attn': {'k': cache_attn_k, 'v': cache_attn_v, 'layer': l, 'cos': cos, 'sin': sin},
            'gdn': (state_gdn, l),
            'mlstm': (state_mlstm_C, l, n0[l], m0[l]),
        }
        y_s, _ = _layer(y_s, c, l, lp, B=Bs, L=Ls, cache=cache)
    outs = [jnp.stack([s[i] for s in ctx_states], axis=1) for i in range(6)]
    return (y_p.reshape(Bp, Lp, D_MODEL), y_s.reshape(Bs, Ls, D_MODEL), *outs)
```

```python
import functools
import math

import jax
import jax.numpy as jnp
from jax import lax
from jax.experimental import pallas as pl
from jax.experimental.pallas import tpu as pltpu

F32 = jnp.float32
BF16 = jnp.bfloat16

D_MODEL = 1024
DEPTH = 2
GRID_W = 64
EPS = 1e-6
LANE = 128
HEADS = 4
HEAD_DIM = 128
A_DIM = 64
ROPE_BASE = 10000.0
POOL_WINDOWS = (2, 4, 8, 16)
CHUNK = 64
N_BRANCH = 4
BRANCH_WIDTH = 512
N_EXPERTS = 32
TOP_K = 4
D_EXPERT = 1024
SWIGLU_LIMIT = 7.0
SWIGLU_ALPHA = 1.702
MOE_ROWS = 256
VMEM_LIMIT = 56 * 1024 * 1024

COL_AQ, COL_AK, COL_AV, COL_POOL, COL_CQ, COL_CK, COL_CV, COL_CG, COL_MQ, COL_MK, COL_MV, COL_MG, COL_SM = range(13)
N_PROJ = 13 * BRANCH_WIDTH
SM_BETA, SM_DEC, SM_IG, SM_FG = 0, 2, 4, 6


def _params(*sem):
    return pltpu.CompilerParams(dimension_semantics=sem, vmem_limit_bytes=VMEM_LIMIT)


def _dot(a, b):
    return jnp.dot(a.astype(BF16), b.astype(BF16), preferred_element_type=F32)


def _dot_nt(a, b):
    return lax.dot_general(a.astype(BF16), b.astype(BF16), (((1,), (1,)), ((), ())),
                           preferred_element_type=F32)


def _dot_tn(a, b):
    return lax.dot_general(a.astype(BF16), b.astype(BF16), (((0,), (0,)), ((), ())),
                           preferred_element_type=F32)


def _split(a):
    hi = a.astype(BF16)
    lo = (a - hi.astype(F32)).astype(BF16)
    return hi, lo


def _dot3(a, b):
    ah, al = _split(a)
    bh, bl = _split(b)
    d = lambda x, y: jnp.dot(x, y, preferred_element_type=F32)
    return d(ah, bh) + (d(ah, bl) + d(al, bh))


def _dot_mask(mask, b):
    m = jnp.where(mask, 1.0, 0.0).astype(BF16)
    b0 = b.astype(BF16)
    r1 = b - b0.astype(F32)
    b1 = r1.astype(BF16)
    b2 = (r1 - b1.astype(F32)).astype(BF16)
    d = lambda y: jnp.dot(m, y, preferred_element_type=F32)
    return d(b0) + (d(b1) + d(b2))


ROW_TILE = 8
assert ROW_TILE * LANE == D_MODEL


def _store_row_tiles(ref, x):
    rows = x.shape[0]
    for c in range(ROW_TILE):
        ref[pl.ds(c, rows, stride=ROW_TILE), :] = x[:, c * LANE:(c + 1) * LANE]


def _load_row_tiles(ref, rows, dtype=F32):
    return jnp.concatenate([ref[pl.ds(c, rows, stride=ROW_TILE), :].astype(dtype) for c in range(ROW_TILE)],
                           axis=1)


def _rms(x):
    return x * lax.rsqrt(jnp.mean(x * x, axis=-1, keepdims=True) + EPS)


def _sigmoid(x):
    return 1.0 / (1.0 + jnp.exp(-x))


def _softplus(x):
    return jnp.maximum(x, 0.0) + jnp.log(1.0 + jnp.exp(-jnp.abs(x)))


def _proj_kernel(x_ref, g_ref, sh_ref, sc_ref, w_ref, b_ref, o_ref, h_scr, *, sigmoid):
    @pl.when(pl.program_id(1) == 0)
    def _():
        y = _rms(x_ref[...]) * g_ref[...]
        h_scr[...] = (y * (1.0 + sc_ref[0]) + sh_ref[0]).astype(BF16)

    acc = jnp.dot(h_scr[...], w_ref[...], preferred_element_type=F32) + b_ref[...]
    o_ref[...] = _sigmoid(acc) if sigmoid else acc


def _norm_mod_matmul(x, g, shift, scale, w, bias, *, rows_per_mod, sigmoid, tm, tn):
    T, Dm = x.shape
    N = w.shape[1]
    tm = min(tm, rows_per_mod)
    mod_idx = lambda i, j: ((i * tm) // rows_per_mod, 0, 0)
    return pl.pallas_call(
        functools.partial(_proj_kernel, sigmoid=sigmoid),
        out_shape=jax.ShapeDtypeStruct((T, N), F32),
        grid=(T // tm, N // tn),
        in_specs=[
            pl.BlockSpec((tm, Dm), lambda i, j: (i, 0)),
            pl.BlockSpec((1, Dm), lambda i, j: (0, 0)),
            pl.BlockSpec((1, 1, Dm), mod_idx),
            pl.BlockSpec((1, 1, Dm), mod_idx),
            pl.BlockSpec((Dm, tn), lambda i, j: (0, j)),
            pl.BlockSpec((1, tn), lambda i, j: (0, j)),
        ],
        out_specs=pl.BlockSpec((tm, tn), lambda i, j: (i, j)),
        scratch_shapes=[pltpu.VMEM((tm, Dm), BF16)],
        compiler_params=_params("parallel", "arbitrary"),
        name="proj",
    )(x, g, shift, scale, w, bias)


def _rope(x, cos, sin):
    lane = lax.broadcasted_iota(jnp.int32, x.shape, 1)
    first = (lane % 32) < 16
    partner = jnp.where(first, pltpu.roll(x, LANE - 16, 1), pltpu.roll(x, 16, 1))
    return x * cos + partner * sin


def _attn_kernel(*refs, has_cache, n_ctx, out_scale):
    if has_cache:
        (lam_ref, q_ref, k_ref, v_ref, ck_ref, cv_ref, cosq_ref, sinq_ref, cosk_ref, sink_ref, nrm_ref,
         o_ref, kall, vall) = refs
    else:
        lam_ref, q_ref, k_ref, v_ref, nrm_ref, o_ref, ko_ref, vo_ref, kall, vall = refs

    @pl.when(pl.program_id(2) == 0)
    def _():
        k = k_ref[...]
        v = v_ref[...]
        if has_cache:
            kall[:, pl.ds(0, n_ctx)] = ck_ref[...].T.astype(BF16)
            vall[pl.ds(0, n_ctx), :] = cv_ref[...].astype(BF16)
            k = _rope(k, cosk_ref[...], sink_ref[...])
        else:
            ko_ref[...] = k
            vo_ref[...] = v
        kall[:, pl.ds(n_ctx, k.shape[0])] = k.T.astype(BF16)
        vall[pl.ds(n_ctx, k.shape[0]), :] = v.astype(BF16)

    q = q_ref[...]
    if has_cache:
        q = _rope(q, cosq_ref[...], sinq_ref[...])
    q = q * (A_DIM ** -0.5 * math.log2(math.e))
    lane = lax.broadcasted_iota(jnp.int32, q.shape, 1)
    q1 = jnp.where(lane < A_DIM, q, 0.0)
    q2 = jnp.where(lane >= A_DIM, q, 0.0)
    keys = kall[...]
    lam = lam_ref[0, 0]

    def probs(qm):
        s = _dot(qm, keys)
        p = jnp.exp2(s - jnp.max(s, axis=-1, keepdims=True))
        return p, jnp.sum(p, axis=-1, keepdims=True)

    p1, l1 = probs(q1)
    p2, l2 = probs(q2)
    a = p1 * (1.0 / l1) - p2 * (lam / l2)
    o = jnp.dot(a.astype(BF16), vall[...], preferred_element_type=F32)
    o_ref[...] = _rms(o) * nrm_ref[...] * out_scale


def _attention(P, lam, norm, *, B, L, tq, lam_init, cache=None):
    T = B * L
    nq = L // tq
    has_cache = cache is not None
    n_ctx = cache["k"].shape[3] if has_cache else 0
    colq, colk, colv = COL_AQ * HEADS, COL_AK * HEADS, COL_AV * HEADS
    in_specs = [
        pl.BlockSpec(memory_space=pltpu.SMEM),
        pl.BlockSpec((tq, LANE), lambda b, h, i: (b * nq + i, colq + h)),
        pl.BlockSpec((L, LANE), lambda b, h, i: (b, colk + h)),
        pl.BlockSpec((L, LANE), lambda b, h, i: (b, colv + h)),
    ]
    args = [lam, P, P, P]
    if has_cache:
        l = cache["layer"]
        cspec = pl.BlockSpec((None, None, None, n_ctx, LANE), lambda b, h, i: (b, l, h, 0, 0))
        in_specs += [cspec, cspec,
                     pl.BlockSpec((tq, LANE), lambda b, h, i: (i, 0)),
                     pl.BlockSpec((tq, LANE), lambda b, h, i: (i, 0)),
                     pl.BlockSpec((L, LANE), lambda b, h, i: (0, 0)),
                     pl.BlockSpec((L, LANE), lambda b, h, i: (0, 0))]
        args += [cache["k"], cache["v"], cache["cos"], cache["sin"], cache["cos"], cache["sin"]]
    in_specs.append(pl.BlockSpec((1, LANE), lambda b, h, i: (0, 0)))
    args.append(norm)
    out_shape = [jax.ShapeDtypeStruct((T, HEADS * LANE), F32)]
    out_specs = [pl.BlockSpec((tq, LANE), lambda b, h, i: (b * nq + i, h))]
    if not has_cache:
        kv_shape = jax.ShapeDtypeStruct((B, HEADS, L, LANE), F32)
        kv_spec = pl.BlockSpec((None, None, L, LANE), lambda b, h, i: (b, h, 0, 0))
        out_shape += [kv_shape, kv_shape]
        out_specs += [kv_spec, kv_spec]
    return pl.pallas_call(
        functools.partial(_attn_kernel, has_cache=has_cache, n_ctx=n_ctx, out_scale=1.0 - lam_init),
        out_shape=out_shape,
        grid=(B, HEADS, nq),
        in_specs=in_specs,
        out_specs=out_specs,
        scratch_shapes=[pltpu.VMEM((LANE, n_ctx + L), BF16), pltpu.VMEM((n_ctx + L, LANE), BF16)],
        compiler_params=_params("parallel", "parallel", "arbitrary"),
        name="diff_attention",
    )(*args)


POOL_PAD = 16


def _pool_kernel(x_ref, w_ref, b_ref, s_ref, o_ref, pad, *, L):
    zeros = jnp.zeros((POOL_PAD, LANE), F32)
    pad[pl.ds(0, POOL_PAD), :] = zeros
    pad[pl.ds(POOL_PAD + L, POOL_PAD), :] = zeros
    x = x_ref[...]
    pad[pl.ds(POOL_PAD, L), :] = x
    t = lax.broadcasted_iota(jnp.int32, (L, LANE), 0)
    g = pl.program_id(1)
    for gi, win in enumerate(POOL_WINDOWS):
        @pl.when(g == gi)
        def _(win=win):
            half = win // 2
            acc = pad[pl.ds(POOL_PAD - half, L), :]
            for k in range(1 - half, half):
                acc = acc + pad[pl.ds(POOL_PAD + k, L), :]
            cnt = (jnp.minimum(t + half, L) - jnp.maximum(t - half, 0)).astype(F32)
            pooled = acc / cnt - x
            o_ref[...] = (_dot(pooled, w_ref[...]) + b_ref[...]) * s_ref[...]


def _pool(P, w, b, scale, *, B, L):
    T = B * L
    G = len(POOL_WINDOWS)
    return pl.pallas_call(
        functools.partial(_pool_kernel, L=L),
        out_shape=jax.ShapeDtypeStruct((T, G * LANE), F32),
        grid=(B, G),
        in_specs=[
            pl.BlockSpec((L, LANE), lambda bi, g: (bi, COL_POOL * HEADS + g)),
            pl.BlockSpec((None, LANE, LANE), lambda bi, g: (g, 0, 0)),
            pl.BlockSpec((None, 1, LANE), lambda bi, g: (g, 0, 0)),
            pl.BlockSpec((1, LANE), lambda bi, g: (0, g)),
        ],
        out_specs=pl.BlockSpec((L, LANE), lambda bi, g: (bi, g)),
        scratch_shapes=[pltpu.VMEM((L + 2 * POOL_PAD, LANE), F32)],
        compiler_params=_params("parallel", "parallel"),
        name="pool_mixer",
    )(P, w, b, scale)


GROUP_CHUNKS = 2
GROUP = GROUP_CHUNKS * CHUNK
MAX_GROUPS_PER_STEP = 4


def _groups_per_step(L):
    return min(MAX_GROUPS_PER_STEP, L // GROUP)


def _group_masks(backward):
    row = lax.broadcasted_iota(jnp.int32, (GROUP, GROUP), 0)
    col = lax.broadcasted_iota(jnp.int32, (GROUP, GROUP), 1)
    same = (row // CHUNK) == (col // CHUNK)
    lower, upper = row >= col, row <= col
    if backward:
        lower, upper = upper, lower
    return dict(same=same, incl=same & lower, strict=same & lower & (row != col), incl_t=same & upper,
                eye=row == col)


INV_BASE = 8


def _block_masks():
    row = lax.broadcasted_iota(jnp.int32, (GROUP, GROUP), 0)
    col = lax.broadcasted_iota(jnp.int32, (GROUP, GROUP), 1)
    sizes = [INV_BASE << i for i in range(int(math.log2(CHUNK // INV_BASE)) + 1)]
    same = [(row // s) == (col // s) for s in sizes]
    return [same[0]] + [cur & jnp.logical_not(prev) for prev, cur in zip(same[:-1], same[1:])]


def _unit_triangular_inverses(ns, eye, blocks):
    base = [jnp.where(blocks[0], n, 0.0) for n in ns]
    invs = [eye - b for b in base]
    pws = [_dot(b, b) for b in base]
    rounds = int(math.log2(INV_BASE)) - 1
    for s in range(rounds):
        invs = [t + _dot(t, p) for t, p in zip(invs, pws)]
        if s + 1 < rounds:
            pws = [_dot(p, p) for p in pws]
    for join in blocks[1:]:
        offs = [jnp.where(join, n, 0.0) for n in ns]
        xs = [_dot(o, t) for o, t in zip(offs, invs)]
        invs = [t - _dot(t, x) for t, x in zip(invs, xs)]
    return invs


def _dot_mask2(mask, b):
    m = jnp.where(mask, 1.0, 0.0).astype(BF16)
    b0, b1 = _split(b)
    return jnp.dot(m, b0, preferred_element_type=F32) + jnp.dot(m, b1, preferred_element_type=F32)


def _per_chunk(x, reduce):
    parts = [reduce(x[c * CHUNK:(c + 1) * CHUNK], axis=0, keepdims=True) for c in range(GROUP_CHUNKS)]
    col = jnp.concatenate([jnp.broadcast_to(p, (CHUNK, 1)) for p in parts], axis=0)
    return parts, col


def _scan_order(step, n_steps, groups):
    base = (step * groups * GROUP, (n_steps - 1 - step) * groups * GROUP)
    fwd = [(gi, c) for gi in range(groups) for c in range(GROUP_CHUNKS)]
    return base, (fwd, fwd[::-1])


CONV_PAD = 8


def _gdn_kernel(*refs, L, has_state):
    alog_ref, dtb_ref, q_ref, k_ref, v_ref, gate_ref, sm_ref, cwq_ref, cwk_ref, cwv_ref, nrm_ref = refs[:11]
    refs = refs[11:]
    if has_state:
        s0_ref, o_ref, pad, qs, ks, vs, o_f, o_b, state = refs
    else:
        o_ref, sout_ref, pad, qs, ks, vs, o_f, o_b, state = refs
    h = pl.program_id(1)
    n_chunks = L // CHUNK

    zeros = jnp.zeros((CONV_PAD, LANE), F32)
    pad[pl.ds(0, CONV_PAD), :] = zeros
    pad[pl.ds(CONV_PAD + L, CONV_PAD), :] = zeros

    def conv_silu(x_ref, w_ref):
        pad[pl.ds(CONV_PAD, L), :] = x_ref[...]
        w = w_ref[...]
        y = (pad[pl.ds(CONV_PAD - 1, L), :] * w[0:1] + pad[pl.ds(CONV_PAD, L), :] * w[1:2]
             + pad[pl.ds(CONV_PAD + 1, L), :] * w[2:3])
        return y * _sigmoid(y)

    def l2n(x):
        return x * lax.rsqrt(jnp.sum(x * x, axis=-1, keepdims=True) + EPS)

    qs[...] = l2n(conv_silu(q_ref, cwq_ref)) * (HEAD_DIM ** -0.5)
    ks[...] = l2n(conv_silu(k_ref, cwk_ref))
    vs[...] = conv_silu(v_ref, cwv_ref)
    if has_state:
        state[...] = s0_ref[...]
    else:
        state[...] = jnp.zeros_like(state)

    groups = _groups_per_step(L)
    n_steps = L // (groups * GROUP)
    o_scr = (o_f, o_b)
    masks = (_group_masks(False), _group_masks(True))
    eye = jnp.where(masks[0]['eye'], 1.0, 0.0)
    blocks = _block_masks()
    items = [(d, gi) for gi in range(groups) for d in (0, 1)]
    chunks = [slice(c * CHUNK, (c + 1) * CHUNK) for c in range(GROUP_CHUNKS)]

    def body(step, carry):
        base, order = _scan_order(step, n_steps, groups)
        it = {}
        for key in items:
            d, gi = key
            rows = pl.ds(pl.multiple_of(base[d] + gi * GROUP, GROUP), GROUP)
            q, k, v, sm = qs[rows, :], ks[rows, :], vs[rows, :], sm_ref[rows, :]
            beta = _sigmoid(sm[:, SM_BETA + d:SM_BETA + d + 1])
            dec = sm[:, SM_DEC + d:SM_DEC + d + 1]
            g = -jnp.exp(alog_ref[d, h]) * _softplus(dec + dtb_ref[d, h])
            it[key] = dict(q=q, k=k, kb=k * beta, vb=v * beta, g=g, gb=jnp.broadcast_to(g, (GROUP, GROUP)))
        for (d, gi), x in it.items():
            m = masks[d]
            x['cum_i'] = _dot_mask2(m['incl'], x['gb'])
            x['cum_j'] = _dot_mask2(m['same'], jnp.where(m['incl_t'], x['gb'], 0.0))
            x['kk'] = _dot_nt(x['kb'], x['k'])
            x['qk'] = _dot_nt(x['q'], x['k'])
        for (d, gi), x in it.items():
            m = masks[d]
            decay = jnp.where(m['incl'], jnp.exp(jnp.where(m['incl'], x['cum_i'] - x['cum_j'], 0.0)), 0.0)
            x['gc'] = x['cum_i'][:, 0:1]
            x['tot'], x['tot_col'] = _per_chunk(x['g'], jnp.sum)
            x['qk'] = x['qk'] * decay
            x['n'] = jnp.where(m['strict'], x['kk'] * decay, 0.0)
        invs = _unit_triangular_inverses([x['n'] for x in it.values()], eye, blocks)
        for x, inv in zip(it.values(), invs):
            x['inv'] = inv
            x['egc'] = jnp.exp(x['gc'])
            x['sol'] = _dot3(x['inv'], jnp.concatenate([x['vb'], x['kb'] * x['egc']], axis=-1))
        for x in it.values():
            u, w = x['sol'][:, :HEAD_DIM], x['sol'][:, HEAD_DIM:]
            x['qp'] = x['q'] * x['egc'] - _dot(x['qk'], w)
            x['op'] = _dot(x['qk'], u)
            kd = x['k'] * jnp.exp(x['tot_col'] - x['gc'])
            x['ab'] = [_dot_tn(kd[cs], x['sol'][cs]) for cs in chunks]
            x['gl'] = [jnp.exp(t) for t in x['tot']]
        S = [state[0], state[1]]
        for stp in range(len(order[0])):
            for d in (0, 1):
                gi, c = order[d][stp]
                x = it[(d, gi)]
                ab = x['ab'][c]
                r = _dot(jnp.concatenate([ab[:, HEAD_DIM:], x['qp'][chunks[c]]], axis=0), S[d])
                rows = pl.ds(pl.multiple_of(base[d] + gi * GROUP + c * CHUNK, CHUNK), CHUNK)
                o_scr[d][rows, :] = x['op'][chunks[c]] + r[HEAD_DIM:]
                S[d] = x['gl'][c] * S[d] + (ab[:, :HEAD_DIM] - r[:HEAD_DIM])
        state[0] = S[0]
        state[1] = S[1]
        return carry

    lax.fori_loop(0, n_steps, body, 0)
    gate = gate_ref[...]
    o_ref[...] = _rms(o_f[...] + o_b[...]) * nrm_ref[...] * (gate * _sigmoid(gate))
    if not has_state:
        sout_ref[...] = state[...]


def _col_spec(L, col):
    return pl.BlockSpec((L, LANE), lambda b, h: (b, col * HEADS + h))


def _gdn(P, a_log, dt_bias, conv_w, norm, *, B, L, state=None):
    T = B * L
    has_state = state is not None
    smem = pl.BlockSpec(memory_space=pltpu.SMEM)
    in_specs = [smem, smem, _col_spec(L, COL_CQ), _col_spec(L, COL_CK), _col_spec(L, COL_CV),
                _col_spec(L, COL_CG), _col_spec(L, COL_SM)]
    in_specs += [pl.BlockSpec((3, LANE), lambda b, h, j=j: (0, j * HEADS + h)) for j in range(3)]
    in_specs.append(pl.BlockSpec((1, LANE), lambda b, h: (0, 0)))
    args = [a_log, dt_bias, P, P, P, P, P, conv_w, conv_w, conv_w, norm]
    out_shape = [jax.ShapeDtypeStruct((T, HEADS * LANE), F32)]
    out_specs = [pl.BlockSpec((L, LANE), lambda b, h: (b, h))]
    if has_state:
        arr, l = state
        in_specs.append(pl.BlockSpec((None, None, 2, None, HEAD_DIM, HEAD_DIM), lambda b, h: (b, l, 0, h, 0, 0)))
        args.append(arr)
    else:
        out_shape.append(jax.ShapeDtypeStruct((B, 2, HEADS, HEAD_DIM, HEAD_DIM), F32))
        out_specs.append(pl.BlockSpec((None, 2, None, HEAD_DIM, HEAD_DIM), lambda b, h: (b, 0, h, 0, 0)))
    seq = pltpu.VMEM((L, LANE), F32)
    return pl.pallas_call(
        functools.partial(_gdn_kernel, L=L, has_state=has_state),
        out_shape=out_shape,
        grid=(B, HEADS),
        in_specs=in_specs,
        out_specs=out_specs,
        scratch_shapes=[pltpu.VMEM((L + 2 * CONV_PAD, LANE), F32), seq, seq, seq, seq, seq,
                        pltpu.VMEM((2, HEAD_DIM, HEAD_DIM), F32)],
        compiler_params=_params("parallel", "parallel"),
        name="gated_deltanet",
    )(*args)


def _mlstm_kernel(*refs, L, has_state):
    bi_ref, bf_ref, q_ref, k_ref, v_ref, gate_ref, sm_ref, nrm_ref = refs[:8]
    refs = refs[8:]
    if has_state:
        c0_ref, n0_ref, m0_ref, o_ref, h_f, h_b, c_st, n_st, m_st = refs
    else:
        o_ref, cout_ref, nout_ref, mout_ref, h_f, h_b, c_st, n_st, m_st = refs
    h = pl.program_id(1)
    n_chunks = L // CHUNK

    if has_state:
        c_st[...] = c0_ref[...]
        n_st[...] = n0_ref[...]
        m_st[...] = m0_ref[...]
    else:
        c_st[...] = jnp.zeros_like(c_st)
        n_st[...] = jnp.zeros_like(n_st)
        m_st[...] = jnp.zeros_like(m_st)

    groups = _groups_per_step(L)
    n_steps = L // (groups * GROUP)
    h_scr = (h_f, h_b)
    masks = (_group_masks(False), _group_masks(True))
    items = [(d, gi) for gi in range(groups) for d in (0, 1)]
    chunks = [slice(c * CHUNK, (c + 1) * CHUNK) for c in range(GROUP_CHUNKS)]

    def body(step, carry):
        base, order = _scan_order(step, n_steps, groups)
        it = {}
        for key in items:
            d, gi = key
            rows = pl.ds(pl.multiple_of(base[d] + gi * GROUP, GROUP), GROUP)
            q, v, sm = q_ref[rows, :], v_ref[rows, :], sm_ref[rows, :]
            k = k_ref[rows, :] * (HEAD_DIM ** -0.5)
            ig = sm[:, SM_IG + d:SM_IG + d + 1] + bi_ref[d, h]
            fg = sm[:, SM_FG + d:SM_FG + d + 1] + bf_ref[d, h]
            lf = -_softplus(-fg)
            it[key] = dict(q=q, k=k, v=v, ig=ig, lf=lf, lfb=jnp.broadcast_to(lf, (GROUP, GROUP)),
                           igb=jnp.broadcast_to(ig, (GROUP, GROUP)))
        for (d, gi), x in it.items():
            m = masks[d]
            x['cum_i'] = _dot_mask2(m['incl'], x['lfb'])
            x['cum_j'] = _dot_mask2(m['same'], jnp.where(m['incl_t'], x['lfb'], 0.0)
                                    - jnp.where(m['eye'], x['igb'], 0.0))
            x['qk'] = _dot_nt(x['q'], x['k'])
        for (d, gi), x in it.items():
            m = masks[d]
            dm = x['cum_i'] - x['cum_j']
            x['b'] = x['cum_i'][:, 0:1]
            x['tot'], tot_col = _per_chunk(x['lf'], jnp.sum)
            x['m_intra'] = jnp.max(jnp.where(m['incl'], dm, -jnp.inf), axis=-1, keepdims=True)
            e = tot_col - x['b'] + x['ig']
            x['m_end'], m_end_col = _per_chunk(e, jnp.max)
            p = jnp.where(m['incl'], jnp.exp(jnp.where(m['incl'], dm - x['m_intra'], 0.0)), 0.0) * x['qk']
            x['p_sum'] = jnp.sum(p, axis=-1, keepdims=True)
            x['p'] = p
            x['kw'] = x['k'] * jnp.exp(e - m_end_col)
        for x in it.values():
            x['pv'] = _dot(x['p'], x['v'])
            x['kv'] = [_dot_tn(x['kw'][cs], x['v'][cs]) for cs in chunks]
            x['k_sum'] = [jnp.sum(x['kw'][cs], axis=0, keepdims=True) for cs in chunks]
        outs = []
        for d in (0, 1):
            C, n, m = c_st[d], n_st[d], m_st[d][:, 0:1]
            for gi, c in order[d]:
                x = it[(d, gi)]
                cs = chunks[c]
                m_t = jnp.maximum(x['b'][cs] + m, x['m_intra'][cs])
                w_inter = jnp.exp(x['b'][cs] + m - m_t)
                local = jnp.exp(x['m_intra'][cs] - m_t)
                den = (w_inter * jnp.sum(x['q'][cs] * n, axis=-1, keepdims=True) + local * x['p_sum'][cs])
                scale = 1.0 / jnp.maximum(jnp.abs(den), jnp.exp(-m_t))
                outs.append((d, gi, c, C, w_inter * scale, local * scale))
                m_new = jnp.maximum(x['tot'][c] + m, x['m_end'][c])
                carry_decay = jnp.exp(x['tot'][c] + m - m_new)
                local_new = jnp.exp(x['m_end'][c] - m_new)
                C = carry_decay * C + local_new * x['kv'][c]
                n = carry_decay * n + local_new * x['k_sum'][c]
                m = m_new
            c_st[d] = C
            n_st[d] = n
            m_st[d] = jnp.broadcast_to(m, (1, LANE))
        for d, gi, c, C, w_scale, p_scale in outs:
            x = it[(d, gi)]
            cs = chunks[c]
            rows = pl.ds(pl.multiple_of(base[d] + gi * GROUP + c * CHUNK, CHUNK), CHUNK)
            h_scr[d][rows, :] = w_scale * _dot(x['q'][cs], C) + p_scale * x['pv'][cs]
        return carry

    lax.fori_loop(0, n_steps, body, 0)
    o_ref[...] = _rms(h_f[...] + h_b[...]) * nrm_ref[...] * _sigmoid(gate_ref[...])
    if not has_state:
        cout_ref[...] = c_st[...]
        nout_ref[...] = n_st[...]
        mout_ref[...] = m_st[...]


def _mlstm(P, bias_i, bias_f, norm, *, B, L, state=None):
    T = B * L
    has_state = state is not None
    smem = pl.BlockSpec(memory_space=pltpu.SMEM)
    in_specs = [smem, smem, _col_spec(L, COL_MQ), _col_spec(L, COL_MK), _col_spec(L, COL_MV),
                _col_spec(L, COL_MG), _col_spec(L, COL_SM), pl.BlockSpec((1, LANE), lambda b, h: (0, 0))]
    args = [bias_i, bias_f, P, P, P, P, P, norm]
    out_shape = [jax.ShapeDtypeStruct((T, HEADS * LANE), F32)]
    out_specs = [pl.BlockSpec((L, LANE), lambda b, h: (b, h))]
    vec_spec = pl.BlockSpec((None, None, 2, 1, LANE), lambda b, h: (b, h, 0, 0, 0))
    if has_state:
        c_arr, l, n_arr, m_arr = state
        in_specs += [pl.BlockSpec((None, None, 2, None, HEAD_DIM, HEAD_DIM), lambda b, h: (b, l, 0, h, 0, 0)),
                     vec_spec, vec_spec]
        args += [c_arr, n_arr, m_arr]
    else:
        out_shape += [jax.ShapeDtypeStruct((B, 2, HEADS, HEAD_DIM, HEAD_DIM), F32),
                      jax.ShapeDtypeStruct((B, HEADS, 2, 1, LANE), F32),
                      jax.ShapeDtypeStruct((B, HEADS, 2, 1, LANE), F32)]
        out_specs += [pl.BlockSpec((None, 2, None, HEAD_DIM, HEAD_DIM), lambda b, h: (b, 0, h, 0, 0)),
                      vec_spec, vec_spec]
    seq = pltpu.VMEM((L, LANE), F32)
    return pl.pallas_call(
        functools.partial(_mlstm_kernel, L=L, has_state=has_state),
        out_shape=out_shape,
        grid=(B, HEADS),
        in_specs=in_specs,
        out_specs=out_specs,
        scratch_shapes=[seq, seq, pltpu.VMEM((2, HEAD_DIM, HEAD_DIM), F32), pltpu.VMEM((2, 1, LANE), F32),
                        pltpu.VMEM((2, 1, LANE), F32)],
        compiler_params=_params("parallel", "parallel"),
        name="mlstm",
    )(*args)


def _merge_kernel(x_ref, gates_ref, a_ref, b_ref, c_ref, d_ref, wb_ref, wo_ref, g1_ref, gate1_ref, g2_ref,
                  sh2_ref, sc2_ref, rw_ref, rb_ref, xo_ref, h2_ref, lg_ref):
    merged = None
    for i, br in enumerate((a_ref, b_ref, c_ref, d_ref)):
        term = gates_ref[:, i * D_MODEL:(i + 1) * D_MODEL] * _dot(br[...], wb_ref[i])
        merged = term if merged is None else merged + term
    t = _rms(_dot(merged, wo_ref[...])) * g1_ref[...]
    x = x_ref[...] + gate1_ref[0] * t
    xo_ref[...] = x
    h2 = _rms(x) * g2_ref[...] * (1.0 + sc2_ref[0]) + sh2_ref[0]
    _store_row_tiles(h2_ref, h2)
    lg_ref[...] = _dot3(h2, rw_ref[...]) + rb_ref[...]


def _merge(x, gates, branches, wb, wo, g1, gate1, g2, shift2, scale2, rw, rb, *, rows_per_mod, tm):
    T = x.shape[0]
    tm = min(tm, rows_per_mod)
    row = lambda n: pl.BlockSpec((tm, n), lambda i: (i, 0))
    const = lambda *shape: pl.BlockSpec(shape, lambda i: (0,) * len(shape))
    mod = pl.BlockSpec((1, 1, D_MODEL), lambda i: ((i * tm) // rows_per_mod, 0, 0))
    return pl.pallas_call(
        _merge_kernel,
        out_shape=[jax.ShapeDtypeStruct((T, D_MODEL), F32), jax.ShapeDtypeStruct((T * ROW_TILE, LANE), F32),
                   jax.ShapeDtypeStruct((T, LANE), F32)],
        grid=(T // tm,),
        in_specs=[row(D_MODEL), row(N_BRANCH * D_MODEL)] + [row(BRANCH_WIDTH)] * 4
                 + [const(N_BRANCH, BRANCH_WIDTH, D_MODEL), const(D_MODEL, D_MODEL), const(1, D_MODEL), mod,
                    const(1, D_MODEL), mod, mod, const(D_MODEL, LANE), const(1, LANE)],
        out_specs=[row(D_MODEL), pl.BlockSpec((tm * ROW_TILE, LANE), lambda i: (i, 0)), row(LANE)],
        compiler_params=_params("parallel"),
        name="merge",
    )(x, gates, *branches, wb, wo, g1, gate1, g2, shift2, scale2, rw, rb)


def _moe_kernel(be_ref, nu_ref, nv_ref, idx_hbm, h2_hbm, wgu_ref, bgu_ref, wdn_ref, bdn_ref, out_hbm,
                idx_smem, x0, x1, y0, y1, p0, p1, sem_idx, sem_in, sem_out):
    i = pl.program_id(0)
    n_used = nu_ref[0]
    last_block = pl.num_programs(0) - 1
    xbuf, ybuf, pad_rows = (x0, x1), (y0, y1), (p0, p1)

    def idx_copy(block, s):
        return pltpu.make_async_copy(idx_hbm.at[block], idx_smem.at[s], sem_idx.at[s])

    def tile(ref, first_row):
        if not isinstance(first_row, int):
            first_row = pl.multiple_of(first_row, ROW_TILE)
        return ref.at[pl.ds(first_row, ROW_TILE), :]

    def row_in(s, r, tok_row):
        return pltpu.make_async_copy(tile(h2_hbm, tok_row), tile(xbuf[s], r * ROW_TILE), sem_in.at[s])

    def row_out(s, r, dst_row):
        return pltpu.make_async_copy(tile(ybuf[s], r * ROW_TILE), tile(out_hbm, dst_row), sem_out.at[s])

    def row_drop(s, r):
        return pltpu.make_async_copy(tile(ybuf[s], r * ROW_TILE), tile(pad_rows[s], r * ROW_TILE), sem_out.at[s])

    def wait_rows(copy, s):
        for _ in range(MOE_ROWS):
            copy(s, 0, 0).wait()

    @pl.when(i == 0)
    def _():
        idx_copy(0, 0).start()
        idx_copy(0, 0).wait()

        def body(r, carry):
            row_in(0, r, idx_smem[0, 0, r]).start()
            return carry
        lax.fori_loop(0, MOE_ROWS, body, 0, unroll=8)

    def step(slot):
        other = 1 - slot
        nxt = jnp.minimum(i + 1, last_block)
        idx_copy(nxt, other).start()
        wait_rows(row_in, slot)
        idx_copy(nxt, other).wait()

        @pl.when(i >= 2)
        def _():
            wait_rows(row_out, slot)

        for r in range(MOE_ROWS):
            row_in(other, r, idx_smem[other, 0, r]).start(priority=1)
        hgu = _dot(_load_row_tiles(xbuf[slot], MOE_ROWS, BF16), wgu_ref[...]) + bgu_ref[...]
        hg = jnp.minimum(hgu[:, :D_EXPERT], SWIGLU_LIMIT)
        hu = jnp.clip(hgu[:, D_EXPERT:], -SWIGLU_LIMIT, SWIGLU_LIMIT)
        act = (hu + 1.0) * (hg * _sigmoid(SWIGLU_ALPHA * hg))
        _store_row_tiles(ybuf[slot], _dot(act, wdn_ref[...]) + bdn_ref[...])

        n_valid = nv_ref[i]

        @pl.when(n_valid == MOE_ROWS)
        def _():
            for r in range(MOE_ROWS):
                row_out(slot, r, idx_smem[slot, 1, r]).start()

        @pl.when(n_valid < MOE_ROWS)
        def _():
            def keep(r, carry):
                row_out(slot, r, idx_smem[slot, 1, r]).start()
                return carry

            def drop(r, carry):
                row_drop(slot, r).start()
                return carry
            lax.fori_loop(0, n_valid, keep, 0)
            lax.fori_loop(n_valid, MOE_ROWS, drop, 0)

        @pl.when(i == n_used - 1)
        def _():
            wait_rows(row_out, slot)
            wait_rows(row_in, other)

            @pl.when(i >= 1)
            def _():
                wait_rows(row_out, other)

    for parity in (0, 1):
        pl.when((i < n_used) & (i % 2 == parity))(functools.partial(step, parity))


def _moe_blocks(idx, h2, block_e, n_used, n_valid, w_gu, b_gu, w_dn, b_dn, n_out_rows):
    n_blocks = idx.shape[0]
    exp = lambda i, be, nu, nv: (be[jnp.minimum(i, nu[0] - 1)], 0, 0)
    any_spec = pl.BlockSpec(memory_space=pl.ANY)
    return pl.pallas_call(
        _moe_kernel,
        out_shape=jax.ShapeDtypeStruct((n_out_rows * ROW_TILE, LANE), F32),
        grid_spec=pltpu.PrefetchScalarGridSpec(
            num_scalar_prefetch=3,
            grid=(n_blocks,),
            in_specs=[
                any_spec,
                any_spec,
                pl.BlockSpec((None, D_MODEL, 2 * D_EXPERT), exp),
                pl.BlockSpec((None, 1, 2 * D_EXPERT), exp),
                pl.BlockSpec((None, D_EXPERT, D_MODEL), exp),
                pl.BlockSpec((None, 1, D_MODEL), exp),
            ],
            out_specs=any_spec,
            scratch_shapes=[pltpu.SMEM((2, 2, MOE_ROWS), jnp.int32)]
                           + [pltpu.VMEM((MOE_ROWS * ROW_TILE, LANE), F32)] * 6
                           + [pltpu.SemaphoreType.DMA((2,)),
                            pltpu.SemaphoreType.DMA((2,)),
                            pltpu.SemaphoreType.DMA((2,))],
        ),
        compiler_params=_params("arbitrary"),
        name="moe_experts",
    )(block_e, n_used, n_valid, idx, h2, w_gu, b_gu, w_dn, b_dn)


def _moe(h2, logits, w_gu, b_gu, w_dn, b_dn):
    N = logits.shape[0]
    NK = N * TOP_K
    top_val, top_idx = lax.top_k(logits, TOP_K)
    gate = jax.nn.softmax(top_val, axis=-1)
    flat_e = top_idx.reshape(-1)
    order = jnp.argsort(flat_e).astype(jnp.int32)
    counts = jnp.sum((flat_e[:, None] == jnp.arange(N_EXPERTS)[None, :]).astype(jnp.int32), axis=0)
    padded = (counts + MOE_ROWS - 1) // MOE_ROWS * MOE_ROWS
    pad_end = jnp.cumsum(padded)
    pad_start = pad_end - padded
    start = jnp.cumsum(counts) - counts
    n_blocks = (NK + N_EXPERTS * (MOE_ROWS - 1) + MOE_ROWS - 1) // MOE_ROWS
    P = n_blocks * MOE_ROWS
    block_first = jnp.arange(n_blocks, dtype=jnp.int32) * MOE_ROWS
    block_e = jnp.minimum(jnp.sum((block_first[:, None] >= pad_end[None, :]).astype(jnp.int32), axis=1),
                          N_EXPERTS - 1)
    n_used = (pad_end[-1] // MOE_ROWS).astype(jnp.int32).reshape(1)
    pos = jnp.arange(P, dtype=jnp.int32)
    e_pos = jnp.repeat(block_e, MOE_ROWS)
    rank = pos - pad_start[e_pos].astype(jnp.int32)
    valid = rank < counts[e_pos]
    flat = order[jnp.clip(start[e_pos].astype(jnp.int32) + rank, 0, NK - 1)]
    tok = jnp.where(valid, flat // TOP_K, 0)
    dst = jnp.where(valid, (flat % TOP_K) * N + flat // TOP_K, 0)
    idx = jnp.stack([tok.reshape(n_blocks, MOE_ROWS), dst.reshape(n_blocks, MOE_ROWS)], axis=1) * ROW_TILE
    n_valid = jnp.sum(valid.reshape(n_blocks, MOE_ROWS).astype(jnp.int32), axis=1)
    expert_out = _moe_blocks(idx, h2, block_e, n_used, n_valid, w_gu, b_gu, w_dn, b_dn, NK)
    return expert_out, gate


def _combine_kernel(x_ref, y0_ref, y1_ref, y2_ref, y3_ref, w_ref, g_ref, gate_ref, o_ref):
    w = w_ref[...]
    y = None
    for k, y_ref in enumerate((y0_ref, y1_ref, y2_ref, y3_ref)):
        term = _load_row_tiles(y_ref, w.shape[0]) * w[:, k:k + 1]
        y = term if y is None else y + term
    o_ref[...] = x_ref[...] + gate_ref[0] * (_rms(y) * g_ref[...])


def _combine(x, expert_out, weights, g, gate, *, rows_per_mod, tm):
    T = x.shape[0]
    tm = min(tm, rows_per_mod)
    row = pl.BlockSpec((tm, D_MODEL), lambda i: (i, 0))
    slabs = [pl.BlockSpec((tm * ROW_TILE, LANE), lambda i, k=k: (k * (T // tm) + i, 0)) for k in range(TOP_K)]
    return pl.pallas_call(
        _combine_kernel,
        out_shape=jax.ShapeDtypeStruct((T, D_MODEL), F32),
        grid=(T // tm,),
        in_specs=[row] + slabs + [pl.BlockSpec((tm, TOP_K), lambda i: (i, 0)),
                                  pl.BlockSpec((1, D_MODEL), lambda i: (0, 0)),
                                  pl.BlockSpec((1, 1, D_MODEL), lambda i: ((i * tm) // rows_per_mod, 0, 0))],
        out_specs=row,
        compiler_params=_params("parallel"),
        name="combine",
    )(x, *[expert_out] * TOP_K, weights, g, gate)


def _rope_tables(L):
    t = jnp.arange(L)
    lane = jnp.arange(LANE)
    axis = (lane % A_DIM) // 32
    half = (lane % 32) // 16
    n_freq = A_DIM // 4
    inv = ROPE_BASE ** (-(lane % n_freq).astype(F32) / n_freq)
    pos = jnp.where(axis[None, :] == 0, (t // GRID_W)[:, None], (t % GRID_W)[:, None]).astype(F32)
    ang = pos * inv[None, :]
    return jnp.cos(ang), jnp.where(half[None, :] == 0, -jnp.sin(ang), jnp.sin(ang))


def _arrange_w_in(w_in):
    sizes = [512] * 8 + [8, 8] + [512] * 4 + [8, 8]
    offs = [0]
    for s in sizes:
        offs.append(offs[-1] + s)
    part = lambda i: w_in[:, offs[i]:offs[i + 1]]
    big = [part(i) for i in (0, 1, 2, 3, 4, 5, 6, 7, 10, 11, 12, 13)]
    small = jnp.zeros((w_in.shape[0], HEADS, LANE), w_in.dtype)
    for base, i in ((SM_BETA, 8), (SM_DEC, 9), (SM_IG, 14), (SM_FG, 15)):
        cols = part(i).reshape(-1, 2, HEADS)
        for d in range(2):
            small = small.at[:, :, base + d].set(cols[:, d, :])
    return jnp.concatenate(big + [small.reshape(w_in.shape[0], HEADS * LANE)], axis=1)


def _layer(x, cond, lidx, lp, *, B, L, cache):
    T = B * L
    is_ctx = cache is None
    mod = (jax.nn.silu(cond) @ lp['ada_w'] + lp['ada_b']).reshape(-1, 6, 1, D_MODEL)
    shift1, scale1, gate1, shift2, scale2, gate2 = [mod[:, i] for i in range(6)]
    rows_per_mod = T if mod.shape[0] == 1 else L
    g = lp['norm_g']

    tm = 1024
    P = _norm_mod_matmul(x, g[0:1], shift1, scale1, lp['w_in'], jnp.zeros((1, N_PROJ), F32),
                         rows_per_mod=rows_per_mod, sigmoid=False, tm=tm, tn=N_PROJ // 4)
    gates = _norm_mod_matmul(x, g[0:1], shift1, scale1, lp['w_bgate'], lp['b_bgate'][None],
                             rows_per_mod=rows_per_mod, sigmoid=True, tm=tm, tn=2048)

    lam_init = 0.8 - 0.6 * math.exp(-0.3 * lidx)
    lq1, lk1, lq2, lk2 = lp['attn_lambda']
    lam = (jnp.exp(jnp.sum(lq1 * lk1)) - jnp.exp(jnp.sum(lq2 * lk2)) + lam_init).reshape(1, 1)
    attn_norm = lp['attn_norm'][None]
    if is_ctx:
        br_a, new_k, new_v = _attention(P, lam, attn_norm, B=B, L=L, tq=L, lam_init=lam_init)
    else:
        (br_a,) = _attention(P, lam, attn_norm, B=B, L=L, tq=256, lam_init=lam_init, cache=cache['attn'])

    br_b = _pool(P, lp['pool_w'], lp['pool_b'][:, None], lp['pool_scale'][None], B=B, L=L)

    gdn_args = (P, lp['gdn_A_log'], lp['gdn_dt_bias'], lp['gdn_conv'], lp['gdn_norm'][None])
    ml_args = (P, lp['mlstm_bias_i'], lp['mlstm_bias_f'], lp['mlstm_norm'][None])
    if is_ctx:
        br_c, new_gdn = _gdn(*gdn_args, B=B, L=L)
        br_d, new_c, new_n, new_m = _mlstm(*ml_args, B=B, L=L)
        ctx_out = (new_k, new_v, new_gdn, new_c,
                   new_n[:, :, :, 0, :HEAD_DIM].transpose(0, 2, 1, 3), new_m[:, :, :, 0, 0].transpose(0, 2, 1))
    else:
        (br_c,) = _gdn(*gdn_args, B=B, L=L, state=cache['gdn'])
        (br_d,) = _mlstm(*ml_args, B=B, L=L, state=cache['mlstm'])
        ctx_out = None

    x, h2, logits = _merge(x, gates, (br_a, br_b, br_c, br_d), lp['w_branch'], lp['w_out'], g[1:2], gate1,
                           g[2:3], shift2, scale2, lp['router_w'], lp['router_b'],
                           rows_per_mod=rows_per_mod, tm=256)
    expert_out, weights = _moe(h2, logits[:, :N_EXPERTS], lp['moe_w_gu'], lp['moe_b_gu'], lp['moe_w_dn'],
                               lp['moe_b_dn'])
    x = _combine(x, expert_out, weights, g[3:4], gate2, rows_per_mod=rows_per_mod, tm=512)
    return x, ctx_out


def kernel(x_prompt, x_sample, cache_attn_k, cache_attn_v, state_gdn, state_mlstm_C, state_mlstm_n, state_mlstm_m, c, c_ctx, ada_w, ada_b, norm_g, w_in, w_bgate, b_bgate, w_branch, w_out, attn_lambda, attn_norm, pool_w, pool_b, pool_scale, gdn_conv, gdn_A_log, gdn_dt_bias, gdn_norm, mlstm_bias_i, mlstm_bias_f, mlstm_norm, router_w, router_b, moe_w_gu, moe_b_gu, moe_w_dn, moe_b_dn):
    Bp, Lp, _ = x_prompt.shape
    Bs, Ls, _ = x_sample.shape
    y_p = x_prompt.reshape(Bp * Lp, D_MODEL)
    y_s = x_sample.reshape(Bs * Ls, D_MODEL)
    cos, sin = _rope_tables(Ls)
    n0 = state_mlstm_n.transpose(1, 0, 3, 2, 4)[:, :, :, :, None, :]
    m0 = jnp.broadcast_to(state_mlstm_m.transpose(1, 0, 3, 2)[..., None, None], n0.shape)
    ctx_states = []
    for l in range(DEPTH):
        lp = {
            'ada_w': ada_w[l], 'ada_b': ada_b[l], 'norm_g': norm_g[l],
            'w_in': _arrange_w_in(w_in[l].astype(BF16)), 'w_bgate': w_bgate[l].astype(BF16),
            'b_bgate': b_bgate[l], 'w_branch': w_branch[l].astype(BF16), 'w_out': w_out[l].astype(BF16),
            'attn_lambda': attn_lambda[l], 'attn_norm': attn_norm[l],
            'pool_w': pool_w[l], 'pool_b': pool_b[l], 'pool_scale': pool_scale[l],
            'gdn_conv': gdn_conv[l], 'gdn_A_log': gdn_A_log[l], 'gdn_dt_bias': gdn_dt_bias[l],
            'gdn_norm': gdn_norm[l], 'mlstm_bias_i': mlstm_bias_i[l], 'mlstm_bias_f': mlstm_bias_f[l],
            'mlstm_norm': mlstm_norm[l],
            'router_w': jnp.pad(router_w[l], ((0, 0), (0, LANE - N_EXPERTS))),
            'router_b': jnp.pad(router_b[l], (0, LANE - N_EXPERTS))[None],
            'moe_w_gu': moe_w_gu[l].astype(BF16), 'moe_b_gu': moe_b_gu[l][:, None],
            'moe_w_dn': moe_w_dn[l].astype(BF16), 'moe_b_dn': moe_b_dn[l][:, None],
        }
        y_p, st = _layer(y_p, c_ctx[None], l, lp, B=Bp, L=Lp, cache=None)
        ctx_states.append(st)
        cache = {
            'attn': {'k': cache_attn_k, 'v': cache_attn_v, 'layer': l, 'cos': cos, 'sin': sin},
            'gdn': (state_gdn, l),
            'mlstm': (state_mlstm_C, l, n0[l], m0[l]),
        }
        y_s, _ = _layer(y_s, c, l, lp, B=Bs, L=Ls, cache=cache)
    outs = [jnp.stack([s[i] for s in ctx_states], axis=1) for i in range(6)]
    return (y_p.reshape(Bp, Lp, D_MODEL), y_s.reshape(Bs, Ls, D_MODEL), *outs)
```

```python
import functools
import math

import jax
import jax.numpy as jnp
from jax import lax
from jax.experimental import pallas as pl
from jax.experimental.pallas import tpu as pltpu

F32 = jnp.float32
BF16 = jnp.bfloat16

D_MODEL = 1024
DEPTH = 2
GRID_W = 64
EPS = 1e-6
LANE = 128
HEADS = 4
HEAD_DIM = 128
A_DIM = 64
ROPE_BASE = 10000.0
POOL_WINDOWS = (2, 4, 8, 16)
CHUNK = 64
N_BRANCH = 4
BRANCH_WIDTH = 512
N_EXPERTS = 32
TOP_K = 4
D_EXPERT = 1024
SWIGLU_LIMIT = 7.0
SWIGLU_ALPHA = 1.702
MOE_ROWS = 256
VMEM_LIMIT = 56 * 1024 * 1024

COL_AQ, COL_AK, COL_AV, COL_POOL, COL_CQ, COL_CK, COL_CV, COL_CG, COL_MQ, COL_MK, COL_MV, COL_MG, COL_SM = range(13)
N_PROJ = 13 * BRANCH_WIDTH
SM_BETA, SM_DEC, SM_IG, SM_FG = 0, 2, 4, 6


def _params(*sem):
    return pltpu.CompilerParams(dimension_semantics=sem, vmem_limit_bytes=VMEM_LIMIT)


def _dot(a, b):
    return jnp.dot(a.astype(BF16), b.astype(BF16), preferred_element_type=F32)


def _dot_nt(a, b):
    return lax.dot_general(a.astype(BF16), b.astype(BF16), (((1,), (1,)), ((), ())),
                           preferred_element_type=F32)


def _dot_tn(a, b):
    return lax.dot_general(a.astype(BF16), b.astype(BF16), (((0,), (0,)), ((), ())),
                           preferred_element_type=F32)


def _split(a):
    hi = a.astype(BF16)
    lo = (a - hi.astype(F32)).astype(BF16)
    return hi, lo


def _dot3(a, b):
    ah, al = _split(a)
    bh, bl = _split(b)
    d = lambda x, y: jnp.dot(x, y, preferred_element_type=F32)
    return d(ah, bh) + (d(ah, bl) + d(al, bh))


def _dot_mask(mask, b):
    m = jnp.where(mask, 1.0, 0.0).astype(BF16)
    b0 = b.astype(BF16)
    r1 = b - b0.astype(F32)
    b1 = r1.astype(BF16)
    b2 = (r1 - b1.astype(F32)).astype(BF16)
    d = lambda y: jnp.dot(m, y, preferred_element_type=F32)
    return d(b0) + (d(b1) + d(b2))


ROW_TILE = 8
assert ROW_TILE * LANE == D_MODEL


def _store_row_tiles(ref, x):
    rows = x.shape[0]
    for c in range(ROW_TILE):
        ref[pl.ds(c, rows, stride=ROW_TILE), :] = x[:, c * LANE:(c + 1) * LANE]


def _load_row_tiles(ref, rows, dtype=F32):
    return jnp.concatenate([ref[pl.ds(c, rows, stride=ROW_TILE), :].astype(dtype) for c in range(ROW_TILE)],
                           axis=1)


def _rms(x):
    return x * lax.rsqrt(jnp.mean(x * x, axis=-1, keepdims=True) + EPS)


def _sigmoid(x):
    return 1.0 / (1.0 + jnp.exp(-x))


def _softplus(x):
    return jnp.maximum(x, 0.0) + jnp.log(1.0 + jnp.exp(-jnp.abs(x)))


def _proj_kernel(x_ref, g_ref, sh_ref, sc_ref, w_ref, b_ref, o_ref, h_scr, *, sigmoid):
    @pl.when(pl.program_id(1) == 0)
    def _():
        y = _rms(x_ref[...]) * g_ref[...]
        h_scr[...] = (y * (1.0 + sc_ref[0]) + sh_ref[0]).astype(BF16)

    acc = jnp.dot(h_scr[...], w_ref[...], preferred_element_type=F32) + b_ref[...]
    o_ref[...] = _sigmoid(acc) if sigmoid else acc


def _norm_mod_matmul(x, g, shift, scale, w, bias, *, rows_per_mod, sigmoid, tm, tn):
    T, Dm = x.shape
    N = w.shape[1]
    tm = min(tm, rows_per_mod)
    mod_idx = lambda i, j: ((i * tm) // rows_per_mod, 0, 0)
    return pl.pallas_call(
        functools.partial(_proj_kernel, sigmoid=sigmoid),
        out_shape=jax.ShapeDtypeStruct((T, N), F32),
        grid=(T // tm, N // tn),
        in_specs=[
            pl.BlockSpec((tm, Dm), lambda i, j: (i, 0)),
            pl.BlockSpec((1, Dm), lambda i, j: (0, 0)),
            pl.BlockSpec((1, 1, Dm), mod_idx),
            pl.BlockSpec((1, 1, Dm), mod_idx),
            pl.BlockSpec((Dm, tn), lambda i, j: (0, j)),
            pl.BlockSpec((1, tn), lambda i, j: (0, j)),
        ],
        out_specs=pl.BlockSpec((tm, tn), lambda i, j: (i, j)),
        scratch_shapes=[pltpu.VMEM((tm, Dm), BF16)],
        compiler_params=_params("parallel", "arbitrary"),
        name="proj",
    )(x, g, shift, scale, w, bias)


def _rope(x, cos, sin):
    lane = lax.broadcasted_iota(jnp.int32, x.shape, 1)
    first = (lane % 32) < 16
    partner = jnp.where(first, pltpu.roll(x, LANE - 16, 1), pltpu.roll(x, 16, 1))
    return x * cos + partner * sin


def _attn_kernel(*refs, has_cache, n_ctx, out_scale):
    if has_cache:
        (lam_ref, q_ref, k_ref, v_ref, ck_ref, cv_ref, cosq_ref, sinq_ref, cosk_ref, sink_ref, nrm_ref,
         o_ref, kall, vall) = refs
    else:
        lam_ref, q_ref, k_ref, v_ref, nrm_ref, o_ref, ko_ref, vo_ref, kall, vall = refs

    @pl.when(pl.program_id(2) == 0)
    def _():
        k = k_ref[...]
        v = v_ref[...]
        if has_cache:
            kall[:, pl.ds(0, n_ctx)] = ck_ref[...].T.astype(BF16)
            vall[pl.ds(0, n_ctx), :] = cv_ref[...].astype(BF16)
            k = _rope(k, cosk_ref[...], sink_ref[...])
        else:
            ko_ref[...] = k
            vo_ref[...] = v
        kall[:, pl.ds(n_ctx, k.shape[0])] = k.T.astype(BF16)
        vall[pl.ds(n_ctx, k.shape[0]), :] = v.astype(BF16)

    q = q_ref[...]
    if has_cache:
        q = _rope(q, cosq_ref[...], sinq_ref[...])
    q = q * (A_DIM ** -0.5 * math.log2(math.e))
    lane = lax.broadcasted_iota(jnp.int32, q.shape, 1)
    q1 = jnp.where(lane < A_DIM, q, 0.0)
    q2 = jnp.where(lane >= A_DIM, q, 0.0)
    keys = kall[...]
    lam = lam_ref[0, 0]

    def probs(qm):
        s = _dot(qm, keys)
        p = jnp.exp2(s - jnp.max(s, axis=-1, keepdims=True))
        return p, jnp.sum(p, axis=-1, keepdims=True)

    p1, l1 = probs(q1)
    p2, l2 = probs(q2)
    a = p1 * (1.0 / l1) - p2 * (lam / l2)
    o = jnp.dot(a.astype(BF16), vall[...], preferred_element_type=F32)
    o_ref[...] = _rms(o) * nrm_ref[...] * out_scale


def _attention(P, lam, norm, *, B, L, tq, lam_init, cache=None):
    T = B * L
    nq = L // tq
    has_cache = cache is not None
    n_ctx = cache["k"].shape[3] if has_cache else 0
    colq, colk, colv = COL_AQ * HEADS, COL_AK * HEADS, COL_AV * HEADS
    in_specs = [
        pl.BlockSpec(memory_space=pltpu.SMEM),
        pl.BlockSpec((tq, LANE), lambda b, h, i: (b * nq + i, colq + h)),
        pl.BlockSpec((L, LANE), lambda b, h, i: (b, colk + h)),
        pl.BlockSpec((L, LANE), lambda b, h, i: (b, colv + h)),
    ]
    args = [lam, P, P, P]
    if has_cache:
        l = cache["layer"]
        cspec = pl.BlockSpec((None, None, None, n_ctx, LANE), lambda b, h, i: (b, l, h, 0, 0))
        in_specs += [cspec, cspec,
                     pl.BlockSpec((tq, LANE), lambda b, h, i: (i, 0)),
                     pl.BlockSpec((tq, LANE), lambda b, h, i: (i, 0)),
                     pl.BlockSpec((L, LANE), lambda b, h, i: (0, 0)),
                     pl.BlockSpec((L, LANE), lambda b, h, i: (0, 0))]
        args += [cache["k"], cache["v"], cache["cos"], cache["sin"], cache["cos"], cache["sin"]]
    in_specs.append(pl.BlockSpec((1, LANE), lambda b, h, i: (0, 0)))
    args.append(norm)
    out_shape = [jax.ShapeDtypeStruct((T, HEADS * LANE), F32)]
    out_specs = [pl.BlockSpec((tq, LANE), lambda b, h, i: (b * nq + i, h))]
    if not has_cache:
        kv_shape = jax.ShapeDtypeStruct((B, HEADS, L, LANE), F32)
        kv_spec = pl.BlockSpec((None, None, L, LANE), lambda b, h, i: (b, h, 0, 0))
        out_shape += [kv_shape, kv_shape]
        out_specs += [kv_spec, kv_spec]
    return pl.pallas_call(
        functools.partial(_attn_kernel, has_cache=has_cache, n_ctx=n_ctx, out_scale=1.0 - lam_init),
        out_shape=out_shape,
        grid=(B, HEADS, nq),
        in_specs=in_specs,
        out_specs=out_specs,
        scratch_shapes=[pltpu.VMEM((LANE, n_ctx + L), BF16), pltpu.VMEM((n_ctx + L, LANE), BF16)],
        compiler_params=_params("parallel", "parallel", "arbitrary"),
        name="diff_attention",
    )(*args)


POOL_PAD = 16


def _pool_kernel(x_ref, w_ref, b_ref, s_ref, o_ref, pad, *, L):
    zeros = jnp.zeros((POOL_PAD, LANE), F32)
    pad[pl.ds(0, POOL_PAD), :] = zeros
    pad[pl.ds(POOL_PAD + L, POOL_PAD), :] = zeros
    x = x_ref[...]
    pad[pl.ds(POOL_PAD, L), :] = x
    t = lax.broadcasted_iota(jnp.int32, (L, LANE), 0)
    g = pl.program_id(1)
    for gi, win in enumerate(POOL_WINDOWS):
        @pl.when(g == gi)
        def _(win=win):
            half = win // 2
            acc = pad[pl.ds(POOL_PAD - half, L), :]
            for k in range(1 - half, half):
                acc = acc + pad[pl.ds(POOL_PAD + k, L), :]
            cnt = (jnp.minimum(t + half, L) - jnp.maximum(t - half, 0)).astype(F32)
            pooled = acc / cnt - x
            o_ref[...] = (_dot(pooled, w_ref[...]) + b_ref[...]) * s_ref[...]


def _pool(P, w, b, scale, *, B, L):
    T = B * L
    G = len(POOL_WINDOWS)
    return pl.pallas_call(
        functools.partial(_pool_kernel, L=L),
        out_shape=jax.ShapeDtypeStruct((T, G * LANE), F32),
        grid=(B, G),
        in_specs=[
            pl.BlockSpec((L, LANE), lambda bi, g: (bi, COL_POOL * HEADS + g)),
            pl.BlockSpec((None, LANE, LANE), lambda bi, g: (g, 0, 0)),
            pl.BlockSpec((None, 1, LANE), lambda bi, g: (g, 0, 0)),
            pl.BlockSpec((1, LANE), lambda bi, g: (0, g)),
        ],
        out_specs=pl.BlockSpec((L, LANE), lambda bi, g: (bi, g)),
        scratch_shapes=[pltpu.VMEM((L + 2 * POOL_PAD, LANE), F32)],
        compiler_params=_params("parallel", "parallel"),
        name="pool_mixer",
    )(P, w, b, scale)


GROUP_CHUNKS = 2
GROUP = GROUP_CHUNKS * CHUNK
MAX_GROUPS_PER_STEP = 4


def _groups_per_step(L):
    return min(MAX_GROUPS_PER_STEP, L // GROUP)


def _group_masks(backward):
    row = lax.broadcasted_iota(jnp.int32, (GROUP, GROUP), 0)
    col = lax.broadcasted_iota(jnp.int32, (GROUP, GROUP), 1)
    same = (row // CHUNK) == (col // CHUNK)
    lower, upper = row >= col, row <= col
    if backward:
        lower, upper = upper, lower
    return dict(same=same, incl=same & lower, strict=same & lower & (row != col), incl_t=same & upper,
                eye=row == col)


INV_BASE = 8


def _block_masks():
    row = lax.broadcasted_iota(jnp.int32, (GROUP, GROUP), 0)
    col = lax.broadcasted_iota(jnp.int32, (GROUP, GROUP), 1)
    sizes = [INV_BASE << i for i in range(int(math.log2(CHUNK // INV_BASE)) + 1)]
    same = [(row // s) == (col // s) for s in sizes]
    return [same[0]] + [cur & jnp.logical_not(prev) for prev, cur in zip(same[:-1], same[1:])]


def _unit_triangular_inverses(ns, eye, blocks):
    base = [jnp.where(blocks[0], n, 0.0) for n in ns]
    invs = [eye - b for b in base]
    pws = [_dot(b, b) for b in base]
    rounds = int(math.log2(INV_BASE)) - 1
    for s in range(rounds):
        invs = [t + _dot(t, p) for t, p in zip(invs, pws)]
        if s + 1 < rounds:
            pws = [_dot(p, p) for p in pws]
    for join in blocks[1:]:
        offs = [jnp.where(join, n, 0.0) for n in ns]
        xs = [_dot(o, t) for o, t in zip(offs, invs)]
        invs = [t - _dot(t, x) for t, x in zip(invs, xs)]
    return invs


def _dot_mask2(mask, b):
    m = jnp.where(mask, 1.0, 0.0).astype(BF16)
    b0, b1 = _split(b)
    return jnp.dot(m, b0, preferred_element_type=F32) + jnp.dot(m, b1, preferred_element_type=F32)


def _per_chunk(x, reduce):
    parts = [reduce(x[c * CHUNK:(c + 1) * CHUNK], axis=0, keepdims=True) for c in range(GROUP_CHUNKS)]
    col = jnp.concatenate([jnp.broadcast_to(p, (CHUNK, 1)) for p in parts], axis=0)
    return parts, col


def _scan_order(step, n_steps, groups):
    base = (step * groups * GROUP, (n_steps - 1 - step) * groups * GROUP)
    fwd = [(gi, c) for gi in range(groups) for c in range(GROUP_CHUNKS)]
    return base, (fwd, fwd[::-1])


CONV_PAD = 8


def _gdn_kernel(*refs, L, has_state):
    alog_ref, dtb_ref, q_ref, k_ref, v_ref, gate_ref, sm_ref, cwq_ref, cwk_ref, cwv_ref, nrm_ref = refs[:11]
    refs = refs[11:]
    if has_state:
        s0_ref, o_ref, pad, qs, ks, vs, o_f, o_b, state = refs
    else:
        o_ref, sout_ref, pad, qs, ks, vs, o_f, o_b, state = refs
    h = pl.program_id(1)
    n_chunks = L // CHUNK

    zeros = jnp.zeros((CONV_PAD, LANE), F32)
    pad[pl.ds(0, CONV_PAD), :] = zeros
    pad[pl.ds(CONV_PAD + L, CONV_PAD), :] = zeros

    def conv_silu(x_ref, w_ref):
        pad[pl.ds(CONV_PAD, L), :] = x_ref[...]
        w = w_ref[...]
        y = (pad[pl.ds(CONV_PAD - 1, L), :] * w[0:1] + pad[pl.ds(CONV_PAD, L), :] * w[1:2]
             + pad[pl.ds(CONV_PAD + 1, L), :] * w[2:3])
        return y * _sigmoid(y)

    def l2n(x):
        return x * lax.rsqrt(jnp.sum(x * x, axis=-1, keepdims=True) + EPS)

    qs[...] = l2n(conv_silu(q_ref, cwq_ref)) * (HEAD_DIM ** -0.5)
    ks[...] = l2n(conv_silu(k_ref, cwk_ref))
    vs[...] = conv_silu(v_ref, cwv_ref)
    if has_state:
        state[...] = s0_ref[...]
    else:
        state[...] = jnp.zeros_like(state)

    groups = _groups_per_step(L)
    n_steps = L // (groups * GROUP)
    o_scr = (o_f, o_b)
    masks = (_group_masks(False), _group_masks(True))
    eye = jnp.where(masks[0]['eye'], 1.0, 0.0)
    blocks = _block_masks()
    items = [(d, gi) for gi in range(groups) for d in (0, 1)]
    chunks = [slice(c * CHUNK, (c + 1) * CHUNK) for c in range(GROUP_CHUNKS)]

    def body(step, carry):
        base, order = _scan_order(step, n_steps, groups)
        it = {}
        for key in items:
            d, gi = key
            rows = pl.ds(pl.multiple_of(base[d] + gi * GROUP, GROUP), GROUP)
            q, k, v, sm = qs[rows, :], ks[rows, :], vs[rows, :], sm_ref[rows, :]
            beta = _sigmoid(sm[:, SM_BETA + d:SM_BETA + d + 1])
            dec = sm[:, SM_DEC + d:SM_DEC + d + 1]
            g = -jnp.exp(alog_ref[d, h]) * _softplus(dec + dtb_ref[d, h])
            it[key] = dict(q=q, k=k, kb=k * beta, vb=v * beta, g=g, gb=jnp.broadcast_to(g, (GROUP, GROUP)))
        for (d, gi), x in it.items():
            m = masks[d]
            x['cum_i'] = _dot_mask2(m['incl'], x['gb'])
            x['cum_j'] = _dot_mask2(m['same'], jnp.where(m['incl_t'], x['gb'], 0.0))
            x['kk'] = _dot_nt(x['kb'], x['k'])
            x['qk'] = _dot_nt(x['q'], x['k'])
        for (d, gi), x in it.items():
            m = masks[d]
            decay = jnp.where(m['incl'], jnp.exp(jnp.where(m['incl'], x['cum_i'] - x['cum_j'], 0.0)), 0.0)
            x['gc'] = x['cum_i'][:, 0:1]
            x['tot'], x['tot_col'] = _per_chunk(x['g'], jnp.sum)
            x['qk'] = x['qk'] * decay
            x['n'] = jnp.where(m['strict'], x['kk'] * decay, 0.0)
        invs = _unit_triangular_inverses([x['n'] for x in it.values()], eye, blocks)
        for x, inv in zip(it.values(), invs):
            x['inv'] = inv
            x['egc'] = jnp.exp(x['gc'])
            x['sol'] = _dot3(x['inv'], jnp.concatenate([x['vb'], x['kb'] * x['egc']], axis=-1))
        for x in it.values():
            u, w = x['sol'][:, :HEAD_DIM], x['sol'][:, HEAD_DIM:]
            x['qp'] = x['q'] * x['egc'] - _dot(x['qk'], w)
            x['op'] = _dot(x['qk'], u)
            kd = x['k'] * jnp.exp(x['tot_col'] - x['gc'])
            x['ab'] = [_dot_tn(kd[cs], x['sol'][cs]) for cs in chunks]
            x['gl'] = [jnp.exp(t) for t in x['tot']]
        S = [state[0], state[1]]
        for stp in range(len(order[0])):
            for d in (0, 1):
                gi, c = order[d][stp]
                x = it[(d, gi)]
                ab = x['ab'][c]
                r = _dot(jnp.concatenate([ab[:, HEAD_DIM:], x['qp'][chunks[c]]], axis=0), S[d])
                rows = pl.ds(pl.multiple_of(base[d] + gi * GROUP + c * CHUNK, CHUNK), CHUNK)
                o_scr[d][rows, :] = x['op'][chunks[c]] + r[HEAD_DIM:]
                S[d] = x['gl'][c] * S[d] + (ab[:, :HEAD_DIM] - r[:HEAD_DIM])
        state[0] = S[0]
        state[1] = S[1]
        return carry

    lax.fori_loop(0, n_steps, body, 0)
    gate = gate_ref[...]
    o_ref[...] = _rms(o_f[...] + o_b[...]) * nrm_ref[...] * (gate * _sigmoid(gate))
    if not has_state:
        sout_ref[...] = state[...]


def _col_spec(L, col):
    return pl.BlockSpec((L, LANE), lambda b, h: (b, col * HEADS + h))


def _gdn(P, a_log, dt_bias, conv_w, norm, *, B, L, state=None):
    T = B * L
    has_state = state is not None
    smem = pl.BlockSpec(memory_space=pltpu.SMEM)
    in_specs = [smem, smem, _col_spec(L, COL_CQ), _col_spec(L, COL_CK), _col_spec(L, COL_CV),
                _col_spec(L, COL_CG), _col_spec(L, COL_SM)]
    in_specs += [pl.BlockSpec((3, LANE), lambda b, h, j=j: (0, j * HEADS + h)) for j in range(3)]
    in_specs.append(pl.BlockSpec((1, LANE), lambda b, h: (0, 0)))
    args = [a_log, dt_bias, P, P, P, P, P, conv_w, conv_w, conv_w, norm]
    out_shape = [jax.ShapeDtypeStruct((T, HEADS * LANE), F32)]
    out_specs = [pl.BlockSpec((L, LANE), lambda b, h: (b, h))]
    if has_state:
        arr, l = state
        in_specs.append(pl.BlockSpec((None, None, 2, None, HEAD_DIM, HEAD_DIM), lambda b, h: (b, l, 0, h, 0, 0)))
        args.append(arr)
    else:
        out_shape.append(jax.ShapeDtypeStruct((B, 2, HEADS, HEAD_DIM, HEAD_DIM), F32))
        out_specs.append(pl.BlockSpec((None, 2, None, HEAD_DIM, HEAD_DIM), lambda b, h: (b, 0, h, 0, 0)))
    seq = pltpu.VMEM((L, LANE), F32)
    return pl.pallas_call(
        functools.partial(_gdn_kernel, L=L, has_state=has_state),
        out_shape=out_shape,
        grid=(B, HEADS),
        in_specs=in_specs,
        out_specs=out_specs,
        scratch_shapes=[pltpu.VMEM((L + 2 * CONV_PAD, LANE), F32), seq, seq, seq, seq, seq,
                        pltpu.VMEM((2, HEAD_DIM, HEAD_DIM), F32)],
        compiler_params=_params("parallel", "parallel"),
        name="gated_deltanet",
    )(*args)


def _mlstm_kernel(*refs, L, has_state):
    bi_ref, bf_ref, q_ref, k_ref, v_ref, gate_ref, sm_ref, nrm_ref = refs[:8]
    refs = refs[8:]
    if has_state:
        c0_ref, n0_ref, m0_ref, o_ref, h_f, h_b, c_st, n_st, m_st = refs
    else:
        o_ref, cout_ref, nout_ref, mout_ref, h_f, h_b, c_st, n_st, m_st = refs
    h = pl.program_id(1)
    n_chunks = L // CHUNK

    if has_state:
        c_st[...] = c0_ref[...]
        n_st[...] = n0_ref[...]
        m_st[...] = m0_ref[...]
    else:
        c_st[...] = jnp.zeros_like(c_st)
        n_st[...] = jnp.zeros_like(n_st)
        m_st[...] = jnp.zeros_like(m_st)

    groups = _groups_per_step(L)
    n_steps = L // (groups * GROUP)
    h_scr = (h_f, h_b)
    masks = (_group_masks(False), _group_masks(True))
    items = [(d, gi) for gi in range(groups) for d in (0, 1)]
    chunks = [slice(c * CHUNK, (c + 1) * CHUNK) for c in range(GROUP_CHUNKS)]

    def body(step, carry):
        base, order = _scan_order(step, n_steps, groups)
        it = {}
        for key in items:
            d, gi = key
            rows = pl.ds(pl.multiple_of(base[d] + gi * GROUP, GROUP), GROUP)
            q, v, sm = q_ref[rows, :], v_ref[rows, :], sm_ref[rows, :]
            k = k_ref[rows, :] * (HEAD_DIM ** -0.5)
            ig = sm[:, SM_IG + d:SM_IG + d + 1] + bi_ref[d, h]
            fg = sm[:, SM_FG + d:SM_FG + d + 1] + bf_ref[d, h]
            lf = -_softplus(-fg)
            it[key] = dict(q=q, k=k, v=v, ig=ig, lf=lf, lfb=jnp.broadcast_to(lf, (GROUP, GROUP)),
                           igb=jnp.broadcast_to(ig, (GROUP, GROUP)))
        for (d, gi), x in it.items():
            m = masks[d]
            x['cum_i'] = _dot_mask2(m['incl'], x['lfb'])
            x['cum_j'] = _dot_mask2(m['same'], jnp.where(m['incl_t'], x['lfb'], 0.0)
                                    - jnp.where(m['eye'], x['igb'], 0.0))
            x['qk'] = _dot_nt(x['q'], x['k'])
        for (d, gi), x in it.items():
            m = masks[d]
            dm = x['cum_i'] - x['cum_j']
            x['b'] = x['cum_i'][:, 0:1]
            x['tot'], tot_col = _per_chunk(x['lf'], jnp.sum)
            x['m_intra'] = jnp.max(jnp.where(m['incl'], dm, -jnp.inf), axis=-1, keepdims=True)
            e = tot_col - x['b'] + x['ig']
            x['m_end'], m_end_col = _per_chunk(e, jnp.max)
            p = jnp.where(m['incl'], jnp.exp(jnp.where(m['incl'], dm - x['m_intra'], 0.0)), 0.0) * x['qk']
            x['p_sum'] = jnp.sum(p, axis=-1, keepdims=True)
            x['p'] = p
            x['kw'] = x['k'] * jnp.exp(e - m_end_col)
        for x in it.values():
            x['pv'] = _dot(x['p'], x['v'])
            x['kv'] = [_dot_tn(x['kw'][cs], x['v'][cs]) for cs in chunks]
            x['k_sum'] = [jnp.sum(x['kw'][cs], axis=0, keepdims=True) for cs in chunks]
        outs = []
        for d in (0, 1):
            C, n, m = c_st[d], n_st[d], m_st[d][:, 0:1]
            for gi, c in order[d]:
                x = it[(d, gi)]
                cs = chunks[c]
                m_t = jnp.maximum(x['b'][cs] + m, x['m_intra'][cs])
                w_inter = jnp.exp(x['b'][cs] + m - m_t)
                local = jnp.exp(x['m_intra'][cs] - m_t)
                den = (w_inter * jnp.sum(x['q'][cs] * n, axis=-1, keepdims=True) + local * x['p_sum'][cs])
                scale = 1.0 / jnp.maximum(jnp.abs(den), jnp.exp(-m_t))
                outs.append((d, gi, c, C, w_inter * scale, local * scale))
                m_new = jnp.maximum(x['tot'][c] + m, x['m_end'][c])
                carry_decay = jnp.exp(x['tot'][c] + m - m_new)
                local_new = jnp.exp(x['m_end'][c] - m_new)
                C = carry_decay * C + local_new * x['kv'][c]
                n = carry_decay * n + local_new * x['k_sum'][c]
                m = m_new
            c_st[d] = C
            n_st[d] = n
            m_st[d] = jnp.broadcast_to(m, (1, LANE))
        for d, gi, c, C, w_scale, p_scale in outs:
            x = it[(d, gi)]
            cs = chunks[c]
            rows = pl.ds(pl.multiple_of(base[d] + gi * GROUP + c * CHUNK, CHUNK), CHUNK)
            h_scr[d][rows, :] = w_scale * _dot(x['q'][cs], C) + p_scale * x['pv'][cs]
        return carry

    lax.fori_loop(0, n_steps, body, 0)
    o_ref[...] = _rms(h_f[...] + h_b[...]) * nrm_ref[...] * _sigmoid(gate_ref[...])
    if not has_state:
        cout_ref[...] = c_st[...]
        nout_ref[...] = n_st[...]
        mout_ref[...] = m_st[...]


def _mlstm(P, bias_i, bias_f, norm, *, B, L, state=None):
    T = B * L
    has_state = state is not None
    smem = pl.BlockSpec(memory_space=pltpu.SMEM)
    in_specs = [smem, smem, _col_spec(L, COL_MQ), _col_spec(L, COL_MK), _col_spec(L, COL_MV),
                _col_spec(L, COL_MG), _col_spec(L, COL_SM), pl.BlockSpec((1, LANE), lambda b, h: (0, 0))]
    args = [bias_i, bias_f, P, P, P, P, P, norm]
    out_shape = [jax.ShapeDtypeStruct((T, HEADS * LANE), F32)]
    out_specs = [pl.BlockSpec((L, LANE), lambda b, h: (b, h))]
    vec_spec = pl.BlockSpec((None, None, 2, 1, LANE), lambda b, h: (b, h, 0, 0, 0))
    if has_state:
        c_arr, l, n_arr, m_arr = state
        in_specs += [pl.BlockSpec((None, None, 2, None, HEAD_DIM, HEAD_DIM), lambda b, h: (b, l, 0, h, 0, 0)),
                     vec_spec, vec_spec]
        args += [c_arr, n_arr, m_arr]
    else:
        out_shape += [jax.ShapeDtypeStruct((B, 2, HEADS, HEAD_DIM, HEAD_DIM), F32),
                      jax.ShapeDtypeStruct((B, HEADS, 2, 1, LANE), F32),
                      jax.ShapeDtypeStruct((B, HEADS, 2, 1, LANE), F32)]
        out_specs += [pl.BlockSpec((None, 2, None, HEAD_DIM, HEAD_DIM), lambda b, h: (b, 0, h, 0, 0)),
                      vec_spec, vec_spec]
    seq = pltpu.VMEM((L, LANE), F32)
    return pl.pallas_call(
        functools.partial(_mlstm_kernel, L=L, has_state=has_state),
        out_shape=out_shape,
        grid=(B, HEADS),
        in_specs=in_specs,
        out_specs=out_specs,
        scratch_shapes=[seq, seq, pltpu.VMEM((2, HEAD_DIM, HEAD_DIM), F32), pltpu.VMEM((2, 1, LANE), F32),
                        pltpu.VMEM((2, 1, LANE), F32)],
        compiler_params=_params("parallel", "parallel"),
        name="mlstm",
    )(*args)


def _merge_kernel(x_ref, gates_ref, a_ref, b_ref, c_ref, d_ref, wb_ref, wo_ref, g1_ref, gate1_ref, g2_ref,
                  sh2_ref, sc2_ref, rw_ref, rb_ref, xo_ref, h2_ref, lg_ref):
    merged = None
    for i, br in enumerate((a_ref, b_ref, c_ref, d_ref)):
        term = gates_ref[:, i * D_MODEL:(i + 1) * D_MODEL] * _dot(br[...], wb_ref[i])
        merged = term if merged is None else merged + term
    t = _rms(_dot(merged, wo_ref[...])) * g1_ref[...]
    x = x_ref[...] + gate1_ref[0] * t
    xo_ref[...] = x
    h2 = _rms(x) * g2_ref[...] * (1.0 + sc2_ref[0]) + sh2_ref[0]
    _store_row_tiles(h2_ref, h2)
    lg_ref[...] = _dot3(h2, rw_ref[...]) + rb_ref[...]


def _merge(x, gates, branches, wb, wo, g1, gate1, g2, shift2, scale2, rw, rb, *, rows_per_mod, tm):
    T = x.shape[0]
    tm = min(tm, rows_per_mod)
    row = lambda n: pl.BlockSpec((tm, n), lambda i: (i, 0))
    const = lambda *shape: pl.BlockSpec(shape, lambda i: (0,) * len(shape))
    mod = pl.BlockSpec((1, 1, D_MODEL), lambda i: ((i * tm) // rows_per_mod, 0, 0))
    return pl.pallas_call(
        _merge_kernel,
        out_shape=[jax.ShapeDtypeStruct((T, D_MODEL), F32), jax.ShapeDtypeStruct((T * ROW_TILE, LANE), F32),
                   jax.ShapeDtypeStruct((T, LANE), F32)],
        grid=(T // tm,),
        in_specs=[row(D_MODEL), row(N_BRANCH * D_MODEL)] + [row(BRANCH_WIDTH)] * 4
                 + [const(N_BRANCH, BRANCH_WIDTH, D_MODEL), const(D_MODEL, D_MODEL), const(1, D_MODEL), mod,
                    const(1, D_MODEL), mod, mod, const(D_MODEL, LANE), const(1, LANE)],
        out_specs=[row(D_MODEL), pl.BlockSpec((tm * ROW_TILE, LANE), lambda i: (i, 0)), row(LANE)],
        compiler_params=_params("parallel"),
        name="merge",
    )(x, gates, *branches, wb, wo, g1, gate1, g2, shift2, scale2, rw, rb)


MOE_CHUNKS = 4
MOE_DUMP_ROWS = 2 * MOE_ROWS


def _moe_kernel(be_ref, nu_ref, idx_hbm, h2_hbm, wgu_ref, bgu_ref, wdn_ref, bdn_ref, out_hbm,
                idx_smem, x0, x1, y0, y1, sem_idx, sem_in, sem_out, sem_fill, sem_pace):
    i = pl.program_id(0)
    n_used = nu_ref[0]
    n_blocks = pl.num_programs(0)
    xbuf, ybuf = (x0, x1), (y0, y1)
    n_tokens_rows = out_hbm.shape[0] - MOE_DUMP_ROWS * ROW_TILE
    q_prev, q_cur, q_next = (i + 2) % 3, i % 3, (i + 1) % 3

    def idx_copy(block, s):
        return pltpu.make_async_copy(idx_hbm.at[block], idx_smem.at[s], sem_idx.at[s])

    def tile(ref, first_row):
        if not isinstance(first_row, int):
            first_row = pl.multiple_of(first_row, ROW_TILE)
        return ref.at[pl.ds(first_row, ROW_TILE), :]

    def row_in(s, r, tok_row):
        return pltpu.make_async_copy(tile(h2_hbm, tok_row), tile(xbuf[s], r * ROW_TILE), sem_in.at[s])

    def row_out(s, r, dst_row):
        return pltpu.make_async_copy(tile(ybuf[s], r * ROW_TILE), tile(out_hbm, dst_row), sem_out.at[s])

    def fill(half):
        rows = MOE_ROWS * ROW_TILE
        return pltpu.make_async_copy(y1, out_hbm.at[pl.ds(n_tokens_rows + half * rows, rows), :], sem_fill)

    def wait_rows(copy, s):
        for _ in range(MOE_ROWS):
            copy(s, 0, 0).wait()

    @pl.when(i == 0)
    def _():
        idx_copy(0, 0).start()
        idx_copy(n_blocks, 2).start()
        y1[...] = jnp.zeros_like(y1)
        fill(0).start()
        fill(1).start()
        idx_copy(0, 0).wait()
        idx_copy(n_blocks, 2).wait()

        def body(r, carry):
            row_in(0, r, idx_smem[0, 0, r]).start()
            return carry
        lax.fori_loop(0, MOE_ROWS, body, 0, unroll=8)
        fill(0).wait()
        fill(1).wait()

    def step(slot):
        other = 1 - slot
        nxt = jnp.minimum(i + 1, n_blocks - 1)
        idx_copy(nxt, q_next).start()
        wait_rows(row_in, slot)
        idx_copy(nxt, q_next).wait()

        @pl.when(i >= 1)
        def _():
            wait_rows(row_out, slot)

        x = _load_row_tiles(xbuf[slot], MOE_ROWS, BF16)
        y = jnp.broadcast_to(bdn_ref[...], (MOE_ROWS, D_MODEL))
        cols = D_EXPERT // MOE_CHUNKS
        rows = MOE_ROWS // MOE_CHUNKS
        for c in range(MOE_CHUNKS):
            if c:
                pl.semaphore_signal(sem_pace, 1)
                pl.semaphore_wait(sem_pace, 1)
            for r in range(c * rows, (c + 1) * rows):
                row_in(other, r, idx_smem[q_next, 0, r]).start()
                row_out(other, r, idx_smem[q_prev, 1, r]).start()
            g_cols = slice(c * cols, (c + 1) * cols)
            u_cols = slice(D_EXPERT + c * cols, D_EXPERT + (c + 1) * cols)
            hg = jnp.minimum(_dot(x, wgu_ref[:, g_cols]) + bgu_ref[:, g_cols], SWIGLU_LIMIT)
            hu = jnp.clip(_dot(x, wgu_ref[:, u_cols]) + bgu_ref[:, u_cols], -SWIGLU_LIMIT, SWIGLU_LIMIT)
            act = (hu + 1.0) * (hg * _sigmoid(SWIGLU_ALPHA * hg))
            y = y + _dot(act, wdn_ref[g_cols, :])
        _store_row_tiles(ybuf[slot], y)

        @pl.when(i == n_used - 1)
        def _():
            for r in range(MOE_ROWS):
                row_out(slot, r, idx_smem[q_cur, 1, r]).start()
            wait_rows(row_out, other)
            wait_rows(row_out, slot)
            wait_rows(row_in, other)

    for parity in (0, 1):
        pl.when((i < n_used) & (i % 2 == parity))(functools.partial(step, parity))


def _moe_blocks(idx, h2, block_e, n_used, w_gu, b_gu, w_dn, b_dn, n_out_rows):
    n_blocks = idx.shape[0] - 1
    exp = lambda i, be, nu: (be[jnp.minimum(i, nu[0] - 1)], 0, 0)
    any_spec = pl.BlockSpec(memory_space=pl.ANY)
    return pl.pallas_call(
        _moe_kernel,
        out_shape=jax.ShapeDtypeStruct(((n_out_rows + MOE_DUMP_ROWS) * ROW_TILE, LANE), F32),
        grid_spec=pltpu.PrefetchScalarGridSpec(
            num_scalar_prefetch=2,
            grid=(n_blocks,),
            in_specs=[
                any_spec,
                any_spec,
                pl.BlockSpec((None, D_MODEL, 2 * D_EXPERT), exp),
                pl.BlockSpec((None, 1, 2 * D_EXPERT), exp),
                pl.BlockSpec((None, D_EXPERT, D_MODEL), exp),
                pl.BlockSpec((None, 1, D_MODEL), exp),
            ],
            out_specs=any_spec,
            scratch_shapes=[pltpu.SMEM((3, 2, MOE_ROWS), jnp.int32)]
                           + [pltpu.VMEM((MOE_ROWS * ROW_TILE, LANE), F32)] * 4
                           + [pltpu.SemaphoreType.DMA((3,)),
                              pltpu.SemaphoreType.DMA((2,)),
                              pltpu.SemaphoreType.DMA((2,)),
                              pltpu.SemaphoreType.DMA,
                              pltpu.SemaphoreType.REGULAR],
        ),
        compiler_params=_params("arbitrary"),
        name="moe_experts",
    )(block_e, n_used, idx, h2, w_gu, b_gu, w_dn, b_dn)


def _moe(h2, logits, w_gu, b_gu, w_dn, b_dn):
    N = logits.shape[0]
    NK = N * TOP_K
    top_val, top_idx = lax.top_k(logits, TOP_K)
    gate = jax.nn.softmax(top_val, axis=-1)
    flat_e = top_idx.reshape(-1)
    order = jnp.argsort(flat_e).astype(jnp.int32)
    counts = jnp.sum((flat_e[:, None] == jnp.arange(N_EXPERTS)[None, :]).astype(jnp.int32), axis=0)
    padded = (counts + MOE_ROWS - 1) // MOE_ROWS * MOE_ROWS
    pad_end = jnp.cumsum(padded)
    pad_start = pad_end - padded
    start = jnp.cumsum(counts) - counts
    n_blocks = (NK + N_EXPERTS * (MOE_ROWS - 1) + MOE_ROWS - 1) // MOE_ROWS
    P = n_blocks * MOE_ROWS
    block_first = jnp.arange(n_blocks, dtype=jnp.int32) * MOE_ROWS
    block_e = jnp.minimum(jnp.sum((block_first[:, None] >= pad_end[None, :]).astype(jnp.int32), axis=1),
                          N_EXPERTS - 1)
    n_used = (pad_end[-1] // MOE_ROWS).astype(jnp.int32).reshape(1)
    pos = jnp.arange(P, dtype=jnp.int32)
    e_pos = jnp.repeat(block_e, MOE_ROWS)
    rank = pos - pad_start[e_pos].astype(jnp.int32)
    valid = rank < counts[e_pos]
    flat = order[jnp.clip(start[e_pos].astype(jnp.int32) + rank, 0, NK - 1)]
    tok = jnp.where(valid, flat // TOP_K, 0)
    spare = NK + (pos // MOE_ROWS % 2) * MOE_ROWS + pos % MOE_ROWS
    dst = jnp.where(valid, (flat % TOP_K) * N + flat // TOP_K, spare)
    idx = jnp.stack([tok.reshape(n_blocks, MOE_ROWS), dst.reshape(n_blocks, MOE_ROWS)], axis=1)
    stand_in = jnp.stack([jnp.zeros((MOE_ROWS,), jnp.int32), NK + MOE_ROWS + jnp.arange(MOE_ROWS, dtype=jnp.int32)])
    idx = jnp.concatenate([idx, stand_in[None]], axis=0) * ROW_TILE
    expert_out = _moe_blocks(idx, h2, block_e, n_used, w_gu, b_gu, w_dn, b_dn, NK)
    return expert_out, gate


def _combine_kernel(x_ref, y0_ref, y1_ref, y2_ref, y3_ref, w_ref, g_ref, gate_ref, o_ref):
    w = w_ref[...]
    y = None
    for k, y_ref in enumerate((y0_ref, y1_ref, y2_ref, y3_ref)):
        term = _load_row_tiles(y_ref, w.shape[0]) * w[:, k:k + 1]
        y = term if y is None else y + term
    o_ref[...] = x_ref[...] + gate_ref[0] * (_rms(y) * g_ref[...])


def _combine(x, expert_out, weights, g, gate, *, rows_per_mod, tm):
    T = x.shape[0]
    tm = min(tm, rows_per_mod)
    row = pl.BlockSpec((tm, D_MODEL), lambda i: (i, 0))
    slabs = [pl.BlockSpec((tm * ROW_TILE, LANE), lambda i, k=k: (k * (T // tm) + i, 0)) for k in range(TOP_K)]
    return pl.pallas_call(
        _combine_kernel,
        out_shape=jax.ShapeDtypeStruct((T, D_MODEL), F32),
        grid=(T // tm,),
        in_specs=[row] + slabs + [pl.BlockSpec((tm, TOP_K), lambda i: (i, 0)),
                                  pl.BlockSpec((1, D_MODEL), lambda i: (0, 0)),
                                  pl.BlockSpec((1, 1, D_MODEL), lambda i: ((i * tm) // rows_per_mod, 0, 0))],
        out_specs=row,
        compiler_params=_params("parallel"),
        name="combine",
    )(x, *[expert_out] * TOP_K, weights, g, gate)


def _rope_tables(L):
    t = jnp.arange(L)
    lane = jnp.arange(LANE)
    axis = (lane % A_DIM) // 32
    half = (lane % 32) // 16
    n_freq = A_DIM // 4
    inv = ROPE_BASE ** (-(lane % n_freq).astype(F32) / n_freq)
    pos = jnp.where(axis[None, :] == 0, (t // GRID_W)[:, None], (t % GRID_W)[:, None]).astype(F32)
    ang = pos * inv[None, :]
    return jnp.cos(ang), jnp.where(half[None, :] == 0, -jnp.sin(ang), jnp.sin(ang))


def _arrange_w_in(w_in):
    sizes = [512] * 8 + [8, 8] + [512] * 4 + [8, 8]
    offs = [0]
    for s in sizes:
        offs.append(offs[-1] + s)
    part = lambda i: w_in[:, offs[i]:offs[i + 1]]
    big = [part(i) for i in (0, 1, 2, 3, 4, 5, 6, 7, 10, 11, 12, 13)]
    small = jnp.zeros((w_in.shape[0], HEADS, LANE), w_in.dtype)
    for base, i in ((SM_BETA, 8), (SM_DEC, 9), (SM_IG, 14), (SM_FG, 15)):
        cols = part(i).reshape(-1, 2, HEADS)
        for d in range(2):
            small = small.at[:, :, base + d].set(cols[:, d, :])
    return jnp.concatenate(big + [small.reshape(w_in.shape[0], HEADS * LANE)], axis=1)


def _layer(x, cond, lidx, lp, *, B, L, cache):
    T = B * L
    is_ctx = cache is None
    mod = (jax.nn.silu(cond) @ lp['ada_w'] + lp['ada_b']).reshape(-1, 6, 1, D_MODEL)
    shift1, scale1, gate1, shift2, scale2, gate2 = [mod[:, i] for i in range(6)]
    rows_per_mod = T if mod.shape[0] == 1 else L
    g = lp['norm_g']

    tm = 1024
    P = _norm_mod_matmul(x, g[0:1], shift1, scale1, lp['w_in'], jnp.zeros((1, N_PROJ), F32),
                         rows_per_mod=rows_per_mod, sigmoid=False, tm=tm, tn=N_PROJ // 4)
    gates = _norm_mod_matmul(x, g[0:1], shift1, scale1, lp['w_bgate'], lp['b_bgate'][None],
                             rows_per_mod=rows_per_mod, sigmoid=True, tm=tm, tn=2048)

    lam_init = 0.8 - 0.6 * math.exp(-0.3 * lidx)
    lq1, lk1, lq2, lk2 = lp['attn_lambda']
    lam = (jnp.exp(jnp.sum(lq1 * lk1)) - jnp.exp(jnp.sum(lq2 * lk2)) + lam_init).reshape(1, 1)
    attn_norm = lp['attn_norm'][None]
    if is_ctx:
        br_a, new_k, new_v = _attention(P, lam, attn_norm, B=B, L=L, tq=L, lam_init=lam_init)
    else:
        (br_a,) = _attention(P, lam, attn_norm, B=B, L=L, tq=256, lam_init=lam_init, cache=cache['attn'])

    br_b = _pool(P, lp['pool_w'], lp['pool_b'][:, None], lp['pool_scale'][None], B=B, L=L)

    gdn_args = (P, lp['gdn_A_log'], lp['gdn_dt_bias'], lp['gdn_conv'], lp['gdn_norm'][None])
    ml_args = (P, lp['mlstm_bias_i'], lp['mlstm_bias_f'], lp['mlstm_norm'][None])
    if is_ctx:
        br_c, new_gdn = _gdn(*gdn_args, B=B, L=L)
        br_d, new_c, new_n, new_m = _mlstm(*ml_args, B=B, L=L)
        ctx_out = (new_k, new_v, new_gdn, new_c,
                   new_n[:, :, :, 0, :HEAD_DIM].transpose(0, 2, 1, 3), new_m[:, :, :, 0, 0].transpose(0, 2, 1))
    else:
        (br_c,) = _gdn(*gdn_args, B=B, L=L, state=cache['gdn'])
        (br_d,) = _mlstm(*ml_args, B=B, L=L, state=cache['mlstm'])
        ctx_out = None

    x, h2, logits = _merge(x, gates, (br_a, br_b, br_c, br_d), lp['w_branch'], lp['w_out'], g[1:2], gate1,
                           g[2:3], shift2, scale2, lp['router_w'], lp['router_b'],
                           rows_per_mod=rows_per_mod, tm=256)
    expert_out, weights = _moe(h2, logits[:, :N_EXPERTS], lp['moe_w_gu'], lp['moe_b_gu'], lp['moe_w_dn'],
                               lp['moe_b_dn'])
    x = _combine(x, expert_out, weights, g[3:4], gate2, rows_per_mod=rows_per_mod, tm=512)
    return x, ctx_out


def kernel(x_prompt, x_sample, cache_attn_k, cache_attn_v, state_gdn, state_mlstm_C, state_mlstm_n, state_mlstm_m, c, c_ctx, ada_w, ada_b, norm_g, w_in, w_bgate, b_bgate, w_branch, w_out, attn_lambda, attn_norm, pool_w, pool_b, pool_scale, gdn_conv, gdn_A_log, gdn_dt_bias, gdn_norm, mlstm_bias_i, mlstm_bias_f, mlstm_norm, router_w, router_b, moe_w_gu, moe_b_gu, moe_w_dn, moe_b_dn):
    Bp, Lp, _ = x_prompt.shape
    Bs, Ls, _ = x_sample.shape
    y_p = x_prompt.reshape(Bp * Lp, D_MODEL)
    y_s = x_sample.reshape(Bs * Ls, D_MODEL)
    cos, sin = _rope_tables(Ls)
    n0 = state_mlstm_n.transpose(1, 0, 3, 2, 4)[:, :, :, :, None, :]
    m0 = jnp.broadcast_to(state_mlstm_m.transpose(1, 0, 3, 2)[..., None, None], n0.shape)
    ctx_states = []
    for l in range(DEPTH):
        lp = {
            'ada_w': ada_w[l], 'ada_b': ada_b[l], 'norm_g': norm_g[l],
            'w_in': _arrange_w_in(w_in[l].astype(BF16)), 'w_bgate': w_bgate[l].astype(BF16),
            'b_bgate': b_bgate[l], 'w_branch': w_branch[l].astype(BF16), 'w_out': w_out[l].astype(BF16),
            'attn_lambda': attn_lambda[l], 'attn_norm': attn_norm[l],
            'pool_w': pool_w[l], 'pool_b': pool_b[l], 'pool_scale': pool_scale[l],
            'gdn_conv': gdn_conv[l], 'gdn_A_log': gdn_A_log[l], 'gdn_dt_bias': gdn_dt_bias[l],
            'gdn_norm': gdn_norm[l], 'mlstm_bias_i': mlstm_bias_i[l], 'mlstm_bias_f': mlstm_bias_f[l],
            'mlstm_norm': mlstm_norm[l],
            'router_w': jnp.pad(router_w[l], ((0, 0), (0, LANE - N_EXPERTS))),
            'router_b': jnp.pad(router_b[l], (0, LANE - N_EXPERTS))[None],
            'moe_w_gu': moe_w_gu[l].astype(BF16), 'moe_b_gu': moe_b_gu[l][:, None],
            'moe_w_dn': moe_w_dn[l].astype(BF16), 'moe_b_dn': moe_b_dn[l][:, None],
        }
        y_p, st = _layer(y_p, c_ctx[None], l, lp, B=Bp, L=Lp, cache=None)
        ctx_states.append(st)
        cache = {
            'attn': {'k': cache_attn_k, 'v': cache_attn_v, 'layer': l, 'cos': cos, 'sin': sin},
            'gdn': (state_gdn, l),
            'mlstm': (state_mlstm_C, l, n0[l], m0[l]),
        }
        y_s, _ = _layer(y_s, c, l, lp, B=Bs, L=Ls, cache=cache)
    outs = [jnp.stack([s[i] for s in ctx_states], axis=1) for i in range(6)]
    return (y_p.reshape(Bp, Lp, D_MODEL), y_s.reshape(Bs, Ls, D_MODEL), *outs)
```

```python
import functools
import math

import jax
import jax.numpy as jnp
from jax import lax
from jax.experimental import pallas as pl
from jax.experimental.pallas import tpu as pltpu

F32 = jnp.float32
BF16 = jnp.bfloat16

D_MODEL = 1024
DEPTH = 2
GRID_W = 64
EPS = 1e-6
LANE = 128
HEADS = 4
HEAD_DIM = 128
A_DIM = 64
ROPE_BASE = 10000.0
POOL_WINDOWS = (2, 4, 8, 16)
CHUNK = 64
N_BRANCH = 4
BRANCH_WIDTH = 512
N_EXPERTS = 32
TOP_K = 4
D_EXPERT = 1024
SWIGLU_LIMIT = 7.0
SWIGLU_ALPHA = 1.702
MOE_ROWS = 256
VMEM_LIMIT = 56 * 1024 * 1024

COL_AQ, COL_AK, COL_AV, COL_POOL, COL_CQ, COL_CK, COL_CV, COL_CG, COL_MQ, COL_MK, COL_MV, COL_MG, COL_SM = range(13)
N_PROJ = 13 * BRANCH_WIDTH
SM_BETA, SM_DEC, SM_IG, SM_FG = 0, 2, 4, 6


def _params(*sem):
    return pltpu.CompilerParams(dimension_semantics=sem, vmem_limit_bytes=VMEM_LIMIT)


def _dot(a, b):
    return jnp.dot(a.astype(BF16), b.astype(BF16), preferred_element_type=F32)


def _dot_nt(a, b):
    return lax.dot_general(a.astype(BF16), b.astype(BF16), (((1,), (1,)), ((), ())),
                           preferred_element_type=F32)


def _dot_tn(a, b):
    return lax.dot_general(a.astype(BF16), b.astype(BF16), (((0,), (0,)), ((), ())),
                           preferred_element_type=F32)


def _split(a):
    hi = a.astype(BF16)
    lo = (a - hi.astype(F32)).astype(BF16)
    return hi, lo


def _dot3(a, b):
    ah, al = _split(a)
    bh, bl = _split(b)
    d = lambda x, y: jnp.dot(x, y, preferred_element_type=F32)
    return d(ah, bh) + (d(ah, bl) + d(al, bh))


def _dot_mask(mask, b):
    m = jnp.where(mask, 1.0, 0.0).astype(BF16)
    b0 = b.astype(BF16)
    r1 = b - b0.astype(F32)
    b1 = r1.astype(BF16)
    b2 = (r1 - b1.astype(F32)).astype(BF16)
    d = lambda y: jnp.dot(m, y, preferred_element_type=F32)
    return d(b0) + (d(b1) + d(b2))


ROW_TILE = 8
assert ROW_TILE * LANE == D_MODEL


def _store_row_tiles(ref, x):
    rows = x.shape[0]
    for c in range(ROW_TILE):
        ref[pl.ds(c, rows, stride=ROW_TILE), :] = x[:, c * LANE:(c + 1) * LANE]


def _load_row_tiles(ref, rows, dtype=F32):
    return jnp.concatenate([ref[pl.ds(c, rows, stride=ROW_TILE), :].astype(dtype) for c in range(ROW_TILE)],
                           axis=1)


def _rms(x):
    return x * lax.rsqrt(jnp.mean(x * x, axis=-1, keepdims=True) + EPS)


def _sigmoid(x):
    return 1.0 / (1.0 + jnp.exp(-x))


def _softplus(x):
    return jnp.maximum(x, 0.0) + jnp.log(1.0 + jnp.exp(-jnp.abs(x)))


def _proj_kernel(x_ref, g_ref, sh_ref, sc_ref, w_ref, b_ref, o_ref, h_scr, *, sigmoid):
    @pl.when(pl.program_id(1) == 0)
    def _():
        y = _rms(x_ref[...]) * g_ref[...]
        h_scr[...] = (y * (1.0 + sc_ref[0]) + sh_ref[0]).astype(BF16)

    acc = jnp.dot(h_scr[...], w_ref[...], preferred_element_type=F32) + b_ref[...]
    o_ref[...] = _sigmoid(acc) if sigmoid else acc


def _norm_mod_matmul(x, g, shift, scale, w, bias, *, rows_per_mod, sigmoid, tm, tn):
    T, Dm = x.shape
    N = w.shape[1]
    tm = min(tm, rows_per_mod)
    mod_idx = lambda i, j: ((i * tm) // rows_per_mod, 0, 0)
    return pl.pallas_call(
        functools.partial(_proj_kernel, sigmoid=sigmoid),
        out_shape=jax.ShapeDtypeStruct((T, N), F32),
        grid=(T // tm, N // tn),
        in_specs=[
            pl.BlockSpec((tm, Dm), lambda i, j: (i, 0)),
            pl.BlockSpec((1, Dm), lambda i, j: (0, 0)),
            pl.BlockSpec((1, 1, Dm), mod_idx),
            pl.BlockSpec((1, 1, Dm), mod_idx),
            pl.BlockSpec((Dm, tn), lambda i, j: (0, j)),
            pl.BlockSpec((1, tn), lambda i, j: (0, j)),
        ],
        out_specs=pl.BlockSpec((tm, tn), lambda i, j: (i, j)),
        scratch_shapes=[pltpu.VMEM((tm, Dm), BF16)],
        compiler_params=_params("parallel", "arbitrary"),
        name="proj",
    )(x, g, shift, scale, w, bias)


def _rope(x, cos, sin):
    lane = lax.broadcasted_iota(jnp.int32, x.shape, 1)
    first = (lane % 32) < 16
    partner = jnp.where(first, pltpu.roll(x, LANE - 16, 1), pltpu.roll(x, 16, 1))
    return x * cos + partner * sin


def _attn_kernel(*refs, has_cache, n_ctx, out_scale):
    if has_cache:
        (lam_ref, q_ref, k_ref, v_ref, ck_ref, cv_ref, cosq_ref, sinq_ref, cosk_ref, sink_ref, nrm_ref,
         o_ref, kall, vall) = refs
    else:
        lam_ref, q_ref, k_ref, v_ref, nrm_ref, o_ref, ko_ref, vo_ref, kall, vall = refs

    @pl.when(pl.program_id(2) == 0)
    def _():
        k = k_ref[...]
        v = v_ref[...]
        if has_cache:
            kall[:, pl.ds(0, n_ctx)] = ck_ref[...].T.astype(BF16)
            vall[pl.ds(0, n_ctx), :] = cv_ref[...].astype(BF16)
            k = _rope(k, cosk_ref[...], sink_ref[...])
        else:
            ko_ref[...] = k
            vo_ref[...] = v
        kall[:, pl.ds(n_ctx, k.shape[0])] = k.T.astype(BF16)
        vall[pl.ds(n_ctx, k.shape[0]), :] = v.astype(BF16)

    q = q_ref[...]
    if has_cache:
        q = _rope(q, cosq_ref[...], sinq_ref[...])
    q = q * (A_DIM ** -0.5 * math.log2(math.e))
    lane = lax.broadcasted_iota(jnp.int32, q.shape, 1)
    q1 = jnp.where(lane < A_DIM, q, 0.0)
    q2 = jnp.where(lane >= A_DIM, q, 0.0)
    keys = kall[...]
    lam = lam_ref[0, 0]

    def probs(qm):
        s = _dot(qm, keys)
        p = jnp.exp2(s - jnp.max(s, axis=-1, keepdims=True))
        return p, jnp.sum(p, axis=-1, keepdims=True)

    p1, l1 = probs(q1)
    p2, l2 = probs(q2)
    a = p1 * (1.0 / l1) - p2 * (lam / l2)
    o = jnp.dot(a.astype(BF16), vall[...], preferred_element_type=F32)
    o_ref[...] = _rms(o) * nrm_ref[...] * out_scale


def _attention(P, lam, norm, *, B, L, tq, lam_init, cache=None):
    T = B * L
    nq = L // tq
    has_cache = cache is not None
    n_ctx = cache["k"].shape[3] if has_cache else 0
    colq, colk, colv = COL_AQ * HEADS, COL_AK * HEADS, COL_AV * HEADS
    in_specs = [
        pl.BlockSpec(memory_space=pltpu.SMEM),
        pl.BlockSpec((tq, LANE), lambda b, h, i: (b * nq + i, colq + h)),
        pl.BlockSpec((L, LANE), lambda b, h, i: (b, colk + h)),
        pl.BlockSpec((L, LANE), lambda b, h, i: (b, colv + h)),
    ]
    args = [lam, P, P, P]
    if has_cache:
        l = cache["layer"]
        cspec = pl.BlockSpec((None, None, None, n_ctx, LANE), lambda b, h, i: (b, l, h, 0, 0))
        in_specs += [cspec, cspec,
                     pl.BlockSpec((tq, LANE), lambda b, h, i: (i, 0)),
                     pl.BlockSpec((tq, LANE), lambda b, h, i: (i, 0)),
                     pl.BlockSpec((L, LANE), lambda b, h, i: (0, 0)),
                     pl.BlockSpec((L, LANE), lambda b, h, i: (0, 0))]
        args += [cache["k"], cache["v"], cache["cos"], cache["sin"], cache["cos"], cache["sin"]]
    in_specs.append(pl.BlockSpec((1, LANE), lambda b, h, i: (0, 0)))
    args.append(norm)
    out_shape = [jax.ShapeDtypeStruct((T, HEADS * LANE), F32)]
    out_specs = [pl.BlockSpec((tq, LANE), lambda b, h, i: (b * nq + i, h))]
    if not has_cache:
        kv_shape = jax.ShapeDtypeStruct((B, HEADS, L, LANE), F32)
        kv_spec = pl.BlockSpec((None, None, L, LANE), lambda b, h, i: (b, h, 0, 0))
        out_shape += [kv_shape, kv_shape]
        out_specs += [kv_spec, kv_spec]
    return pl.pallas_call(
        functools.partial(_attn_kernel, has_cache=has_cache, n_ctx=n_ctx, out_scale=1.0 - lam_init),
        out_shape=out_shape,
        grid=(B, HEADS, nq),
        in_specs=in_specs,
        out_specs=out_specs,
        scratch_shapes=[pltpu.VMEM((LANE, n_ctx + L), BF16), pltpu.VMEM((n_ctx + L, LANE), BF16)],
        compiler_params=_params("parallel", "parallel", "arbitrary"),
        name="diff_attention",
    )(*args)


POOL_PAD = 16


def _pool_kernel(x_ref, w_ref, b_ref, s_ref, o_ref, pad, *, L):
    zeros = jnp.zeros((POOL_PAD, LANE), F32)
    pad[pl.ds(0, POOL_PAD), :] = zeros
    pad[pl.ds(POOL_PAD + L, POOL_PAD), :] = zeros
    x = x_ref[...]
    pad[pl.ds(POOL_PAD, L), :] = x
    t = lax.broadcasted_iota(jnp.int32, (L, LANE), 0)
    g = pl.program_id(1)
    for gi, win in enumerate(POOL_WINDOWS):
        @pl.when(g == gi)
        def _(win=win):
            half = win // 2
            acc = pad[pl.ds(POOL_PAD - half, L), :]
            for k in range(1 - half, half):
                acc = acc + pad[pl.ds(POOL_PAD + k, L), :]
            cnt = (jnp.minimum(t + half, L) - jnp.maximum(t - half, 0)).astype(F32)
            pooled = acc / cnt - x
            o_ref[...] = (_dot(pooled, w_ref[...]) + b_ref[...]) * s_ref[...]


def _pool(P, w, b, scale, *, B, L):
    T = B * L
    G = len(POOL_WINDOWS)
    return pl.pallas_call(
        functools.partial(_pool_kernel, L=L),
        out_shape=jax.ShapeDtypeStruct((T, G * LANE), F32),
        grid=(B, G),
        in_specs=[
            pl.BlockSpec((L, LANE), lambda bi, g: (bi, COL_POOL * HEADS + g)),
            pl.BlockSpec((None, LANE, LANE), lambda bi, g: (g, 0, 0)),
            pl.BlockSpec((None, 1, LANE), lambda bi, g: (g, 0, 0)),
            pl.BlockSpec((1, LANE), lambda bi, g: (0, g)),
        ],
        out_specs=pl.BlockSpec((L, LANE), lambda bi, g: (bi, g)),
        scratch_shapes=[pltpu.VMEM((L + 2 * POOL_PAD, LANE), F32)],
        compiler_params=_params("parallel", "parallel"),
        name="pool_mixer",
    )(P, w, b, scale)


GROUP_CHUNKS = 2
GROUP = GROUP_CHUNKS * CHUNK
MAX_GROUPS_PER_STEP = 4


def _groups_per_step(L):
    return min(MAX_GROUPS_PER_STEP, L // GROUP)


def _group_masks(backward):
    row = lax.broadcasted_iota(jnp.int32, (GROUP, GROUP), 0)
    col = lax.broadcasted_iota(jnp.int32, (GROUP, GROUP), 1)
    same = (row // CHUNK) == (col // CHUNK)
    lower, upper = row >= col, row <= col
    if backward:
        lower, upper = upper, lower
    return dict(same=same, incl=same & lower, strict=same & lower & (row != col), incl_t=same & upper,
                eye=row == col)


INV_BASE = 8


def _block_masks():
    row = lax.broadcasted_iota(jnp.int32, (GROUP, GROUP), 0)
    col = lax.broadcasted_iota(jnp.int32, (GROUP, GROUP), 1)
    sizes = [INV_BASE << i for i in range(int(math.log2(CHUNK // INV_BASE)) + 1)]
    same = [(row // s) == (col // s) for s in sizes]
    return [same[0]] + [cur & jnp.logical_not(prev) for prev, cur in zip(same[:-1], same[1:])]


def _unit_triangular_inverses(ns, eye, blocks):
    base = [jnp.where(blocks[0], n, 0.0) for n in ns]
    invs = [eye - b for b in base]
    pws = [_dot(b, b) for b in base]
    rounds = int(math.log2(INV_BASE)) - 1
    for s in range(rounds):
        invs = [t + _dot(t, p) for t, p in zip(invs, pws)]
        if s + 1 < rounds:
            pws = [_dot(p, p) for p in pws]
    for join in blocks[1:]:
        offs = [jnp.where(join, n, 0.0) for n in ns]
        xs = [_dot(o, t) for o, t in zip(offs, invs)]
        invs = [t - _dot(t, x) for t, x in zip(invs, xs)]
    return invs


def _dot_mask2(mask, b):
    m = jnp.where(mask, 1.0, 0.0).astype(BF16)
    b0, b1 = _split(b)
    return jnp.dot(m, b0, preferred_element_type=F32) + jnp.dot(m, b1, preferred_element_type=F32)


def _per_chunk(x, reduce):
    parts = [reduce(x[c * CHUNK:(c + 1) * CHUNK], axis=0, keepdims=True) for c in range(GROUP_CHUNKS)]
    col = jnp.concatenate([jnp.broadcast_to(p, (CHUNK, 1)) for p in parts], axis=0)
    return parts, col


def _scan_order(step, n_steps, groups):
    base = (step * groups * GROUP, (n_steps - 1 - step) * groups * GROUP)
    fwd = [(gi, c) for gi in range(groups) for c in range(GROUP_CHUNKS)]
    return base, (fwd, fwd[::-1])


CONV_PAD = 8


def _gdn_kernel(*refs, L, has_state):
    alog_ref, dtb_ref, q_ref, k_ref, v_ref, gate_ref, sm_ref, cwq_ref, cwk_ref, cwv_ref, nrm_ref = refs[:11]
    refs = refs[11:]
    if has_state:
        s0_ref, o_ref, pad, qs, ks, vs, o_f, o_b, state = refs
    else:
        o_ref, sout_ref, pad, qs, ks, vs, o_f, o_b, state = refs
    h = pl.program_id(1)
    n_chunks = L // CHUNK

    zeros = jnp.zeros((CONV_PAD, LANE), F32)
    pad[pl.ds(0, CONV_PAD), :] = zeros
    pad[pl.ds(CONV_PAD + L, CONV_PAD), :] = zeros

    def conv_silu(x_ref, w_ref):
        pad[pl.ds(CONV_PAD, L), :] = x_ref[...]
        w = w_ref[...]
        y = (pad[pl.ds(CONV_PAD - 1, L), :] * w[0:1] + pad[pl.ds(CONV_PAD, L), :] * w[1:2]
             + pad[pl.ds(CONV_PAD + 1, L), :] * w[2:3])
        return y * _sigmoid(y)

    def l2n(x):
        return x * lax.rsqrt(jnp.sum(x * x, axis=-1, keepdims=True) + EPS)

    qs[...] = l2n(conv_silu(q_ref, cwq_ref)) * (HEAD_DIM ** -0.5)
    ks[...] = l2n(conv_silu(k_ref, cwk_ref))
    vs[...] = conv_silu(v_ref, cwv_ref)
    if has_state:
        state[...] = s0_ref[...]
    else:
        state[...] = jnp.zeros_like(state)

    groups = _groups_per_step(L)
    n_steps = L // (groups * GROUP)
    o_scr = (o_f, o_b)
    masks = (_group_masks(False), _group_masks(True))
    eye = jnp.where(masks[0]['eye'], 1.0, 0.0)
    blocks = _block_masks()
    items = [(d, gi) for gi in range(groups) for d in (0, 1)]
    chunks = [slice(c * CHUNK, (c + 1) * CHUNK) for c in range(GROUP_CHUNKS)]

    def body(step, carry):
        base, order = _scan_order(step, n_steps, groups)
        it = {}
        for key in items:
            d, gi = key
            rows = pl.ds(pl.multiple_of(base[d] + gi * GROUP, GROUP), GROUP)
            q, k, v, sm = qs[rows, :], ks[rows, :], vs[rows, :], sm_ref[rows, :]
            beta = _sigmoid(sm[:, SM_BETA + d:SM_BETA + d + 1])
            dec = sm[:, SM_DEC + d:SM_DEC + d + 1]
            g = -jnp.exp(alog_ref[d, h]) * _softplus(dec + dtb_ref[d, h])
            it[key] = dict(q=q, k=k, kb=k * beta, vb=v * beta, g=g, gb=jnp.broadcast_to(g, (GROUP, GROUP)))
        for (d, gi), x in it.items():
            m = masks[d]
            x['cum_i'] = _dot_mask2(m['incl'], x['gb'])
            x['cum_j'] = _dot_mask2(m['same'], jnp.where(m['incl_t'], x['gb'], 0.0))
            x['kk'] = _dot_nt(x['kb'], x['k'])
            x['qk'] = _dot_nt(x['q'], x['k'])
        for (d, gi), x in it.items():
            m = masks[d]
            decay = jnp.where(m['incl'], jnp.exp(jnp.where(m['incl'], x['cum_i'] - x['cum_j'], 0.0)), 0.0)
            x['gc'] = x['cum_i'][:, 0:1]
            x['tot'], x['tot_col'] = _per_chunk(x['g'], jnp.sum)
            x['qk'] = x['qk'] * decay
            x['n'] = jnp.where(m['strict'], x['kk'] * decay, 0.0)
        invs = _unit_triangular_inverses([x['n'] for x in it.values()], eye, blocks)
        for x, inv in zip(it.values(), invs):
            x['inv'] = inv
            x['egc'] = jnp.exp(x['gc'])
            x['sol'] = _dot3(x['inv'], jnp.concatenate([x['vb'], x['kb'] * x['egc']], axis=-1))
        for x in it.values():
            u, w = x['sol'][:, :HEAD_DIM], x['sol'][:, HEAD_DIM:]
            x['qp'] = x['q'] * x['egc'] - _dot(x['qk'], w)
            x['op'] = _dot(x['qk'], u)
            kd = x['k'] * jnp.exp(x['tot_col'] - x['gc'])
            x['ab'] = [_dot_tn(kd[cs], x['sol'][cs]) for cs in chunks]
            x['gl'] = [jnp.exp(t) for t in x['tot']]
        S = [state[0], state[1]]
        for stp in range(len(order[0])):
            for d in (0, 1):
                gi, c = order[d][stp]
                x = it[(d, gi)]
                ab = x['ab'][c]
                r = _dot(jnp.concatenate([ab[:, HEAD_DIM:], x['qp'][chunks[c]]], axis=0), S[d])
                rows = pl.ds(pl.multiple_of(base[d] + gi * GROUP + c * CHUNK, CHUNK), CHUNK)
                o_scr[d][rows, :] = x['op'][chunks[c]] + r[HEAD_DIM:]
                S[d] = x['gl'][c] * S[d] + (ab[:, :HEAD_DIM] - r[:HEAD_DIM])
        state[0] = S[0]
        state[1] = S[1]
        return carry

    lax.fori_loop(0, n_steps, body, 0)
    gate = gate_ref[...]
    o_ref[...] = _rms(o_f[...] + o_b[...]) * nrm_ref[...] * (gate * _sigmoid(gate))
    if not has_state:
        sout_ref[...] = state[...]


def _col_spec(L, col):
    return pl.BlockSpec((L, LANE), lambda b, h: (b, col * HEADS + h))


def _gdn(P, a_log, dt_bias, conv_w, norm, *, B, L, state=None):
    T = B * L
    has_state = state is not None
    smem = pl.BlockSpec(memory_space=pltpu.SMEM)
    in_specs = [smem, smem, _col_spec(L, COL_CQ), _col_spec(L, COL_CK), _col_spec(L, COL_CV),
                _col_spec(L, COL_CG), _col_spec(L, COL_SM)]
    in_specs += [pl.BlockSpec((3, LANE), lambda b, h, j=j: (0, j * HEADS + h)) for j in range(3)]
    in_specs.append(pl.BlockSpec((1, LANE), lambda b, h: (0, 0)))
    args = [a_log, dt_bias, P, P, P, P, P, conv_w, conv_w, conv_w, norm]
    out_shape = [jax.ShapeDtypeStruct((T, HEADS * LANE), F32)]
    out_specs = [pl.BlockSpec((L, LANE), lambda b, h: (b, h))]
    if has_state:
        arr, l = state
        in_specs.append(pl.BlockSpec((None, None, 2, None, HEAD_DIM, HEAD_DIM), lambda b, h: (b, l, 0, h, 0, 0)))
        args.append(arr)
    else:
        out_shape.append(jax.ShapeDtypeStruct((B, 2, HEADS, HEAD_DIM, HEAD_DIM), F32))
        out_specs.append(pl.BlockSpec((None, 2, None, HEAD_DIM, HEAD_DIM), lambda b, h: (b, 0, h, 0, 0)))
    seq = pltpu.VMEM((L, LANE), F32)
    return pl.pallas_call(
        functools.partial(_gdn_kernel, L=L, has_state=has_state),
        out_shape=out_shape,
        grid=(B, HEADS),
        in_specs=in_specs,
        out_specs=out_specs,
        scratch_shapes=[pltpu.VMEM((L + 2 * CONV_PAD, LANE), F32), seq, seq, seq, seq, seq,
                        pltpu.VMEM((2, HEAD_DIM, HEAD_DIM), F32)],
        compiler_params=_params("parallel", "parallel"),
        name="gated_deltanet",
    )(*args)


def _mlstm_kernel(*refs, L, has_state):
    bi_ref, bf_ref, q_ref, k_ref, v_ref, gate_ref, sm_ref, nrm_ref = refs[:8]
    refs = refs[8:]
    if has_state:
        c0_ref, n0_ref, m0_ref, o_ref, h_f, h_b, c_st, n_st, m_st = refs
    else:
        o_ref, cout_ref, nout_ref, mout_ref, h_f, h_b, c_st, n_st, m_st = refs
    h = pl.program_id(1)
    n_chunks = L // CHUNK

    if has_state:
        c_st[...] = c0_ref[...]
        n_st[...] = n0_ref[...]
        m_st[...] = m0_ref[...]
    else:
        c_st[...] = jnp.zeros_like(c_st)
        n_st[...] = jnp.zeros_like(n_st)
        m_st[...] = jnp.zeros_like(m_st)

    groups = _groups_per_step(L)
    n_steps = L // (groups * GROUP)
    h_scr = (h_f, h_b)
    masks = (_group_masks(False), _group_masks(True))
    items = [(d, gi) for gi in range(groups) for d in (0, 1)]
    chunks = [slice(c * CHUNK, (c + 1) * CHUNK) for c in range(GROUP_CHUNKS)]

    def body(step, carry):
        base, order = _scan_order(step, n_steps, groups)
        it = {}
        for key in items:
            d, gi = key
            rows = pl.ds(pl.multiple_of(base[d] + gi * GROUP, GROUP), GROUP)
            q, v, sm = q_ref[rows, :], v_ref[rows, :], sm_ref[rows, :]
            k = k_ref[rows, :] * (HEAD_DIM ** -0.5)
            ig = sm[:, SM_IG + d:SM_IG + d + 1] + bi_ref[d, h]
            fg = sm[:, SM_FG + d:SM_FG + d + 1] + bf_ref[d, h]
            lf = -_softplus(-fg)
            it[key] = dict(q=q, k=k, v=v, ig=ig, lf=lf, lfb=jnp.broadcast_to(lf, (GROUP, GROUP)),
                           igb=jnp.broadcast_to(ig, (GROUP, GROUP)))
        for (d, gi), x in it.items():
            m = masks[d]
            x['cum_i'] = _dot_mask2(m['incl'], x['lfb'])
            x['cum_j'] = _dot_mask2(m['same'], jnp.where(m['incl_t'], x['lfb'], 0.0)
                                    - jnp.where(m['eye'], x['igb'], 0.0))
            x['qk'] = _dot_nt(x['q'], x['k'])
        for (d, gi), x in it.items():
            m = masks[d]
            dm = x['cum_i'] - x['cum_j']
            x['b'] = x['cum_i'][:, 0:1]
            x['tot'], tot_col = _per_chunk(x['lf'], jnp.sum)
            x['m_intra'] = jnp.max(jnp.where(m['incl'], dm, -jnp.inf), axis=-1, keepdims=True)
            e = tot_col - x['b'] + x['ig']
            x['m_end'], m_end_col = _per_chunk(e, jnp.max)
            p = jnp.where(m['incl'], jnp.exp(jnp.where(m['incl'], dm - x['m_intra'], 0.0)), 0.0) * x['qk']
            x['p_sum'] = jnp.sum(p, axis=-1, keepdims=True)
            x['p'] = p
            x['kw'] = x['k'] * jnp.exp(e - m_end_col)
        for x in it.values():
            x['pv'] = _dot(x['p'], x['v'])
            x['kv'] = [_dot_tn(x['kw'][cs], x['v'][cs]) for cs in chunks]
            x['k_sum'] = [jnp.sum(x['kw'][cs], axis=0, keepdims=True) for cs in chunks]
        outs = []
        for d in (0, 1):
            C, n, m = c_st[d], n_st[d], m_st[d][:, 0:1]
            for gi, c in order[d]:
                x = it[(d, gi)]
                cs = chunks[c]
                m_t = jnp.maximum(x['b'][cs] + m, x['m_intra'][cs])
                w_inter = jnp.exp(x['b'][cs] + m - m_t)
                local = jnp.exp(x['m_intra'][cs] - m_t)
                den = (w_inter * jnp.sum(x['q'][cs] * n, axis=-1, keepdims=True) + local * x['p_sum'][cs])
                scale = 1.0 / jnp.maximum(jnp.abs(den), jnp.exp(-m_t))
                outs.append((d, gi, c, C, w_inter * scale, local * scale))
                m_new = jnp.maximum(x['tot'][c] + m, x['m_end'][c])
                carry_decay = jnp.exp(x['tot'][c] + m - m_new)
                local_new = jnp.exp(x['m_end'][c] - m_new)
                C = carry_decay * C + local_new * x['kv'][c]
                n = carry_decay * n + local_new * x['k_sum'][c]
                m = m_new
            c_st[d] = C
            n_st[d] = n
            m_st[d] = jnp.broadcast_to(m, (1, LANE))
        for d, gi, c, C, w_scale, p_scale in outs:
            x = it[(d, gi)]
            cs = chunks[c]
            rows = pl.ds(pl.multiple_of(base[d] + gi * GROUP + c * CHUNK, CHUNK), CHUNK)
            h_scr[d][rows, :] = w_scale * _dot(x['q'][cs], C) + p_scale * x['pv'][cs]
        return carry

    lax.fori_loop(0, n_steps, body, 0)
    o_ref[...] = _rms(h_f[...] + h_b[...]) * nrm_ref[...] * _sigmoid(gate_ref[...])
    if not has_state:
        cout_ref[...] = c_st[...]
        nout_ref[...] = n_st[...]
        mout_ref[...] = m_st[...]


def _mlstm(P, bias_i, bias_f, norm, *, B, L, state=None):
    T = B * L
    has_state = state is not None
    smem = pl.BlockSpec(memory_space=pltpu.SMEM)
    in_specs = [smem, smem, _col_spec(L, COL_MQ), _col_spec(L, COL_MK), _col_spec(L, COL_MV),
                _col_spec(L, COL_MG), _col_spec(L, COL_SM), pl.BlockSpec((1, LANE), lambda b, h: (0, 0))]
    args = [bias_i, bias_f, P, P, P, P, P, norm]
    out_shape = [jax.ShapeDtypeStruct((T, HEADS * LANE), F32)]
    out_specs = [pl.BlockSpec((L, LANE), lambda b, h: (b, h))]
    vec_spec = pl.BlockSpec((None, None, 2, 1, LANE), lambda b, h: (b, h, 0, 0, 0))
    if has_state:
        c_arr, l, n_arr, m_arr = state
        in_specs += [pl.BlockSpec((None, None, 2, None, HEAD_DIM, HEAD_DIM), lambda b, h: (b, l, 0, h, 0, 0)),
                     vec_spec, vec_spec]
        args += [c_arr, n_arr, m_arr]
    else:
        out_shape += [jax.ShapeDtypeStruct((B, 2, HEADS, HEAD_DIM, HEAD_DIM), F32),
                      jax.ShapeDtypeStruct((B, HEADS, 2, 1, LANE), F32),
                      jax.ShapeDtypeStruct((B, HEADS, 2, 1, LANE), F32)]
        out_specs += [pl.BlockSpec((None, 2, None, HEAD_DIM, HEAD_DIM), lambda b, h: (b, 0, h, 0, 0)),
                      vec_spec, vec_spec]
    seq = pltpu.VMEM((L, LANE), F32)
    return pl.pallas_call(
        functools.partial(_mlstm_kernel, L=L, has_state=has_state),
        out_shape=out_shape,
        grid=(B, HEADS),
        in_specs=in_specs,
        out_specs=out_specs,
        scratch_shapes=[seq, seq, pltpu.VMEM((2, HEAD_DIM, HEAD_DIM), F32), pltpu.VMEM((2, 1, LANE), F32),
                        pltpu.VMEM((2, 1, LANE), F32)],
        compiler_params=_params("parallel", "parallel"),
        name="mlstm",
    )(*args)


def _merge_kernel(x_ref, gates_ref, a_ref, b_ref, c_ref, d_ref, wb_ref, wo_ref, g1_ref, gate1_ref, g2_ref,
                  sh2_ref, sc2_ref, rw_ref, rb_ref, xo_ref, h2_ref, lg_ref):
    merged = None
    for i, br in enumerate((a_ref, b_ref, c_ref, d_ref)):
        term = gates_ref[:, i * D_MODEL:(i + 1) * D_MODEL] * _dot(br[...], wb_ref[i])
        merged = term if merged is None else merged + term
    t = _rms(_dot(merged, wo_ref[...])) * g1_ref[...]
    x = x_ref[...] + gate1_ref[0] * t
    xo_ref[...] = x
    h2 = _rms(x) * g2_ref[...] * (1.0 + sc2_ref[0]) + sh2_ref[0]
    _store_row_tiles(h2_ref, h2)
    lg_ref[...] = _dot3(h2, rw_ref[...]) + rb_ref[...]


def _merge(x, gates, branches, wb, wo, g1, gate1, g2, shift2, scale2, rw, rb, *, rows_per_mod, tm):
    T = x.shape[0]
    tm = min(tm, rows_per_mod)
    row = lambda n: pl.BlockSpec((tm, n), lambda i: (i, 0))
    const = lambda *shape: pl.BlockSpec(shape, lambda i: (0,) * len(shape))
    mod = pl.BlockSpec((1, 1, D_MODEL), lambda i: ((i * tm) // rows_per_mod, 0, 0))
    return pl.pallas_call(
        _merge_kernel,
        out_shape=[jax.ShapeDtypeStruct((T, D_MODEL), F32), jax.ShapeDtypeStruct((T * ROW_TILE, LANE), F32),
                   jax.ShapeDtypeStruct((T, LANE), F32)],
        grid=(T // tm,),
        in_specs=[row(D_MODEL), row(N_BRANCH * D_MODEL)] + [row(BRANCH_WIDTH)] * 4
                 + [const(N_BRANCH, BRANCH_WIDTH, D_MODEL), const(D_MODEL, D_MODEL), const(1, D_MODEL), mod,
                    const(1, D_MODEL), mod, mod, const(D_MODEL, LANE), const(1, LANE)],
        out_specs=[row(D_MODEL), pl.BlockSpec((tm * ROW_TILE, LANE), lambda i: (i, 0)), row(LANE)],
        compiler_params=_params("parallel"),
        name="merge",
    )(x, gates, *branches, wb, wo, g1, gate1, g2, shift2, scale2, rw, rb)


MOE_CHUNKS = 4
MOE_DUMP_ROWS = 2 * MOE_ROWS


def _moe_kernel(be_ref, nu_ref, idx_hbm, h2_hbm, wgu_f32, bgu_ref, wdn_f32, bdn_ref, out_hbm,
                idx_smem, x0, x1, y0, y1, wgu_ref, wdn_ref, sem_idx, sem_in, sem_out, sem_fill, sem_pace):
    i = pl.program_id(0)
    n_used = nu_ref[0]
    n_blocks = pl.num_programs(0)
    xbuf, ybuf = (x0, x1), (y0, y1)

    @pl.when((i < n_used) & ((i == 0) | (be_ref[i] != be_ref[jnp.maximum(i - 1, 0)])))
    def _():
        wgu_ref[...] = wgu_f32[...].astype(BF16)
        wdn_ref[...] = wdn_f32[...].astype(BF16)
    n_tokens_rows = out_hbm.shape[0] - MOE_DUMP_ROWS * ROW_TILE
    q_prev, q_cur, q_next = (i + 2) % 3, i % 3, (i + 1) % 3

    def idx_copy(block, s):
        return pltpu.make_async_copy(idx_hbm.at[block], idx_smem.at[s], sem_idx.at[s])

    def tile(ref, first_row):
        if not isinstance(first_row, int):
            first_row = pl.multiple_of(first_row, ROW_TILE)
        return ref.at[pl.ds(first_row, ROW_TILE), :]

    def row_in(s, r, tok_row):
        return pltpu.make_async_copy(tile(h2_hbm, tok_row), tile(xbuf[s], r * ROW_TILE), sem_in.at[s])

    def row_out(s, r, dst_row):
        return pltpu.make_async_copy(tile(ybuf[s], r * ROW_TILE), tile(out_hbm, dst_row), sem_out.at[s])

    def fill(half):
        rows = MOE_ROWS * ROW_TILE
        return pltpu.make_async_copy(y1, out_hbm.at[pl.ds(n_tokens_rows + half * rows, rows), :], sem_fill)

    def wait_rows(copy, s):
        for _ in range(MOE_ROWS):
            copy(s, 0, 0).wait()

    @pl.when(i == 0)
    def _():
        idx_copy(0, 0).start()
        idx_copy(n_blocks, 2).start()
        y1[...] = jnp.zeros_like(y1)
        fill(0).start()
        fill(1).start()
        idx_copy(0, 0).wait()
        idx_copy(n_blocks, 2).wait()

        def body(r, carry):
            row_in(0, r, idx_smem[0, 0, r]).start()
            return carry
        lax.fori_loop(0, MOE_ROWS, body, 0, unroll=8)
        fill(0).wait()
        fill(1).wait()

    def step(slot):
        other = 1 - slot
        nxt = jnp.minimum(i + 1, n_blocks - 1)
        idx_copy(nxt, q_next).start()
        wait_rows(row_in, slot)
        idx_copy(nxt, q_next).wait()

        @pl.when(i >= 1)
        def _():
            wait_rows(row_out, slot)

        x = _load_row_tiles(xbuf[slot], MOE_ROWS, BF16)
        y = jnp.broadcast_to(bdn_ref[...], (MOE_ROWS, D_MODEL))
        cols = D_EXPERT // MOE_CHUNKS
        rows = MOE_ROWS // MOE_CHUNKS
        for c in range(MOE_CHUNKS):
            if c:
                pl.semaphore_signal(sem_pace, 1)
                pl.semaphore_wait(sem_pace, 1)
            for r in range(c * rows, (c + 1) * rows):
                row_in(other, r, idx_smem[q_next, 0, r]).start()
                row_out(other, r, idx_smem[q_prev, 1, r]).start()
            g_cols = slice(c * cols, (c + 1) * cols)
            u_cols = slice(D_EXPERT + c * cols, D_EXPERT + (c + 1) * cols)
            hg = jnp.minimum(_dot(x, wgu_ref[:, g_cols]) + bgu_ref[:, g_cols], SWIGLU_LIMIT)
            hu = jnp.clip(_dot(x, wgu_ref[:, u_cols]) + bgu_ref[:, u_cols], -SWIGLU_LIMIT, SWIGLU_LIMIT)
            act = (hu + 1.0) * (hg * _sigmoid(SWIGLU_ALPHA * hg))
            y = y + _dot(act, wdn_ref[g_cols, :])
        _store_row_tiles(ybuf[slot], y)

        @pl.when(i == n_used - 1)
        def _():
            for r in range(MOE_ROWS):
                row_out(slot, r, idx_smem[q_cur, 1, r]).start()
            wait_rows(row_out, other)
            wait_rows(row_out, slot)
            wait_rows(row_in, other)

    for parity in (0, 1):
        pl.when((i < n_used) & (i % 2 == parity))(functools.partial(step, parity))


def _moe_blocks(idx, h2, block_e, n_used, layer, w_gu, b_gu, w_dn, b_dn, n_out_rows):
    n_blocks = idx.shape[0] - 1
    exp = lambda i, be, nu: (be[jnp.minimum(i, nu[0] - 1)], 0, 0)
    exp_w = lambda i, be, nu: (layer, be[jnp.minimum(i, nu[0] - 1)], 0, 0)
    any_spec = pl.BlockSpec(memory_space=pl.ANY)
    return pl.pallas_call(
        _moe_kernel,
        out_shape=jax.ShapeDtypeStruct(((n_out_rows + MOE_DUMP_ROWS) * ROW_TILE, LANE), F32),
        grid_spec=pltpu.PrefetchScalarGridSpec(
            num_scalar_prefetch=2,
            grid=(n_blocks,),
            in_specs=[
                any_spec,
                any_spec,
                pl.BlockSpec((None, None, D_MODEL, 2 * D_EXPERT), exp_w),
                pl.BlockSpec((None, 1, 2 * D_EXPERT), exp),
                pl.BlockSpec((None, None, D_EXPERT, D_MODEL), exp_w),
                pl.BlockSpec((None, 1, D_MODEL), exp),
            ],
            out_specs=any_spec,
            scratch_shapes=[pltpu.SMEM((3, 2, MOE_ROWS), jnp.int32)]
                           + [pltpu.VMEM((MOE_ROWS * ROW_TILE, LANE), F32)] * 4
                           + [pltpu.VMEM((D_MODEL, 2 * D_EXPERT), BF16), pltpu.VMEM((D_EXPERT, D_MODEL), BF16)]
                           + [pltpu.SemaphoreType.DMA((3,)),
                              pltpu.SemaphoreType.DMA((2,)),
                              pltpu.SemaphoreType.DMA((2,)),
                              pltpu.SemaphoreType.DMA,
                              pltpu.SemaphoreType.REGULAR],
        ),
        compiler_params=_params("arbitrary"),
        name="moe_experts",
    )(block_e, n_used, idx, h2, w_gu, b_gu, w_dn, b_dn)


def _moe(h2, logits, layer, w_gu, b_gu, w_dn, b_dn):
    N = logits.shape[0]
    NK = N * TOP_K
    top_val, top_idx = lax.top_k(logits, TOP_K)
    gate = jax.nn.softmax(top_val, axis=-1)
    flat_e = top_idx.reshape(-1)
    order = jnp.argsort(flat_e).astype(jnp.int32)
    counts = jnp.sum((flat_e[:, None] == jnp.arange(N_EXPERTS)[None, :]).astype(jnp.int32), axis=0)
    padded = (counts + MOE_ROWS - 1) // MOE_ROWS * MOE_ROWS
    pad_end = jnp.cumsum(padded)
    pad_start = pad_end - padded
    start = jnp.cumsum(counts) - counts
    n_blocks = (NK + N_EXPERTS * (MOE_ROWS - 1) + MOE_ROWS - 1) // MOE_ROWS
    P = n_blocks * MOE_ROWS
    block_first = jnp.arange(n_blocks, dtype=jnp.int32) * MOE_ROWS
    block_e = jnp.minimum(jnp.sum((block_first[:, None] >= pad_end[None, :]).astype(jnp.int32), axis=1),
                          N_EXPERTS - 1)
    n_used = (pad_end[-1] // MOE_ROWS).astype(jnp.int32).reshape(1)
    pos = jnp.arange(P, dtype=jnp.int32)
    e_pos = jnp.repeat(block_e, MOE_ROWS)
    rank = pos - pad_start[e_pos].astype(jnp.int32)
    valid = rank < counts[e_pos]
    flat = order[jnp.clip(start[e_pos].astype(jnp.int32) + rank, 0, NK - 1)]
    tok = jnp.where(valid, flat // TOP_K, 0)
    spare = NK + (pos // MOE_ROWS % 2) * MOE_ROWS + pos % MOE_ROWS
    dst = jnp.where(valid, (flat % TOP_K) * N + flat // TOP_K, spare)
    idx = jnp.stack([tok.reshape(n_blocks, MOE_ROWS), dst.reshape(n_blocks, MOE_ROWS)], axis=1)
    stand_in = jnp.stack([jnp.zeros((MOE_ROWS,), jnp.int32), NK + MOE_ROWS + jnp.arange(MOE_ROWS, dtype=jnp.int32)])
    idx = jnp.concatenate([idx, stand_in[None]], axis=0) * ROW_TILE
    expert_out = _moe_blocks(idx, h2, block_e, n_used, layer, w_gu, b_gu, w_dn, b_dn, NK)
    return expert_out, gate


def _combine_kernel(x_ref, y0_ref, y1_ref, y2_ref, y3_ref, w_ref, g_ref, gate_ref, o_ref):
    w = w_ref[...]
    y = None
    for k, y_ref in enumerate((y0_ref, y1_ref, y2_ref, y3_ref)):
        term = _load_row_tiles(y_ref, w.shape[0]) * w[:, k:k + 1]
        y = term if y is None else y + term
    o_ref[...] = x_ref[...] + gate_ref[0] * (_rms(y) * g_ref[...])


def _combine(x, expert_out, weights, g, gate, *, rows_per_mod, tm):
    T = x.shape[0]
    tm = min(tm, rows_per_mod)
    row = pl.BlockSpec((tm, D_MODEL), lambda i: (i, 0))
    slabs = [pl.BlockSpec((tm * ROW_TILE, LANE), lambda i, k=k: (k * (T // tm) + i, 0)) for k in range(TOP_K)]
    return pl.pallas_call(
        _combine_kernel,
        out_shape=jax.ShapeDtypeStruct((T, D_MODEL), F32),
        grid=(T // tm,),
        in_specs=[row] + slabs + [pl.BlockSpec((tm, TOP_K), lambda i: (i, 0)),
                                  pl.BlockSpec((1, D_MODEL), lambda i: (0, 0)),
                                  pl.BlockSpec((1, 1, D_MODEL), lambda i: ((i * tm) // rows_per_mod, 0, 0))],
        out_specs=row,
        compiler_params=_params("parallel"),
        name="combine",
    )(x, *[expert_out] * TOP_K, weights, g, gate)


def _rope_tables(L):
    t = jnp.arange(L)
    lane = jnp.arange(LANE)
    axis = (lane % A_DIM) // 32
    half = (lane % 32) // 16
    n_freq = A_DIM // 4
    inv = ROPE_BASE ** (-(lane % n_freq).astype(F32) / n_freq)
    pos = jnp.where(axis[None, :] == 0, (t // GRID_W)[:, None], (t % GRID_W)[:, None]).astype(F32)
    ang = pos * inv[None, :]
    return jnp.cos(ang), jnp.where(half[None, :] == 0, -jnp.sin(ang), jnp.sin(ang))


def _arrange_w_in(w_in):
    sizes = [512] * 8 + [8, 8] + [512] * 4 + [8, 8]
    offs = [0]
    for s in sizes:
        offs.append(offs[-1] + s)
    part = lambda i: w_in[:, offs[i]:offs[i + 1]]
    big = [part(i) for i in (0, 1, 2, 3, 4, 5, 6, 7, 10, 11, 12, 13)]
    small = jnp.zeros((w_in.shape[0], HEADS, LANE), w_in.dtype)
    for base, i in ((SM_BETA, 8), (SM_DEC, 9), (SM_IG, 14), (SM_FG, 15)):
        cols = part(i).reshape(-1, 2, HEADS)
        for d in range(2):
            small = small.at[:, :, base + d].set(cols[:, d, :])
    return jnp.concatenate(big + [small.reshape(w_in.shape[0], HEADS * LANE)], axis=1)


def _layer(x, cond, lidx, lp, *, B, L, cache):
    T = B * L
    is_ctx = cache is None
    mod = (jax.nn.silu(cond) @ lp['ada_w'] + lp['ada_b']).reshape(-1, 6, 1, D_MODEL)
    shift1, scale1, gate1, shift2, scale2, gate2 = [mod[:, i] for i in range(6)]
    rows_per_mod = T if mod.shape[0] == 1 else L
    g = lp['norm_g']

    tm = 1024
    P = _norm_mod_matmul(x, g[0:1], shift1, scale1, lp['w_in'], jnp.zeros((1, N_PROJ), F32),
                         rows_per_mod=rows_per_mod, sigmoid=False, tm=tm, tn=N_PROJ // 4)
    gates = _norm_mod_matmul(x, g[0:1], shift1, scale1, lp['w_bgate'], lp['b_bgate'][None],
                             rows_per_mod=rows_per_mod, sigmoid=True, tm=tm, tn=2048)

    lam_init = 0.8 - 0.6 * math.exp(-0.3 * lidx)
    lq1, lk1, lq2, lk2 = lp['attn_lambda']
    lam = (jnp.exp(jnp.sum(lq1 * lk1)) - jnp.exp(jnp.sum(lq2 * lk2)) + lam_init).reshape(1, 1)
    attn_norm = lp['attn_norm'][None]
    if is_ctx:
        br_a, new_k, new_v = _attention(P, lam, attn_norm, B=B, L=L, tq=L, lam_init=lam_init)
    else:
        (br_a,) = _attention(P, lam, attn_norm, B=B, L=L, tq=256, lam_init=lam_init, cache=cache['attn'])

    br_b = _pool(P, lp['pool_w'], lp['pool_b'][:, None], lp['pool_scale'][None], B=B, L=L)

    gdn_args = (P, lp['gdn_A_log'], lp['gdn_dt_bias'], lp['gdn_conv'], lp['gdn_norm'][None])
    ml_args = (P, lp['mlstm_bias_i'], lp['mlstm_bias_f'], lp['mlstm_norm'][None])
    if is_ctx:
        br_c, new_gdn = _gdn(*gdn_args, B=B, L=L)
        br_d, new_c, new_n, new_m = _mlstm(*ml_args, B=B, L=L)
        ctx_out = (new_k, new_v, new_gdn, new_c,
                   new_n[:, :, :, 0, :HEAD_DIM].transpose(0, 2, 1, 3), new_m[:, :, :, 0, 0].transpose(0, 2, 1))
    else:
        (br_c,) = _gdn(*gdn_args, B=B, L=L, state=cache['gdn'])
        (br_d,) = _mlstm(*ml_args, B=B, L=L, state=cache['mlstm'])
        ctx_out = None

    x, h2, logits = _merge(x, gates, (br_a, br_b, br_c, br_d), lp['w_branch'], lp['w_out'], g[1:2], gate1,
                           g[2:3], shift2, scale2, lp['router_w'], lp['router_b'],
                           rows_per_mod=rows_per_mod, tm=256)
    expert_out, weights = _moe(h2, logits[:, :N_EXPERTS], lidx, lp['moe_w_gu'], lp['moe_b_gu'], lp['moe_w_dn'],
                               lp['moe_b_dn'])
    x = _combine(x, expert_out, weights, g[3:4], gate2, rows_per_mod=rows_per_mod, tm=512)
    return x, ctx_out


def kernel(x_prompt, x_sample, cache_attn_k, cache_attn_v, state_gdn, state_mlstm_C, state_mlstm_n, state_mlstm_m, c, c_ctx, ada_w, ada_b, norm_g, w_in, w_bgate, b_bgate, w_branch, w_out, attn_lambda, attn_norm, pool_w, pool_b, pool_scale, gdn_conv, gdn_A_log, gdn_dt_bias, gdn_norm, mlstm_bias_i, mlstm_bias_f, mlstm_norm, router_w, router_b, moe_w_gu, moe_b_gu, moe_w_dn, moe_b_dn):
    Bp, Lp, _ = x_prompt.shape
    Bs, Ls, _ = x_sample.shape
    y_p = x_prompt.reshape(Bp * Lp, D_MODEL)
    y_s = x_sample.reshape(Bs * Ls, D_MODEL)
    cos, sin = _rope_tables(Ls)
    n0 = state_mlstm_n.transpose(1, 0, 3, 2, 4)[:, :, :, :, None, :]
    m0 = jnp.broadcast_to(state_mlstm_m.transpose(1, 0, 3, 2)[..., None, None], n0.shape)
    ctx_states = []
    for l in range(DEPTH):
        lp = {
            'ada_w': ada_w[l], 'ada_b': ada_b[l], 'norm_g': norm_g[l],
            'w_in': _arrange_w_in(w_in[l].astype(BF16)), 'w_bgate': w_bgate[l].astype(BF16),
            'b_bgate': b_bgate[l], 'w_branch': w_branch[l].astype(BF16), 'w_out': w_out[l].astype(BF16),
            'attn_lambda': attn_lambda[l], 'attn_norm': attn_norm[l],
            'pool_w': pool_w[l], 'pool_b': pool_b[l], 'pool_scale': pool_scale[l],
            'gdn_conv': gdn_conv[l], 'gdn_A_log': gdn_A_log[l], 'gdn_dt_bias': gdn_dt_bias[l],
            'gdn_norm': gdn_norm[l], 'mlstm_bias_i': mlstm_bias_i[l], 'mlstm_bias_f': mlstm_bias_f[l],
            'mlstm_norm': mlstm_norm[l],
            'router_w': jnp.pad(router_w[l], ((0, 0), (0, LANE - N_EXPERTS))),
            'router_b': jnp.pad(router_b[l], (0, LANE - N_EXPERTS))[None],
            'moe_w_gu': moe_w_gu, 'moe_b_gu': moe_b_gu[l][:, None],
            'moe_w_dn': moe_w_dn, 'moe_b_dn': moe_b_dn[l][:, None],
        }
        y_p, st = _layer(y_p, c_ctx[None], l, lp, B=Bp, L=Lp, cache=None)
        ctx_states.append(st)
        cache = {
            'attn': {'k': cache_attn_k, 'v': cache_attn_v, 'layer': l, 'cos': cos, 'sin': sin},
            'gdn': (state_gdn, l),
            'mlstm': (state_mlstm_C, l, n0[l], m0[l]),
        }
        y_s, _ = _layer(y_s, c, l, lp, B=Bs, L=Ls, cache=cache)
    outs = [jnp.stack([s[i] for s in ctx_states], axis=1) for i in range(6)]
    return (y_p.reshape(Bp, Lp, D_MODEL), y_s.reshape(Bs, Ls, D_MODEL), *outs)
```

```python
import functools
import math

import jax
import jax.numpy as jnp
from jax import lax
from jax.experimental import pallas as pl
from jax.experimental.pallas import tpu as pltpu

F32 = jnp.float32
BF16 = jnp.bfloat16

D_MODEL = 1024
DEPTH = 2
GRID_W = 64
EPS = 1e-6
LANE = 128
HEADS = 4
HEAD_DIM = 128
A_DIM = 64
ROPE_BASE = 10000.0
POOL_WINDOWS = (2, 4, 8, 16)
CHUNK = 64
N_BRANCH = 4
BRANCH_WIDTH = 512
N_EXPERTS = 32
TOP_K = 4
D_EXPERT = 1024
SWIGLU_LIMIT = 7.0
SWIGLU_ALPHA = 1.702
MOE_ROWS = 256
VMEM_LIMIT = 56 * 1024 * 1024

COL_AQ, COL_AK, COL_AV, COL_POOL, COL_CQ, COL_CK, COL_CV, COL_CG, COL_MQ, COL_MK, COL_MV, COL_MG, COL_SM = range(13)
N_PROJ = 13 * BRANCH_WIDTH
SM_BETA, SM_DEC, SM_IG, SM_FG = 0, 2, 4, 6


def _params(*sem):
    return pltpu.CompilerParams(dimension_semantics=sem, vmem_limit_bytes=VMEM_LIMIT)


def _dot(a, b):
    return jnp.dot(a.astype(BF16), b.astype(BF16), preferred_element_type=F32)


def _dot_nt(a, b):
    return lax.dot_general(a.astype(BF16), b.astype(BF16), (((1,), (1,)), ((), ())),
                           preferred_element_type=F32)


def _dot_tn(a, b):
    return lax.dot_general(a.astype(BF16), b.astype(BF16), (((0,), (0,)), ((), ())),
                           preferred_element_type=F32)


def _split(a):
    hi = a.astype(BF16)
    lo = (a - hi.astype(F32)).astype(BF16)
    return hi, lo


def _dot3(a, b):
    ah, al = _split(a)
    bh, bl = _split(b)
    d = lambda x, y: jnp.dot(x, y, preferred_element_type=F32)
    return d(ah, bh) + (d(ah, bl) + d(al, bh))


def _dot_mask(mask, b):
    m = jnp.where(mask, 1.0, 0.0).astype(BF16)
    b0 = b.astype(BF16)
    r1 = b - b0.astype(F32)
    b1 = r1.astype(BF16)
    b2 = (r1 - b1.astype(F32)).astype(BF16)
    d = lambda y: jnp.dot(m, y, preferred_element_type=F32)
    return d(b0) + (d(b1) + d(b2))


ROW_TILE = 8
assert ROW_TILE * LANE == D_MODEL


def _store_row_tiles(ref, x):
    rows = x.shape[0]
    for c in range(ROW_TILE):
        ref[pl.ds(c, rows, stride=ROW_TILE), :] = x[:, c * LANE:(c + 1) * LANE]


def _load_row_tiles(ref, rows, dtype=F32):
    return jnp.concatenate([ref[pl.ds(c, rows, stride=ROW_TILE), :].astype(dtype) for c in range(ROW_TILE)],
                           axis=1)


def _rms(x):
    return x * lax.rsqrt(jnp.mean(x * x, axis=-1, keepdims=True) + EPS)


def _sigmoid(x):
    return 1.0 / (1.0 + jnp.exp(-x))


def _softplus(x):
    return jnp.maximum(x, 0.0) + jnp.log(1.0 + jnp.exp(-jnp.abs(x)))


def _proj_kernel(x_ref, g_ref, sh_ref, sc_ref, w_ref, o_ref, h_scr):
    @pl.when(pl.program_id(1) == 0)
    def _():
        y = _rms(x_ref[...]) * g_ref[...]
        h_scr[...] = (y * (1.0 + sc_ref[0]) + sh_ref[0]).astype(BF16)

    o_ref[...] = jnp.dot(h_scr[...], w_ref[...], preferred_element_type=F32)


def _norm_mod_matmul(x, g, shift, scale, w, *, rows_per_mod, tm, tn):
    T, Dm = x.shape
    N = w.shape[1]
    tm = min(tm, rows_per_mod)
    mod_idx = lambda i, j: ((i * tm) // rows_per_mod, 0, 0)
    return pl.pallas_call(
        _proj_kernel,
        out_shape=jax.ShapeDtypeStruct((T, N), F32),
        grid=(T // tm, N // tn),
        in_specs=[
            pl.BlockSpec((tm, Dm), lambda i, j: (i, 0)),
            pl.BlockSpec((1, Dm), lambda i, j: (0, 0)),
            pl.BlockSpec((1, 1, Dm), mod_idx),
            pl.BlockSpec((1, 1, Dm), mod_idx),
            pl.BlockSpec((Dm, tn), lambda i, j: (0, j)),
        ],
        out_specs=pl.BlockSpec((tm, tn), lambda i, j: (i, j)),
        scratch_shapes=[pltpu.VMEM((tm, Dm), BF16)],
        compiler_params=_params("parallel", "arbitrary"),
        name="proj",
    )(x, g, shift, scale, w)


def _rope(x, cos, sin):
    lane = lax.broadcasted_iota(jnp.int32, x.shape, 1)
    first = (lane % 32) < 16
    partner = jnp.where(first, pltpu.roll(x, LANE - 16, 1), pltpu.roll(x, 16, 1))
    return x * cos + partner * sin


def _attn_kernel(*refs, has_cache, n_ctx, out_scale):
    if has_cache:
        (lam_ref, q_ref, k_ref, v_ref, ck_ref, cv_ref, cosq_ref, sinq_ref, cosk_ref, sink_ref, nrm_ref,
         o_ref, kall, vall) = refs
    else:
        lam_ref, q_ref, k_ref, v_ref, nrm_ref, o_ref, ko_ref, vo_ref, kall, vall = refs

    @pl.when(pl.program_id(2) == 0)
    def _():
        k = k_ref[...]
        v = v_ref[...]
        if has_cache:
            kall[:, pl.ds(0, n_ctx)] = ck_ref[...].T.astype(BF16)
            vall[pl.ds(0, n_ctx), :] = cv_ref[...].astype(BF16)
            k = _rope(k, cosk_ref[...], sink_ref[...])
        else:
            ko_ref[...] = k
            vo_ref[...] = v
        kall[:, pl.ds(n_ctx, k.shape[0])] = k.T.astype(BF16)
        vall[pl.ds(n_ctx, k.shape[0]), :] = v.astype(BF16)

    q = q_ref[...]
    if has_cache:
        q = _rope(q, cosq_ref[...], sinq_ref[...])
    q = q * (A_DIM ** -0.5 * math.log2(math.e))
    lane = lax.broadcasted_iota(jnp.int32, q.shape, 1)
    q1 = jnp.where(lane < A_DIM, q, 0.0)
    q2 = jnp.where(lane >= A_DIM, q, 0.0)
    keys = kall[...]
    lam = lam_ref[0, 0]

    def probs(qm):
        s = _dot(qm, keys)
        p = jnp.exp2(s - jnp.max(s, axis=-1, keepdims=True))
        return p, jnp.sum(p, axis=-1, keepdims=True)

    p1, l1 = probs(q1)
    p2, l2 = probs(q2)
    a = p1 * (1.0 / l1) - p2 * (lam / l2)
    o = jnp.dot(a.astype(BF16), vall[...], preferred_element_type=F32)
    o_ref[...] = _rms(o) * nrm_ref[...] * out_scale


def _attention(P, lam, norm, *, B, L, tq, lam_init, cache=None):
    T = B * L
    nq = L // tq
    has_cache = cache is not None
    n_ctx = cache["k"].shape[3] if has_cache else 0
    colq, colk, colv = COL_AQ * HEADS, COL_AK * HEADS, COL_AV * HEADS
    in_specs = [
        pl.BlockSpec(memory_space=pltpu.SMEM),
        pl.BlockSpec((tq, LANE), lambda b, h, i: (b * nq + i, colq + h)),
        pl.BlockSpec((L, LANE), lambda b, h, i: (b, colk + h)),
        pl.BlockSpec((L, LANE), lambda b, h, i: (b, colv + h)),
    ]
    args = [lam, P, P, P]
    if has_cache:
        l = cache["layer"]
        cspec = pl.BlockSpec((None, None, None, n_ctx, LANE), lambda b, h, i: (b, l, h, 0, 0))
        in_specs += [cspec, cspec,
                     pl.BlockSpec((tq, LANE), lambda b, h, i: (i, 0)),
                     pl.BlockSpec((tq, LANE), lambda b, h, i: (i, 0)),
                     pl.BlockSpec((L, LANE), lambda b, h, i: (0, 0)),
                     pl.BlockSpec((L, LANE), lambda b, h, i: (0, 0))]
        args += [cache["k"], cache["v"], cache["cos"], cache["sin"], cache["cos"], cache["sin"]]
    in_specs.append(pl.BlockSpec((1, LANE), lambda b, h, i: (0, 0)))
    args.append(norm)
    out_shape = [jax.ShapeDtypeStruct((T, HEADS * LANE), F32)]
    out_specs = [pl.BlockSpec((tq, LANE), lambda b, h, i: (b * nq + i, h))]
    if not has_cache:
        kv_shape = jax.ShapeDtypeStruct((B, HEADS, L, LANE), F32)
        kv_spec = pl.BlockSpec((None, None, L, LANE), lambda b, h, i: (b, h, 0, 0))
        out_shape += [kv_shape, kv_shape]
        out_specs += [kv_spec, kv_spec]
    return pl.pallas_call(
        functools.partial(_attn_kernel, has_cache=has_cache, n_ctx=n_ctx, out_scale=1.0 - lam_init),
        out_shape=out_shape,
        grid=(B, HEADS, nq),
        in_specs=in_specs,
        out_specs=out_specs,
        scratch_shapes=[pltpu.VMEM((LANE, n_ctx + L), BF16), pltpu.VMEM((n_ctx + L, LANE), BF16)],
        compiler_params=_params("parallel", "parallel", "arbitrary"),
        name="diff_attention",
    )(*args)


POOL_PAD = 16


def _pool_kernel(x_ref, w_ref, b_ref, s_ref, o_ref, pad, *, L):
    zeros = jnp.zeros((POOL_PAD, LANE), F32)
    pad[pl.ds(0, POOL_PAD), :] = zeros
    pad[pl.ds(POOL_PAD + L, POOL_PAD), :] = zeros
    x = x_ref[...]
    pad[pl.ds(POOL_PAD, L), :] = x
    t = lax.broadcasted_iota(jnp.int32, (L, LANE), 0)
    g = pl.program_id(1)
    for gi, win in enumerate(POOL_WINDOWS):
        @pl.when(g == gi)
        def _(win=win):
            half = win // 2
            acc = pad[pl.ds(POOL_PAD - half, L), :]
            for k in range(1 - half, half):
                acc = acc + pad[pl.ds(POOL_PAD + k, L), :]
            cnt = (jnp.minimum(t + half, L) - jnp.maximum(t - half, 0)).astype(F32)
            pooled = acc / cnt - x
            o_ref[...] = (_dot(pooled, w_ref[...]) + b_ref[...]) * s_ref[...]


def _pool(P, w, b, scale, *, B, L):
    T = B * L
    G = len(POOL_WINDOWS)
    return pl.pallas_call(
        functools.partial(_pool_kernel, L=L),
        out_shape=jax.ShapeDtypeStruct((T, G * LANE), F32),
        grid=(B, G),
        in_specs=[
            pl.BlockSpec((L, LANE), lambda bi, g: (bi, COL_POOL * HEADS + g)),
            pl.BlockSpec((None, LANE, LANE), lambda bi, g: (g, 0, 0)),
            pl.BlockSpec((None, 1, LANE), lambda bi, g: (g, 0, 0)),
            pl.BlockSpec((1, LANE), lambda bi, g: (0, g)),
        ],
        out_specs=pl.BlockSpec((L, LANE), lambda bi, g: (bi, g)),
        scratch_shapes=[pltpu.VMEM((L + 2 * POOL_PAD, LANE), F32)],
        compiler_params=_params("parallel", "parallel"),
        name="pool_mixer",
    )(P, w, b, scale)


GROUP_CHUNKS = 2
GROUP = GROUP_CHUNKS * CHUNK
MAX_GROUPS_PER_STEP = 4


def _groups_per_step(L):
    return min(MAX_GROUPS_PER_STEP, L // GROUP)


def _group_masks(backward):
    row = lax.broadcasted_iota(jnp.int32, (GROUP, GROUP), 0)
    col = lax.broadcasted_iota(jnp.int32, (GROUP, GROUP), 1)
    same = (row // CHUNK) == (col // CHUNK)
    lower, upper = row >= col, row <= col
    if backward:
        lower, upper = upper, lower
    return dict(same=same, incl=same & lower, strict=same & lower & (row != col), incl_t=same & upper,
                eye=row == col)


INV_BASE = 8


def _block_masks():
    row = lax.broadcasted_iota(jnp.int32, (GROUP, GROUP), 0)
    col = lax.broadcasted_iota(jnp.int32, (GROUP, GROUP), 1)
    sizes = [INV_BASE << i for i in range(int(math.log2(CHUNK // INV_BASE)) + 1)]
    same = [(row // s) == (col // s) for s in sizes]
    return [same[0]] + [cur & jnp.logical_not(prev) for prev, cur in zip(same[:-1], same[1:])]


def _unit_triangular_inverses(ns, eye, blocks):
    base = [jnp.where(blocks[0], n, 0.0) for n in ns]
    invs = [eye - b for b in base]
    pws = [_dot(b, b) for b in base]
    rounds = int(math.log2(INV_BASE)) - 1
    for s in range(rounds):
        invs = [t + _dot(t, p) for t, p in zip(invs, pws)]
        if s + 1 < rounds:
            pws = [_dot(p, p) for p in pws]
    for join in blocks[1:]:
        offs = [jnp.where(join, n, 0.0) for n in ns]
        xs = [_dot(o, t) for o, t in zip(offs, invs)]
        invs = [t - _dot(t, x) for t, x in zip(invs, xs)]
    return invs


def _dot_mask2(mask, b):
    m = jnp.where(mask, 1.0, 0.0).astype(BF16)
    b0, b1 = _split(b)
    return jnp.dot(m, b0, preferred_element_type=F32) + jnp.dot(m, b1, preferred_element_type=F32)


def _per_chunk(x, reduce):
    parts = [reduce(x[c * CHUNK:(c + 1) * CHUNK], axis=0, keepdims=True) for c in range(GROUP_CHUNKS)]
    col = jnp.concatenate([jnp.broadcast_to(p, (CHUNK, 1)) for p in parts], axis=0)
    return parts, col


def _scan_order(step, n_steps, groups):
    base = (step * groups * GROUP, (n_steps - 1 - step) * groups * GROUP)
    fwd = [(gi, c) for gi in range(groups) for c in range(GROUP_CHUNKS)]
    return base, (fwd, fwd[::-1])


CONV_PAD = 8


def _gdn_kernel(*refs, L, has_state):
    alog_ref, dtb_ref, q_ref, k_ref, v_ref, gate_ref, sm_ref, cwq_ref, cwk_ref, cwv_ref, nrm_ref = refs[:11]
    refs = refs[11:]
    if has_state:
        s0_ref, o_ref, pad, qs, ks, vs, o_f, o_b, state = refs
    else:
        o_ref, sout_ref, pad, qs, ks, vs, o_f, o_b, state = refs
    h = pl.program_id(1)
    n_chunks = L // CHUNK

    zeros = jnp.zeros((CONV_PAD, LANE), F32)
    pad[pl.ds(0, CONV_PAD), :] = zeros
    pad[pl.ds(CONV_PAD + L, CONV_PAD), :] = zeros

    def conv_silu(x_ref, w_ref):
        pad[pl.ds(CONV_PAD, L), :] = x_ref[...]
        w = w_ref[...]
        y = (pad[pl.ds(CONV_PAD - 1, L), :] * w[0:1] + pad[pl.ds(CONV_PAD, L), :] * w[1:2]
             + pad[pl.ds(CONV_PAD + 1, L), :] * w[2:3])
        return y * _sigmoid(y)

    def l2n(x):
        return x * lax.rsqrt(jnp.sum(x * x, axis=-1, keepdims=True) + EPS)

    qs[...] = l2n(conv_silu(q_ref, cwq_ref)) * (HEAD_DIM ** -0.5)
    ks[...] = l2n(conv_silu(k_ref, cwk_ref))
    vs[...] = conv_silu(v_ref, cwv_ref)
    if has_state:
        state[...] = s0_ref[...]
    else:
        state[...] = jnp.zeros_like(state)

    groups = _groups_per_step(L)
    n_steps = L // (groups * GROUP)
    o_scr = (o_f, o_b)
    masks = (_group_masks(False), _group_masks(True))
    eye = jnp.where(masks[0]['eye'], 1.0, 0.0)
    blocks = _block_masks()
    items = [(d, gi) for gi in range(groups) for d in (0, 1)]
    chunks = [slice(c * CHUNK, (c + 1) * CHUNK) for c in range(GROUP_CHUNKS)]

    def body(step, carry):
        base, order = _scan_order(step, n_steps, groups)
        it = {}
        for key in items:
            d, gi = key
            rows = pl.ds(pl.multiple_of(base[d] + gi * GROUP, GROUP), GROUP)
            q, k, v, sm = qs[rows, :], ks[rows, :], vs[rows, :], sm_ref[rows, :]
            beta = _sigmoid(sm[:, SM_BETA + d:SM_BETA + d + 1])
            dec = sm[:, SM_DEC + d:SM_DEC + d + 1]
            g = -jnp.exp(alog_ref[d, h]) * _softplus(dec + dtb_ref[d, h])
            it[key] = dict(q=q, k=k, kb=k * beta, vb=v * beta, g=g, gb=jnp.broadcast_to(g, (GROUP, GROUP)))
        for (d, gi), x in it.items():
            m = masks[d]
            x['cum_i'] = _dot_mask2(m['incl'], x['gb'])
            x['cum_j'] = _dot_mask2(m['same'], jnp.where(m['incl_t'], x['gb'], 0.0))
            x['kk'] = _dot_nt(x['kb'], x['k'])
            x['qk'] = _dot_nt(x['q'], x['k'])
        for (d, gi), x in it.items():
            m = masks[d]
            decay = jnp.where(m['incl'], jnp.exp(jnp.where(m['incl'], x['cum_i'] - x['cum_j'], 0.0)), 0.0)
            x['gc'] = x['cum_i'][:, 0:1]
            x['tot'], x['tot_col'] = _per_chunk(x['g'], jnp.sum)
            x['qk'] = x['qk'] * decay
            x['n'] = jnp.where(m['strict'], x['kk'] * decay, 0.0)
        invs = _unit_triangular_inverses([x['n'] for x in it.values()], eye, blocks)
        for x, inv in zip(it.values(), invs):
            x['inv'] = inv
            x['egc'] = jnp.exp(x['gc'])
            x['sol'] = _dot3(x['inv'], jnp.concatenate([x['vb'], x['kb'] * x['egc']], axis=-1))
        for x in it.values():
            u, w = x['sol'][:, :HEAD_DIM], x['sol'][:, HEAD_DIM:]
            x['qp'] = x['q'] * x['egc'] - _dot(x['qk'], w)
            x['op'] = _dot(x['qk'], u)
            kd = x['k'] * jnp.exp(x['tot_col'] - x['gc'])
            x['ab'] = [_dot_tn(kd[cs], x['sol'][cs]) for cs in chunks]
            x['gl'] = [jnp.exp(t) for t in x['tot']]
        S = [state[0], state[1]]
        for stp in range(len(order[0])):
            for d in (0, 1):
                gi, c = order[d][stp]
                x = it[(d, gi)]
                ab = x['ab'][c]
                r = _dot(jnp.concatenate([ab[:, HEAD_DIM:], x['qp'][chunks[c]]], axis=0), S[d])
                rows = pl.ds(pl.multiple_of(base[d] + gi * GROUP + c * CHUNK, CHUNK), CHUNK)
                o_scr[d][rows, :] = x['op'][chunks[c]] + r[HEAD_DIM:]
                S[d] = x['gl'][c] * S[d] + (ab[:, :HEAD_DIM] - r[:HEAD_DIM])
        state[0] = S[0]
        state[1] = S[1]
        return carry

    lax.fori_loop(0, n_steps, body, 0)
    gate = gate_ref[...]
    o_ref[...] = _rms(o_f[...] + o_b[...]) * nrm_ref[...] * (gate * _sigmoid(gate))
    if not has_state:
        sout_ref[...] = state[...]


def _col_spec(L, col):
    return pl.BlockSpec((L, LANE), lambda b, h: (b, col * HEADS + h))


def _gdn(P, a_log, dt_bias, conv_w, norm, *, B, L, state=None):
    T = B * L
    has_state = state is not None
    smem = pl.BlockSpec(memory_space=pltpu.SMEM)
    in_specs = [smem, smem, _col_spec(L, COL_CQ), _col_spec(L, COL_CK), _col_spec(L, COL_CV),
                _col_spec(L, COL_CG), _col_spec(L, COL_SM)]
    in_specs += [pl.BlockSpec((3, LANE), lambda b, h, j=j: (0, j * HEADS + h)) for j in range(3)]
    in_specs.append(pl.BlockSpec((1, LANE), lambda b, h: (0, 0)))
    args = [a_log, dt_bias, P, P, P, P, P, conv_w, conv_w, conv_w, norm]
    out_shape = [jax.ShapeDtypeStruct((T, HEADS * LANE), F32)]
    out_specs = [pl.BlockSpec((L, LANE), lambda b, h: (b, h))]
    if has_state:
        arr, l = state
        in_specs.append(pl.BlockSpec((None, None, 2, None, HEAD_DIM, HEAD_DIM), lambda b, h: (b, l, 0, h, 0, 0)))
        args.append(arr)
    else:
        out_shape.append(jax.ShapeDtypeStruct((B, 2, HEADS, HEAD_DIM, HEAD_DIM), F32))
        out_specs.append(pl.BlockSpec((None, 2, None, HEAD_DIM, HEAD_DIM), lambda b, h: (b, 0, h, 0, 0)))
    seq = pltpu.VMEM((L, LANE), F32)
    return pl.pallas_call(
        functools.partial(_gdn_kernel, L=L, has_state=has_state),
        out_shape=out_shape,
        grid=(B, HEADS),
        in_specs=in_specs,
        out_specs=out_specs,
        scratch_shapes=[pltpu.VMEM((L + 2 * CONV_PAD, LANE), F32), seq, seq, seq, seq, seq,
                        pltpu.VMEM((2, HEAD_DIM, HEAD_DIM), F32)],
        compiler_params=_params("parallel", "parallel"),
        name="gated_deltanet",
    )(*args)


def _mlstm_kernel(*refs, L, has_state):
    bi_ref, bf_ref, q_ref, k_ref, v_ref, gate_ref, sm_ref, nrm_ref = refs[:8]
    refs = refs[8:]
    if has_state:
        c0_ref, n0_ref, m0_ref, o_ref, h_f, h_b, c_st, n_st, m_st = refs
    else:
        o_ref, cout_ref, nout_ref, mout_ref, h_f, h_b, c_st, n_st, m_st = refs
    h = pl.program_id(1)
    n_chunks = L // CHUNK

    if has_state:
        c_st[...] = c0_ref[...]
        n_st[...] = n0_ref[...]
        m_st[...] = m0_ref[...]
    else:
        c_st[...] = jnp.zeros_like(c_st)
        n_st[...] = jnp.zeros_like(n_st)
        m_st[...] = jnp.zeros_like(m_st)

    groups = _groups_per_step(L)
    n_steps = L // (groups * GROUP)
    h_scr = (h_f, h_b)
    masks = (_group_masks(False), _group_masks(True))
    items = [(d, gi) for gi in range(groups) for d in (0, 1)]
    chunks = [slice(c * CHUNK, (c + 1) * CHUNK) for c in range(GROUP_CHUNKS)]

    def body(step, carry):
        base, order = _scan_order(step, n_steps, groups)
        it = {}
        for key in items:
            d, gi = key
            rows = pl.ds(pl.multiple_of(base[d] + gi * GROUP, GROUP), GROUP)
            q, v, sm = q_ref[rows, :], v_ref[rows, :], sm_ref[rows, :]
            k = k_ref[rows, :] * (HEAD_DIM ** -0.5)
            ig = sm[:, SM_IG + d:SM_IG + d + 1] + bi_ref[d, h]
            fg = sm[:, SM_FG + d:SM_FG + d + 1] + bf_ref[d, h]
            lf = -_softplus(-fg)
            it[key] = dict(q=q, k=k, v=v, ig=ig, lf=lf, lfb=jnp.broadcast_to(lf, (GROUP, GROUP)),
                           igb=jnp.broadcast_to(ig, (GROUP, GROUP)))
        for (d, gi), x in it.items():
            m = masks[d]
            x['cum_i'] = _dot_mask2(m['incl'], x['lfb'])
            x['cum_j'] = _dot_mask2(m['same'], jnp.where(m['incl_t'], x['lfb'], 0.0)
                                    - jnp.where(m['eye'], x['igb'], 0.0))
            x['qk'] = _dot_nt(x['q'], x['k'])
        for (d, gi), x in it.items():
            m = masks[d]
            dm = x['cum_i'] - x['cum_j']
            x['b'] = x['cum_i'][:, 0:1]
            x['tot'], tot_col = _per_chunk(x['lf'], jnp.sum)
            x['m_intra'] = jnp.max(jnp.where(m['incl'], dm, -jnp.inf), axis=-1, keepdims=True)
            e = tot_col - x['b'] + x['ig']
            x['m_end'], m_end_col = _per_chunk(e, jnp.max)
            p = jnp.where(m['incl'], jnp.exp(jnp.where(m['incl'], dm - x['m_intra'], 0.0)), 0.0) * x['qk']
            x['p_sum'] = jnp.sum(p, axis=-1, keepdims=True)
            x['p'] = p
            x['kw'] = x['k'] * jnp.exp(e - m_end_col)
        for x in it.values():
            x['pv'] = _dot(x['p'], x['v'])
            x['kv'] = [_dot_tn(x['kw'][cs], x['v'][cs]) for cs in chunks]
            x['k_sum'] = [jnp.sum(x['kw'][cs], axis=0, keepdims=True) for cs in chunks]
        outs = []
        for d in (0, 1):
            C, n, m = c_st[d], n_st[d], m_st[d][:, 0:1]
            for gi, c in order[d]:
                x = it[(d, gi)]
                cs = chunks[c]
                m_t = jnp.maximum(x['b'][cs] + m, x['m_intra'][cs])
                w_inter = jnp.exp(x['b'][cs] + m - m_t)
                local = jnp.exp(x['m_intra'][cs] - m_t)
                den = (w_inter * jnp.sum(x['q'][cs] * n, axis=-1, keepdims=True) + local * x['p_sum'][cs])
                scale = 1.0 / jnp.maximum(jnp.abs(den), jnp.exp(-m_t))
                outs.append((d, gi, c, C, w_inter * scale, local * scale))
                m_new = jnp.maximum(x['tot'][c] + m, x['m_end'][c])
                carry_decay = jnp.exp(x['tot'][c] + m - m_new)
                local_new = jnp.exp(x['m_end'][c] - m_new)
                C = carry_decay * C + local_new * x['kv'][c]
                n = carry_decay * n + local_new * x['k_sum'][c]
                m = m_new
            c_st[d] = C
            n_st[d] = n
            m_st[d] = jnp.broadcast_to(m, (1, LANE))
        for d, gi, c, C, w_scale, p_scale in outs:
            x = it[(d, gi)]
            cs = chunks[c]
            rows = pl.ds(pl.multiple_of(base[d] + gi * GROUP + c * CHUNK, CHUNK), CHUNK)
            h_scr[d][rows, :] = w_scale * _dot(x['q'][cs], C) + p_scale * x['pv'][cs]
        return carry

    lax.fori_loop(0, n_steps, body, 0)
    o_ref[...] = _rms(h_f[...] + h_b[...]) * nrm_ref[...] * _sigmoid(gate_ref[...])
    if not has_state:
        cout_ref[...] = c_st[...]
        nout_ref[...] = n_st[...]
        mout_ref[...] = m_st[...]


def _mlstm(P, bias_i, bias_f, norm, *, B, L, state=None):
    T = B * L
    has_state = state is not None
    smem = pl.BlockSpec(memory_space=pltpu.SMEM)
    in_specs = [smem, smem, _col_spec(L, COL_MQ), _col_spec(L, COL_MK), _col_spec(L, COL_MV),
                _col_spec(L, COL_MG), _col_spec(L, COL_SM), pl.BlockSpec((1, LANE), lambda b, h: (0, 0))]
    args = [bias_i, bias_f, P, P, P, P, P, norm]
    out_shape = [jax.ShapeDtypeStruct((T, HEADS * LANE), F32)]
    out_specs = [pl.BlockSpec((L, LANE), lambda b, h: (b, h))]
    vec_spec = pl.BlockSpec((None, None, 2, 1, LANE), lambda b, h: (b, h, 0, 0, 0))
    if has_state:
        c_arr, l, n_arr, m_arr = state
        in_specs += [pl.BlockSpec((None, None, 2, None, HEAD_DIM, HEAD_DIM), lambda b, h: (b, l, 0, h, 0, 0)),
                     vec_spec, vec_spec]
        args += [c_arr, n_arr, m_arr]
    else:
        out_shape += [jax.ShapeDtypeStruct((B, 2, HEADS, HEAD_DIM, HEAD_DIM), F32),
                      jax.ShapeDtypeStruct((B, HEADS, 2, 1, LANE), F32),
                      jax.ShapeDtypeStruct((B, HEADS, 2, 1, LANE), F32)]
        out_specs += [pl.BlockSpec((None, 2, None, HEAD_DIM, HEAD_DIM), lambda b, h: (b, 0, h, 0, 0)),
                      vec_spec, vec_spec]
    seq = pltpu.VMEM((L, LANE), F32)
    return pl.pallas_call(
        functools.partial(_mlstm_kernel, L=L, has_state=has_state),
        out_shape=out_shape,
        grid=(B, HEADS),
        in_specs=in_specs,
        out_specs=out_specs,
        scratch_shapes=[seq, seq, pltpu.VMEM((2, HEAD_DIM, HEAD_DIM), F32), pltpu.VMEM((2, 1, LANE), F32),
                        pltpu.VMEM((2, 1, LANE), F32)],
        compiler_params=_params("parallel", "parallel"),
        name="mlstm",
    )(*args)


def _merge_kernel(x_ref, g0_ref, sh1_ref, sc1_ref, wg_ref, bg_ref, a_ref, b_ref, c_ref, d_ref, wb_ref, wo_ref,
                  g1_ref, gate1_ref, g2_ref, sh2_ref, sc2_ref, rw_ref, rb_ref, xo_ref, h2_ref, lg_ref):
    x_in = x_ref[...]
    h = (_rms(x_in) * g0_ref[...] * (1.0 + sc1_ref[0]) + sh1_ref[0]).astype(BF16)
    merged = None
    for i, br in enumerate((a_ref, b_ref, c_ref, d_ref)):
        cols = slice(i * D_MODEL, (i + 1) * D_MODEL)
        gate = _sigmoid(jnp.dot(h, wg_ref[:, cols], preferred_element_type=F32) + bg_ref[:, cols])
        term = gate * _dot(br[...], wb_ref[i])
        merged = term if merged is None else merged + term
    t = _rms(_dot(merged, wo_ref[...])) * g1_ref[...]
    x = x_in + gate1_ref[0] * t
    xo_ref[...] = x
    h2 = _rms(x) * g2_ref[...] * (1.0 + sc2_ref[0]) + sh2_ref[0]
    _store_row_tiles(h2_ref, h2)
    lg_ref[...] = _dot3(h2, rw_ref[...]) + rb_ref[...]


def _merge(x, g0, shift1, scale1, wg, bg, branches, wb, wo, g1, gate1, g2, shift2, scale2, rw, rb, *,
           rows_per_mod, tm):
    T = x.shape[0]
    tm = min(tm, rows_per_mod)
    row = lambda n: pl.BlockSpec((tm, n), lambda i: (i, 0))
    const = lambda *shape: pl.BlockSpec(shape, lambda i: (0,) * len(shape))
    mod = pl.BlockSpec((1, 1, D_MODEL), lambda i: ((i * tm) // rows_per_mod, 0, 0))
    return pl.pallas_call(
        _merge_kernel,
        out_shape=[jax.ShapeDtypeStruct((T, D_MODEL), F32), jax.ShapeDtypeStruct((T * ROW_TILE, LANE), F32),
                   jax.ShapeDtypeStruct((T, LANE), F32)],
        grid=(T // tm,),
        in_specs=[row(D_MODEL), const(1, D_MODEL), mod, mod, const(D_MODEL, N_BRANCH * D_MODEL),
                  const(1, N_BRANCH * D_MODEL)] + [row(BRANCH_WIDTH)] * 4
                 + [const(N_BRANCH, BRANCH_WIDTH, D_MODEL), const(D_MODEL, D_MODEL), const(1, D_MODEL), mod,
                    const(1, D_MODEL), mod, mod, const(D_MODEL, LANE), const(1, LANE)],
        out_specs=[row(D_MODEL), pl.BlockSpec((tm * ROW_TILE, LANE), lambda i: (i, 0)), row(LANE)],
        compiler_params=_params("parallel"),
        name="merge",
    )(x, g0, shift1, scale1, wg, bg, *branches, wb, wo, g1, gate1, g2, shift2, scale2, rw, rb)


MOE_CHUNKS = 4
MOE_DUMP_ROWS = 2 * MOE_ROWS


def _moe_kernel(be_ref, nu_ref, idx_hbm, h2_hbm, wgu_f32, bgu_ref, wdn_f32, bdn_ref, out_hbm,
                idx_smem, x0, x1, y0, y1, wgu_ref, wdn_ref, sem_idx, sem_in, sem_out, sem_fill, sem_pace):
    i = pl.program_id(0)
    n_used = nu_ref[0]
    n_blocks = pl.num_programs(0)
    xbuf, ybuf = (x0, x1), (y0, y1)

    @pl.when((i < n_used) & ((i == 0) | (be_ref[i] != be_ref[jnp.maximum(i - 1, 0)])))
    def _():
        wgu_ref[...] = wgu_f32[...].astype(BF16)
        wdn_ref[...] = wdn_f32[...].astype(BF16)
    n_tokens_rows = out_hbm.shape[0] - MOE_DUMP_ROWS * ROW_TILE
    q_prev, q_cur, q_next = (i + 2) % 3, i % 3, (i + 1) % 3

    def idx_copy(block, s):
        return pltpu.make_async_copy(idx_hbm.at[block], idx_smem.at[s], sem_idx.at[s])

    def tile(ref, first_row):
        if not isinstance(first_row, int):
            first_row = pl.multiple_of(first_row, ROW_TILE)
        return ref.at[pl.ds(first_row, ROW_TILE), :]

    def row_in(s, r, tok_row):
        return pltpu.make_async_copy(tile(h2_hbm, tok_row), tile(xbuf[s], r * ROW_TILE), sem_in.at[s])

    def row_out(s, r, dst_row):
        return pltpu.make_async_copy(tile(ybuf[s], r * ROW_TILE), tile(out_hbm, dst_row), sem_out.at[s])

    def fill(half):
        rows = MOE_ROWS * ROW_TILE
        return pltpu.make_async_copy(y1, out_hbm.at[pl.ds(n_tokens_rows + half * rows, rows), :], sem_fill)

    def wait_rows(copy, s):
        for _ in range(MOE_ROWS):
            copy(s, 0, 0).wait()

    @pl.when(i == 0)
    def _():
        idx_copy(0, 0).start()
        idx_copy(n_blocks, 2).start()
        y1[...] = jnp.zeros_like(y1)
        fill(0).start()
        fill(1).start()
        idx_copy(0, 0).wait()
        idx_copy(n_blocks, 2).wait()

        def body(r, carry):
            row_in(0, r, idx_smem[0, 0, r]).start()
            return carry
        lax.fori_loop(0, MOE_ROWS, body, 0, unroll=8)
        fill(0).wait()
        fill(1).wait()

    def step(slot):
        other = 1 - slot
        nxt = jnp.minimum(i + 1, n_blocks - 1)
        idx_copy(nxt, q_next).start()
        wait_rows(row_in, slot)
        idx_copy(nxt, q_next).wait()

        @pl.when(i >= 1)
        def _():
            wait_rows(row_out, slot)

        x = _load_row_tiles(xbuf[slot], MOE_ROWS, BF16)
        y = jnp.broadcast_to(bdn_ref[...], (MOE_ROWS, D_MODEL))
        cols = D_EXPERT // MOE_CHUNKS
        rows = MOE_ROWS // MOE_CHUNKS
        for c in range(MOE_CHUNKS):
            if c:
                pl.semaphore_signal(sem_pace, 1)
                pl.semaphore_wait(sem_pace, 1)
            for r in range(c * rows, (c + 1) * rows):
                row_in(other, r, idx_smem[q_next, 0, r]).start()
                row_out(other, r, idx_smem[q_prev, 1, r]).start()
            g_cols = slice(c * cols, (c + 1) * cols)
            u_cols = slice(D_EXPERT + c * cols, D_EXPERT + (c + 1) * cols)
            hg = jnp.minimum(_dot(x, wgu_ref[:, g_cols]) + bgu_ref[:, g_cols], SWIGLU_LIMIT)
            hu = jnp.clip(_dot(x, wgu_ref[:, u_cols]) + bgu_ref[:, u_cols], -SWIGLU_LIMIT, SWIGLU_LIMIT)
            act = (hu + 1.0) * (hg * _sigmoid(SWIGLU_ALPHA * hg))
            y = y + _dot(act, wdn_ref[g_cols, :])
        _store_row_tiles(ybuf[slot], y)

        @pl.when(i == n_used - 1)
        def _():
            for r in range(MOE_ROWS):
                row_out(slot, r, idx_smem[q_cur, 1, r]).start()
            wait_rows(row_out, other)
            wait_rows(row_out, slot)
            wait_rows(row_in, other)

    for parity in (0, 1):
        pl.when((i < n_used) & (i % 2 == parity))(functools.partial(step, parity))


def _moe_blocks(idx, h2, block_e, n_used, layer, w_gu, b_gu, w_dn, b_dn, n_out_rows):
    n_blocks = idx.shape[0] - 1
    exp = lambda i, be, nu: (be[jnp.minimum(i, nu[0] - 1)], 0, 0)
    exp_w = lambda i, be, nu: (layer, be[jnp.minimum(i, nu[0] - 1)], 0, 0)
    any_spec = pl.BlockSpec(memory_space=pl.ANY)
    return pl.pallas_call(
        _moe_kernel,
        out_shape=jax.ShapeDtypeStruct(((n_out_rows + MOE_DUMP_ROWS) * ROW_TILE, LANE), F32),
        grid_spec=pltpu.PrefetchScalarGridSpec(
            num_scalar_prefetch=2,
            grid=(n_blocks,),
            in_specs=[
                any_spec,
                any_spec,
                pl.BlockSpec((None, None, D_MODEL, 2 * D_EXPERT), exp_w),
                pl.BlockSpec((None, 1, 2 * D_EXPERT), exp),
                pl.BlockSpec((None, None, D_EXPERT, D_MODEL), exp_w),
                pl.BlockSpec((None, 1, D_MODEL), exp),
            ],
            out_specs=any_spec,
            scratch_shapes=[pltpu.SMEM((3, 2, MOE_ROWS), jnp.int32)]
                           + [pltpu.VMEM((MOE_ROWS * ROW_TILE, LANE), F32)] * 4
                           + [pltpu.VMEM((D_MODEL, 2 * D_EXPERT), BF16), pltpu.VMEM((D_EXPERT, D_MODEL), BF16)]
                           + [pltpu.SemaphoreType.DMA((3,)),
                              pltpu.SemaphoreType.DMA((2,)),
                              pltpu.SemaphoreType.DMA((2,)),
                              pltpu.SemaphoreType.DMA,
                              pltpu.SemaphoreType.REGULAR],
        ),
        compiler_params=_params("arbitrary"),
        name="moe_experts",
    )(block_e, n_used, idx, h2, w_gu, b_gu, w_dn, b_dn)


def _moe(h2, logits, layer, w_gu, b_gu, w_dn, b_dn):
    N = logits.shape[0]
    NK = N * TOP_K
    top_val, top_idx = lax.top_k(logits, TOP_K)
    gate = jax.nn.softmax(top_val, axis=-1)
    flat_e = top_idx.reshape(-1)
    order = jnp.argsort(flat_e).astype(jnp.int32)
    counts = jnp.sum((flat_e[:, None] == jnp.arange(N_EXPERTS)[None, :]).astype(jnp.int32), axis=0)
    padded = (counts + MOE_ROWS - 1) // MOE_ROWS * MOE_ROWS
    pad_end = jnp.cumsum(padded)
    pad_start = pad_end - padded
    start = jnp.cumsum(counts) - counts
    n_blocks = (NK + N_EXPERTS * (MOE_ROWS - 1) + MOE_ROWS - 1) // MOE_ROWS
    P = n_blocks * MOE_ROWS
    block_first = jnp.arange(n_blocks, dtype=jnp.int32) * MOE_ROWS
    block_e = jnp.minimum(jnp.sum((block_first[:, None] >= pad_end[None, :]).astype(jnp.int32), axis=1),
                          N_EXPERTS - 1)
    n_used = (pad_end[-1] // MOE_ROWS).astype(jnp.int32).reshape(1)
    pos = jnp.arange(P, dtype=jnp.int32)
    e_pos = jnp.repeat(block_e, MOE_ROWS)
    rank = pos - pad_start[e_pos].astype(jnp.int32)
    valid = rank < counts[e_pos]
    flat = order[jnp.clip(start[e_pos].astype(jnp.int32) + rank, 0, NK - 1)]
    tok = jnp.where(valid, flat // TOP_K, 0)
    spare = NK + (pos // MOE_ROWS % 2) * MOE_ROWS + pos % MOE_ROWS
    dst = jnp.where(valid, (flat % TOP_K) * N + flat // TOP_K, spare)
    idx = jnp.stack([tok.reshape(n_blocks, MOE_ROWS), dst.reshape(n_blocks, MOE_ROWS)], axis=1)
    stand_in = jnp.stack([jnp.zeros((MOE_ROWS,), jnp.int32), NK + MOE_ROWS + jnp.arange(MOE_ROWS, dtype=jnp.int32)])
    idx = jnp.concatenate([idx, stand_in[None]], axis=0) * ROW_TILE
    expert_out = _moe_blocks(idx, h2, block_e, n_used, layer, w_gu, b_gu, w_dn, b_dn, NK)
    return expert_out, gate


def _combine_kernel(x_ref, y0_ref, y1_ref, y2_ref, y3_ref, w_ref, g_ref, gate_ref, o_ref):
    w = w_ref[...]
    y = None
    for k, y_ref in enumerate((y0_ref, y1_ref, y2_ref, y3_ref)):
        term = _load_row_tiles(y_ref, w.shape[0]) * w[:, k:k + 1]
        y = term if y is None else y + term
    o_ref[...] = x_ref[...] + gate_ref[0] * (_rms(y) * g_ref[...])


def _combine(x, expert_out, weights, g, gate, *, rows_per_mod, tm):
    T = x.shape[0]
    tm = min(tm, rows_per_mod)
    row = pl.BlockSpec((tm, D_MODEL), lambda i: (i, 0))
    slabs = [pl.BlockSpec((tm * ROW_TILE, LANE), lambda i, k=k: (k * (T // tm) + i, 0)) for k in range(TOP_K)]
    return pl.pallas_call(
        _combine_kernel,
        out_shape=jax.ShapeDtypeStruct((T, D_MODEL), F32),
        grid=(T // tm,),
        in_specs=[row] + slabs + [pl.BlockSpec((tm, TOP_K), lambda i: (i, 0)),
                                  pl.BlockSpec((1, D_MODEL), lambda i: (0, 0)),
                                  pl.BlockSpec((1, 1, D_MODEL), lambda i: ((i * tm) // rows_per_mod, 0, 0))],
        out_specs=row,
        compiler_params=_params("parallel"),
        name="combine",
    )(x, *[expert_out] * TOP_K, weights, g, gate)


def _rope_tables(L):
    t = jnp.arange(L)
    lane = jnp.arange(LANE)
    axis = (lane % A_DIM) // 32
    half = (lane % 32) // 16
    n_freq = A_DIM // 4
    inv = ROPE_BASE ** (-(lane % n_freq).astype(F32) / n_freq)
    pos = jnp.where(axis[None, :] == 0, (t // GRID_W)[:, None], (t % GRID_W)[:, None]).astype(F32)
    ang = pos * inv[None, :]
    return jnp.cos(ang), jnp.where(half[None, :] == 0, -jnp.sin(ang), jnp.sin(ang))


def _arrange_w_in(w_in):
    sizes = [512] * 8 + [8, 8] + [512] * 4 + [8, 8]
    offs = [0]
    for s in sizes:
        offs.append(offs[-1] + s)
    part = lambda i: w_in[:, offs[i]:offs[i + 1]]
    big = [part(i) for i in (0, 1, 2, 3, 4, 5, 6, 7, 10, 11, 12, 13)]
    small = jnp.zeros((w_in.shape[0], HEADS, LANE), w_in.dtype)
    for base, i in ((SM_BETA, 8), (SM_DEC, 9), (SM_IG, 14), (SM_FG, 15)):
        cols = part(i).reshape(-1, 2, HEADS)
        for d in range(2):
            small = small.at[:, :, base + d].set(cols[:, d, :])
    return jnp.concatenate(big + [small.reshape(w_in.shape[0], HEADS * LANE)], axis=1)


def _layer(x, cond, lidx, lp, *, B, L, cache):
    T = B * L
    is_ctx = cache is None
    mod = (jax.nn.silu(cond) @ lp['ada_w'] + lp['ada_b']).reshape(-1, 6, 1, D_MODEL)
    shift1, scale1, gate1, shift2, scale2, gate2 = [mod[:, i] for i in range(6)]
    rows_per_mod = T if mod.shape[0] == 1 else L
    g = lp['norm_g']

    tm = 1024
    P = _norm_mod_matmul(x, g[0:1], shift1, scale1, lp['w_in'], rows_per_mod=rows_per_mod, tm=tm,
                         tn=N_PROJ // 4)

    lam_init = 0.8 - 0.6 * math.exp(-0.3 * lidx)
    lq1, lk1, lq2, lk2 = lp['attn_lambda']
    lam = (jnp.exp(jnp.sum(lq1 * lk1)) - jnp.exp(jnp.sum(lq2 * lk2)) + lam_init).reshape(1, 1)
    attn_norm = lp['attn_norm'][None]
    if is_ctx:
        br_a, new_k, new_v = _attention(P, lam, attn_norm, B=B, L=L, tq=L, lam_init=lam_init)
    else:
        (br_a,) = _attention(P, lam, attn_norm, B=B, L=L, tq=256, lam_init=lam_init, cache=cache['attn'])

    br_b = _pool(P, lp['pool_w'], lp['pool_b'][:, None], lp['pool_scale'][None], B=B, L=L)

    gdn_args = (P, lp['gdn_A_log'], lp['gdn_dt_bias'], lp['gdn_conv'], lp['gdn_norm'][None])
    ml_args = (P, lp['mlstm_bias_i'], lp['mlstm_bias_f'], lp['mlstm_norm'][None])
    if is_ctx:
        br_c, new_gdn = _gdn(*gdn_args, B=B, L=L)
        br_d, new_c, new_n, new_m = _mlstm(*ml_args, B=B, L=L)
        ctx_out = (new_k, new_v, new_gdn, new_c,
                   new_n[:, :, :, 0, :HEAD_DIM].transpose(0, 2, 1, 3), new_m[:, :, :, 0, 0].transpose(0, 2, 1))
    else:
        (br_c,) = _gdn(*gdn_args, B=B, L=L, state=cache['gdn'])
        (br_d,) = _mlstm(*ml_args, B=B, L=L, state=cache['mlstm'])
        ctx_out = None

    x, h2, logits = _merge(x, g[0:1], shift1, scale1, lp['w_bgate'], lp['b_bgate'][None],
                           (br_a, br_b, br_c, br_d), lp['w_branch'], lp['w_out'], g[1:2], gate1,
                           g[2:3], shift2, scale2, lp['router_w'], lp['router_b'],
                           rows_per_mod=rows_per_mod, tm=256)
    expert_out, weights = _moe(h2, logits[:, :N_EXPERTS], lidx, lp['moe_w_gu'], lp['moe_b_gu'], lp['moe_w_dn'],
                               lp['moe_b_dn'])
    x = _combine(x, expert_out, weights, g[3:4], gate2, rows_per_mod=rows_per_mod, tm=512)
    return x, ctx_out


def kernel(x_prompt, x_sample, cache_attn_k, cache_attn_v, state_gdn, state_mlstm_C, state_mlstm_n, state_mlstm_m, c, c_ctx, ada_w, ada_b, norm_g, w_in, w_bgate, b_bgate, w_branch, w_out, attn_lambda, attn_norm, pool_w, pool_b, pool_scale, gdn_conv, gdn_A_log, gdn_dt_bias, gdn_norm, mlstm_bias_i, mlstm_bias_f, mlstm_norm, router_w, router_b, moe_w_gu, moe_b_gu, moe_w_dn, moe_b_dn):
    Bp, Lp, _ = x_prompt.shape
    Bs, Ls, _ = x_sample.shape
    y_p = x_prompt.reshape(Bp * Lp, D_MODEL)
    y_s = x_sample.reshape(Bs * Ls, D_MODEL)
    cos, sin = _rope_tables(Ls)
    n0 = state_mlstm_n.transpose(1, 0, 3, 2, 4)[:, :, :, :, None, :]
    m0 = jnp.broadcast_to(state_mlstm_m.transpose(1, 0, 3, 2)[..., None, None], n0.shape)
    ctx_states = []
    for l in range(DEPTH):
        lp = {
            'ada_w': ada_w[l], 'ada_b': ada_b[l], 'norm_g': norm_g[l],
            'w_in': _arrange_w_in(w_in[l].astype(BF16)), 'w_bgate': w_bgate[l].astype(BF16),
            'b_bgate': b_bgate[l], 'w_branch': w_branch[l].astype(BF16), 'w_out': w_out[l].astype(BF16),
            'attn_lambda': attn_lambda[l], 'attn_norm': attn_norm[l],
            'pool_w': pool_w[l], 'pool_b': pool_b[l], 'pool_scale': pool_scale[l],
            'gdn_conv': gdn_conv[l], 'gdn_A_log': gdn_A_log[l], 'gdn_dt_bias': gdn_dt_bias[l],
            'gdn_norm': gdn_norm[l], 'mlstm_bias_i': mlstm_bias_i[l], 'mlstm_bias_f': mlstm_bias_f[l],
            'mlstm_norm': mlstm_norm[l],
            'router_w': jnp.pad(router_w[l], ((0, 0), (0, LANE - N_EXPERTS))),
            'router_b': jnp.pad(router_b[l], (0, LANE - N_EXPERTS))[None],
            'moe_w_gu': moe_w_gu, 'moe_b_gu': moe_b_gu[l][:, None],
            'moe_w_dn': moe_w_dn, 'moe_b_dn': moe_b_dn[l][:, None],
        }
        y_p, st = _layer(y_p, c_ctx[None], l, lp, B=Bp, L=Lp, cache=None)
        ctx_states.append(st)
        cache = {
            'attn': {'k': cache_attn_k, 'v': cache_attn_v, 'layer': l, 'cos': cos, 'sin': sin},
            'gdn': (state_gdn, l),
            'mlstm': (state_mlstm_C, l, n0[l], m0[l]),
        }
        y_s, _ = _layer(y_s, c, l, lp, B=Bs, L=Ls, cache=cache)
    outs = [jnp.stack([s[i] for s in ctx_states], axis=1) for i in range(6)]
    return (y_p.reshape(Bp, Lp, D_MODEL), y_s.reshape(Bs, Ls, D_MODEL), *outs)
```

```python
import functools
import math

import jax
import jax.numpy as jnp
from jax import lax
from jax.experimental import pallas as pl
from jax.experimental.pallas import tpu as pltpu

F32 = jnp.float32
BF16 = jnp.bfloat16

D_MODEL = 1024
DEPTH = 2
GRID_W = 64
EPS = 1e-6
LANE = 128
HEADS = 4
HEAD_DIM = 128
A_DIM = 64
ROPE_BASE = 10000.0
POOL_WINDOWS = (2, 4, 8, 16)
CHUNK = 64
N_BRANCH = 4
BRANCH_WIDTH = 512
N_EXPERTS = 32
TOP_K = 4
D_EXPERT = 1024
SWIGLU_LIMIT = 7.0
SWIGLU_ALPHA = 1.702
MOE_ROWS = 256
VMEM_LIMIT = 56 * 1024 * 1024

COL_AQ, COL_AK, COL_AV, COL_POOL, COL_CQ, COL_CK, COL_CV, COL_CG, COL_MQ, COL_MK, COL_MV, COL_MG, COL_SM = range(13)
N_PROJ = 13 * BRANCH_WIDTH
SM_BETA, SM_DEC, SM_IG, SM_FG = 0, 2, 4, 6


def _params(*sem):
    return pltpu.CompilerParams(dimension_semantics=sem, vmem_limit_bytes=VMEM_LIMIT)


def _dot(a, b):
    return jnp.dot(a.astype(BF16), b.astype(BF16), preferred_element_type=F32)


def _dot_nt(a, b):
    return lax.dot_general(a.astype(BF16), b.astype(BF16), (((1,), (1,)), ((), ())),
                           preferred_element_type=F32)


def _dot_tn(a, b):
    return lax.dot_general(a.astype(BF16), b.astype(BF16), (((0,), (0,)), ((), ())),
                           preferred_element_type=F32)


def _split(a):
    hi = a.astype(BF16)
    lo = (a - hi.astype(F32)).astype(BF16)
    return hi, lo


def _dot3(a, b):
    ah, al = _split(a)
    bh, bl = _split(b)
    d = lambda x, y: jnp.dot(x, y, preferred_element_type=F32)
    return d(ah, bh) + (d(ah, bl) + d(al, bh))


def _dot_mask(mask, b):
    m = jnp.where(mask, 1.0, 0.0).astype(BF16)
    b0 = b.astype(BF16)
    r1 = b - b0.astype(F32)
    b1 = r1.astype(BF16)
    b2 = (r1 - b1.astype(F32)).astype(BF16)
    d = lambda y: jnp.dot(m, y, preferred_element_type=F32)
    return d(b0) + (d(b1) + d(b2))


ROW_TILE = 8
assert ROW_TILE * LANE == D_MODEL


def _store_row_tiles(ref, x):
    rows = x.shape[0]
    for c in range(ROW_TILE):
        ref[pl.ds(c, rows, stride=ROW_TILE), :] = x[:, c * LANE:(c + 1) * LANE]


def _load_row_tiles(ref, rows, dtype=F32):
    return jnp.concatenate([ref[pl.ds(c, rows, stride=ROW_TILE), :].astype(dtype) for c in range(ROW_TILE)],
                           axis=1)


def _rms(x):
    return x * lax.rsqrt(jnp.mean(x * x, axis=-1, keepdims=True) + EPS)


def _sigmoid(x):
    return 1.0 / (1.0 + jnp.exp(-x))


def _softplus(x):
    return jnp.maximum(x, 0.0) + jnp.log(1.0 + jnp.exp(-jnp.abs(x)))


def _proj_kernel(x_ref, g_ref, sh_ref, sc_ref, w_ref, o_ref, h_scr):
    @pl.when(pl.program_id(1) == 0)
    def _():
        y = _rms(x_ref[...]) * g_ref[...]
        h_scr[...] = (y * (1.0 + sc_ref[0]) + sh_ref[0]).astype(BF16)

    o_ref[...] = jnp.dot(h_scr[...], w_ref[...], preferred_element_type=F32)


def _norm_mod_matmul(x, g, shift, scale, w, *, rows_per_mod, tm, tn):
    T, Dm = x.shape
    N = w.shape[1]
    tm = min(tm, rows_per_mod)
    mod_idx = lambda i, j: ((i * tm) // rows_per_mod, 0, 0)
    return pl.pallas_call(
        _proj_kernel,
        out_shape=jax.ShapeDtypeStruct((T, N), F32),
        grid=(T // tm, N // tn),
        in_specs=[
            pl.BlockSpec((tm, Dm), lambda i, j: (i, 0)),
            pl.BlockSpec((1, Dm), lambda i, j: (0, 0)),
            pl.BlockSpec((1, 1, Dm), mod_idx),
            pl.BlockSpec((1, 1, Dm), mod_idx),
            pl.BlockSpec((Dm, tn), lambda i, j: (0, j)),
        ],
        out_specs=pl.BlockSpec((tm, tn), lambda i, j: (i, j)),
        scratch_shapes=[pltpu.VMEM((tm, Dm), BF16)],
        compiler_params=_params("parallel", "arbitrary"),
        name="proj",
    )(x, g, shift, scale, w)


def _rope(x, cos, sin):
    lane = lax.broadcasted_iota(jnp.int32, x.shape, 1)
    first = (lane % 32) < 16
    partner = jnp.where(first, pltpu.roll(x, LANE - 16, 1), pltpu.roll(x, 16, 1))
    return x * cos + partner * sin


def _attn_kernel(*refs, has_cache, n_ctx, out_scale):
    if has_cache:
        (lam_ref, q_ref, k_ref, v_ref, ck_ref, cv_ref, cosq_ref, sinq_ref, cosk_ref, sink_ref, nrm_ref,
         o_ref, kall, vall) = refs
    else:
        lam_ref, q_ref, k_ref, v_ref, nrm_ref, o_ref, ko_ref, vo_ref, kall, vall = refs

    @pl.when(pl.program_id(2) == 0)
    def _():
        k = k_ref[...]
        v = v_ref[...]
        if has_cache:
            kall[:, pl.ds(0, n_ctx)] = ck_ref[...].T.astype(BF16)
            vall[pl.ds(0, n_ctx), :] = cv_ref[...].astype(BF16)
            k = _rope(k, cosk_ref[...], sink_ref[...])
        else:
            ko_ref[...] = k
            vo_ref[...] = v
        kall[:, pl.ds(n_ctx, k.shape[0])] = k.T.astype(BF16)
        vall[pl.ds(n_ctx, k.shape[0]), :] = v.astype(BF16)

    q = q_ref[...]
    if has_cache:
        q = _rope(q, cosq_ref[...], sinq_ref[...])
    q = q * (A_DIM ** -0.5 * math.log2(math.e))
    lane = lax.broadcasted_iota(jnp.int32, q.shape, 1)
    q1 = jnp.where(lane < A_DIM, q, 0.0)
    q2 = jnp.where(lane >= A_DIM, q, 0.0)
    keys = kall[...]
    lam = lam_ref[0, 0]

    def probs(qm):
        s = _dot(qm, keys)
        p = jnp.exp2(s - jnp.max(s, axis=-1, keepdims=True))
        return p, jnp.sum(p, axis=-1, keepdims=True)

    p1, l1 = probs(q1)
    p2, l2 = probs(q2)
    a = p1 * (1.0 / l1) - p2 * (lam / l2)
    o = jnp.dot(a.astype(BF16), vall[...], preferred_element_type=F32)
    o_ref[...] = _rms(o) * nrm_ref[...] * out_scale


def _attention(P, lam, norm, *, B, L, tq, lam_init, cache=None):
    T = B * L
    nq = L // tq
    has_cache = cache is not None
    n_ctx = cache["k"].shape[3] if has_cache else 0
    colq, colk, colv = COL_AQ * HEADS, COL_AK * HEADS, COL_AV * HEADS
    in_specs = [
        pl.BlockSpec(memory_space=pltpu.SMEM),
        pl.BlockSpec((tq, LANE), lambda b, h, i: (b * nq + i, colq + h)),
        pl.BlockSpec((L, LANE), lambda b, h, i: (b, colk + h)),
        pl.BlockSpec((L, LANE), lambda b, h, i: (b, colv + h)),
    ]
    args = [lam, P, P, P]
    if has_cache:
        l = cache["layer"]
        cspec = pl.BlockSpec((None, None, None, n_ctx, LANE), lambda b, h, i: (b, l, h, 0, 0))
        in_specs += [cspec, cspec,
                     pl.BlockSpec((tq, LANE), lambda b, h, i: (i, 0)),
                     pl.BlockSpec((tq, LANE), lambda b, h, i: (i, 0)),
                     pl.BlockSpec((L, LANE), lambda b, h, i: (0, 0)),
                     pl.BlockSpec((L, LANE), lambda b, h, i: (0, 0))]
        args += [cache["k"], cache["v"], cache["cos"], cache["sin"], cache["cos"], cache["sin"]]
    in_specs.append(pl.BlockSpec((1, LANE), lambda b, h, i: (0, 0)))
    args.append(norm)
    out_shape = [jax.ShapeDtypeStruct((T, HEADS * LANE), F32)]
    out_specs = [pl.BlockSpec((tq, LANE), lambda b, h, i: (b * nq + i, h))]
    if not has_cache:
        kv_shape = jax.ShapeDtypeStruct((B, HEADS, L, LANE), F32)
        kv_spec = pl.BlockSpec((None, None, L, LANE), lambda b, h, i: (b, h, 0, 0))
        out_shape += [kv_shape, kv_shape]
        out_specs += [kv_spec, kv_spec]
    return pl.pallas_call(
        functools.partial(_attn_kernel, has_cache=has_cache, n_ctx=n_ctx, out_scale=1.0 - lam_init),
        out_shape=out_shape,
        grid=(B, HEADS, nq),
        in_specs=in_specs,
        out_specs=out_specs,
        scratch_shapes=[pltpu.VMEM((LANE, n_ctx + L), BF16), pltpu.VMEM((n_ctx + L, LANE), BF16)],
        compiler_params=_params("parallel", "parallel", "arbitrary"),
        name="diff_attention",
    )(*args)


POOL_PAD = 16


def _pool_kernel(x_ref, w_ref, b_ref, s_ref, o_ref, pad, *, L):
    zeros = jnp.zeros((POOL_PAD, LANE), F32)
    pad[pl.ds(0, POOL_PAD), :] = zeros
    pad[pl.ds(POOL_PAD + L, POOL_PAD), :] = zeros
    x = x_ref[...]
    pad[pl.ds(POOL_PAD, L), :] = x
    t = lax.broadcasted_iota(jnp.int32, (L, LANE), 0)
    g = pl.program_id(1)
    for gi, win in enumerate(POOL_WINDOWS):
        @pl.when(g == gi)
        def _(win=win):
            half = win // 2
            acc = pad[pl.ds(POOL_PAD - half, L), :]
            for k in range(1 - half, half):
                acc = acc + pad[pl.ds(POOL_PAD + k, L), :]
            cnt = (jnp.minimum(t + half, L) - jnp.maximum(t - half, 0)).astype(F32)
            pooled = acc / cnt - x
            o_ref[...] = (_dot(pooled, w_ref[...]) + b_ref[...]) * s_ref[...]


def _pool(P, w, b, scale, *, B, L):
    T = B * L
    G = len(POOL_WINDOWS)
    return pl.pallas_call(
        functools.partial(_pool_kernel, L=L),
        out_shape=jax.ShapeDtypeStruct((T, G * LANE), F32),
        grid=(B, G),
        in_specs=[
            pl.BlockSpec((L, LANE), lambda bi, g: (bi, COL_POOL * HEADS + g)),
            pl.BlockSpec((None, LANE, LANE), lambda bi, g: (g, 0, 0)),
            pl.BlockSpec((None, 1, LANE), lambda bi, g: (g, 0, 0)),
            pl.BlockSpec((1, LANE), lambda bi, g: (0, g)),
        ],
        out_specs=pl.BlockSpec((L, LANE), lambda bi, g: (bi, g)),
        scratch_shapes=[pltpu.VMEM((L + 2 * POOL_PAD, LANE), F32)],
        compiler_params=_params("parallel", "parallel"),
        name="pool_mixer",
    )(P, w, b, scale)


GROUP_CHUNKS = 2
GROUP = GROUP_CHUNKS * CHUNK
MAX_GROUPS_PER_STEP = 8


def _groups_per_step(L):
    return min(MAX_GROUPS_PER_STEP, L // GROUP)


def _group_masks(backward):
    row = lax.broadcasted_iota(jnp.int32, (GROUP, GROUP), 0)
    col = lax.broadcasted_iota(jnp.int32, (GROUP, GROUP), 1)
    same = (row // CHUNK) == (col // CHUNK)
    lower, upper = row >= col, row <= col
    if backward:
        lower, upper = upper, lower
    return dict(same=same, incl=same & lower, strict=same & lower & (row != col), incl_t=same & upper,
                eye=row == col)


INV_BASE = 8


def _block_masks():
    row = lax.broadcasted_iota(jnp.int32, (GROUP, GROUP), 0)
    col = lax.broadcasted_iota(jnp.int32, (GROUP, GROUP), 1)
    sizes = [INV_BASE << i for i in range(int(math.log2(CHUNK // INV_BASE)) + 1)]
    same = [(row // s) == (col // s) for s in sizes]
    return [same[0]] + [cur & jnp.logical_not(prev) for prev, cur in zip(same[:-1], same[1:])]


def _unit_triangular_inverses(ns, eye, blocks):
    base = [jnp.where(blocks[0], n, 0.0) for n in ns]
    invs = [eye - b for b in base]
    pws = [_dot(b, b) for b in base]
    rounds = int(math.log2(INV_BASE)) - 1
    for s in range(rounds):
        invs = [t + _dot(t, p) for t, p in zip(invs, pws)]
        if s + 1 < rounds:
            pws = [_dot(p, p) for p in pws]
    for join in blocks[1:]:
        offs = [jnp.where(join, n, 0.0) for n in ns]
        xs = [_dot(o, t) for o, t in zip(offs, invs)]
        invs = [t - _dot(t, x) for t, x in zip(invs, xs)]
    return invs


def _dot_mask2(mask, b):
    m = jnp.where(mask, 1.0, 0.0).astype(BF16)
    b0, b1 = _split(b)
    return jnp.dot(m, b0, preferred_element_type=F32) + jnp.dot(m, b1, preferred_element_type=F32)


def _per_chunk(x, reduce):
    parts = [reduce(x[c * CHUNK:(c + 1) * CHUNK], axis=0, keepdims=True) for c in range(GROUP_CHUNKS)]
    col = jnp.concatenate([jnp.broadcast_to(p, (CHUNK, 1)) for p in parts], axis=0)
    return parts, col


def _scan_order(step, n_steps, groups):
    base = (step * groups * GROUP, (n_steps - 1 - step) * groups * GROUP)
    fwd = [(gi, c) for gi in range(groups) for c in range(GROUP_CHUNKS)]
    return base, (fwd, fwd[::-1])


CONV_PAD = 8


def _gdn_kernel(*refs, L, has_state):
    alog_ref, dtb_ref, q_ref, k_ref, v_ref, gate_ref, sm_ref, cwq_ref, cwk_ref, cwv_ref, nrm_ref = refs[:11]
    refs = refs[11:]
    if has_state:
        s0_ref, o_ref, pad, qs, ks, vs, o_f, o_b, state = refs
    else:
        o_ref, sout_ref, pad, qs, ks, vs, o_f, o_b, state = refs
    h = pl.program_id(1)
    n_chunks = L // CHUNK

    zeros = jnp.zeros((CONV_PAD, LANE), F32)
    pad[pl.ds(0, CONV_PAD), :] = zeros
    pad[pl.ds(CONV_PAD + L, CONV_PAD), :] = zeros

    def conv_silu(x_ref, w_ref):
        pad[pl.ds(CONV_PAD, L), :] = x_ref[...]
        w = w_ref[...]
        y = (pad[pl.ds(CONV_PAD - 1, L), :] * w[0:1] + pad[pl.ds(CONV_PAD, L), :] * w[1:2]
             + pad[pl.ds(CONV_PAD + 1, L), :] * w[2:3])
        return y * _sigmoid(y)

    def l2n(x):
        return x * lax.rsqrt(jnp.sum(x * x, axis=-1, keepdims=True) + EPS)

    qs[...] = l2n(conv_silu(q_ref, cwq_ref)) * (HEAD_DIM ** -0.5)
    ks[...] = l2n(conv_silu(k_ref, cwk_ref))
    vs[...] = conv_silu(v_ref, cwv_ref)
    if has_state:
        state[...] = s0_ref[...]
    else:
        state[...] = jnp.zeros_like(state)

    groups = _groups_per_step(L)
    n_steps = L // (groups * GROUP)
    o_scr = (o_f, o_b)
    masks = (_group_masks(False), _group_masks(True))
    eye = jnp.where(masks[0]['eye'], 1.0, 0.0)
    blocks = _block_masks()
    items = [(d, gi) for gi in range(groups) for d in (0, 1)]
    chunks = [slice(c * CHUNK, (c + 1) * CHUNK) for c in range(GROUP_CHUNKS)]

    def body(step, carry):
        base, order = _scan_order(step, n_steps, groups)
        it = {}
        for key in items:
            d, gi = key
            rows = pl.ds(pl.multiple_of(base[d] + gi * GROUP, GROUP), GROUP)
            q, k, v, sm = qs[rows, :], ks[rows, :], vs[rows, :], sm_ref[rows, :]
            beta = _sigmoid(sm[:, SM_BETA + d:SM_BETA + d + 1])
            dec = sm[:, SM_DEC + d:SM_DEC + d + 1]
            g = -jnp.exp(alog_ref[d, h]) * _softplus(dec + dtb_ref[d, h])
            it[key] = dict(q=q, k=k, kb=k * beta, vb=v * beta, g=g, gb=jnp.broadcast_to(g, (GROUP, GROUP)))
        for (d, gi), x in it.items():
            m = masks[d]
            x['cum_i'] = _dot_mask2(m['incl'], x['gb'])
            x['cum_j'] = _dot_mask2(m['same'], jnp.where(m['incl_t'], x['gb'], 0.0))
            x['kk'] = _dot_nt(x['kb'], x['k'])
            x['qk'] = _dot_nt(x['q'], x['k'])
        for (d, gi), x in it.items():
            m = masks[d]
            decay = jnp.where(m['incl'], jnp.exp(jnp.where(m['incl'], x['cum_i'] - x['cum_j'], 0.0)), 0.0)
            x['gc'] = x['cum_i'][:, 0:1]
            x['tot'], x['tot_col'] = _per_chunk(x['g'], jnp.sum)
            x['qk'] = x['qk'] * decay
            x['n'] = jnp.where(m['strict'], x['kk'] * decay, 0.0)
        invs = _unit_triangular_inverses([x['n'] for x in it.values()], eye, blocks)
        for x, inv in zip(it.values(), invs):
            x['inv'] = inv
            x['egc'] = jnp.exp(x['gc'])
            x['sol'] = _dot3(x['inv'], jnp.concatenate([x['vb'], x['kb'] * x['egc']], axis=-1))
        for x in it.values():
            u, w = x['sol'][:, :HEAD_DIM], x['sol'][:, HEAD_DIM:]
            x['qp'] = x['q'] * x['egc'] - _dot(x['qk'], w)
            x['op'] = _dot(x['qk'], u)
            kd = x['k'] * jnp.exp(x['tot_col'] - x['gc'])
            x['ab'] = [_dot_tn(kd[cs], x['sol'][cs]) for cs in chunks]
            x['gl'] = [jnp.exp(t) for t in x['tot']]
        S = [state[0], state[1]]
        for stp in range(len(order[0])):
            for d in (0, 1):
                gi, c = order[d][stp]
                x = it[(d, gi)]
                ab = x['ab'][c]
                r = _dot(jnp.concatenate([ab[:, HEAD_DIM:], x['qp'][chunks[c]]], axis=0), S[d])
                rows = pl.ds(pl.multiple_of(base[d] + gi * GROUP + c * CHUNK, CHUNK), CHUNK)
                o_scr[d][rows, :] = x['op'][chunks[c]] + r[HEAD_DIM:]
                S[d] = x['gl'][c] * S[d] + (ab[:, :HEAD_DIM] - r[:HEAD_DIM])
        state[0] = S[0]
        state[1] = S[1]
        return carry

    lax.fori_loop(0, n_steps, body, 0)
    gate = gate_ref[...]
    o_ref[...] = _rms(o_f[...] + o_b[...]) * nrm_ref[...] * (gate * _sigmoid(gate))
    if not has_state:
        sout_ref[...] = state[...]


def _col_spec(L, col):
    return pl.BlockSpec((L, LANE), lambda b, h: (b, col * HEADS + h))


def _gdn(P, a_log, dt_bias, conv_w, norm, *, B, L, state=None):
    T = B * L
    has_state = state is not None
    smem = pl.BlockSpec(memory_space=pltpu.SMEM)
    in_specs = [smem, smem, _col_spec(L, COL_CQ), _col_spec(L, COL_CK), _col_spec(L, COL_CV),
                _col_spec(L, COL_CG), _col_spec(L, COL_SM)]
    in_specs += [pl.BlockSpec((3, LANE), lambda b, h, j=j: (0, j * HEADS + h)) for j in range(3)]
    in_specs.append(pl.BlockSpec((1, LANE), lambda b, h: (0, 0)))
    args = [a_log, dt_bias, P, P, P, P, P, conv_w, conv_w, conv_w, norm]
    out_shape = [jax.ShapeDtypeStruct((T, HEADS * LANE), F32)]
    out_specs = [pl.BlockSpec((L, LANE), lambda b, h: (b, h))]
    if has_state:
        arr, l = state
        in_specs.append(pl.BlockSpec((None, None, 2, None, HEAD_DIM, HEAD_DIM), lambda b, h: (b, l, 0, h, 0, 0)))
        args.append(arr)
    else:
        out_shape.append(jax.ShapeDtypeStruct((B, 2, HEADS, HEAD_DIM, HEAD_DIM), F32))
        out_specs.append(pl.BlockSpec((None, 2, None, HEAD_DIM, HEAD_DIM), lambda b, h: (b, 0, h, 0, 0)))
    seq = pltpu.VMEM((L, LANE), F32)
    return pl.pallas_call(
        functools.partial(_gdn_kernel, L=L, has_state=has_state),
        out_shape=out_shape,
        grid=(B, HEADS),
        in_specs=in_specs,
        out_specs=out_specs,
        scratch_shapes=[pltpu.VMEM((L + 2 * CONV_PAD, LANE), F32), seq, seq, seq, seq, seq,
                        pltpu.VMEM((2, HEAD_DIM, HEAD_DIM), F32)],
        compiler_params=_params("parallel", "parallel"),
        name="gated_deltanet",
    )(*args)


def _mlstm_kernel(*refs, L, has_state):
    bi_ref, bf_ref, q_ref, k_ref, v_ref, gate_ref, sm_ref, nrm_ref = refs[:8]
    refs = refs[8:]
    if has_state:
        c0_ref, n0_ref, m0_ref, o_ref, h_f, h_b, c_st, n_st, m_st = refs
    else:
        o_ref, cout_ref, nout_ref, mout_ref, h_f, h_b, c_st, n_st, m_st = refs
    h = pl.program_id(1)
    n_chunks = L // CHUNK

    if has_state:
        c_st[...] = c0_ref[...]
        n_st[...] = n0_ref[...]
        m_st[...] = m0_ref[...]
    else:
        c_st[...] = jnp.zeros_like(c_st)
        n_st[...] = jnp.zeros_like(n_st)
        m_st[...] = jnp.zeros_like(m_st)

    groups = _groups_per_step(L)
    n_steps = L // (groups * GROUP)
    h_scr = (h_f, h_b)
    masks = (_group_masks(False), _group_masks(True))
    items = [(d, gi) for gi in range(groups) for d in (0, 1)]
    chunks = [slice(c * CHUNK, (c + 1) * CHUNK) for c in range(GROUP_CHUNKS)]

    def body(step, carry):
        base, order = _scan_order(step, n_steps, groups)
        it = {}
        for key in items:
            d, gi = key
            rows = pl.ds(pl.multiple_of(base[d] + gi * GROUP, GROUP), GROUP)
            q, v, sm = q_ref[rows, :], v_ref[rows, :], sm_ref[rows, :]
            k = k_ref[rows, :] * (HEAD_DIM ** -0.5)
            ig = sm[:, SM_IG + d:SM_IG + d + 1] + bi_ref[d, h]
            fg = sm[:, SM_FG + d:SM_FG + d + 1] + bf_ref[d, h]
            lf = -_softplus(-fg)
            it[key] = dict(q=q, k=k, v=v, ig=ig, lf=lf, lfb=jnp.broadcast_to(lf, (GROUP, GROUP)),
                           igb=jnp.broadcast_to(ig, (GROUP, GROUP)))
        for (d, gi), x in it.items():
            m = masks[d]
            x['cum_i'] = _dot_mask2(m['incl'], x['lfb'])
            x['cum_j'] = _dot_mask2(m['same'], jnp.where(m['incl_t'], x['lfb'], 0.0)
                                    - jnp.where(m['eye'], x['igb'], 0.0))
            x['qk'] = _dot_nt(x['q'], x['k'])
        for (d, gi), x in it.items():
            m = masks[d]
            dm = x['cum_i'] - x['cum_j']
            x['b'] = x['cum_i'][:, 0:1]
            x['tot'], tot_col = _per_chunk(x['lf'], jnp.sum)
            x['m_intra'] = jnp.max(jnp.where(m['incl'], dm, -jnp.inf), axis=-1, keepdims=True)
            e = tot_col - x['b'] + x['ig']
            x['m_end'], m_end_col = _per_chunk(e, jnp.max)
            p = jnp.where(m['incl'], jnp.exp(jnp.where(m['incl'], dm - x['m_intra'], 0.0)), 0.0) * x['qk']
            x['p_sum'] = jnp.sum(p, axis=-1, keepdims=True)
            x['p'] = p
            x['kw'] = x['k'] * jnp.exp(e - m_end_col)
        for x in it.values():
            x['pv'] = _dot(x['p'], x['v'])
            x['kv'] = [_dot_tn(x['kw'][cs], x['v'][cs]) for cs in chunks]
            x['k_sum'] = [jnp.sum(x['kw'][cs], axis=0, keepdims=True) for cs in chunks]
        outs = []
        for d in (0, 1):
            C, n, m = c_st[d], n_st[d], m_st[d][:, 0:1]
            for gi, c in order[d]:
                x = it[(d, gi)]
                cs = chunks[c]
                m_t = jnp.maximum(x['b'][cs] + m, x['m_intra'][cs])
                w_inter = jnp.exp(x['b'][cs] + m - m_t)
                local = jnp.exp(x['m_intra'][cs] - m_t)
                den = (w_inter * jnp.sum(x['q'][cs] * n, axis=-1, keepdims=True) + local * x['p_sum'][cs])
                scale = 1.0 / jnp.maximum(jnp.abs(den), jnp.exp(-m_t))
                outs.append((d, gi, c, C, w_inter * scale, local * scale))
                m_new = jnp.maximum(x['tot'][c] + m, x['m_end'][c])
                carry_decay = jnp.exp(x['tot'][c] + m - m_new)
                local_new = jnp.exp(x['m_end'][c] - m_new)
                C = carry_decay * C + local_new * x['kv'][c]
                n = carry_decay * n + local_new * x['k_sum'][c]
                m = m_new
            c_st[d] = C
            n_st[d] = n
            m_st[d] = jnp.broadcast_to(m, (1, LANE))
        for d, gi, c, C, w_scale, p_scale in outs:
            x = it[(d, gi)]
            cs = chunks[c]
            rows = pl.ds(pl.multiple_of(base[d] + gi * GROUP + c * CHUNK, CHUNK), CHUNK)
            h_scr[d][rows, :] = w_scale * _dot(x['q'][cs], C) + p_scale * x['pv'][cs]
        return carry

    lax.fori_loop(0, n_steps, body, 0)
    o_ref[...] = _rms(h_f[...] + h_b[...]) * nrm_ref[...] * _sigmoid(gate_ref[...])
    if not has_state:
        cout_ref[...] = c_st[...]
        nout_ref[...] = n_st[...]
        mout_ref[...] = m_st[...]


def _mlstm(P, bias_i, bias_f, norm, *, B, L, state=None):
    T = B * L
    has_state = state is not None
    smem = pl.BlockSpec(memory_space=pltpu.SMEM)
    in_specs = [smem, smem, _col_spec(L, COL_MQ), _col_spec(L, COL_MK), _col_spec(L, COL_MV),
                _col_spec(L, COL_MG), _col_spec(L, COL_SM), pl.BlockSpec((1, LANE), lambda b, h: (0, 0))]
    args = [bias_i, bias_f, P, P, P, P, P, norm]
    out_shape = [jax.ShapeDtypeStruct((T, HEADS * LANE), F32)]
    out_specs = [pl.BlockSpec((L, LANE), lambda b, h: (b, h))]
    vec_spec = pl.BlockSpec((None, None, 2, 1, LANE), lambda b, h: (b, h, 0, 0, 0))
    if has_state:
        c_arr, l, n_arr, m_arr = state
        in_specs += [pl.BlockSpec((None, None, 2, None, HEAD_DIM, HEAD_DIM), lambda b, h: (b, l, 0, h, 0, 0)),
                     vec_spec, vec_spec]
        args += [c_arr, n_arr, m_arr]
    else:
        out_shape += [jax.ShapeDtypeStruct((B, 2, HEADS, HEAD_DIM, HEAD_DIM), F32),
                      jax.ShapeDtypeStruct((B, HEADS, 2, 1, LANE), F32),
                      jax.ShapeDtypeStruct((B, HEADS, 2, 1, LANE), F32)]
        out_specs += [pl.BlockSpec((None, 2, None, HEAD_DIM, HEAD_DIM), lambda b, h: (b, 0, h, 0, 0)),
                      vec_spec, vec_spec]
    seq = pltpu.VMEM((L, LANE), F32)
    return pl.pallas_call(
        functools.partial(_mlstm_kernel, L=L, has_state=has_state),
        out_shape=out_shape,
        grid=(B, HEADS),
        in_specs=in_specs,
        out_specs=out_specs,
        scratch_shapes=[seq, seq, pltpu.VMEM((2, HEAD_DIM, HEAD_DIM), F32), pltpu.VMEM((2, 1, LANE), F32),
                        pltpu.VMEM((2, 1, LANE), F32)],
        compiler_params=_params("parallel", "parallel"),
        name="mlstm",
    )(*args)


def _merge_kernel(x_ref, g0_ref, sh1_ref, sc1_ref, wg_ref, bg_ref, a_ref, b_ref, c_ref, d_ref, wb_ref, wo_ref,
                  g1_ref, gate1_ref, g2_ref, sh2_ref, sc2_ref, rw_ref, rb_ref, xo_ref, h2_ref, lg_ref):
    x_in = x_ref[...]
    h = (_rms(x_in) * g0_ref[...] * (1.0 + sc1_ref[0]) + sh1_ref[0]).astype(BF16)
    merged = None
    for i, br in enumerate((a_ref, b_ref, c_ref, d_ref)):
        cols = slice(i * D_MODEL, (i + 1) * D_MODEL)
        gate = _sigmoid(jnp.dot(h, wg_ref[:, cols], preferred_element_type=F32) + bg_ref[:, cols])
        term = gate * _dot(br[...], wb_ref[i])
        merged = term if merged is None else merged + term
    t = _rms(_dot(merged, wo_ref[...])) * g1_ref[...]
    x = x_in + gate1_ref[0] * t
    xo_ref[...] = x
    h2 = _rms(x) * g2_ref[...] * (1.0 + sc2_ref[0]) + sh2_ref[0]
    _store_row_tiles(h2_ref, h2)
    lg_ref[...] = _dot3(h2, rw_ref[...]) + rb_ref[...]


def _merge(x, g0, shift1, scale1, wg, bg, branches, wb, wo, g1, gate1, g2, shift2, scale2, rw, rb, *,
           rows_per_mod, tm):
    T = x.shape[0]
    tm = min(tm, rows_per_mod)
    row = lambda n: pl.BlockSpec((tm, n), lambda i: (i, 0))
    const = lambda *shape: pl.BlockSpec(shape, lambda i: (0,) * len(shape))
    mod = pl.BlockSpec((1, 1, D_MODEL), lambda i: ((i * tm) // rows_per_mod, 0, 0))
    return pl.pallas_call(
        _merge_kernel,
        out_shape=[jax.ShapeDtypeStruct((T, D_MODEL), F32), jax.ShapeDtypeStruct((T * ROW_TILE, LANE), F32),
                   jax.ShapeDtypeStruct((T, LANE), F32)],
        grid=(T // tm,),
        in_specs=[row(D_MODEL), const(1, D_MODEL), mod, mod, const(D_MODEL, N_BRANCH * D_MODEL),
                  const(1, N_BRANCH * D_MODEL)] + [row(BRANCH_WIDTH)] * 4
                 + [const(N_BRANCH, BRANCH_WIDTH, D_MODEL), const(D_MODEL, D_MODEL), const(1, D_MODEL), mod,
                    const(1, D_MODEL), mod, mod, const(D_MODEL, LANE), const(1, LANE)],
        out_specs=[row(D_MODEL), pl.BlockSpec((tm * ROW_TILE, LANE), lambda i: (i, 0)), row(LANE)],
        compiler_params=_params("parallel"),
        name="merge",
    )(x, g0, shift1, scale1, wg, bg, *branches, wb, wo, g1, gate1, g2, shift2, scale2, rw, rb)


MOE_CHUNKS = 4
MOE_DUMP_ROWS = 2 * MOE_ROWS


def _moe_kernel(be_ref, nu_ref, idx_hbm, h2_hbm, wgu_f32, bgu_ref, wdn_f32, bdn_ref, out_hbm,
                idx_smem, x0, x1, y0, y1, wgu_ref, wdn_ref, sem_idx, sem_in, sem_out, sem_fill, sem_pace):
    i = pl.program_id(0)
    n_used = nu_ref[0]
    n_blocks = pl.num_programs(0)
    xbuf, ybuf = (x0, x1), (y0, y1)

    @pl.when((i < n_used) & ((i == 0) | (be_ref[i] != be_ref[jnp.maximum(i - 1, 0)])))
    def _():
        wgu_ref[...] = wgu_f32[...].astype(BF16)
        wdn_ref[...] = wdn_f32[...].astype(BF16)
    n_tokens_rows = out_hbm.shape[0] - MOE_DUMP_ROWS * ROW_TILE
    q_prev, q_cur, q_next = (i + 2) % 3, i % 3, (i + 1) % 3

    def idx_copy(block, s):
        return pltpu.make_async_copy(idx_hbm.at[block], idx_smem.at[s], sem_idx.at[s])

    def tile(ref, first_row):
        if not isinstance(first_row, int):
            first_row = pl.multiple_of(first_row, ROW_TILE)
        return ref.at[pl.ds(first_row, ROW_TILE), :]

    def row_in(s, r, tok_row):
        return pltpu.make_async_copy(tile(h2_hbm, tok_row), tile(xbuf[s], r * ROW_TILE), sem_in.at[s])

    def row_out(s, r, dst_row):
        return pltpu.make_async_copy(tile(ybuf[s], r * ROW_TILE), tile(out_hbm, dst_row), sem_out.at[s])

    def fill(half):
        rows = MOE_ROWS * ROW_TILE
        return pltpu.make_async_copy(y1, out_hbm.at[pl.ds(n_tokens_rows + half * rows, rows), :], sem_fill)

    def wait_rows(copy, s):
        for _ in range(MOE_ROWS):
            copy(s, 0, 0).wait()

    @pl.when(i == 0)
    def _():
        idx_copy(0, 0).start()
        idx_copy(n_blocks, 2).start()
        y1[...] = jnp.zeros_like(y1)
        fill(0).start()
        fill(1).start()
        idx_copy(0, 0).wait()
        idx_copy(n_blocks, 2).wait()

        def body(r, carry):
            row_in(0, r, idx_smem[0, 0, r]).start()
            return carry
        lax.fori_loop(0, MOE_ROWS, body, 0, unroll=8)
        fill(0).wait()
        fill(1).wait()

    def step(slot):
        other = 1 - slot
        nxt = jnp.minimum(i + 1, n_blocks - 1)
        idx_copy(nxt, q_next).start()
        wait_rows(row_in, slot)
        idx_copy(nxt, q_next).wait()

        @pl.when(i >= 1)
        def _():
            wait_rows(row_out, slot)

        x = _load_row_tiles(xbuf[slot], MOE_ROWS, BF16)
        y = jnp.broadcast_to(bdn_ref[...], (MOE_ROWS, D_MODEL))
        cols = D_EXPERT // MOE_CHUNKS
        rows = MOE_ROWS // MOE_CHUNKS
        for c in range(MOE_CHUNKS):
            if c:
                pl.semaphore_signal(sem_pace, 1)
                pl.semaphore_wait(sem_pace, 1)
            for r in range(c * rows, (c + 1) * rows):
                row_in(other, r, idx_smem[q_next, 0, r]).start()
                row_out(other, r, idx_smem[q_prev, 1, r]).start()
            g_cols = slice(c * cols, (c + 1) * cols)
            u_cols = slice(D_EXPERT + c * cols, D_EXPERT + (c + 1) * cols)
            hg = jnp.minimum(_dot(x, wgu_ref[:, g_cols]) + bgu_ref[:, g_cols], SWIGLU_LIMIT)
            hu = jnp.clip(_dot(x, wgu_ref[:, u_cols]) + bgu_ref[:, u_cols], -SWIGLU_LIMIT, SWIGLU_LIMIT)
            act = (hu + 1.0) * (hg * _sigmoid(SWIGLU_ALPHA * hg))
            y = y + _dot(act, wdn_ref[g_cols, :])
        _store_row_tiles(ybuf[slot], y)

        @pl.when(i == n_used - 1)
        def _():
            for r in range(MOE_ROWS):
                row_out(slot, r, idx_smem[q_cur, 1, r]).start()
            wait_rows(row_out, other)
            wait_rows(row_out, slot)
            wait_rows(row_in, other)

    for parity in (0, 1):
        pl.when((i < n_used) & (i % 2 == parity))(functools.partial(step, parity))


def _moe_blocks(idx, h2, block_e, n_used, layer, w_gu, b_gu, w_dn, b_dn, n_out_rows):
    n_blocks = idx.shape[0] - 1
    exp = lambda i, be, nu: (be[jnp.minimum(i, nu[0] - 1)], 0, 0)
    exp_w = lambda i, be, nu: (layer, be[jnp.minimum(i, nu[0] - 1)], 0, 0)
    any_spec = pl.BlockSpec(memory_space=pl.ANY)
    return pl.pallas_call(
        _moe_kernel,
        out_shape=jax.ShapeDtypeStruct(((n_out_rows + MOE_DUMP_ROWS) * ROW_TILE, LANE), F32),
        grid_spec=pltpu.PrefetchScalarGridSpec(
            num_scalar_prefetch=2,
            grid=(n_blocks,),
            in_specs=[
                any_spec,
                any_spec,
                pl.BlockSpec((None, None, D_MODEL, 2 * D_EXPERT), exp_w),
                pl.BlockSpec((None, 1, 2 * D_EXPERT), exp),
                pl.BlockSpec((None, None, D_EXPERT, D_MODEL), exp_w),
                pl.BlockSpec((None, 1, D_MODEL), exp),
            ],
            out_specs=any_spec,
            scratch_shapes=[pltpu.SMEM((3, 2, MOE_ROWS), jnp.int32)]
                           + [pltpu.VMEM((MOE_ROWS * ROW_TILE, LANE), F32)] * 4
                           + [pltpu.VMEM((D_MODEL, 2 * D_EXPERT), BF16), pltpu.VMEM((D_EXPERT, D_MODEL), BF16)]
                           + [pltpu.SemaphoreType.DMA((3,)),
                              pltpu.SemaphoreType.DMA((2,)),
                              pltpu.SemaphoreType.DMA((2,)),
                              pltpu.SemaphoreType.DMA,
                              pltpu.SemaphoreType.REGULAR],
        ),
        compiler_params=_params("arbitrary"),
        name="moe_experts",
    )(block_e, n_used, idx, h2, w_gu, b_gu, w_dn, b_dn)


def _moe(h2, logits, layer, w_gu, b_gu, w_dn, b_dn):
    N = logits.shape[0]
    NK = N * TOP_K
    top_val, top_idx = lax.top_k(logits, TOP_K)
    gate = jax.nn.softmax(top_val, axis=-1)
    flat_e = top_idx.reshape(-1)
    order = jnp.argsort(flat_e).astype(jnp.int32)
    counts = jnp.sum((flat_e[:, None] == jnp.arange(N_EXPERTS)[None, :]).astype(jnp.int32), axis=0)
    padded = (counts + MOE_ROWS - 1) // MOE_ROWS * MOE_ROWS
    pad_end = jnp.cumsum(padded)
    pad_start = pad_end - padded
    start = jnp.cumsum(counts) - counts
    n_blocks = (NK + N_EXPERTS * (MOE_ROWS - 1) + MOE_ROWS - 1) // MOE_ROWS
    P = n_blocks * MOE_ROWS
    block_first = jnp.arange(n_blocks, dtype=jnp.int32) * MOE_ROWS
    block_e = jnp.minimum(jnp.sum((block_first[:, None] >= pad_end[None, :]).astype(jnp.int32), axis=1),
                          N_EXPERTS - 1)
    n_used = (pad_end[-1] // MOE_ROWS).astype(jnp.int32).reshape(1)
    pos = jnp.arange(P, dtype=jnp.int32)
    e_pos = jnp.repeat(block_e, MOE_ROWS)
    rank = pos - pad_start[e_pos].astype(jnp.int32)
    valid = rank < counts[e_pos]
    flat = order[jnp.clip(start[e_pos].astype(jnp.int32) + rank, 0, NK - 1)]
    tok = jnp.where(valid, flat // TOP_K, 0)
    spare = NK + (pos // MOE_ROWS % 2) * MOE_ROWS + pos % MOE_ROWS
    dst = jnp.where(valid, (flat % TOP_K) * N + flat // TOP_K, spare)
    idx = jnp.stack([tok.reshape(n_blocks, MOE_ROWS), dst.reshape(n_blocks, MOE_ROWS)], axis=1)
    stand_in = jnp.stack([jnp.zeros((MOE_ROWS,), jnp.int32), NK + MOE_ROWS + jnp.arange(MOE_ROWS, dtype=jnp.int32)])
    idx = jnp.concatenate([idx, stand_in[None]], axis=0) * ROW_TILE
    expert_out = _moe_blocks(idx, h2, block_e, n_used, layer, w_gu, b_gu, w_dn, b_dn, NK)
    return expert_out, gate


def _combine_kernel(x_ref, y0_ref, y1_ref, y2_ref, y3_ref, w_ref, g_ref, gate_ref, o_ref):
    w = w_ref[...]
    y = None
    for k, y_ref in enumerate((y0_ref, y1_ref, y2_ref, y3_ref)):
        term = _load_row_tiles(y_ref, w.shape[0]) * w[:, k:k + 1]
        y = term if y is None else y + term
    o_ref[...] = x_ref[...] + gate_ref[0] * (_rms(y) * g_ref[...])


def _combine(x, expert_out, weights, g, gate, *, rows_per_mod, tm):
    T = x.shape[0]
    tm = min(tm, rows_per_mod)
    row = pl.BlockSpec((tm, D_MODEL), lambda i: (i, 0))
    slabs = [pl.BlockSpec((tm * ROW_TILE, LANE), lambda i, k=k: (k * (T // tm) + i, 0)) for k in range(TOP_K)]
    return pl.pallas_call(
        _combine_kernel,
        out_shape=jax.ShapeDtypeStruct((T, D_MODEL), F32),
        grid=(T // tm,),
        in_specs=[row] + slabs + [pl.BlockSpec((tm, TOP_K), lambda i: (i, 0)),
                                  pl.BlockSpec((1, D_MODEL), lambda i: (0, 0)),
                                  pl.BlockSpec((1, 1, D_MODEL), lambda i: ((i * tm) // rows_per_mod, 0, 0))],
        out_specs=row,
        compiler_params=_params("parallel"),
        name="combine",
    )(x, *[expert_out] * TOP_K, weights, g, gate)


def _rope_tables(L):
    t = jnp.arange(L)
    lane = jnp.arange(LANE)
    axis = (lane % A_DIM) // 32
    half = (lane % 32) // 16
    n_freq = A_DIM // 4
    inv = ROPE_BASE ** (-(lane % n_freq).astype(F32) / n_freq)
    pos = jnp.where(axis[None, :] == 0, (t // GRID_W)[:, None], (t % GRID_W)[:, None]).astype(F32)
    ang = pos * inv[None, :]
    return jnp.cos(ang), jnp.where(half[None, :] == 0, -jnp.sin(ang), jnp.sin(ang))


def _arrange_w_in(w_in):
    sizes = [512] * 8 + [8, 8] + [512] * 4 + [8, 8]
    offs = [0]
    for s in sizes:
        offs.append(offs[-1] + s)
    part = lambda i: w_in[:, offs[i]:offs[i + 1]]
    big = [part(i) for i in (0, 1, 2, 3, 4, 5, 6, 7, 10, 11, 12, 13)]
    small = jnp.zeros((w_in.shape[0], HEADS, LANE), w_in.dtype)
    for base, i in ((SM_BETA, 8), (SM_DEC, 9), (SM_IG, 14), (SM_FG, 15)):
        cols = part(i).reshape(-1, 2, HEADS)
        for d in range(2):
            small = small.at[:, :, base + d].set(cols[:, d, :])
    return jnp.concatenate(big + [small.reshape(w_in.shape[0], HEADS * LANE)], axis=1)


def _layer(x, cond, lidx, lp, *, B, L, cache):
    T = B * L
    is_ctx = cache is None
    mod = (jax.nn.silu(cond) @ lp['ada_w'] + lp['ada_b']).reshape(-1, 6, 1, D_MODEL)
    shift1, scale1, gate1, shift2, scale2, gate2 = [mod[:, i] for i in range(6)]
    rows_per_mod = T if mod.shape[0] == 1 else L
    g = lp['norm_g']

    tm = 1024
    P = _norm_mod_matmul(x, g[0:1], shift1, scale1, lp['w_in'], rows_per_mod=rows_per_mod, tm=tm,
                         tn=N_PROJ // 4)

    lam_init = 0.8 - 0.6 * math.exp(-0.3 * lidx)
    lq1, lk1, lq2, lk2 = lp['attn_lambda']
    lam = (jnp.exp(jnp.sum(lq1 * lk1)) - jnp.exp(jnp.sum(lq2 * lk2)) + lam_init).reshape(1, 1)
    attn_norm = lp['attn_norm'][None]
    if is_ctx:
        br_a, new_k, new_v = _attention(P, lam, attn_norm, B=B, L=L, tq=L, lam_init=lam_init)
    else:
        (br_a,) = _attention(P, lam, attn_norm, B=B, L=L, tq=256, lam_init=lam_init, cache=cache['attn'])

    br_b = _pool(P, lp['pool_w'], lp['pool_b'][:, None], lp['pool_scale'][None], B=B, L=L)

    gdn_args = (P, lp['gdn_A_log'], lp['gdn_dt_bias'], lp['gdn_conv'], lp['gdn_norm'][None])
    ml_args = (P, lp['mlstm_bias_i'], lp['mlstm_bias_f'], lp['mlstm_norm'][None])
    if is_ctx:
        br_c, new_gdn = _gdn(*gdn_args, B=B, L=L)
        br_d, new_c, new_n, new_m = _mlstm(*ml_args, B=B, L=L)
        ctx_out = (new_k, new_v, new_gdn, new_c,
                   new_n[:, :, :, 0, :HEAD_DIM].transpose(0, 2, 1, 3), new_m[:, :, :, 0, 0].transpose(0, 2, 1))
    else:
        (br_c,) = _gdn(*gdn_args, B=B, L=L, state=cache['gdn'])
        (br_d,) = _mlstm(*ml_args, B=B, L=L, state=cache['mlstm'])
        ctx_out = None

    x, h2, logits = _merge(x, g[0:1], shift1, scale1, lp['w_bgate'], lp['b_bgate'][None],
                           (br_a, br_b, br_c, br_d), lp['w_branch'], lp['w_out'], g[1:2], gate1,
                           g[2:3], shift2, scale2, lp['router_w'], lp['router_b'],
                           rows_per_mod=rows_per_mod, tm=256)
    expert_out, weights = _moe(h2, logits[:, :N_EXPERTS], lidx, lp['moe_w_gu'], lp['moe_b_gu'], lp['moe_w_dn'],
                               lp['moe_b_dn'])
    x = _combine(x, expert_out, weights, g[3:4], gate2, rows_per_mod=rows_per_mod, tm=512)
    return x, ctx_out


def kernel(x_prompt, x_sample, cache_attn_k, cache_attn_v, state_gdn, state_mlstm_C, state_mlstm_n, state_mlstm_m, c, c_ctx, ada_w, ada_b, norm_g, w_in, w_bgate, b_bgate, w_branch, w_out, attn_lambda, attn_norm, pool_w, pool_b, pool_scale, gdn_conv, gdn_A_log, gdn_dt_bias, gdn_norm, mlstm_bias_i, mlstm_bias_f, mlstm_norm, router_w, router_b, moe_w_gu, moe_b_gu, moe_w_dn, moe_b_dn):
    Bp, Lp, _ = x_prompt.shape
    Bs, Ls, _ = x_sample.shape
    y_p = x_prompt.reshape(Bp * Lp, D_MODEL)
    y_s = x_sample.reshape(Bs * Ls, D_MODEL)
    cos, sin = _rope_tables(Ls)
    n0 = state_mlstm_n.transpose(1, 0, 3, 2, 4)[:, :, :, :, None, :]
    m0 = jnp.broadcast_to(state_mlstm_m.transpose(1, 0, 3, 2)[..., None, None], n0.shape)
    ctx_states = []
    for l in range(DEPTH):
        lp = {
            'ada_w': ada_w[l], 'ada_b': ada_b[l], 'norm_g': norm_g[l],
            'w_in': _arrange_w_in(w_in[l].astype(BF16)), 'w_bgate': w_bgate[l].astype(BF16),
            'b_bgate': b_bgate[l], 'w_branch': w_branch[l].astype(BF16), 'w_out': w_out[l].astype(BF16),
            'attn_lambda': attn_lambda[l], 'attn_norm': attn_norm[l],
            'pool_w': pool_w[l], 'pool_b': pool_b[l], 'pool_scale': pool_scale[l],
            'gdn_conv': gdn_conv[l], 'gdn_A_log': gdn_A_log[l], 'gdn_dt_bias': gdn_dt_bias[l],
            'gdn_norm': gdn_norm[l], 'mlstm_bias_i': mlstm_bias_i[l], 'mlstm_bias_f': mlstm_bias_f[l],
            'mlstm_norm': mlstm_norm[l],
            'router_w': jnp.pad(router_w[l], ((0, 0), (0, LANE - N_EXPERTS))),
            'router_b': jnp.pad(router_b[l], (0, LANE - N_EXPERTS))[None],
            'moe_w_gu': moe_w_gu, 'moe_b_gu': moe_b_gu[l][:, None],
            'moe_w_dn': moe_w_dn, 'moe_b_dn': moe_b_dn[l][:, None],
        }
        y_p, st = _layer(y_p, c_ctx[None], l, lp, B=Bp, L=Lp, cache=None)
        ctx_states.append(st)
        cache = {
            'attn': {'k': cache_attn_k, 'v': cache_attn_v, 'layer': l, 'cos': cos, 'sin': sin},
            'gdn': (state_gdn, l),
            'mlstm': (state_mlstm_C, l, n0[l], m0[l]),
        }
        y_s, _ = _layer(y_s, c, l, lp, B=Bs, L=Ls, cache=cache)
    outs = [jnp.stack([s[i] for s in ctx_states], axis=1) for i in range(6)]
    return (y_p.reshape(Bp, Lp, D_MODEL), y_s.reshape(Bs, Ls, D_MODEL), *outs)
```

```python
import functools
import math

import jax
import jax.numpy as jnp
from jax import lax
from jax.experimental import pallas as pl
from jax.experimental.pallas import tpu as pltpu

F32 = jnp.float32
BF16 = jnp.bfloat16

D_MODEL = 1024
DEPTH = 2
GRID_W = 64
EPS = 1e-6
LANE = 128
HEADS = 4
HEAD_DIM = 128
A_DIM = 64
ROPE_BASE = 10000.0
POOL_WINDOWS = (2, 4, 8, 16)
CHUNK = 64
N_BRANCH = 4
BRANCH_WIDTH = 512
N_EXPERTS = 32
TOP_K = 4
D_EXPERT = 1024
SWIGLU_LIMIT = 7.0
SWIGLU_ALPHA = 1.702
MOE_ROWS = 256
VMEM_LIMIT = 56 * 1024 * 1024

COL_AQ, COL_AK, COL_AV, COL_POOL, COL_CQ, COL_CK, COL_CV, COL_CG, COL_MQ, COL_MK, COL_MV, COL_MG, COL_SM = range(13)
N_PROJ = 13 * BRANCH_WIDTH
SM_BETA, SM_DEC, SM_IG, SM_FG = 0, 2, 4, 6


def _params(*sem):
    return pltpu.CompilerParams(dimension_semantics=sem, vmem_limit_bytes=VMEM_LIMIT)


def _dot(a, b):
    return jnp.dot(a.astype(BF16), b.astype(BF16), preferred_element_type=F32)


def _dot_nt(a, b):
    return lax.dot_general(a.astype(BF16), b.astype(BF16), (((1,), (1,)), ((), ())),
                           preferred_element_type=F32)


def _dot_tn(a, b):
    return lax.dot_general(a.astype(BF16), b.astype(BF16), (((0,), (0,)), ((), ())),
                           preferred_element_type=F32)


def _split(a):
    hi = a.astype(BF16)
    lo = (a - hi.astype(F32)).astype(BF16)
    return hi, lo


def _dot3(a, b):
    ah, al = _split(a)
    bh, bl = _split(b)
    d = lambda x, y: jnp.dot(x, y, preferred_element_type=F32)
    return d(ah, bh) + (d(ah, bl) + d(al, bh))


def _dot_mask(mask, b):
    m = jnp.where(mask, 1.0, 0.0).astype(BF16)
    b0 = b.astype(BF16)
    r1 = b - b0.astype(F32)
    b1 = r1.astype(BF16)
    b2 = (r1 - b1.astype(F32)).astype(BF16)
    d = lambda y: jnp.dot(m, y, preferred_element_type=F32)
    return d(b0) + (d(b1) + d(b2))


ROW_TILE = 8
assert ROW_TILE * LANE == D_MODEL


def _store_row_tiles(ref, x):
    rows = x.shape[0]
    for c in range(ROW_TILE):
        ref[pl.ds(c, rows, stride=ROW_TILE), :] = x[:, c * LANE:(c + 1) * LANE]


def _load_row_tiles(ref, rows, dtype=F32):
    return jnp.concatenate([ref[pl.ds(c, rows, stride=ROW_TILE), :].astype(dtype) for c in range(ROW_TILE)],
                           axis=1)


def _rms(x):
    return x * lax.rsqrt(jnp.mean(x * x, axis=-1, keepdims=True) + EPS)


def _sigmoid(x):
    return 1.0 / (1.0 + jnp.exp(-x))


def _softplus(x):
    return jnp.maximum(x, 0.0) + jnp.log(1.0 + jnp.exp(-jnp.abs(x)))


def _proj_kernel(x_ref, g_ref, sh_ref, sc_ref, w_ref, o_ref, h_scr):
    @pl.when(pl.program_id(1) == 0)
    def _():
        y = _rms(x_ref[...]) * g_ref[...]
        h_scr[...] = (y * (1.0 + sc_ref[0]) + sh_ref[0]).astype(BF16)

    o_ref[...] = jnp.dot(h_scr[...], w_ref[...], preferred_element_type=F32)


def _norm_mod_matmul(x, g, shift, scale, w, *, rows_per_mod, tm, tn):
    T, Dm = x.shape
    N = w.shape[1]
    tm = min(tm, rows_per_mod)
    mod_idx = lambda i, j: ((i * tm) // rows_per_mod, 0, 0)
    return pl.pallas_call(
        _proj_kernel,
        out_shape=jax.ShapeDtypeStruct((T, N), F32),
        grid=(T // tm, N // tn),
        in_specs=[
            pl.BlockSpec((tm, Dm), lambda i, j: (i, 0)),
            pl.BlockSpec((1, Dm), lambda i, j: (0, 0)),
            pl.BlockSpec((1, 1, Dm), mod_idx),
            pl.BlockSpec((1, 1, Dm), mod_idx),
            pl.BlockSpec((Dm, tn), lambda i, j: (0, j)),
        ],
        out_specs=pl.BlockSpec((tm, tn), lambda i, j: (i, j)),
        scratch_shapes=[pltpu.VMEM((tm, Dm), BF16)],
        compiler_params=_params("parallel", "arbitrary"),
        name="proj",
    )(x, g, shift, scale, w)


def _rope(x, cos, sin):
    lane = lax.broadcasted_iota(jnp.int32, x.shape, 1)
    first = (lane % 32) < 16
    partner = jnp.where(first, pltpu.roll(x, LANE - 16, 1), pltpu.roll(x, 16, 1))
    return x * cos + partner * sin


def _attn_kernel(*refs, has_cache, n_ctx, out_scale):
    if has_cache:
        (lam_ref, q_ref, k_ref, v_ref, ck_ref, cv_ref, cosq_ref, sinq_ref, cosk_ref, sink_ref, nrm_ref,
         o_ref, kall, vall) = refs
    else:
        lam_ref, q_ref, k_ref, v_ref, nrm_ref, o_ref, ko_ref, vo_ref, kall, vall = refs

    @pl.when(pl.program_id(2) == 0)
    def _():
        k = k_ref[...]
        v = v_ref[...]
        if has_cache:
            kall[:, pl.ds(0, n_ctx)] = ck_ref[...].T.astype(BF16)
            vall[pl.ds(0, n_ctx), :] = cv_ref[...].astype(BF16)
            k = _rope(k, cosk_ref[...], sink_ref[...])
        else:
            ko_ref[...] = k
            vo_ref[...] = v
        kall[:, pl.ds(n_ctx, k.shape[0])] = k.T.astype(BF16)
        vall[pl.ds(n_ctx, k.shape[0]), :] = v.astype(BF16)

    q = q_ref[...]
    if has_cache:
        q = _rope(q, cosq_ref[...], sinq_ref[...])
    q = q * (A_DIM ** -0.5 * math.log2(math.e))
    lane = lax.broadcasted_iota(jnp.int32, q.shape, 1)
    q1 = jnp.where(lane < A_DIM, q, 0.0)
    q2 = jnp.where(lane >= A_DIM, q, 0.0)
    keys = kall[...]
    lam = lam_ref[0, 0]

    def probs(qm):
        s = _dot(qm, keys)
        p = jnp.exp2(s - jnp.max(s, axis=-1, keepdims=True))
        return p, jnp.sum(p, axis=-1, keepdims=True)

    p1, l1 = probs(q1)
    p2, l2 = probs(q2)
    a = p1 * (1.0 / l1) - p2 * (lam / l2)
    o = jnp.dot(a.astype(BF16), vall[...], preferred_element_type=F32)
    o_ref[...] = _rms(o) * nrm_ref[...] * out_scale


def _attention(P, lam, norm, *, B, L, tq, lam_init, cache=None):
    T = B * L
    nq = L // tq
    has_cache = cache is not None
    n_ctx = cache["k"].shape[3] if has_cache else 0
    colq, colk, colv = COL_AQ * HEADS, COL_AK * HEADS, COL_AV * HEADS
    in_specs = [
        pl.BlockSpec(memory_space=pltpu.SMEM),
        pl.BlockSpec((tq, LANE), lambda b, h, i: (b * nq + i, colq + h)),
        pl.BlockSpec((L, LANE), lambda b, h, i: (b, colk + h)),
        pl.BlockSpec((L, LANE), lambda b, h, i: (b, colv + h)),
    ]
    args = [lam, P, P, P]
    if has_cache:
        l = cache["layer"]
        cspec = pl.BlockSpec((None, None, None, n_ctx, LANE), lambda b, h, i: (b, l, h, 0, 0))
        in_specs += [cspec, cspec,
                     pl.BlockSpec((tq, LANE), lambda b, h, i: (i, 0)),
                     pl.BlockSpec((tq, LANE), lambda b, h, i: (i, 0)),
                     pl.BlockSpec((L, LANE), lambda b, h, i: (0, 0)),
                     pl.BlockSpec((L, LANE), lambda b, h, i: (0, 0))]
        args += [cache["k"], cache["v"], cache["cos"], cache["sin"], cache["cos"], cache["sin"]]
    in_specs.append(pl.BlockSpec((1, LANE), lambda b, h, i: (0, 0)))
    args.append(norm)
    out_shape = [jax.ShapeDtypeStruct((T, HEADS * LANE), F32)]
    out_specs = [pl.BlockSpec((tq, LANE), lambda b, h, i: (b * nq + i, h))]
    if not has_cache:
        kv_shape = jax.ShapeDtypeStruct((B, HEADS, L, LANE), F32)
        kv_spec = pl.BlockSpec((None, None, L, LANE), lambda b, h, i: (b, h, 0, 0))
        out_shape += [kv_shape, kv_shape]
        out_specs += [kv_spec, kv_spec]
    return pl.pallas_call(
        functools.partial(_attn_kernel, has_cache=has_cache, n_ctx=n_ctx, out_scale=1.0 - lam_init),
        out_shape=out_shape,
        grid=(B, HEADS, nq),
        in_specs=in_specs,
        out_specs=out_specs,
        scratch_shapes=[pltpu.VMEM((LANE, n_ctx + L), BF16), pltpu.VMEM((n_ctx + L, LANE), BF16)],
        compiler_params=_params("parallel", "parallel", "arbitrary"),
        name="diff_attention",
    )(*args)


POOL_PAD = 16


def _pool_kernel(x_ref, w_ref, b_ref, s_ref, o_ref, pad, *, L):
    zeros = jnp.zeros((POOL_PAD, LANE), F32)
    pad[pl.ds(0, POOL_PAD), :] = zeros
    pad[pl.ds(POOL_PAD + L, POOL_PAD), :] = zeros
    x = x_ref[...]
    pad[pl.ds(POOL_PAD, L), :] = x
    t = lax.broadcasted_iota(jnp.int32, (L, LANE), 0)
    g = pl.program_id(1)
    for gi, win in enumerate(POOL_WINDOWS):
        @pl.when(g == gi)
        def _(win=win):
            half = win // 2
            acc = pad[pl.ds(POOL_PAD - half, L), :]
            for k in range(1 - half, half):
                acc = acc + pad[pl.ds(POOL_PAD + k, L), :]
            cnt = (jnp.minimum(t + half, L) - jnp.maximum(t - half, 0)).astype(F32)
            pooled = acc / cnt - x
            o_ref[...] = (_dot(pooled, w_ref[...]) + b_ref[...]) * s_ref[...]


def _pool(P, w, b, scale, *, B, L):
    T = B * L
    G = len(POOL_WINDOWS)
    return pl.pallas_call(
        functools.partial(_pool_kernel, L=L),
        out_shape=jax.ShapeDtypeStruct((T, G * LANE), F32),
        grid=(B, G),
        in_specs=[
            pl.BlockSpec((L, LANE), lambda bi, g: (bi, COL_POOL * HEADS + g)),
            pl.BlockSpec((None, LANE, LANE), lambda bi, g: (g, 0, 0)),
            pl.BlockSpec((None, 1, LANE), lambda bi, g: (g, 0, 0)),
            pl.BlockSpec((1, LANE), lambda bi, g: (0, g)),
        ],
        out_specs=pl.BlockSpec((L, LANE), lambda bi, g: (bi, g)),
        scratch_shapes=[pltpu.VMEM((L + 2 * POOL_PAD, LANE), F32)],
        compiler_params=_params("parallel", "parallel"),
        name="pool_mixer",
    )(P, w, b, scale)


GROUP_CHUNKS = 2
GROUP = GROUP_CHUNKS * CHUNK
MAX_GROUPS_PER_STEP = 8


def _groups_per_step(L):
    return min(MAX_GROUPS_PER_STEP, L // GROUP)


def _group_masks(backward):
    row = lax.broadcasted_iota(jnp.int32, (GROUP, GROUP), 0)
    col = lax.broadcasted_iota(jnp.int32, (GROUP, GROUP), 1)
    same = (row // CHUNK) == (col // CHUNK)
    lower, upper = row >= col, row <= col
    if backward:
        lower, upper = upper, lower
    return dict(same=same, incl=same & lower, strict=same & lower & (row != col), incl_t=same & upper,
                eye=row == col)


INV_BASE = 8


def _block_masks():
    row = lax.broadcasted_iota(jnp.int32, (GROUP, GROUP), 0)
    col = lax.broadcasted_iota(jnp.int32, (GROUP, GROUP), 1)
    sizes = [INV_BASE << i for i in range(int(math.log2(CHUNK // INV_BASE)) + 1)]
    same = [(row // s) == (col // s) for s in sizes]
    return [same[0]] + [cur & jnp.logical_not(prev) for prev, cur in zip(same[:-1], same[1:])]


def _unit_triangular_inverses(ns, eye, blocks):
    base = [jnp.where(blocks[0], n, 0.0) for n in ns]
    invs = [eye - b for b in base]
    pws = [_dot(b, b) for b in base]
    rounds = int(math.log2(INV_BASE)) - 1
    for s in range(rounds):
        invs = [t + _dot(t, p) for t, p in zip(invs, pws)]
        if s + 1 < rounds:
            pws = [_dot(p, p) for p in pws]
    for join in blocks[1:]:
        offs = [jnp.where(join, n, 0.0) for n in ns]
        xs = [_dot(o, t) for o, t in zip(offs, invs)]
        invs = [t - _dot(t, x) for t, x in zip(invs, xs)]
    return invs


def _dot_mask2(mask, b):
    m = jnp.where(mask, 1.0, 0.0).astype(BF16)
    b0, b1 = _split(b)
    return jnp.dot(m, b0, preferred_element_type=F32) + jnp.dot(m, b1, preferred_element_type=F32)


def _per_chunk(x, reduce):
    parts = [reduce(x[c * CHUNK:(c + 1) * CHUNK], axis=0, keepdims=True) for c in range(GROUP_CHUNKS)]
    col = jnp.concatenate([jnp.broadcast_to(p, (CHUNK, 1)) for p in parts], axis=0)
    return parts, col


def _scan_order(step, n_steps, groups):
    base = (step * groups * GROUP, (n_steps - 1 - step) * groups * GROUP)
    fwd = [(gi, c) for gi in range(groups) for c in range(GROUP_CHUNKS)]
    return base, (fwd, fwd[::-1])


CONV_PAD = 8


def _gdn_kernel(*refs, L, has_state):
    alog_ref, dtb_ref, q_ref, k_ref, v_ref, gate_ref, sm_ref, cwq_ref, cwk_ref, cwv_ref, nrm_ref = refs[:11]
    refs = refs[11:]
    if has_state:
        s0_ref, o_ref, pad, qs, ks, vs, o_f, o_b, state = refs
    else:
        o_ref, sout_ref, pad, qs, ks, vs, o_f, o_b, state = refs
    h = pl.program_id(1)
    n_chunks = L // CHUNK

    zeros = jnp.zeros((CONV_PAD, LANE), F32)
    pad[pl.ds(0, CONV_PAD), :] = zeros
    pad[pl.ds(CONV_PAD + L, CONV_PAD), :] = zeros

    def conv_silu(x_ref, w_ref):
        pad[pl.ds(CONV_PAD, L), :] = x_ref[...]
        w = w_ref[...]
        y = (pad[pl.ds(CONV_PAD - 1, L), :] * w[0:1] + pad[pl.ds(CONV_PAD, L), :] * w[1:2]
             + pad[pl.ds(CONV_PAD + 1, L), :] * w[2:3])
        return y * _sigmoid(y)

    def l2n(x):
        return x * lax.rsqrt(jnp.sum(x * x, axis=-1, keepdims=True) + EPS)

    qs[...] = l2n(conv_silu(q_ref, cwq_ref)) * (HEAD_DIM ** -0.5)
    ks[...] = l2n(conv_silu(k_ref, cwk_ref))
    vs[...] = conv_silu(v_ref, cwv_ref)
    if has_state:
        state[...] = s0_ref[...]
    else:
        state[...] = jnp.zeros_like(state)

    groups = _groups_per_step(L)
    n_steps = L // (groups * GROUP)
    o_scr = (o_f, o_b)
    masks = (_group_masks(False), _group_masks(True))
    eye = jnp.where(masks[0]['eye'], 1.0, 0.0)
    blocks = _block_masks()
    items = [(d, gi) for gi in range(groups) for d in (0, 1)]
    chunks = [slice(c * CHUNK, (c + 1) * CHUNK) for c in range(GROUP_CHUNKS)]

    def body(step, carry):
        base, order = _scan_order(step, n_steps, groups)
        it = {}
        for key in items:
            d, gi = key
            rows = pl.ds(pl.multiple_of(base[d] + gi * GROUP, GROUP), GROUP)
            q, k, v, sm = qs[rows, :], ks[rows, :], vs[rows, :], sm_ref[rows, :]
            beta = _sigmoid(sm[:, SM_BETA + d:SM_BETA + d + 1])
            dec = sm[:, SM_DEC + d:SM_DEC + d + 1]
            g = -jnp.exp(alog_ref[d, h]) * _softplus(dec + dtb_ref[d, h])
            it[key] = dict(q=q, k=k, kb=k * beta, vb=v * beta, g=g, gb=jnp.broadcast_to(g, (GROUP, GROUP)))
        for (d, gi), x in it.items():
            m = masks[d]
            x['cum_i'] = _dot_mask2(m['incl'], x['gb'])
            x['cum_j'] = _dot_mask2(m['same'], jnp.where(m['incl_t'], x['gb'], 0.0))
            x['kk'] = _dot_nt(x['kb'], x['k'])
            x['qk'] = _dot_nt(x['q'], x['k'])
        for (d, gi), x in it.items():
            m = masks[d]
            decay = jnp.where(m['incl'], jnp.exp(jnp.where(m['incl'], x['cum_i'] - x['cum_j'], 0.0)), 0.0)
            x['gc'] = x['cum_i'][:, 0:1]
            x['tot'], x['tot_col'] = _per_chunk(x['g'], jnp.sum)
            x['qk'] = x['qk'] * decay
            x['n'] = jnp.where(m['strict'], x['kk'] * decay, 0.0)
        invs = _unit_triangular_inverses([x['n'] for x in it.values()], eye, blocks)
        for x, inv in zip(it.values(), invs):
            x['inv'] = inv
            x['egc'] = jnp.exp(x['gc'])
            x['sol'] = _dot3(x['inv'], jnp.concatenate([x['vb'], x['kb'] * x['egc']], axis=-1))
        for x in it.values():
            u, w = x['sol'][:, :HEAD_DIM], x['sol'][:, HEAD_DIM:]
            x['qp'] = x['q'] * x['egc'] - _dot(x['qk'], w)
            x['op'] = _dot(x['qk'], u)
            kd = x['k'] * jnp.exp(x['tot_col'] - x['gc'])
            x['ab'] = [_dot_tn(kd[cs], x['sol'][cs]) for cs in chunks]
            x['gl'] = [jnp.exp(t) for t in x['tot']]
        S = [state[0], state[1]]
        for stp in range(len(order[0])):
            for d in (0, 1):
                gi, c = order[d][stp]
                x = it[(d, gi)]
                ab = x['ab'][c]
                r = _dot(jnp.concatenate([ab[:, HEAD_DIM:], x['qp'][chunks[c]]], axis=0), S[d])
                rows = pl.ds(pl.multiple_of(base[d] + gi * GROUP + c * CHUNK, CHUNK), CHUNK)
                o_scr[d][rows, :] = x['op'][chunks[c]] + r[HEAD_DIM:]
                S[d] = x['gl'][c] * S[d] + (ab[:, :HEAD_DIM] - r[:HEAD_DIM])
        state[0] = S[0]
        state[1] = S[1]
        return carry

    lax.fori_loop(0, n_steps, body, 0)
    gate = gate_ref[...]
    o_ref[...] = _rms(o_f[...] + o_b[...]) * nrm_ref[...] * (gate * _sigmoid(gate))
    if not has_state:
        sout_ref[...] = state[...]


def _col_spec(L, col):
    return pl.BlockSpec((L, LANE), lambda b, h: (b, col * HEADS + h))


def _gdn(P, a_log, dt_bias, conv_w, norm, *, B, L, state=None):
    T = B * L
    has_state = state is not None
    smem = pl.BlockSpec(memory_space=pltpu.SMEM)
    in_specs = [smem, smem, _col_spec(L, COL_CQ), _col_spec(L, COL_CK), _col_spec(L, COL_CV),
                _col_spec(L, COL_CG), _col_spec(L, COL_SM)]
    in_specs += [pl.BlockSpec((3, LANE), lambda b, h, j=j: (0, j * HEADS + h)) for j in range(3)]
    in_specs.append(pl.BlockSpec((1, LANE), lambda b, h: (0, 0)))
    args = [a_log, dt_bias, P, P, P, P, P, conv_w, conv_w, conv_w, norm]
    out_shape = [jax.ShapeDtypeStruct((T, HEADS * LANE), F32)]
    out_specs = [pl.BlockSpec((L, LANE), lambda b, h: (b, h))]
    if has_state:
        arr, l = state
        in_specs.append(pl.BlockSpec((None, None, 2, None, HEAD_DIM, HEAD_DIM), lambda b, h: (b, l, 0, h, 0, 0)))
        args.append(arr)
    else:
        out_shape.append(jax.ShapeDtypeStruct((B, 2, HEADS, HEAD_DIM, HEAD_DIM), F32))
        out_specs.append(pl.BlockSpec((None, 2, None, HEAD_DIM, HEAD_DIM), lambda b, h: (b, 0, h, 0, 0)))
    seq = pltpu.VMEM((L, LANE), F32)
    return pl.pallas_call(
        functools.partial(_gdn_kernel, L=L, has_state=has_state),
        out_shape=out_shape,
        grid=(B, HEADS),
        in_specs=in_specs,
        out_specs=out_specs,
        scratch_shapes=[pltpu.VMEM((L + 2 * CONV_PAD, LANE), F32), seq, seq, seq, seq, seq,
                        pltpu.VMEM((2, HEAD_DIM, HEAD_DIM), F32)],
        compiler_params=_params("parallel", "parallel"),
        name="gated_deltanet",
    )(*args)


def _mlstm_kernel(*refs, L, has_state):
    bi_ref, bf_ref, q_ref, k_ref, v_ref, gate_ref, sm_ref, nrm_ref = refs[:8]
    refs = refs[8:]
    if has_state:
        c0_ref, n0_ref, m0_ref, o_ref, h_f, h_b, c_st, n_st, m_st = refs
    else:
        o_ref, cout_ref, nout_ref, mout_ref, h_f, h_b, c_st, n_st, m_st = refs
    h = pl.program_id(1)
    n_chunks = L // CHUNK

    if has_state:
        c_st[...] = c0_ref[...]
        n_st[...] = n0_ref[...]
        m_st[...] = m0_ref[...]
    else:
        c_st[...] = jnp.zeros_like(c_st)
        n_st[...] = jnp.zeros_like(n_st)
        m_st[...] = jnp.zeros_like(m_st)

    groups = _groups_per_step(L)
    n_steps = L // (groups * GROUP)
    h_scr = (h_f, h_b)
    masks = (_group_masks(False), _group_masks(True))
    items = [(d, gi) for gi in range(groups) for d in (0, 1)]
    chunks = [slice(c * CHUNK, (c + 1) * CHUNK) for c in range(GROUP_CHUNKS)]

    def body(step, carry):
        base, order = _scan_order(step, n_steps, groups)
        it = {}
        for key in items:
            d, gi = key
            rows = pl.ds(pl.multiple_of(base[d] + gi * GROUP, GROUP), GROUP)
            q, v, sm = q_ref[rows, :], v_ref[rows, :], sm_ref[rows, :]
            k = k_ref[rows, :] * (HEAD_DIM ** -0.5)
            ig = sm[:, SM_IG + d:SM_IG + d + 1] + bi_ref[d, h]
            fg = sm[:, SM_FG + d:SM_FG + d + 1] + bf_ref[d, h]
            lf = -_softplus(-fg)
            it[key] = dict(q=q, k=k, v=v, ig=ig, lf=lf, lfb=jnp.broadcast_to(lf, (GROUP, GROUP)),
                           igb=jnp.broadcast_to(ig, (GROUP, GROUP)))
        for (d, gi), x in it.items():
            m = masks[d]
            x['cum_i'] = _dot_mask2(m['incl'], x['lfb'])
            x['cum_j'] = _dot_mask2(m['same'], jnp.where(m['incl_t'], x['lfb'], 0.0)
                                    - jnp.where(m['eye'], x['igb'], 0.0))
            x['qk'] = _dot_nt(x['q'], x['k'])
        for (d, gi), x in it.items():
            m = masks[d]
            dm = x['cum_i'] - x['cum_j']
            x['b'] = x['cum_i'][:, 0:1]
            x['tot'], tot_col = _per_chunk(x['lf'], jnp.sum)
            x['m_intra'] = jnp.max(jnp.where(m['incl'], dm, -jnp.inf), axis=-1, keepdims=True)
            e = tot_col - x['b'] + x['ig']
            x['m_end'], m_end_col = _per_chunk(e, jnp.max)
            p = jnp.where(m['incl'], jnp.exp(jnp.where(m['incl'], dm - x['m_intra'], 0.0)), 0.0) * x['qk']
            x['p_sum'] = jnp.sum(p, axis=-1, keepdims=True)
            x['p'] = p
            x['kw'] = x['k'] * jnp.exp(e - m_end_col)
        for x in it.values():
            x['pv'] = _dot(x['p'], x['v'])
            x['kv'] = [_dot_tn(x['kw'][cs], x['v'][cs]) for cs in chunks]
            x['k_sum'] = [jnp.sum(x['kw'][cs], axis=0, keepdims=True) for cs in chunks]
        outs = []
        for d in (0, 1):
            C, n, m = c_st[d], n_st[d], m_st[d][:, 0:1]
            for gi, c in order[d]:
                x = it[(d, gi)]
                cs = chunks[c]
                m_t = jnp.maximum(x['b'][cs] + m, x['m_intra'][cs])
                w_inter = jnp.exp(x['b'][cs] + m - m_t)
                local = jnp.exp(x['m_intra'][cs] - m_t)
                den = (w_inter * jnp.sum(x['q'][cs] * n, axis=-1, keepdims=True) + local * x['p_sum'][cs])
                scale = 1.0 / jnp.maximum(jnp.abs(den), jnp.exp(-m_t))
                outs.append((d, gi, c, C, w_inter * scale, local * scale))
                m_new = jnp.maximum(x['tot'][c] + m, x['m_end'][c])
                carry_decay = jnp.exp(x['tot'][c] + m - m_new)
                local_new = jnp.exp(x['m_end'][c] - m_new)
                C = carry_decay * C + local_new * x['kv'][c]
                n = carry_decay * n + local_new * x['k_sum'][c]
                m = m_new
            c_st[d] = C
            n_st[d] = n
            m_st[d] = jnp.broadcast_to(m, (1, LANE))
        for d, gi, c, C, w_scale, p_scale in outs:
            x = it[(d, gi)]
            cs = chunks[c]
            rows = pl.ds(pl.multiple_of(base[d] + gi * GROUP + c * CHUNK, CHUNK), CHUNK)
            h_scr[d][rows, :] = w_scale * _dot(x['q'][cs], C) + p_scale * x['pv'][cs]
        return carry

    lax.fori_loop(0, n_steps, body, 0)
    o_ref[...] = _rms(h_f[...] + h_b[...]) * nrm_ref[...] * _sigmoid(gate_ref[...])
    if not has_state:
        cout_ref[...] = c_st[...]
        nout_ref[...] = n_st[...]
        mout_ref[...] = m_st[...]


def _mlstm(P, bias_i, bias_f, norm, *, B, L, state=None):
    T = B * L
    has_state = state is not None
    smem = pl.BlockSpec(memory_space=pltpu.SMEM)
    in_specs = [smem, smem, _col_spec(L, COL_MQ), _col_spec(L, COL_MK), _col_spec(L, COL_MV),
                _col_spec(L, COL_MG), _col_spec(L, COL_SM), pl.BlockSpec((1, LANE), lambda b, h: (0, 0))]
    args = [bias_i, bias_f, P, P, P, P, P, norm]
    out_shape = [jax.ShapeDtypeStruct((T, HEADS * LANE), F32)]
    out_specs = [pl.BlockSpec((L, LANE), lambda b, h: (b, h))]
    vec_spec = pl.BlockSpec((None, None, 2, 1, LANE), lambda b, h: (b, h, 0, 0, 0))
    if has_state:
        c_arr, l, n_arr, m_arr = state
        in_specs += [pl.BlockSpec((None, None, 2, None, HEAD_DIM, HEAD_DIM), lambda b, h: (b, l, 0, h, 0, 0)),
                     vec_spec, vec_spec]
        args += [c_arr, n_arr, m_arr]
    else:
        out_shape += [jax.ShapeDtypeStruct((B, 2, HEADS, HEAD_DIM, HEAD_DIM), F32),
                      jax.ShapeDtypeStruct((B, HEADS, 2, 1, LANE), F32),
                      jax.ShapeDtypeStruct((B, HEADS, 2, 1, LANE), F32)]
        out_specs += [pl.BlockSpec((None, 2, None, HEAD_DIM, HEAD_DIM), lambda b, h: (b, 0, h, 0, 0)),
                      vec_spec, vec_spec]
    seq = pltpu.VMEM((L, LANE), F32)
    return pl.pallas_call(
        functools.partial(_mlstm_kernel, L=L, has_state=has_state),
        out_shape=out_shape,
        grid=(B, HEADS),
        in_specs=in_specs,
        out_specs=out_specs,
        scratch_shapes=[seq, seq, pltpu.VMEM((2, HEAD_DIM, HEAD_DIM), F32), pltpu.VMEM((2, 1, LANE), F32),
                        pltpu.VMEM((2, 1, LANE), F32)],
        compiler_params=_params("parallel", "parallel"),
        name="mlstm",
    )(*args)


def _merge_kernel(x_ref, g0_ref, sh1_ref, sc1_ref, wg_ref, bg_ref, a_ref, b_ref, c_ref, d_ref, wb_ref, wo_ref,
                  g1_ref, gate1_ref, g2_ref, sh2_ref, sc2_ref, rw_ref, rb_ref, xo_ref, h2_ref, lg_ref):
    x_in = x_ref[...]
    h = (_rms(x_in) * g0_ref[...] * (1.0 + sc1_ref[0]) + sh1_ref[0]).astype(BF16)
    merged = None
    for i, br in enumerate((a_ref, b_ref, c_ref, d_ref)):
        cols = slice(i * D_MODEL, (i + 1) * D_MODEL)
        gate = _sigmoid(jnp.dot(h, wg_ref[:, cols], preferred_element_type=F32) + bg_ref[:, cols])
        term = gate * _dot(br[...], wb_ref[i])
        merged = term if merged is None else merged + term
    t = _rms(_dot(merged, wo_ref[...])) * g1_ref[...]
    x = x_in + gate1_ref[0] * t
    xo_ref[...] = x
    h2 = _rms(x) * g2_ref[...] * (1.0 + sc2_ref[0]) + sh2_ref[0]
    _store_row_tiles(h2_ref, h2)
    lg_ref[...] = _dot3(h2, rw_ref[...]) + rb_ref[...]


def _merge(x, g0, shift1, scale1, wg, bg, branches, wb, wo, g1, gate1, g2, shift2, scale2, rw, rb, *,
           rows_per_mod, tm):
    T = x.shape[0]
    tm = min(tm, rows_per_mod)
    row = lambda n: pl.BlockSpec((tm, n), lambda i: (i, 0))
    const = lambda *shape: pl.BlockSpec(shape, lambda i: (0,) * len(shape))
    mod = pl.BlockSpec((1, 1, D_MODEL), lambda i: ((i * tm) // rows_per_mod, 0, 0))
    return pl.pallas_call(
        _merge_kernel,
        out_shape=[jax.ShapeDtypeStruct((T, D_MODEL), F32), jax.ShapeDtypeStruct((T * ROW_TILE, LANE), F32),
                   jax.ShapeDtypeStruct((T, LANE), F32)],
        grid=(T // tm,),
        in_specs=[row(D_MODEL), const(1, D_MODEL), mod, mod, const(D_MODEL, N_BRANCH * D_MODEL),
                  const(1, N_BRANCH * D_MODEL)] + [row(BRANCH_WIDTH)] * 4
                 + [const(N_BRANCH, BRANCH_WIDTH, D_MODEL), const(D_MODEL, D_MODEL), const(1, D_MODEL), mod,
                    const(1, D_MODEL), mod, mod, const(D_MODEL, LANE), const(1, LANE)],
        out_specs=[row(D_MODEL), pl.BlockSpec((tm * ROW_TILE, LANE), lambda i: (i, 0)), row(LANE)],
        compiler_params=_params("parallel"),
        name="merge",
    )(x, g0, shift1, scale1, wg, bg, *branches, wb, wo, g1, gate1, g2, shift2, scale2, rw, rb)


MOE_CHUNKS = 2
MOE_DUMP_ROWS = 2 * MOE_ROWS


def _moe_kernel(be_ref, nu_ref, idx_hbm, h2_hbm, wgu_f32, bgu_ref, wdn_f32, bdn_ref, out_hbm,
                idx_smem, x0, x1, y0, y1, wgu_ref, wdn_ref, sem_idx, sem_in, sem_out, sem_fill, sem_pace):
    i = pl.program_id(0)
    n_used = nu_ref[0]
    n_blocks = pl.num_programs(0)
    xbuf, ybuf = (x0, x1), (y0, y1)

    @pl.when((i < n_used) & ((i == 0) | (be_ref[i] != be_ref[jnp.maximum(i - 1, 0)])))
    def _():
        wgu_ref[...] = wgu_f32[...].astype(BF16)
        wdn_ref[...] = wdn_f32[...].astype(BF16)
    n_tokens_rows = out_hbm.shape[0] - MOE_DUMP_ROWS * ROW_TILE
    q_prev, q_cur, q_next = (i + 2) % 3, i % 3, (i + 1) % 3

    def idx_copy(block, s):
        return pltpu.make_async_copy(idx_hbm.at[block], idx_smem.at[s], sem_idx.at[s])

    def tile(ref, first_row):
        if not isinstance(first_row, int):
            first_row = pl.multiple_of(first_row, ROW_TILE)
        return ref.at[pl.ds(first_row, ROW_TILE), :]

    def row_in(s, r, tok_row):
        return pltpu.make_async_copy(tile(h2_hbm, tok_row), tile(xbuf[s], r * ROW_TILE), sem_in.at[s])

    def row_out(s, r, dst_row):
        return pltpu.make_async_copy(tile(ybuf[s], r * ROW_TILE), tile(out_hbm, dst_row), sem_out.at[s])

    def fill(half):
        rows = MOE_ROWS * ROW_TILE
        return pltpu.make_async_copy(y1, out_hbm.at[pl.ds(n_tokens_rows + half * rows, rows), :], sem_fill)

    def wait_rows(copy, s):
        for _ in range(MOE_ROWS):
            copy(s, 0, 0).wait()

    @pl.when(i == 0)
    def _():
        idx_copy(0, 0).start()
        idx_copy(n_blocks, 2).start()
        y1[...] = jnp.zeros_like(y1)
        fill(0).start()
        fill(1).start()
        idx_copy(0, 0).wait()
        idx_copy(n_blocks, 2).wait()

        def body(r, carry):
            row_in(0, r, idx_smem[0, 0, r]).start()
            return carry
        lax.fori_loop(0, MOE_ROWS, body, 0, unroll=8)
        fill(0).wait()
        fill(1).wait()

    def step(slot):
        other = 1 - slot
        nxt = jnp.minimum(i + 1, n_blocks - 1)
        idx_copy(nxt, q_next).start()
        wait_rows(row_in, slot)
        idx_copy(nxt, q_next).wait()

        @pl.when(i >= 1)
        def _():
            wait_rows(row_out, slot)

        x = _load_row_tiles(xbuf[slot], MOE_ROWS, BF16)
        y = jnp.broadcast_to(bdn_ref[...], (MOE_ROWS, D_MODEL))
        cols = D_EXPERT // MOE_CHUNKS
        rows = MOE_ROWS // MOE_CHUNKS
        for c in range(MOE_CHUNKS):
            if c:
                pl.semaphore_signal(sem_pace, 1)
                pl.semaphore_wait(sem_pace, 1)
            for r in range(c * rows, (c + 1) * rows):
                row_in(other, r, idx_smem[q_next, 0, r]).start()
                row_out(other, r, idx_smem[q_prev, 1, r]).start()
            g_cols = slice(c * cols, (c + 1) * cols)
            u_cols = slice(D_EXPERT + c * cols, D_EXPERT + (c + 1) * cols)
            hg = jnp.minimum(_dot(x, wgu_ref[:, g_cols]) + bgu_ref[:, g_cols], SWIGLU_LIMIT)
            hu = jnp.clip(_dot(x, wgu_ref[:, u_cols]) + bgu_ref[:, u_cols], -SWIGLU_LIMIT, SWIGLU_LIMIT)
            act = (hu + 1.0) * (hg * _sigmoid(SWIGLU_ALPHA * hg))
            y = y + _dot(act, wdn_ref[g_cols, :])
        _store_row_tiles(ybuf[slot], y)

        @pl.when(i == n_used - 1)
        def _():
            for r in range(MOE_ROWS):
                row_out(slot, r, idx_smem[q_cur, 1, r]).start()
            wait_rows(row_out, other)
            wait_rows(row_out, slot)
            wait_rows(row_in, other)

    for parity in (0, 1):
        pl.when((i < n_used) & (i % 2 == parity))(functools.partial(step, parity))


def _moe_blocks(idx, h2, block_e, n_used, layer, w_gu, b_gu, w_dn, b_dn, n_out_rows):
    n_blocks = idx.shape[0] - 1
    exp = lambda i, be, nu: (be[jnp.minimum(i, nu[0] - 1)], 0, 0)
    exp_w = lambda i, be, nu: (layer, be[jnp.minimum(i, nu[0] - 1)], 0, 0)
    any_spec = pl.BlockSpec(memory_space=pl.ANY)
    return pl.pallas_call(
        _moe_kernel,
        out_shape=jax.ShapeDtypeStruct(((n_out_rows + MOE_DUMP_ROWS) * ROW_TILE, LANE), F32),
        grid_spec=pltpu.PrefetchScalarGridSpec(
            num_scalar_prefetch=2,
            grid=(n_blocks,),
            in_specs=[
                any_spec,
                any_spec,
                pl.BlockSpec((None, None, D_MODEL, 2 * D_EXPERT), exp_w),
                pl.BlockSpec((None, 1, 2 * D_EXPERT), exp),
                pl.BlockSpec((None, None, D_EXPERT, D_MODEL), exp_w),
                pl.BlockSpec((None, 1, D_MODEL), exp),
            ],
            out_specs=any_spec,
            scratch_shapes=[pltpu.SMEM((3, 2, MOE_ROWS), jnp.int32)]
                           + [pltpu.VMEM((MOE_ROWS * ROW_TILE, LANE), F32)] * 4
                           + [pltpu.VMEM((D_MODEL, 2 * D_EXPERT), BF16), pltpu.VMEM((D_EXPERT, D_MODEL), BF16)]
                           + [pltpu.SemaphoreType.DMA((3,)),
                              pltpu.SemaphoreType.DMA((2,)),
                              pltpu.SemaphoreType.DMA((2,)),
                              pltpu.SemaphoreType.DMA,
                              pltpu.SemaphoreType.REGULAR],
        ),
        compiler_params=_params("arbitrary"),
        name="moe_experts",
    )(block_e, n_used, idx, h2, w_gu, b_gu, w_dn, b_dn)


def _moe(h2, logits, layer, w_gu, b_gu, w_dn, b_dn):
    N = logits.shape[0]
    NK = N * TOP_K
    top_val, top_idx = lax.top_k(logits, TOP_K)
    gate = jax.nn.softmax(top_val, axis=-1)
    flat_e = top_idx.reshape(-1)
    order = jnp.argsort(flat_e).astype(jnp.int32)
    counts = jnp.sum((flat_e[:, None] == jnp.arange(N_EXPERTS)[None, :]).astype(jnp.int32), axis=0)
    padded = (counts + MOE_ROWS - 1) // MOE_ROWS * MOE_ROWS
    pad_end = jnp.cumsum(padded)
    pad_start = pad_end - padded
    start = jnp.cumsum(counts) - counts
    n_blocks = (NK + N_EXPERTS * (MOE_ROWS - 1) + MOE_ROWS - 1) // MOE_ROWS
    P = n_blocks * MOE_ROWS
    block_first = jnp.arange(n_blocks, dtype=jnp.int32) * MOE_ROWS
    block_e = jnp.minimum(jnp.sum((block_first[:, None] >= pad_end[None, :]).astype(jnp.int32), axis=1),
                          N_EXPERTS - 1)
    n_used = (pad_end[-1] // MOE_ROWS).astype(jnp.int32).reshape(1)
    pos = jnp.arange(P, dtype=jnp.int32)
    e_pos = jnp.repeat(block_e, MOE_ROWS)
    rank = pos - pad_start[e_pos].astype(jnp.int32)
    valid = rank < counts[e_pos]
    flat = order[jnp.clip(start[e_pos].astype(jnp.int32) + rank, 0, NK - 1)]
    tok = jnp.where(valid, flat // TOP_K, 0)
    spare = NK + (pos // MOE_ROWS % 2) * MOE_ROWS + pos % MOE_ROWS
    dst = jnp.where(valid, (flat % TOP_K) * N + flat // TOP_K, spare)
    idx = jnp.stack([tok.reshape(n_blocks, MOE_ROWS), dst.reshape(n_blocks, MOE_ROWS)], axis=1)
    stand_in = jnp.stack([jnp.zeros((MOE_ROWS,), jnp.int32), NK + MOE_ROWS + jnp.arange(MOE_ROWS, dtype=jnp.int32)])
    idx = jnp.concatenate([idx, stand_in[None]], axis=0) * ROW_TILE
    expert_out = _moe_blocks(idx, h2, block_e, n_used, layer, w_gu, b_gu, w_dn, b_dn, NK)
    return expert_out, gate


def _combine_kernel(x_ref, y0_ref, y1_ref, y2_ref, y3_ref, w_ref, g_ref, gate_ref, o_ref):
    w = w_ref[...]
    y = None
    for k, y_ref in enumerate((y0_ref, y1_ref, y2_ref, y3_ref)):
        term = _load_row_tiles(y_ref, w.shape[0]) * w[:, k:k + 1]
        y = term if y is None else y + term
    o_ref[...] = x_ref[...] + gate_ref[0] * (_rms(y) * g_ref[...])


def _combine(x, expert_out, weights, g, gate, *, rows_per_mod, tm):
    T = x.shape[0]
    tm = min(tm, rows_per_mod)
    row = pl.BlockSpec((tm, D_MODEL), lambda i: (i, 0))
    slabs = [pl.BlockSpec((tm * ROW_TILE, LANE), lambda i, k=k: (k * (T // tm) + i, 0)) for k in range(TOP_K)]
    return pl.pallas_call(
        _combine_kernel,
        out_shape=jax.ShapeDtypeStruct((T, D_MODEL), F32),
        grid=(T // tm,),
        in_specs=[row] + slabs + [pl.BlockSpec((tm, TOP_K), lambda i: (i, 0)),
                                  pl.BlockSpec((1, D_MODEL), lambda i: (0, 0)),
                                  pl.BlockSpec((1, 1, D_MODEL), lambda i: ((i * tm) // rows_per_mod, 0, 0))],
        out_specs=row,
        compiler_params=_params("parallel"),
        name="combine",
    )(x, *[expert_out] * TOP_K, weights, g, gate)


def _rope_tables(L):
    t = jnp.arange(L)
    lane = jnp.arange(LANE)
    axis = (lane % A_DIM) // 32
    half = (lane % 32) // 16
    n_freq = A_DIM // 4
    inv = ROPE_BASE ** (-(lane % n_freq).astype(F32) / n_freq)
    pos = jnp.where(axis[None, :] == 0, (t // GRID_W)[:, None], (t % GRID_W)[:, None]).astype(F32)
    ang = pos * inv[None, :]
    return jnp.cos(ang), jnp.where(half[None, :] == 0, -jnp.sin(ang), jnp.sin(ang))


def _arrange_w_in(w_in):
    sizes = [512] * 8 + [8, 8] + [512] * 4 + [8, 8]
    offs = [0]
    for s in sizes:
        offs.append(offs[-1] + s)
    part = lambda i: w_in[:, offs[i]:offs[i + 1]]
    big = [part(i) for i in (0, 1, 2, 3, 4, 5, 6, 7, 10, 11, 12, 13)]
    small = jnp.zeros((w_in.shape[0], HEADS, LANE), w_in.dtype)
    for base, i in ((SM_BETA, 8), (SM_DEC, 9), (SM_IG, 14), (SM_FG, 15)):
        cols = part(i).reshape(-1, 2, HEADS)
        for d in range(2):
            small = small.at[:, :, base + d].set(cols[:, d, :])
    return jnp.concatenate(big + [small.reshape(w_in.shape[0], HEADS * LANE)], axis=1)


def _layer(x, cond, lidx, lp, *, B, L, cache):
    T = B * L
    is_ctx = cache is None
    mod = (jax.nn.silu(cond) @ lp['ada_w'] + lp['ada_b']).reshape(-1, 6, 1, D_MODEL)
    shift1, scale1, gate1, shift2, scale2, gate2 = [mod[:, i] for i in range(6)]
    rows_per_mod = T if mod.shape[0] == 1 else L
    g = lp['norm_g']

    tm = 1024
    P = _norm_mod_matmul(x, g[0:1], shift1, scale1, lp['w_in'], rows_per_mod=rows_per_mod, tm=tm,
                         tn=N_PROJ // 4)

    lam_init = 0.8 - 0.6 * math.exp(-0.3 * lidx)
    lq1, lk1, lq2, lk2 = lp['attn_lambda']
    lam = (jnp.exp(jnp.sum(lq1 * lk1)) - jnp.exp(jnp.sum(lq2 * lk2)) + lam_init).reshape(1, 1)
    attn_norm = lp['attn_norm'][None]
    if is_ctx:
        br_a, new_k, new_v = _attention(P, lam, attn_norm, B=B, L=L, tq=L, lam_init=lam_init)
    else:
        (br_a,) = _attention(P, lam, attn_norm, B=B, L=L, tq=256, lam_init=lam_init, cache=cache['attn'])

    br_b = _pool(P, lp['pool_w'], lp['pool_b'][:, None], lp['pool_scale'][None], B=B, L=L)

    gdn_args = (P, lp['gdn_A_log'], lp['gdn_dt_bias'], lp['gdn_conv'], lp['gdn_norm'][None])
    ml_args = (P, lp['mlstm_bias_i'], lp['mlstm_bias_f'], lp['mlstm_norm'][None])
    if is_ctx:
        br_c, new_gdn = _gdn(*gdn_args, B=B, L=L)
        br_d, new_c, new_n, new_m = _mlstm(*ml_args, B=B, L=L)
        ctx_out = (new_k, new_v, new_gdn, new_c,
                   new_n[:, :, :, 0, :HEAD_DIM].transpose(0, 2, 1, 3), new_m[:, :, :, 0, 0].transpose(0, 2, 1))
    else:
        (br_c,) = _gdn(*gdn_args, B=B, L=L, state=cache['gdn'])
        (br_d,) = _mlstm(*ml_args, B=B, L=L, state=cache['mlstm'])
        ctx_out = None

    x, h2, logits = _merge(x, g[0:1], shift1, scale1, lp['w_bgate'], lp['b_bgate'][None],
                           (br_a, br_b, br_c, br_d), lp['w_branch'], lp['w_out'], g[1:2], gate1,
                           g[2:3], shift2, scale2, lp['router_w'], lp['router_b'],
                           rows_per_mod=rows_per_mod, tm=256)
    expert_out, weights = _moe(h2, logits[:, :N_EXPERTS], lidx, lp['moe_w_gu'], lp['moe_b_gu'], lp['moe_w_dn'],
                               lp['moe_b_dn'])
    x = _combine(x, expert_out, weights, g[3:4], gate2, rows_per_mod=rows_per_mod, tm=512)
    return x, ctx_out


def kernel(x_prompt, x_sample, cache_attn_k, cache_attn_v, state_gdn, state_mlstm_C, state_mlstm_n, state_mlstm_m, c, c_ctx, ada_w, ada_b, norm_g, w_in, w_bgate, b_bgate, w_branch, w_out, attn_lambda, attn_norm, pool_w, pool_b, pool_scale, gdn_conv, gdn_A_log, gdn_dt_bias, gdn_norm, mlstm_bias_i, mlstm_bias_f, mlstm_norm, router_w, router_b, moe_w_gu, moe_b_gu, moe_w_dn, moe_b_dn):
    Bp, Lp, _ = x_prompt.shape
    Bs, Ls, _ = x_sample.shape
    y_p = x_prompt.reshape(Bp * Lp, D_MODEL)
    y_s = x_sample.reshape(Bs * Ls, D_MODEL)
    cos, sin = _rope_tables(Ls)
    n0 = state_mlstm_n.transpose(1, 0, 3, 2, 4)[:, :, :, :, None, :]
    m0 = jnp.broadcast_to(state_mlstm_m.transpose(1, 0, 3, 2)[..., None, None], n0.shape)
    ctx_states = []
    for l in range(DEPTH):
        lp = {
            'ada_w': ada_w[l], 'ada_b': ada_b[l], 'norm_g': norm_g[l],
            'w_in': _arrange_w_in(w_in[l].astype(BF16)), 'w_bgate': w_bgate[l].astype(BF16),
            'b_bgate': b_bgate[l], 'w_branch': w_branch[l].astype(BF16), 'w_out': w_out[l].astype(BF16),
            'attn_lambda': attn_lambda[l], 'attn_norm': attn_norm[l],
            'pool_w': pool_w[l], 'pool_b': pool_b[l], 'pool_scale': pool_scale[l],
            'gdn_conv': gdn_conv[l], 'gdn_A_log': gdn_A_log[l], 'gdn_dt_bias': gdn_dt_bias[l],
            'gdn_norm': gdn_norm[l], 'mlstm_bias_i': mlstm_bias_i[l], 'mlstm_bias_f': mlstm_bias_f[l],
            'mlstm_norm': mlstm_norm[l],
            'router_w': jnp.pad(router_w[l], ((0, 0), (0, LANE - N_EXPERTS))),
            'router_b': jnp.pad(router_b[l], (0, LANE - N_EXPERTS))[None],
            'moe_w_gu': moe_w_gu, 'moe_b_gu': moe_b_gu[l][:, None],
            'moe_w_dn': moe_w_dn, 'moe_b_dn': moe_b_dn[l][:, None],
        }
        y_p, st = _layer(y_p, c_ctx[None], l, lp, B=Bp, L=Lp, cache=None)
        ctx_states.append(st)
        cache = {
            'attn': {'k': cache_attn_k, 'v': cache_attn_v, 'layer': l, 'cos': cos, 'sin': sin},
            'gdn': (state_gdn, l),
            'mlstm': (state_mlstm_C, l, n0[l], m0[l]),
        }
        y_s, _ = _layer(y_s, c, l, lp, B=Bs, L=Ls, cache=cache)
    outs = [jnp.stack([s[i] for s in ctx_states], axis=1) for i in range(6)]
    return (y_p.reshape(Bp, Lp, D_MODEL), y_s.reshape(Bs, Ls, D_MODEL), *outs)
```
